```python
import jax, jax.numpy as jnp
from jax import lax
import numpy as np

D_MODEL = 1024
BATCH = 8
SEQ = 2048
DEPTH = 1

GRID_W = 64
POOL_WINDOWS = (2, 4, 8, 16)
POOL_WIDTH = D_MODEL // 2
POOL_GROUP = POOL_WIDTH // len(POOL_WINDOWS)
ATTN_HEADS = 8
HEAD_DIM = (D_MODEL // 2) // ATTN_HEADS
ATTN_WIDTH = ATTN_HEADS * HEAD_DIM
MIX_WIDTH = POOL_WIDTH + ATTN_WIDTH
WIN_ROWS_MAX = 8
WIN_COLS = 16
N_EXPERTS = 16
EC_CAPACITY = 2
D_EXPERT = 2 * D_MODEL
PLE_DIM = 256
RMS_EPS = 1e-6

kernel_name = "hybrid_pool_natten_ec_block"


def rms_norm(x, g):
    x32 = x.astype(jnp.float32)
    y = x32 * lax.rsqrt(jnp.mean(x32 * x32, axis=-1, keepdims=True) + RMS_EPS)
    return (y * g.astype(jnp.float32)).astype(x.dtype)


def multiscale_pool(u, w_pool, pool_scale):
    B, S, _ = u.shape
    u32 = u.astype(jnp.float32)
    cs = jnp.concatenate([jnp.zeros((B, 1, POOL_WIDTH), jnp.float32), jnp.cumsum(u32, axis=1)], axis=1)
    t = jnp.arange(S)
    outs = []
    for gi, w in enumerate(POOL_WINDOWS):
        lo = jnp.clip(t - w // 2, 0, S - 1)
        hi = jnp.clip(t + (w - w // 2) - 1, 0, S - 1)
        sl = slice(gi * POOL_GROUP, (gi + 1) * POOL_GROUP)
        csg = cs[:, :, sl]
        cnt = (hi - lo + 1).astype(jnp.float32)[None, :, None]
        outs.append((csg[:, hi + 1] - csg[:, lo]) / cnt - u32[:, :, sl])
    d = jnp.stack(outs, axis=2).astype(u.dtype)
    y = jnp.einsum('bsgc,gcd->bsgd', d, w_pool).reshape(B, S, POOL_WIDTH)
    return y * pool_scale


def neighbourhood_attention(q, k, v, q_norm, k_norm, rpb):
    B, S, H, HD = q.shape
    rows = S // GRID_W
    kh = min(WIN_ROWS_MAX, rows)
    q = rms_norm(q, q_norm) * (HD ** -0.5)
    k = rms_norm(k, k_norm)
    to_grid = lambda a: a.reshape(B, rows, GRID_W, H, HD).transpose(0, 3, 1, 2, 4)
    qg, kg, vg = to_grid(q), to_grid(k), to_grid(v)
    c = jnp.arange(GRID_W)
    col_start = jnp.clip(c - WIN_COLS // 2, 0, GRID_W - WIN_COLS)
    col_idx = col_start[:, None] + jnp.arange(WIN_COLS)[None, :]
    dc = col_idx - c[:, None] + (WIN_COLS - 1)

    def row_block(r):
        row_start = jnp.clip(r - kh // 2, 0, rows - kh)
        q_r = lax.dynamic_index_in_dim(qg, r, axis=2, keepdims=False)
        k_band = lax.dynamic_slice_in_dim(kg, row_start, kh, axis=2)
        v_band = lax.dynamic_slice_in_dim(vg, row_start, kh, axis=2)
        k_nb = k_band[:, :, :, col_idx]
        v_nb = v_band[:, :, :, col_idx]
        s = jnp.einsum('bhcd,bhicjd->bhcij', q_r, k_nb).astype(jnp.float32)
        dr = row_start + jnp.arange(kh) - r + (WIN_ROWS_MAX - 1)
        bias = rpb[:, dr[None, :, None], dc[:, None, :]]
        s = s + bias[None].astype(jnp.float32)
        pr = jax.nn.softmax(s.reshape(B, H, GRID_W, kh * WIN_COLS), axis=-1)
        pr = pr.reshape(B, H, GRID_W, kh, WIN_COLS).astype(v.dtype)
        return jnp.einsum('bhcij,bhicjd->bhcd', pr, v_nb)

    o = lax.map(row_block, jnp.arange(rows))
    return o.transpose(1, 0, 3, 2, 4).reshape(B, S, H * HD)


def expert_choice_ffn(h, w_router, w_gate, w_up, w_down):
    B, S, D = h.shape
    cap = EC_CAPACITY * S // N_EXPERTS
    aff = jax.nn.softmax(jnp.einsum('bsd,de->bse', h, w_router).astype(jnp.float32), axis=-1)
    gates, idx = lax.top_k(aff.transpose(0, 2, 1), cap)
    bidx = jnp.arange(B)[:, None, None]
    xe = h[bidx, idx]
    a = jnp.einsum('becd,edf->becf', xe, w_gate)
    b = jnp.einsum('becd,edf->becf', xe, w_up)
    ye = jnp.einsum('becf,efd->becd', jax.nn.silu(a) * b, w_down)
    return jnp.zeros_like(h).at[bidx, idx].add(ye * gates[..., None].astype(h.dtype))


def setup_inputs(seed: int = 0) -> dict:
    key = jax.random.key(seed)
    ks = jax.random.split(key, 20)
    f32 = jnp.float32
    nrm = lambda k, shape, s: jax.random.normal(k, shape, f32) * s
    gain = lambda k, shape: 1.0 + 0.02 * jax.random.normal(k, shape, f32)
    L = DEPTH
    return {
        "x": nrm(ks[0], (BATCH, SEQ, D_MODEL), 1.0),
        "p": nrm(ks[1], (L, BATCH, SEQ, PLE_DIM), 1.0),
        "norm_mix": gain(ks[2], (L, D_MODEL)),
        "w_in": nrm(ks[3], (L, D_MODEL, POOL_WIDTH + 3 * ATTN_WIDTH), D_MODEL ** -0.5),
        "w_pool": nrm(ks[4], (L, len(POOL_WINDOWS), POOL_GROUP, POOL_GROUP), POOL_GROUP ** -0.5),
        "pool_scale": gain(ks[5], (L, POOL_WIDTH)),
        "q_norm": gain(ks[6], (L, HEAD_DIM)),
        "k_norm": gain(ks[7], (L, HEAD_DIM)),
        "rpb": nrm(ks[8], (L, ATTN_HEADS, 2 * WIN_ROWS_MAX - 1, 2 * WIN_COLS - 1), 0.1),
        "w_out": nrm(ks[9], (L, MIX_WIDTH, D_MODEL), MIX_WIDTH ** -0.5),
        "norm_ffn": gain(ks[10], (L, D_MODEL)),
        "w_router": nrm(ks[11], (L, D_MODEL, N_EXPERTS), D_MODEL ** -0.5),
        "w_gate": nrm(ks[12], (L, N_EXPERTS, D_MODEL, D_EXPERT), D_MODEL ** -0.5),
        "w_up": nrm(ks[13], (L, N_EXPERTS, D_MODEL, D_EXPERT), D_MODEL ** -0.5),
        "w_down": nrm(ks[14], (L, N_EXPERTS, D_EXPERT, D_MODEL), D_EXPERT ** -0.5),
        "norm_ple": gain(ks[15], (L, D_MODEL)),
        "w_ple_gate": nrm(ks[16], (L, D_MODEL, D_MODEL), D_MODEL ** -0.5),
        "w_ple_proj": nrm(ks[17], (L, PLE_DIM, D_MODEL), PLE_DIM ** -0.5),
        "norm_ple_post": gain(ks[18], (L, D_MODEL)),
    }


def reference(x, p, norm_mix, w_in, w_pool, pool_scale, q_norm, k_norm, rpb, w_out,
              norm_ffn, w_router, w_gate, w_up, w_down, norm_ple, w_ple_gate,
              w_ple_proj, norm_ple_post):
    B, S, _ = x.shape
    for i in range(DEPTH):
        h = rms_norm(x, norm_mix[i])
        z = h @ w_in[i]
        u = z[..., :POOL_WIDTH]
        qkv = z[..., POOL_WIDTH:].reshape(B, S, 3, ATTN_HEADS, HEAD_DIM)
        y_pool = multiscale_pool(u, w_pool[i], pool_scale[i])
        y_attn = neighbourhood_attention(qkv[:, :, 0], qkv[:, :, 1], qkv[:, :, 2],
                                         q_norm[i], k_norm[i], rpb[i])
        x = x + jnp.concatenate([y_pool, y_attn], axis=-1) @ w_out[i]
        x = x + expert_choice_ffn(rms_norm(x, norm_ffn[i]), w_router[i], w_gate[i], w_up[i], w_down[i])
        g = jax.nn.sigmoid(rms_norm(x, norm_ple[i]) @ w_ple_gate[i])
        e = rms_norm(p[i] @ w_ple_proj[i], norm_ple_post[i])
        x = x + g * e
    return x
```

```python
import functools

import jax
import jax.numpy as jnp
from jax import lax
from jax.experimental import pallas as pl
from jax.experimental.pallas import tpu as pltpu

D_MODEL = 1024
GRID_W = 64
POOL_WINDOWS = (2, 4, 8, 16)
POOL_WIDTH = D_MODEL // 2
POOL_GROUP = POOL_WIDTH // len(POOL_WINDOWS)
ATTN_HEADS = 8
HEAD_DIM = (D_MODEL // 2) // ATTN_HEADS
ATTN_WIDTH = ATTN_HEADS * HEAD_DIM
WIN_ROWS_MAX = 8
WIN_COLS = 16
N_EXPERTS = 16
EC_CAPACITY = 2
D_EXPERT = 2 * D_MODEL
PLE_DIM = 256
RMS_EPS = 1e-6

LANES = 128
POOL_HALO = 8
MASK_BIAS = -1e30
EXPERT_ROWS = 512
VMEM_LIMIT = 56 * 1024 * 1024

BF16 = jnp.bfloat16
F32 = jnp.float32


def _params(*sem):
    return pltpu.CompilerParams(dimension_semantics=sem, vmem_limit_bytes=VMEM_LIMIT)


def _rms(x, g):
    return x * lax.rsqrt(jnp.mean(x * x, axis=-1, keepdims=True) + RMS_EPS) * g


def _dot(a, b):
    return jnp.dot(a, b, preferred_element_type=F32)


def _dot_nt(a, b):
    return lax.dot_general(a, b, (((1,), (1,)), ((), ())), preferred_element_type=F32)


def _in_proj_kernel(x_ref, g_ref, w_ref, bd_ref, qg_ref, kg_ref,
                    u_ref, q_ref, k_ref, v_ref, wbf_ref):
    @pl.when(pl.program_id(0) == 0)
    def _():
        wbf_ref[...] = w_ref[...].astype(BF16)

    h = _rms(x_ref[...], g_ref[...]).astype(BF16)
    z = _dot(h, wbf_ref[...])
    u_ref[...] = z[:, :POOL_WIDTH]
    bd = bd_ref[...]

    def head_norm(t, g):
        ms = _dot((t * t).astype(BF16), bd) * (1.0 / HEAD_DIM)
        return t * lax.rsqrt(ms + RMS_EPS) * g

    q = z[:, POOL_WIDTH:POOL_WIDTH + ATTN_WIDTH]
    k = z[:, POOL_WIDTH + ATTN_WIDTH:POOL_WIDTH + 2 * ATTN_WIDTH]
    v = z[:, POOL_WIDTH + 2 * ATTN_WIDTH:]
    q_ref[...] = (head_norm(q, qg_ref[...]) * (HEAD_DIM ** -0.5)).astype(BF16)
    k_ref[...] = head_norm(k, kg_ref[...]).astype(BF16)
    v_ref[...] = v.astype(BF16)


def _in_proj(x2, g, w, bd, qg, kg, tm=512):
    n = x2.shape[0]
    zw = w.shape[1]
    row = lambda i: (i, 0)
    fixed = lambda i: (0, 0)
    return pl.pallas_call(
        _in_proj_kernel,
        grid=(n // tm,),
        in_specs=[
            pl.BlockSpec((tm, D_MODEL), row),
            pl.BlockSpec((1, D_MODEL), fixed),
            pl.BlockSpec((D_MODEL, zw), fixed),
            pl.BlockSpec((ATTN_WIDTH, ATTN_WIDTH), fixed),
            pl.BlockSpec((1, ATTN_WIDTH), fixed),
            pl.BlockSpec((1, ATTN_WIDTH), fixed),
        ],
        out_specs=[
            pl.BlockSpec((tm, POOL_WIDTH), row),
            pl.BlockSpec((tm, ATTN_WIDTH), row),
            pl.BlockSpec((tm, ATTN_WIDTH), row),
            pl.BlockSpec((tm, ATTN_WIDTH), row),
        ],
        out_shape=[
            jax.ShapeDtypeStruct((n, POOL_WIDTH), F32),
            jax.ShapeDtypeStruct((n, ATTN_WIDTH), BF16),
            jax.ShapeDtypeStruct((n, ATTN_WIDTH), BF16),
            jax.ShapeDtypeStruct((n, ATTN_WIDTH), BF16),
        ],
        scratch_shapes=[pltpu.VMEM((D_MODEL, zw), BF16)],
        compiler_params=_params("arbitrary"),
        name="in_proj",
    )(x2, g, w, bd, qg, kg)


def _natten_kernel(q_ref, k_ref, v_ref, bias_ref, o_ref, *, rows, kh):
    band = kh * GRID_W
    lane = lax.broadcasted_iota(jnp.int32, (GRID_W, LANES), 1)
    first = lane < HEAD_DIM

    def row_block(r, carry):
        rs = jnp.clip(r - kh // 2, 0, rows - kh)
        d0 = rs - r + (WIN_ROWS_MAX - 1)
        q0 = pl.multiple_of(r * GRID_W, GRID_W)
        k0 = pl.multiple_of(rs * GRID_W, GRID_W)
        qr = q_ref[0, pl.ds(q0, GRID_W), :]
        kb = k_ref[0, pl.ds(k0, band), :]
        vb = v_ref[0, pl.ds(k0, band), :]
        outs = []
        for hh in range(2):
            sel = first if hh == 0 else jnp.logical_not(first)
            qm = jnp.where(sel, qr, jnp.zeros_like(qr))
            s = _dot_nt(qm, kb) + bias_ref[hh, d0]
            m = jnp.max(s, axis=-1, keepdims=True)
            e = jnp.exp(s - m)
            l = jnp.sum(e, axis=-1, keepdims=True)
            outs.append(_dot(e.astype(BF16), vb) / l)
        o_ref[0, pl.ds(q0, GRID_W), :] = jnp.where(first, outs[0], outs[1]).astype(o_ref.dtype)
        return carry

    lax.fori_loop(0, rows, row_block, 0)


def _natten(q, k, v, bias):
    b, s, _ = q.shape
    rows = s // GRID_W
    kh = min(WIN_ROWS_MAX, rows)
    pairs = ATTN_WIDTH // LANES
    blk = pl.BlockSpec((1, s, LANES), lambda bi, p: (bi, 0, p))
    return pl.pallas_call(
        functools.partial(_natten_kernel, rows=rows, kh=kh),
        grid=(b, pairs),
        in_specs=[blk, blk, blk,
                  pl.BlockSpec((2, WIN_ROWS_MAX, GRID_W, kh * GRID_W), lambda bi, p: (p, 0, 0, 0))],
        out_specs=blk,
        out_shape=jax.ShapeDtypeStruct((b, s, ATTN_WIDTH), BF16),
        compiler_params=_params("arbitrary", "arbitrary"),
        name="natten",
    )(q, k, v, bias)


def _attn_bias_table(rpb, kh):
    c = jnp.arange(GRID_W)
    cs = jnp.clip(c - WIN_COLS // 2, 0, GRID_W - WIN_COLS)
    j = jnp.arange(GRID_W)
    valid = (j[None, :] >= cs[:, None]) & (j[None, :] < cs[:, None] + WIN_COLS)
    dc = jnp.clip(j[None, :] - c[:, None] + (WIN_COLS - 1), 0, 2 * WIN_COLS - 2)
    dr = jnp.clip(jnp.arange(WIN_ROWS_MAX)[:, None] + jnp.arange(kh)[None, :], 0, 2 * WIN_ROWS_MAX - 2)
    tbl = rpb[:, dr[:, :, None, None], dc[None, None, :, :]]
    tbl = jnp.where(valid[None, None, None], tbl.astype(F32), MASK_BIAS)
    return tbl.transpose(0, 1, 3, 2, 4).reshape(rpb.shape[0], WIN_ROWS_MAX, GRID_W, kh * GRID_W)


def _mix_kernel(x_ref, u_ref, up_ref, un_ref, ya_ref, wp_ref, ps_ref, wo_ref, g_ref, wr_ref,
                x1_ref, h_ref, aff_ref, upad_ref, wobf_ref, wpbf_ref, *, seq, tm):
    i = pl.program_id(1)

    @pl.when((pl.program_id(0) == 0) & (i == 0))
    def _():
        wobf_ref[...] = wo_ref[...].astype(BF16)
        wpbf_ref[...] = wp_ref[...].astype(BF16)

    upad_ref[0:POOL_HALO, :] = jnp.where(i > 0, up_ref[0], 0.0)
    upad_ref[POOL_HALO:POOL_HALO + tm, :] = u_ref[0]
    upad_ref[POOL_HALO + tm:, :] = jnp.where(i < pl.num_programs(1) - 1, un_ref[0], 0.0)

    t = i * tm + lax.broadcasted_iota(jnp.int32, (tm, 1), 0)
    ypool = []
    for gi, w in enumerate(POOL_WINDOWS):
        cols = slice(gi * POOL_GROUP, (gi + 1) * POOL_GROUP)
        acc = None
        for off in range(-(w // 2), w - w // 2):
            piece = upad_ref[POOL_HALO + off:POOL_HALO + off + tm, cols]
            acc = piece if acc is None else acc + piece
        lo = jnp.maximum(t - w // 2, 0)
        hi = jnp.minimum(t + (w - w // 2) - 1, seq - 1)
        cnt = (hi - lo + 1).astype(F32)
        d = acc / cnt - upad_ref[POOL_HALO:POOL_HALO + tm, cols]
        ypool.append(_dot(d.astype(BF16), wpbf_ref[gi]) * ps_ref[:, cols])
    ypool = jnp.concatenate(ypool, axis=-1).astype(BF16)

    mix = _dot(ypool, wobf_ref[:POOL_WIDTH, :]) + _dot(ya_ref[0], wobf_ref[POOL_WIDTH:, :])
    x1 = x_ref[0] + mix
    x1_ref[0] = x1
    h = _rms(x1, g_ref[...]).astype(BF16)
    h_ref[0] = h
    logits = _dot_nt(wr_ref[...].astype(BF16), h)
    m = jnp.max(logits, axis=0, keepdims=True)
    e = jnp.exp(logits - m)
    aff_ref[0] = e / jnp.sum(e, axis=0, keepdims=True)


def _mix(x, u, ya, wp, ps, wo, g, wr_t, tm=512):
    b, s, _ = x.shape
    nt = s // tm
    hb = tm // POOL_HALO
    tile = lambda w: pl.BlockSpec((1, tm, w), lambda bi, i: (bi, i, 0))
    fixed2 = lambda shape: pl.BlockSpec(shape, lambda bi, i: (0, 0))
    return pl.pallas_call(
        functools.partial(_mix_kernel, seq=s, tm=tm),
        grid=(b, nt),
        in_specs=[
            tile(D_MODEL),
            tile(POOL_WIDTH),
            pl.BlockSpec((1, POOL_HALO, POOL_WIDTH), lambda bi, i: (bi, jnp.maximum(i * hb - 1, 0), 0)),
            pl.BlockSpec((1, POOL_HALO, POOL_WIDTH),
                         lambda bi, i: (bi, jnp.minimum((i + 1) * hb, s // POOL_HALO - 1), 0)),
            tile(ATTN_WIDTH),
            pl.BlockSpec(wp.shape, lambda bi, i: (0, 0, 0)),
            fixed2((1, POOL_WIDTH)),
            fixed2(wo.shape),
            fixed2((1, D_MODEL)),
            fixed2(wr_t.shape),
        ],
        out_specs=[
            tile(D_MODEL),
            tile(D_MODEL),
            pl.BlockSpec((1, N_EXPERTS, tm), lambda bi, i: (bi, 0, i)),
        ],
        out_shape=[
            jax.ShapeDtypeStruct((b, s, D_MODEL), F32),
            jax.ShapeDtypeStruct((b, s, D_MODEL), BF16),
            jax.ShapeDtypeStruct((b, N_EXPERTS, s), F32),
        ],
        scratch_shapes=[
            pltpu.VMEM((tm + 2 * POOL_HALO, POOL_WIDTH), F32),
            pltpu.VMEM(wo.shape, BF16),
            pltpu.VMEM(wp.shape, BF16),
        ],
        compiler_params=_params("arbitrary", "arbitrary"),
        name="mix",
    )(x, u, u, u, ya, wp, ps, wo, g, wr_t)


def _lane_cumsum_exclusive(m):
    e, s = m.shape
    r = lax.broadcasted_iota(jnp.int32, (LANES, LANES), 0)
    c = lax.broadcasted_iota(jnp.int32, (LANES, LANES), 1)
    upper = (r < c).astype(BF16)
    carry = jnp.zeros((e, 1), F32)
    out = []
    for blk in range(s // LANES):
        piece = m[:, blk * LANES:(blk + 1) * LANES]
        out.append(_dot(piece.astype(BF16), upper) + carry)
        carry = carry + jnp.sum(piece, axis=-1, keepdims=True)
    return jnp.concatenate(out, axis=-1)


def _route_kernel(aff_ref, slot_ref, slot_t_ref, *, cap):
    aff = aff_ref[0]
    bits = pltpu.bitcast(aff, jnp.int32)
    capf = jnp.float32(cap)

    def count_ge(cand):
        return jnp.sum((bits >= cand).astype(F32), axis=-1, keepdims=True)

    def search(step, ans):
        cand = ans | (jnp.int32(1) << (30 - step))
        return jnp.where(count_ge(cand) >= capf, cand, ans)

    thr = lax.fori_loop(0, 31, search, jnp.zeros((aff.shape[0], 1), jnp.int32))
    gt = bits > thr
    eq = bits == thr
    need = capf - jnp.sum(gt.astype(F32), axis=-1, keepdims=True)
    sel = gt | (eq & (_lane_cumsum_exclusive(eq.astype(F32)) < need))
    slot = _lane_cumsum_exclusive(sel.astype(F32))
    slot = jnp.where(sel, slot, -1.0)
    slot_ref[0] = slot.astype(jnp.int32)
    pad = jnp.full((LANES - slot.shape[0], slot.shape[1]), -1.0, F32)
    slot_t_ref[0] = jnp.concatenate([slot, pad], axis=0).T.astype(jnp.int32)


def _route(aff_t, cap):
    b, e, s = aff_t.shape
    return pl.pallas_call(
        functools.partial(_route_kernel, cap=cap),
        grid=(b,),
        in_specs=[pl.BlockSpec((1, e, s), lambda bi: (bi, 0, 0))],
        out_specs=[
            pl.BlockSpec((1, e, s), lambda bi: (bi, 0, 0)),
            pl.BlockSpec((1, s, LANES), lambda bi: (bi, 0, 0)),
        ],
        out_shape=[
            jax.ShapeDtypeStruct((b, e, s), jnp.int32),
            jax.ShapeDtypeStruct((b, s, LANES), jnp.int32),
        ],
        compiler_params=_params("arbitrary"),
        name="route",
    )(aff_t)


def _dispatch_kernel(slot_ref, aff_ref, h_ref, xe_ref, gate_ref, *, cap):
    slot = slot_ref[0, 0]
    s = slot.shape[-1]
    onehot = lax.broadcasted_iota(jnp.int32, (cap, s), 0) == slot
    p = jnp.where(onehot, 1.0, 0.0).astype(BF16)
    xe_ref[0] = _dot(p, h_ref[0]).astype(BF16)
    gate_ref[0] = jnp.sum(jnp.where(onehot, aff_ref[0, 0], 0.0), axis=-1, keepdims=True)


def _dispatch(slot, aff_t, h, cap):
    b, e, s = slot.shape
    slot4 = slot.reshape(b, e, 1, s)
    aff4 = aff_t.reshape(b, e, 1, s)
    row = pl.BlockSpec((1, 1, 1, s), lambda bi, ei: (bi, ei, 0, 0))
    return pl.pallas_call(
        functools.partial(_dispatch_kernel, cap=cap),
        grid=(b, e),
        in_specs=[row, row, pl.BlockSpec((1, s, D_MODEL), lambda bi, ei: (bi, 0, 0))],
        out_specs=[
            pl.BlockSpec((1, cap, D_MODEL), lambda bi, ei: (ei, bi, 0)),
            pl.BlockSpec((1, cap, 1), lambda bi, ei: (ei, bi, 0)),
        ],
        out_shape=[
            jax.ShapeDtypeStruct((e, b * cap, D_MODEL), BF16),
            jax.ShapeDtypeStruct((e, b * cap, 1), F32),
        ],
        compiler_params=_params("arbitrary", "arbitrary"),
        name="dispatch",
    )(slot4, aff4, h)


def _experts_kernel(xe_ref, gate_ref, wg_ref, wu_ref, wd_ref, y_ref, acc_ref):
    f = pl.program_id(1)
    wg = wg_ref[0].astype(BF16)
    wu = wu_ref[0].astype(BF16)
    wd = wd_ref[0].astype(BF16)

    @pl.when(f == 0)
    def _():
        acc_ref[...] = jnp.zeros_like(acc_ref)

    for mb in range(xe_ref.shape[1] // EXPERT_ROWS):
        rows = slice(mb * EXPERT_ROWS, (mb + 1) * EXPERT_ROWS)
        xe = xe_ref[0, rows, :]
        a = _dot(xe, wg)
        b = _dot(xe, wu)
        hmid = (a * jax.nn.sigmoid(a) * b).astype(BF16)
        acc_ref[rows, :] += _dot(hmid, wd)

    @pl.when(f == pl.num_programs(1) - 1)
    def _():
        y_ref[0] = (acc_ref[...] * gate_ref[0]).astype(y_ref.dtype)


def _experts(xe, gate, wg, wu, wd, tf=512):
    e, m, _ = xe.shape
    nf = D_EXPERT // tf
    return pl.pallas_call(
        _experts_kernel,
        grid=(e, nf),
        in_specs=[
            pl.BlockSpec((1, m, D_MODEL), lambda ei, f: (ei, 0, 0)),
            pl.BlockSpec((1, m, 1), lambda ei, f: (ei, 0, 0)),
            pl.BlockSpec((1, D_MODEL, tf), lambda ei, f: (ei, 0, f)),
            pl.BlockSpec((1, D_MODEL, tf), lambda ei, f: (ei, 0, f)),
            pl.BlockSpec((1, tf, D_MODEL), lambda ei, f: (ei, f, 0)),
        ],
        out_specs=pl.BlockSpec((1, m, D_MODEL), lambda ei, f: (ei, 0, 0)),
        out_shape=jax.ShapeDtypeStruct((e, m, D_MODEL), BF16),
        scratch_shapes=[pltpu.VMEM((m, D_MODEL), F32)],
        compiler_params=_params("arbitrary", "arbitrary"),
        name="experts",
    )(xe, gate, wg, wu, wd)


def _combine_kernel(x1_ref, st_ref, y_ref, p_ref, gn_ref, wg_ref, wp_ref, gp_ref,
                    o_ref, wgbf_ref, wpbf_ref, *, cap):
    @pl.when((pl.program_id(0) == 0) & (pl.program_id(1) == 0))
    def _():
        wgbf_ref[...] = wg_ref[...].astype(BF16)
        wpbf_ref[...] = wp_ref[...].astype(BF16)

    tm = x1_ref.shape[1]
    st = st_ref[0]
    lane = lax.broadcasted_iota(jnp.int32, (tm, cap), 1)
    x2 = x1_ref[0]
    for e in range(N_EXPERTS):
        onehot = jnp.where(lane == st[:, e:e + 1], 1.0, 0.0).astype(BF16)
        x2 = x2 + _dot(onehot, y_ref[e])
    g = jax.nn.sigmoid(_dot(_rms(x2, gn_ref[...]).astype(BF16), wgbf_ref[...]))
    emb = _rms(_dot(p_ref[0].astype(BF16), wpbf_ref[...]), gp_ref[...])
    o_ref[0] = x2 + g * emb


def _combine(x1, slot_t, y, p, gn, wg, wp, gp, cap, tm=512):
    b, s, _ = x1.shape
    tile = lambda w: pl.BlockSpec((1, tm, w), lambda bi, i: (bi, i, 0))
    fixed2 = lambda shape: pl.BlockSpec(shape, lambda bi, i: (0, 0))
    return pl.pallas_call(
        functools.partial(_combine_kernel, cap=cap),
        grid=(b, s // tm),
        in_specs=[
            tile(D_MODEL),
            tile(LANES),
            pl.BlockSpec((N_EXPERTS, cap, D_MODEL), lambda bi, i: (0, bi, 0)),
            tile(PLE_DIM),
            fixed2((1, D_MODEL)),
            fixed2(wg.shape),
            fixed2(wp.shape),
            fixed2((1, D_MODEL)),
        ],
        out_specs=tile(D_MODEL),
        out_shape=jax.ShapeDtypeStruct((b, s, D_MODEL), F32),
        scratch_shapes=[pltpu.VMEM(wg.shape, BF16), pltpu.VMEM(wp.shape, BF16)],
        compiler_params=_params("arbitrary", "arbitrary"),
        name="combine",
    )(x1, slot_t, y, p, gn, wg, wp, gp)


def kernel(x, p, norm_mix, w_in, w_pool, pool_scale, q_norm, k_norm, rpb, w_out, norm_ffn, w_router, w_gate, w_up, w_down, norm_ple, w_ple_gate, w_ple_proj, norm_ple_post):
    b, s, d = x.shape
    depth = w_in.shape[0]
    cap = EC_CAPACITY * s // N_EXPERTS
    kh = min(WIN_ROWS_MAX, s // GRID_W)
    head = jnp.arange(ATTN_WIDTH) // HEAD_DIM
    block_diag = (head[:, None] == head[None, :]).astype(BF16)
    row = lambda a: a.reshape(1, -1)
    for i in range(depth):
        u, q, k, v = _in_proj(x.reshape(b * s, d), row(norm_mix[i]), w_in[i], block_diag,
                              row(jnp.tile(q_norm[i], ATTN_HEADS)), row(jnp.tile(k_norm[i], ATTN_HEADS)))
        shp = lambda a: a.reshape(b, s, -1)
        y_attn = _natten(shp(q), shp(k), shp(v), _attn_bias_table(rpb[i], kh))
        x1, h, aff_t = _mix(x, shp(u), y_attn, w_pool[i], row(pool_scale[i]), w_out[i],
                            row(norm_ffn[i]), w_router[i].T)
        slot, slot_t = _route(aff_t, cap)
        xe, gate = _dispatch(slot, aff_t, h, cap)
        y = _experts(xe, gate, w_gate[i], w_up[i], w_down[i])
        x = _combine(x1, slot_t, y, p[i], row(norm_ple[i]), w_ple_gate[i], w_ple_proj[i],
                     row(norm_ple_post[i]), cap)
    return x
```

```python
import functools

import jax
import jax.numpy as jnp
from jax import lax
from jax.experimental import pallas as pl
from jax.experimental.pallas import tpu as pltpu

D_MODEL = 1024
GRID_W = 64
POOL_WINDOWS = (2, 4, 8, 16)
POOL_WIDTH = D_MODEL // 2
POOL_GROUP = POOL_WIDTH // len(POOL_WINDOWS)
ATTN_HEADS = 8
HEAD_DIM = (D_MODEL // 2) // ATTN_HEADS
ATTN_WIDTH = ATTN_HEADS * HEAD_DIM
WIN_ROWS_MAX = 8
WIN_COLS = 16
N_EXPERTS = 16
EC_CAPACITY = 2
D_EXPERT = 2 * D_MODEL
PLE_DIM = 256
RMS_EPS = 1e-6

LANES = 128
POOL_HALO = 8
MASK_BIAS = -1e30
NATTEN_UNROLL = 8
EXPERT_ROWS = 512
VMEM_LIMIT = 56 * 1024 * 1024

BF16 = jnp.bfloat16
F32 = jnp.float32


def _params(*sem):
    return pltpu.CompilerParams(dimension_semantics=sem, vmem_limit_bytes=VMEM_LIMIT)


def _rms(x, g):
    return x * lax.rsqrt(jnp.mean(x * x, axis=-1, keepdims=True) + RMS_EPS) * g


def _dot(a, b):
    return jnp.dot(a, b, preferred_element_type=F32)


def _dot_nt(a, b):
    return lax.dot_general(a, b, (((1,), (1,)), ((), ())), preferred_element_type=F32)


def _in_proj_kernel(x_ref, g_ref, w_ref, bd_ref, qg_ref, kg_ref,
                    u_ref, q_ref, k_ref, v_ref, wbf_ref):
    @pl.when(pl.program_id(0) == 0)
    def _():
        wbf_ref[...] = w_ref[...].astype(BF16)

    h = _rms(x_ref[...], g_ref[...]).astype(BF16)
    z = _dot(h, wbf_ref[...])
    u_ref[...] = z[:, :POOL_WIDTH]
    bd = bd_ref[...]

    def head_norm(t, g):
        ms = _dot((t * t).astype(BF16), bd) * (1.0 / HEAD_DIM)
        return t * lax.rsqrt(ms + RMS_EPS) * g

    q = z[:, POOL_WIDTH:POOL_WIDTH + ATTN_WIDTH]
    k = z[:, POOL_WIDTH + ATTN_WIDTH:POOL_WIDTH + 2 * ATTN_WIDTH]
    v = z[:, POOL_WIDTH + 2 * ATTN_WIDTH:]
    q_ref[...] = (head_norm(q, qg_ref[...]) * (HEAD_DIM ** -0.5)).astype(BF16)
    k_ref[...] = head_norm(k, kg_ref[...]).astype(BF16)
    v_ref[...] = v.astype(BF16)


def _in_proj(x2, g, w, bd, qg, kg, tm=512):
    n = x2.shape[0]
    zw = w.shape[1]
    row = lambda i: (i, 0)
    fixed = lambda i: (0, 0)
    return pl.pallas_call(
        _in_proj_kernel,
        grid=(n // tm,),
        in_specs=[
            pl.BlockSpec((tm, D_MODEL), row),
            pl.BlockSpec((1, D_MODEL), fixed),
            pl.BlockSpec((D_MODEL, zw), fixed),
            pl.BlockSpec((ATTN_WIDTH, ATTN_WIDTH), fixed),
            pl.BlockSpec((1, ATTN_WIDTH), fixed),
            pl.BlockSpec((1, ATTN_WIDTH), fixed),
        ],
        out_specs=[
            pl.BlockSpec((tm, POOL_WIDTH), row),
            pl.BlockSpec((tm, ATTN_WIDTH), row),
            pl.BlockSpec((tm, ATTN_WIDTH), row),
            pl.BlockSpec((tm, ATTN_WIDTH), row),
        ],
        out_shape=[
            jax.ShapeDtypeStruct((n, POOL_WIDTH), F32),
            jax.ShapeDtypeStruct((n, ATTN_WIDTH), BF16),
            jax.ShapeDtypeStruct((n, ATTN_WIDTH), BF16),
            jax.ShapeDtypeStruct((n, ATTN_WIDTH), BF16),
        ],
        scratch_shapes=[pltpu.VMEM((D_MODEL, zw), BF16)],
        compiler_params=_params("arbitrary"),
        name="in_proj",
    )(x2, g, w, bd, qg, kg)


def _natten_kernel(q_ref, k_ref, v_ref, bias_ref, o_ref, *, rows, kh):
    band = kh * GRID_W
    lane = lax.broadcasted_iota(jnp.int32, (GRID_W, LANES), 1)
    first = lane < HEAD_DIM

    def row_group(g, carry):
        geo = []
        for sub in range(NATTEN_UNROLL):
            r = g * NATTEN_UNROLL + sub
            rs = jnp.clip(r - kh // 2, 0, rows - kh)
            geo.append((rs - r + (WIN_ROWS_MAX - 1),
                        pl.multiple_of(r * GRID_W, GRID_W),
                        pl.multiple_of(rs * GRID_W, GRID_W)))
        scores = []
        for d0, q0, k0 in geo:
            qr = q_ref[0, pl.ds(q0, GRID_W), :]
            kb = k_ref[0, pl.ds(k0, band), :]
            for hh in range(2):
                sel = first if hh == 0 else jnp.logical_not(first)
                qm = jnp.where(sel, qr, jnp.zeros_like(qr))
                scores.append(_dot_nt(qm, kb) + bias_ref[hh, d0])
        probs = []
        for s in scores:
            e = jnp.exp(s - jnp.max(s, axis=-1, keepdims=True))
            probs.append((e.astype(BF16), jnp.sum(e, axis=-1, keepdims=True)))
        for idx, (d0, q0, k0) in enumerate(geo):
            vb = v_ref[0, pl.ds(k0, band), :]
            outs = [_dot(e, vb) / l for e, l in probs[2 * idx:2 * idx + 2]]
            o_ref[0, pl.ds(q0, GRID_W), :] = jnp.where(first, outs[0], outs[1]).astype(o_ref.dtype)
        return carry

    lax.fori_loop(0, rows // NATTEN_UNROLL, row_group, 0)


def _natten(q, k, v, bias):
    b, s, _ = q.shape
    rows = s // GRID_W
    kh = min(WIN_ROWS_MAX, rows)
    assert rows % NATTEN_UNROLL == 0
    pairs = ATTN_WIDTH // LANES
    blk = pl.BlockSpec((1, s, LANES), lambda bi, p: (bi, 0, p))
    return pl.pallas_call(
        functools.partial(_natten_kernel, rows=rows, kh=kh),
        grid=(b, pairs),
        in_specs=[blk, blk, blk,
                  pl.BlockSpec((2, WIN_ROWS_MAX, GRID_W, kh * GRID_W), lambda bi, p: (p, 0, 0, 0))],
        out_specs=blk,
        out_shape=jax.ShapeDtypeStruct((b, s, ATTN_WIDTH), BF16),
        compiler_params=_params("arbitrary", "arbitrary"),
        name="natten",
    )(q, k, v, bias)


def _attn_bias_table(rpb, kh):
    c = jnp.arange(GRID_W)
    cs = jnp.clip(c - WIN_COLS // 2, 0, GRID_W - WIN_COLS)
    j = jnp.arange(GRID_W)
    valid = (j[None, :] >= cs[:, None]) & (j[None, :] < cs[:, None] + WIN_COLS)
    dc = j[None, :] - c[:, None] + (WIN_COLS - 1)
    pick = ((dc[None] == jnp.arange(2 * WIN_COLS - 1)[:, None, None]) & valid[None]).astype(F32)
    t = jnp.einsum('hrd,dcj->hrcj', rpb.astype(F32), pick, precision=lax.Precision.HIGHEST)
    t = jnp.where(valid, t, MASK_BIAS)
    tbl = jnp.stack([t[:, d0:d0 + kh] for d0 in range(WIN_ROWS_MAX)], axis=1)
    return tbl.transpose(0, 1, 3, 2, 4).reshape(rpb.shape[0], WIN_ROWS_MAX, GRID_W, kh * GRID_W)


def _mix_kernel(x_ref, u_ref, up_ref, un_ref, ya_ref, wp_ref, ps_ref, wo_ref, g_ref, wr_ref,
                x1_ref, h_ref, aff_ref, upad_ref, wobf_ref, wpbf_ref, *, seq, tm):
    i = pl.program_id(1)

    @pl.when((pl.program_id(0) == 0) & (i == 0))
    def _():
        wobf_ref[...] = wo_ref[...].astype(BF16)
        wpbf_ref[...] = wp_ref[...].astype(BF16)

    upad_ref[0:POOL_HALO, :] = jnp.where(i > 0, up_ref[0], 0.0)
    upad_ref[POOL_HALO:POOL_HALO + tm, :] = u_ref[0]
    upad_ref[POOL_HALO + tm:, :] = jnp.where(i < pl.num_programs(1) - 1, un_ref[0], 0.0)

    t = i * tm + lax.broadcasted_iota(jnp.int32, (tm, 1), 0)
    ypool = []
    for gi, w in enumerate(POOL_WINDOWS):
        cols = slice(gi * POOL_GROUP, (gi + 1) * POOL_GROUP)
        acc = None
        for off in range(-(w // 2), w - w // 2):
            piece = upad_ref[POOL_HALO + off:POOL_HALO + off + tm, cols]
            acc = piece if acc is None else acc + piece
        lo = jnp.maximum(t - w // 2, 0)
        hi = jnp.minimum(t + (w - w // 2) - 1, seq - 1)
        cnt = (hi - lo + 1).astype(F32)
        d = acc / cnt - upad_ref[POOL_HALO:POOL_HALO + tm, cols]
        ypool.append(_dot(d.astype(BF16), wpbf_ref[gi]) * ps_ref[:, cols])
    ypool = jnp.concatenate(ypool, axis=-1).astype(BF16)

    mix = _dot(ypool, wobf_ref[:POOL_WIDTH, :]) + _dot(ya_ref[0], wobf_ref[POOL_WIDTH:, :])
    x1 = x_ref[0] + mix
    x1_ref[0] = x1
    h = _rms(x1, g_ref[...]).astype(BF16)
    h_ref[0] = h
    logits = _dot_nt(wr_ref[...].astype(BF16), h)
    m = jnp.max(logits, axis=0, keepdims=True)
    e = jnp.exp(logits - m)
    aff_ref[0] = e / jnp.sum(e, axis=0, keepdims=True)


def _mix(x, u, ya, wp, ps, wo, g, wr_t, tm=512):
    b, s, _ = x.shape
    nt = s // tm
    hb = tm // POOL_HALO
    tile = lambda w: pl.BlockSpec((1, tm, w), lambda bi, i: (bi, i, 0))
    fixed2 = lambda shape: pl.BlockSpec(shape, lambda bi, i: (0, 0))
    return pl.pallas_call(
        functools.partial(_mix_kernel, seq=s, tm=tm),
        grid=(b, nt),
        in_specs=[
            tile(D_MODEL),
            tile(POOL_WIDTH),
            pl.BlockSpec((1, POOL_HALO, POOL_WIDTH), lambda bi, i: (bi, jnp.maximum(i * hb - 1, 0), 0)),
            pl.BlockSpec((1, POOL_HALO, POOL_WIDTH),
                         lambda bi, i: (bi, jnp.minimum((i + 1) * hb, s // POOL_HALO - 1), 0)),
            tile(ATTN_WIDTH),
            pl.BlockSpec(wp.shape, lambda bi, i: (0, 0, 0)),
            fixed2((1, POOL_WIDTH)),
            fixed2(wo.shape),
            fixed2((1, D_MODEL)),
            fixed2(wr_t.shape),
        ],
        out_specs=[
            tile(D_MODEL),
            tile(D_MODEL),
            pl.BlockSpec((1, N_EXPERTS, tm), lambda bi, i: (bi, 0, i)),
        ],
        out_shape=[
            jax.ShapeDtypeStruct((b, s, D_MODEL), F32),
            jax.ShapeDtypeStruct((b, s, D_MODEL), BF16),
            jax.ShapeDtypeStruct((b, N_EXPERTS, s), F32),
        ],
        scratch_shapes=[
            pltpu.VMEM((tm + 2 * POOL_HALO, POOL_WIDTH), F32),
            pltpu.VMEM(wo.shape, BF16),
            pltpu.VMEM(wp.shape, BF16),
        ],
        compiler_params=_params("arbitrary", "arbitrary"),
        name="mix",
    )(x, u, u, u, ya, wp, ps, wo, g, wr_t)


def _lane_cumsum_exclusive(m):
    e, s = m.shape
    r = lax.broadcasted_iota(jnp.int32, (LANES, LANES), 0)
    c = lax.broadcasted_iota(jnp.int32, (LANES, LANES), 1)
    upper = (r < c).astype(BF16)
    carry = jnp.zeros((e, 1), F32)
    out = []
    for blk in range(s // LANES):
        piece = m[:, blk * LANES:(blk + 1) * LANES]
        out.append(_dot(piece.astype(BF16), upper) + carry)
        carry = carry + jnp.sum(piece, axis=-1, keepdims=True)
    return jnp.concatenate(out, axis=-1)


def _route_kernel(aff_ref, slot_ref, slot_t_ref, *, cap):
    aff = aff_ref[0]
    bits = pltpu.bitcast(aff, jnp.int32)
    capf = jnp.float32(cap)

    def count_ge(cand):
        return jnp.sum((bits >= cand).astype(F32), axis=-1, keepdims=True)

    def search(step, ans):
        cand = ans | (jnp.int32(1) << (30 - step))
        return jnp.where(count_ge(cand) >= capf, cand, ans)

    thr = lax.fori_loop(0, 31, search, jnp.zeros((aff.shape[0], 1), jnp.int32))
    gt = bits > thr
    eq = bits == thr
    need = capf - jnp.sum(gt.astype(F32), axis=-1, keepdims=True)
    sel = gt | (eq & (_lane_cumsum_exclusive(eq.astype(F32)) < need))
    slot = _lane_cumsum_exclusive(sel.astype(F32))
    slot = jnp.where(sel, slot, -1.0)
    slot_ref[0] = slot.astype(jnp.int32)
    pad = jnp.full((LANES - slot.shape[0], slot.shape[1]), -1.0, F32)
    slot_t_ref[0] = jnp.concatenate([slot, pad], axis=0).T.astype(jnp.int32)


def _route(aff_t, cap):
    b, e, s = aff_t.shape
    return pl.pallas_call(
        functools.partial(_route_kernel, cap=cap),
        grid=(b,),
        in_specs=[pl.BlockSpec((1, e, s), lambda bi: (bi, 0, 0))],
        out_specs=[
            pl.BlockSpec((1, e, s), lambda bi: (bi, 0, 0)),
            pl.BlockSpec((1, s, LANES), lambda bi: (bi, 0, 0)),
        ],
        out_shape=[
            jax.ShapeDtypeStruct((b, e, s), jnp.int32),
            jax.ShapeDtypeStruct((b, s, LANES), jnp.int32),
        ],
        compiler_params=_params("arbitrary"),
        name="route",
    )(aff_t)


def _dispatch_kernel(slot_ref, aff_ref, h_ref, xe_ref, gate_ref, *, cap):
    slot = slot_ref[0, 0]
    s = slot.shape[-1]
    onehot = lax.broadcasted_iota(jnp.int32, (cap, s), 0) == slot
    p = jnp.where(onehot, 1.0, 0.0).astype(BF16)
    xe_ref[0] = _dot(p, h_ref[0]).astype(BF16)
    gate_ref[0] = jnp.sum(jnp.where(onehot, aff_ref[0, 0], 0.0), axis=-1, keepdims=True)


def _dispatch(slot, aff_t, h, cap):
    b, e, s = slot.shape
    slot4 = slot.reshape(b, e, 1, s)
    aff4 = aff_t.reshape(b, e, 1, s)
    row = pl.BlockSpec((1, 1, 1, s), lambda bi, ei: (bi, ei, 0, 0))
    return pl.pallas_call(
        functools.partial(_dispatch_kernel, cap=cap),
        grid=(b, e),
        in_specs=[row, row, pl.BlockSpec((1, s, D_MODEL), lambda bi, ei: (bi, 0, 0))],
        out_specs=[
            pl.BlockSpec((1, cap, D_MODEL), lambda bi, ei: (ei, bi, 0)),
            pl.BlockSpec((1, cap, 1), lambda bi, ei: (ei, bi, 0)),
        ],
        out_shape=[
            jax.ShapeDtypeStruct((e, b * cap, D_MODEL), BF16),
            jax.ShapeDtypeStruct((e, b * cap, 1), F32),
        ],
        compiler_params=_params("arbitrary", "arbitrary"),
        name="dispatch",
    )(slot4, aff4, h)


def _experts_kernel(xe_ref, gate_ref, wg_ref, wu_ref, wd_ref, y_ref, acc_ref):
    f = pl.program_id(1)
    wg = wg_ref[0].astype(BF16)
    wu = wu_ref[0].astype(BF16)
    wd = wd_ref[0].astype(BF16)

    @pl.when(f == 0)
    def _():
        acc_ref[...] = jnp.zeros_like(acc_ref)

    for mb in range(xe_ref.shape[1] // EXPERT_ROWS):
        rows = slice(mb * EXPERT_ROWS, (mb + 1) * EXPERT_ROWS)
        xe = xe_ref[0, rows, :]
        a = _dot(xe, wg)
        b = _dot(xe, wu)
        hmid = (a * jax.nn.sigmoid(a) * b).astype(BF16)
        acc_ref[rows, :] += _dot(hmid, wd)

    @pl.when(f == pl.num_programs(1) - 1)
    def _():
        y_ref[0] = (acc_ref[...] * gate_ref[0]).astype(y_ref.dtype)


def _experts(xe, gate, wg, wu, wd, tf=512):
    e, m, _ = xe.shape
    nf = D_EXPERT // tf
    return pl.pallas_call(
        _experts_kernel,
        grid=(e, nf),
        in_specs=[
            pl.BlockSpec((1, m, D_MODEL), lambda ei, f: (ei, 0, 0)),
            pl.BlockSpec((1, m, 1), lambda ei, f: (ei, 0, 0)),
            pl.BlockSpec((1, D_MODEL, tf), lambda ei, f: (ei, 0, f)),
            pl.BlockSpec((1, D_MODEL, tf), lambda ei, f: (ei, 0, f)),
            pl.BlockSpec((1, tf, D_MODEL), lambda ei, f: (ei, f, 0)),
        ],
        out_specs=pl.BlockSpec((1, m, D_MODEL), lambda ei, f: (ei, 0, 0)),
        out_shape=jax.ShapeDtypeStruct((e, m, D_MODEL), BF16),
        scratch_shapes=[pltpu.VMEM((m, D_MODEL), F32)],
        compiler_params=_params("arbitrary", "arbitrary"),
        name="experts",
    )(xe, gate, wg, wu, wd)


def _combine_kernel(x1_ref, st_ref, y_ref, p_ref, gn_ref, wg_ref, wp_ref, gp_ref,
                    o_ref, wgbf_ref, wpbf_ref, *, cap):
    @pl.when((pl.program_id(0) == 0) & (pl.program_id(1) == 0))
    def _():
        wgbf_ref[...] = wg_ref[...].astype(BF16)
        wpbf_ref[...] = wp_ref[...].astype(BF16)

    tm = x1_ref.shape[1]
    st = st_ref[0]
    lane = lax.broadcasted_iota(jnp.int32, (tm, cap), 1)
    x2 = x1_ref[0]
    for e in range(N_EXPERTS):
        onehot = jnp.where(lane == st[:, e:e + 1], 1.0, 0.0).astype(BF16)
        x2 = x2 + _dot(onehot, y_ref[e])
    g = jax.nn.sigmoid(_dot(_rms(x2, gn_ref[...]).astype(BF16), wgbf_ref[...]))
    emb = _rms(_dot(p_ref[0].astype(BF16), wpbf_ref[...]), gp_ref[...])
    o_ref[0] = x2 + g * emb


def _combine(x1, slot_t, y, p, gn, wg, wp, gp, cap, tm=512):
    b, s, _ = x1.shape
    tile = lambda w: pl.BlockSpec((1, tm, w), lambda bi, i: (bi, i, 0))
    fixed2 = lambda shape: pl.BlockSpec(shape, lambda bi, i: (0, 0))
    return pl.pallas_call(
        functools.partial(_combine_kernel, cap=cap),
        grid=(b, s // tm),
        in_specs=[
            tile(D_MODEL),
            tile(LANES),
            pl.BlockSpec((N_EXPERTS, cap, D_MODEL), lambda bi, i: (0, bi, 0)),
            tile(PLE_DIM),
            fixed2((1, D_MODEL)),
            fixed2(wg.shape),
            fixed2(wp.shape),
            fixed2((1, D_MODEL)),
        ],
        out_specs=tile(D_MODEL),
        out_shape=jax.ShapeDtypeStruct((b, s, D_MODEL), F32),
        scratch_shapes=[pltpu.VMEM(wg.shape, BF16), pltpu.VMEM(wp.shape, BF16)],
        compiler_params=_params("arbitrary", "arbitrary"),
        name="combine",
    )(x1, slot_t, y, p, gn, wg, wp, gp)


def kernel(x, p, norm_mix, w_in, w_pool, pool_scale, q_norm, k_norm, rpb, w_out, norm_ffn, w_router, w_gate, w_up, w_down, norm_ple, w_ple_gate, w_ple_proj, norm_ple_post):
    b, s, d = x.shape
    depth = w_in.shape[0]
    cap = EC_CAPACITY * s // N_EXPERTS
    kh = min(WIN_ROWS_MAX, s // GRID_W)
    head = jnp.arange(ATTN_WIDTH) // HEAD_DIM
    block_diag = (head[:, None] == head[None, :]).astype(BF16)
    row = lambda a: a.reshape(1, -1)
    for i in range(depth):
        u, q, k, v = _in_proj(x.reshape(b * s, d), row(norm_mix[i]), w_in[i], block_diag,
                              row(jnp.tile(q_norm[i], ATTN_HEADS)), row(jnp.tile(k_norm[i], ATTN_HEADS)))
        shp = lambda a: a.reshape(b, s, -1)
        y_attn = _natten(shp(q), shp(k), shp(v), _attn_bias_table(rpb[i], kh))
        x1, h, aff_t = _mix(x, shp(u), y_attn, w_pool[i], row(pool_scale[i]), w_out[i],
                            row(norm_ffn[i]), w_router[i].T)
        slot, slot_t = _route(aff_t, cap)
        xe, gate = _dispatch(slot, aff_t, h, cap)
        y = _experts(xe, gate, w_gate[i], w_up[i], w_down[i])
        x = _combine(x1, slot_t, y, p[i], row(norm_ple[i]), w_ple_gate[i], w_ple_proj[i],
                     row(norm_ple_post[i]), cap)
    return x
```

```python
import functools

import jax
import jax.numpy as jnp
from jax import lax
from jax.experimental import pallas as pl
from jax.experimental.pallas import tpu as pltpu

D_MODEL = 1024
GRID_W = 64
POOL_WINDOWS = (2, 4, 8, 16)
POOL_WIDTH = D_MODEL // 2
POOL_GROUP = POOL_WIDTH // len(POOL_WINDOWS)
ATTN_HEADS = 8
HEAD_DIM = (D_MODEL // 2) // ATTN_HEADS
ATTN_WIDTH = ATTN_HEADS * HEAD_DIM
WIN_ROWS_MAX = 8
WIN_COLS = 16
N_EXPERTS = 16
EC_CAPACITY = 2
D_EXPERT = 2 * D_MODEL
PLE_DIM = 256
RMS_EPS = 1e-6

LANES = 128
POOL_HALO = 8
MASK_BIAS = -1e30
NATTEN_UNROLL = 8
ROUTE_TILE = 256
ROUTE_WIN = 64
ROUTE_GROUP = 4
OFF_STRIDE = 16
EXPERT_ROWS = 512
VMEM_LIMIT = 56 * 1024 * 1024

BF16 = jnp.bfloat16
F32 = jnp.float32


def _params(*sem):
    return pltpu.CompilerParams(dimension_semantics=sem, vmem_limit_bytes=VMEM_LIMIT)


def _rms(x, g):
    return x * lax.rsqrt(jnp.mean(x * x, axis=-1, keepdims=True) + RMS_EPS) * g


def _dot(a, b):
    return jnp.dot(a, b, preferred_element_type=F32)


def _dot_nt(a, b):
    return lax.dot_general(a, b, (((1,), (1,)), ((), ())), preferred_element_type=F32)


def _in_proj_kernel(x_ref, g_ref, w_ref, bd_ref, qg_ref, kg_ref,
                    u_ref, q_ref, k_ref, v_ref, wbf_ref):
    @pl.when(pl.program_id(0) == 0)
    def _():
        wbf_ref[...] = w_ref[...].astype(BF16)

    h = _rms(x_ref[...], g_ref[...]).astype(BF16)
    z = _dot(h, wbf_ref[...])
    u_ref[...] = z[:, :POOL_WIDTH]
    bd = bd_ref[...]

    def head_norm(t, g):
        ms = _dot((t * t).astype(BF16), bd) * (1.0 / HEAD_DIM)
        return t * lax.rsqrt(ms + RMS_EPS) * g

    q = z[:, POOL_WIDTH:POOL_WIDTH + ATTN_WIDTH]
    k = z[:, POOL_WIDTH + ATTN_WIDTH:POOL_WIDTH + 2 * ATTN_WIDTH]
    v = z[:, POOL_WIDTH + 2 * ATTN_WIDTH:]
    q_ref[...] = (head_norm(q, qg_ref[...]) * (HEAD_DIM ** -0.5)).astype(BF16)
    k_ref[...] = head_norm(k, kg_ref[...]).astype(BF16)
    v_ref[...] = v.astype(BF16)


def _in_proj(x2, g, w, bd, qg, kg, tm=512):
    n = x2.shape[0]
    zw = w.shape[1]
    row = lambda i: (i, 0)
    fixed = lambda i: (0, 0)
    return pl.pallas_call(
        _in_proj_kernel,
        grid=(n // tm,),
        in_specs=[
            pl.BlockSpec((tm, D_MODEL), row),
            pl.BlockSpec((1, D_MODEL), fixed),
            pl.BlockSpec((D_MODEL, zw), fixed),
            pl.BlockSpec((ATTN_WIDTH, ATTN_WIDTH), fixed),
            pl.BlockSpec((1, ATTN_WIDTH), fixed),
            pl.BlockSpec((1, ATTN_WIDTH), fixed),
        ],
        out_specs=[
            pl.BlockSpec((tm, POOL_WIDTH), row),
            pl.BlockSpec((tm, ATTN_WIDTH), row),
            pl.BlockSpec((tm, ATTN_WIDTH), row),
            pl.BlockSpec((tm, ATTN_WIDTH), row),
        ],
        out_shape=[
            jax.ShapeDtypeStruct((n, POOL_WIDTH), F32),
            jax.ShapeDtypeStruct((n, ATTN_WIDTH), BF16),
            jax.ShapeDtypeStruct((n, ATTN_WIDTH), BF16),
            jax.ShapeDtypeStruct((n, ATTN_WIDTH), BF16),
        ],
        scratch_shapes=[pltpu.VMEM((D_MODEL, zw), BF16)],
        compiler_params=_params("arbitrary"),
        name="in_proj",
    )(x2, g, w, bd, qg, kg)


def _natten_kernel(q_ref, k_ref, v_ref, bias_ref, o_ref, *, rows, kh):
    band = kh * GRID_W
    lane = lax.broadcasted_iota(jnp.int32, (GRID_W, LANES), 1)
    first = lane < HEAD_DIM

    def row_group(g, carry):
        geo = []
        for sub in range(NATTEN_UNROLL):
            r = g * NATTEN_UNROLL + sub
            rs = jnp.clip(r - kh // 2, 0, rows - kh)
            geo.append((rs - r + (WIN_ROWS_MAX - 1),
                        pl.multiple_of(r * GRID_W, GRID_W),
                        pl.multiple_of(rs * GRID_W, GRID_W)))
        scores = []
        for d0, q0, k0 in geo:
            qr = q_ref[0, pl.ds(q0, GRID_W), :]
            kb = k_ref[0, pl.ds(k0, band), :]
            for hh in range(2):
                sel = first if hh == 0 else jnp.logical_not(first)
                qm = jnp.where(sel, qr, jnp.zeros_like(qr))
                scores.append(_dot_nt(qm, kb) + bias_ref[hh, d0])
        probs = []
        for s in scores:
            e = jnp.exp(s - jnp.max(s, axis=-1, keepdims=True))
            probs.append((e.astype(BF16), jnp.sum(e, axis=-1, keepdims=True)))
        for idx, (d0, q0, k0) in enumerate(geo):
            vb = v_ref[0, pl.ds(k0, band), :]
            outs = [_dot(e, vb) / l for e, l in probs[2 * idx:2 * idx + 2]]
            o_ref[0, pl.ds(q0, GRID_W), :] = jnp.where(first, outs[0], outs[1]).astype(o_ref.dtype)
        return carry

    lax.fori_loop(0, rows // NATTEN_UNROLL, row_group, 0)


def _natten(q, k, v, bias):
    b, s, _ = q.shape
    rows = s // GRID_W
    kh = min(WIN_ROWS_MAX, rows)
    assert rows % NATTEN_UNROLL == 0
    pairs = ATTN_WIDTH // LANES
    blk = pl.BlockSpec((1, s, LANES), lambda bi, p: (bi, 0, p))
    return pl.pallas_call(
        functools.partial(_natten_kernel, rows=rows, kh=kh),
        grid=(b, pairs),
        in_specs=[blk, blk, blk,
                  pl.BlockSpec((2, WIN_ROWS_MAX, GRID_W, kh * GRID_W), lambda bi, p: (p, 0, 0, 0))],
        out_specs=blk,
        out_shape=jax.ShapeDtypeStruct((b, s, ATTN_WIDTH), BF16),
        compiler_params=_params("arbitrary", "arbitrary"),
        name="natten",
    )(q, k, v, bias)


def _attn_bias_table(rpb, kh):
    c = jnp.arange(GRID_W)
    cs = jnp.clip(c - WIN_COLS // 2, 0, GRID_W - WIN_COLS)
    j = jnp.arange(GRID_W)
    valid = (j[None, :] >= cs[:, None]) & (j[None, :] < cs[:, None] + WIN_COLS)
    dc = j[None, :] - c[:, None] + (WIN_COLS - 1)
    pick = ((dc[None] == jnp.arange(2 * WIN_COLS - 1)[:, None, None]) & valid[None]).astype(F32)
    t = jnp.einsum('hrd,dcj->hrcj', rpb.astype(F32), pick, precision=lax.Precision.HIGHEST)
    t = jnp.where(valid, t, MASK_BIAS)
    tbl = jnp.stack([t[:, d0:d0 + kh] for d0 in range(WIN_ROWS_MAX)], axis=1)
    return tbl.transpose(0, 1, 3, 2, 4).reshape(rpb.shape[0], WIN_ROWS_MAX, GRID_W, kh * GRID_W)


def _mix_kernel(x_ref, u_ref, up_ref, un_ref, ya_ref, wp_ref, ps_ref, wo_ref, g_ref, wr_ref,
                x1_ref, h_ref, aff_ref, upad_ref, wobf_ref, wpbf_ref, *, seq, tm):
    i = pl.program_id(1)

    @pl.when((pl.program_id(0) == 0) & (i == 0))
    def _():
        wobf_ref[...] = wo_ref[...].astype(BF16)
        wpbf_ref[...] = wp_ref[...].astype(BF16)

    upad_ref[0:POOL_HALO, :] = jnp.where(i > 0, up_ref[0], 0.0)
    upad_ref[POOL_HALO:POOL_HALO + tm, :] = u_ref[0]
    upad_ref[POOL_HALO + tm:, :] = jnp.where(i < pl.num_programs(1) - 1, un_ref[0], 0.0)

    t = i * tm + lax.broadcasted_iota(jnp.int32, (tm, 1), 0)
    ypool = []
    for gi, w in enumerate(POOL_WINDOWS):
        cols = slice(gi * POOL_GROUP, (gi + 1) * POOL_GROUP)
        acc = None
        for off in range(-(w // 2), w - w // 2):
            piece = upad_ref[POOL_HALO + off:POOL_HALO + off + tm, cols]
            acc = piece if acc is None else acc + piece
        lo = jnp.maximum(t - w // 2, 0)
        hi = jnp.minimum(t + (w - w // 2) - 1, seq - 1)
        cnt = (hi - lo + 1).astype(F32)
        d = acc / cnt - upad_ref[POOL_HALO:POOL_HALO + tm, cols]
        ypool.append(_dot(d.astype(BF16), wpbf_ref[gi]) * ps_ref[:, cols])
    ypool = jnp.concatenate(ypool, axis=-1).astype(BF16)

    mix = _dot(ypool, wobf_ref[:POOL_WIDTH, :]) + _dot(ya_ref[0], wobf_ref[POOL_WIDTH:, :])
    x1 = x_ref[0] + mix
    x1_ref[0] = x1
    h = _rms(x1, g_ref[...]).astype(BF16)
    h_ref[0] = h
    logits = _dot_nt(wr_ref[...].astype(BF16), h)
    m = jnp.max(logits, axis=0, keepdims=True)
    e = jnp.exp(logits - m)
    aff_ref[0] = e / jnp.sum(e, axis=0, keepdims=True)


def _mix(x, u, ya, wp, ps, wo, g, wr_t, tm=512):
    b, s, _ = x.shape
    nt = s // tm
    hb = tm // POOL_HALO
    tile = lambda w: pl.BlockSpec((1, tm, w), lambda bi, i: (bi, i, 0))
    fixed2 = lambda shape: pl.BlockSpec(shape, lambda bi, i: (0, 0))
    return pl.pallas_call(
        functools.partial(_mix_kernel, seq=s, tm=tm),
        grid=(b, nt),
        in_specs=[
            tile(D_MODEL),
            tile(POOL_WIDTH),
            pl.BlockSpec((1, POOL_HALO, POOL_WIDTH), lambda bi, i: (bi, jnp.maximum(i * hb - 1, 0), 0)),
            pl.BlockSpec((1, POOL_HALO, POOL_WIDTH),
                         lambda bi, i: (bi, jnp.minimum((i + 1) * hb, s // POOL_HALO - 1), 0)),
            tile(ATTN_WIDTH),
            pl.BlockSpec(wp.shape, lambda bi, i: (0, 0, 0)),
            fixed2((1, POOL_WIDTH)),
            fixed2(wo.shape),
            fixed2((1, D_MODEL)),
            fixed2(wr_t.shape),
        ],
        out_specs=[
            tile(D_MODEL),
            tile(D_MODEL),
            pl.BlockSpec((1, N_EXPERTS, tm), lambda bi, i: (bi, 0, i)),
        ],
        out_shape=[
            jax.ShapeDtypeStruct((b, s, D_MODEL), F32),
            jax.ShapeDtypeStruct((b, s, D_MODEL), BF16),
            jax.ShapeDtypeStruct((b, N_EXPERTS, s), F32),
        ],
        scratch_shapes=[
            pltpu.VMEM((tm + 2 * POOL_HALO, POOL_WIDTH), F32),
            pltpu.VMEM(wo.shape, BF16),
            pltpu.VMEM(wp.shape, BF16),
        ],
        compiler_params=_params("arbitrary", "arbitrary"),
        name="mix",
    )(x, u, u, u, ya, wp, ps, wo, g, wr_t)


def _lane_cumsum_exclusive(m):
    e, s = m.shape
    r = lax.broadcasted_iota(jnp.int32, (LANES, LANES), 0)
    c = lax.broadcasted_iota(jnp.int32, (LANES, LANES), 1)
    upper = (r < c).astype(BF16)
    carry = jnp.zeros((e, 1), F32)
    out = []
    for blk in range(s // LANES):
        piece = m[:, blk * LANES:(blk + 1) * LANES]
        out.append(_dot(piece.astype(BF16), upper) + carry)
        carry = carry + jnp.sum(piece, axis=-1, keepdims=True)
    return jnp.concatenate(out, axis=-1)


def _route_kernel(aff_ref, slot_ref, slot_t_ref, off_ref, *, cap):
    aff = aff_ref[0]
    bits = pltpu.bitcast(aff, jnp.int32)
    capf = jnp.float32(cap)

    def count_ge(cand):
        return jnp.sum((bits >= cand).astype(F32), axis=-1, keepdims=True)

    def search(step, ans):
        cand = ans | (jnp.int32(1) << (30 - step))
        return jnp.where(count_ge(cand) >= capf, cand, ans)

    thr = lax.fori_loop(0, 31, search, jnp.zeros((aff.shape[0], 1), jnp.int32))
    gt = bits > thr
    eq = bits == thr
    need = capf - jnp.sum(gt.astype(F32), axis=-1, keepdims=True)
    sel = gt | (eq & (_lane_cumsum_exclusive(eq.astype(F32)) < need))
    self = sel.astype(F32)
    slot = jnp.where(sel, _lane_cumsum_exclusive(self), -1.0)
    slot_ref[0] = slot.astype(jnp.int32)
    pad = jnp.full((LANES - slot.shape[0], slot.shape[1]), -1.0, F32)
    slot_t_ref[0] = jnp.concatenate([slot, pad], axis=0).T.astype(jnp.int32)
    lane = lax.broadcasted_iota(jnp.int32, (slot.shape[0], LANES), 1)
    off = jnp.zeros((slot.shape[0], LANES), F32)
    run = jnp.zeros((slot.shape[0], 1), F32)
    for i in range(1, slot.shape[1] // ROUTE_TILE + 1):
        run = run + jnp.sum(self[:, (i - 1) * ROUTE_TILE:i * ROUTE_TILE], axis=-1, keepdims=True)
        off = jnp.where(lane == i, run, off)
    off_ref[0] = off.astype(jnp.int32)


def _route(aff_t, cap):
    b, e, s = aff_t.shape
    assert s % ROUTE_TILE == 0 and s // ROUTE_TILE < OFF_STRIDE
    return pl.pallas_call(
        functools.partial(_route_kernel, cap=cap),
        grid=(b,),
        in_specs=[pl.BlockSpec((1, e, s), lambda bi: (bi, 0, 0))],
        out_specs=[
            pl.BlockSpec((1, e, s), lambda bi: (bi, 0, 0)),
            pl.BlockSpec((1, s, LANES), lambda bi: (bi, 0, 0)),
            pl.BlockSpec((1, e, LANES), lambda bi: (bi, 0, 0)),
        ],
        out_shape=[
            jax.ShapeDtypeStruct((b, e, s), jnp.int32),
            jax.ShapeDtypeStruct((b, s, LANES), jnp.int32),
            jax.ShapeDtypeStruct((b, e, LANES), jnp.int32),
        ],
        compiler_params=_params("arbitrary"),
        name="route",
    )(aff_t)


def _window_starts(off_ref, bi, ti, cap):
    starts = []
    ok = None
    for e in range(N_EXPERTS):
        base = (bi * N_EXPERTS + e) * OFF_STRIDE + ti
        lo = off_ref[base]
        hi = off_ref[base + 1]
        st = jnp.minimum(jnp.bitwise_and(lo, -16), cap - ROUTE_WIN)
        fits = hi <= st + ROUTE_WIN
        ok = fits if ok is None else jnp.logical_and(ok, fits)
        starts.append(pl.multiple_of(st, 16))
    return starts, ok


def _dispatch_kernel(off_ref, slot_ref, aff_ref, h_ref, xe_ref, gate_ref, acc_ref, gacc_ref, *, cap):
    bi = pl.program_id(0)
    ti = pl.program_id(1)

    @pl.when(ti == 0)
    def _():
        acc_ref[...] = jnp.zeros_like(acc_ref)
        gacc_ref[...] = jnp.zeros_like(gacc_ref)

    slot = slot_ref[0]
    aff = aff_ref[0]
    h = h_ref[0]
    tile = slot.shape[-1]
    starts, ok = _window_starts(off_ref, bi, ti, cap)

    @pl.when(ok)
    def _():
        row = lax.broadcasted_iota(jnp.int32, (ROUTE_WIN, tile), 0)
        for g in range(N_EXPERTS // ROUTE_GROUP):
            hits = [row == (slot[e:e + 1, :] - starts[e])
                    for e in range(g * ROUTE_GROUP, (g + 1) * ROUTE_GROUP)]
            onehot = jnp.concatenate([jnp.where(hh, 1.0, 0.0).astype(BF16) for hh in hits], axis=0)
            res = _dot(onehot, h)
            for k, hh in enumerate(hits):
                e = g * ROUTE_GROUP + k
                win = pl.ds(starts[e], ROUTE_WIN)
                acc_ref[e, win, :] += res[k * ROUTE_WIN:(k + 1) * ROUTE_WIN]
                gacc_ref[e, win, :] += jnp.sum(jnp.where(hh, aff[e:e + 1, :], 0.0), axis=-1, keepdims=True)

    @pl.when(jnp.logical_not(ok))
    def _():
        row = lax.broadcasted_iota(jnp.int32, (cap, tile), 0)
        for e in range(N_EXPERTS):
            hh = row == slot[e:e + 1, :]
            acc_ref[e] += _dot(jnp.where(hh, 1.0, 0.0).astype(BF16), h)
            gacc_ref[e] += jnp.sum(jnp.where(hh, aff[e:e + 1, :], 0.0), axis=-1, keepdims=True)

    @pl.when(ti == pl.num_programs(1) - 1)
    def _():
        xe_ref[...] = acc_ref[...].astype(BF16)
        gate_ref[...] = gacc_ref[...]


def _dispatch(off, slot, aff_t, h, cap):
    b, e, s = slot.shape
    rows = pl.BlockSpec((1, e, ROUTE_TILE), lambda bi, ti, off_ref: (bi, 0, ti))
    return pl.pallas_call(
        functools.partial(_dispatch_kernel, cap=cap),
        grid_spec=pltpu.PrefetchScalarGridSpec(
            num_scalar_prefetch=1,
            grid=(b, s // ROUTE_TILE),
            in_specs=[rows, rows,
                      pl.BlockSpec((1, ROUTE_TILE, D_MODEL), lambda bi, ti, off_ref: (bi, ti, 0))],
            out_specs=[
                pl.BlockSpec((e, cap, D_MODEL), lambda bi, ti, off_ref: (0, bi, 0)),
                pl.BlockSpec((e, cap, 1), lambda bi, ti, off_ref: (0, bi, 0)),
            ],
            scratch_shapes=[pltpu.VMEM((e, cap, D_MODEL), F32), pltpu.VMEM((e, cap, 1), F32)],
        ),
        out_shape=[
            jax.ShapeDtypeStruct((e, b * cap, D_MODEL), BF16),
            jax.ShapeDtypeStruct((e, b * cap, 1), F32),
        ],
        compiler_params=_params("arbitrary", "arbitrary"),
        name="dispatch",
    )(off, slot, aff_t, h)


def _experts_kernel(xe_ref, gate_ref, wg_ref, wu_ref, wd_ref, y_ref, acc_ref):
    f = pl.program_id(1)
    wg = wg_ref[0].astype(BF16)
    wu = wu_ref[0].astype(BF16)
    wd = wd_ref[0].astype(BF16)

    @pl.when(f == 0)
    def _():
        acc_ref[...] = jnp.zeros_like(acc_ref)

    for mb in range(xe_ref.shape[1] // EXPERT_ROWS):
        rows = slice(mb * EXPERT_ROWS, (mb + 1) * EXPERT_ROWS)
        xe = xe_ref[0, rows, :]
        a = _dot(xe, wg)
        b = _dot(xe, wu)
        hmid = (a * jax.nn.sigmoid(a) * b).astype(BF16)
        acc_ref[rows, :] += _dot(hmid, wd)

    @pl.when(f == pl.num_programs(1) - 1)
    def _():
        y_ref[0] = (acc_ref[...] * gate_ref[0]).astype(y_ref.dtype)


def _experts(xe, gate, wg, wu, wd, tf=512):
    e, m, _ = xe.shape
    nf = D_EXPERT // tf
    return pl.pallas_call(
        _experts_kernel,
        grid=(e, nf),
        in_specs=[
            pl.BlockSpec((1, m, D_MODEL), lambda ei, f: (ei, 0, 0)),
            pl.BlockSpec((1, m, 1), lambda ei, f: (ei, 0, 0)),
            pl.BlockSpec((1, D_MODEL, tf), lambda ei, f: (ei, 0, f)),
            pl.BlockSpec((1, D_MODEL, tf), lambda ei, f: (ei, 0, f)),
            pl.BlockSpec((1, tf, D_MODEL), lambda ei, f: (ei, f, 0)),
        ],
        out_specs=pl.BlockSpec((1, m, D_MODEL), lambda ei, f: (ei, 0, 0)),
        out_shape=jax.ShapeDtypeStruct((e, m, D_MODEL), BF16),
        scratch_shapes=[pltpu.VMEM((m, D_MODEL), F32)],
        compiler_params=_params("arbitrary", "arbitrary"),
        name="experts",
    )(xe, gate, wg, wu, wd)


def _combine_kernel(off_ref, x1_ref, st_ref, y_ref, p_ref, gn_ref, wg_ref, wp_ref, gp_ref,
                    o_ref, wgbf_ref, wpbf_ref, ffn_ref, *, cap):
    bi = pl.program_id(0)

    @pl.when((bi == 0) & (pl.program_id(1) == 0))
    def _():
        wgbf_ref[...] = wg_ref[...].astype(BF16)
        wpbf_ref[...] = wp_ref[...].astype(BF16)

    tm = x1_ref.shape[1]
    for sub in range(tm // ROUTE_TILE):
        rows = slice(sub * ROUTE_TILE, (sub + 1) * ROUTE_TILE)
        st = st_ref[0, rows, :]
        starts, ok = _window_starts(off_ref, bi, pl.program_id(1) * (tm // ROUTE_TILE) + sub, cap)

        @pl.when(ok)
        def _():
            lane = lax.broadcasted_iota(jnp.int32, (ROUTE_TILE, ROUTE_GROUP * ROUTE_WIN), 1)
            total = None
            for g in range(N_EXPERTS // ROUTE_GROUP):
                hit = None
                wins = []
                for k in range(ROUTE_GROUP):
                    e = g * ROUTE_GROUP + k
                    col = st[:, e:e + 1]
                    tgt = jnp.where(col >= 0, col - starts[e] + k * ROUTE_WIN, -1)
                    hk = lane == tgt
                    hit = hk if hit is None else jnp.logical_or(hit, hk)
                    wins.append(y_ref[e, pl.ds(starts[e], ROUTE_WIN), :])
                part = _dot(jnp.where(hit, 1.0, 0.0).astype(BF16), jnp.concatenate(wins, axis=0))
                total = part if total is None else total + part
            ffn_ref[rows, :] = total

        @pl.when(jnp.logical_not(ok))
        def _():
            lane = lax.broadcasted_iota(jnp.int32, (ROUTE_TILE, cap), 1)
            total = None
            for e in range(N_EXPERTS):
                onehot = jnp.where(lane == st[:, e:e + 1], 1.0, 0.0).astype(BF16)
                part = _dot(onehot, y_ref[e])
                total = part if total is None else total + part
            ffn_ref[rows, :] = total

    x2 = x1_ref[0] + ffn_ref[...]
    g = jax.nn.sigmoid(_dot(_rms(x2, gn_ref[...]).astype(BF16), wgbf_ref[...]))
    emb = _rms(_dot(p_ref[0].astype(BF16), wpbf_ref[...]), gp_ref[...])
    o_ref[0] = x2 + g * emb


def _combine(off, x1, slot_t, y, p, gn, wg, wp, gp, cap, tm=512):
    b, s, _ = x1.shape
    assert tm % ROUTE_TILE == 0
    tile = lambda w: pl.BlockSpec((1, tm, w), lambda bi, i, off_ref: (bi, i, 0))
    fixed2 = lambda shape: pl.BlockSpec(shape, lambda bi, i, off_ref: (0, 0))
    return pl.pallas_call(
        functools.partial(_combine_kernel, cap=cap),
        grid_spec=pltpu.PrefetchScalarGridSpec(
            num_scalar_prefetch=1,
            grid=(b, s // tm),
            in_specs=[
                tile(D_MODEL),
                tile(LANES),
                pl.BlockSpec((N_EXPERTS, cap, D_MODEL), lambda bi, i, off_ref: (0, bi, 0)),
                tile(PLE_DIM),
                fixed2((1, D_MODEL)),
                fixed2(wg.shape),
                fixed2(wp.shape),
                fixed2((1, D_MODEL)),
            ],
            out_specs=tile(D_MODEL),
            scratch_shapes=[pltpu.VMEM(wg.shape, BF16), pltpu.VMEM(wp.shape, BF16),
                            pltpu.VMEM((tm, D_MODEL), F32)],
        ),
        out_shape=jax.ShapeDtypeStruct((b, s, D_MODEL), F32),
        compiler_params=_params("arbitrary", "arbitrary"),
        name="combine",
    )(off, x1, slot_t, y, p, gn, wg, wp, gp)


def kernel(x, p, norm_mix, w_in, w_pool, pool_scale, q_norm, k_norm, rpb, w_out, norm_ffn, w_router, w_gate, w_up, w_down, norm_ple, w_ple_gate, w_ple_proj, norm_ple_post):
    b, s, d = x.shape
    depth = w_in.shape[0]
    cap = EC_CAPACITY * s // N_EXPERTS
    kh = min(WIN_ROWS_MAX, s // GRID_W)
    head = jnp.arange(ATTN_WIDTH) // HEAD_DIM
    block_diag = (head[:, None] == head[None, :]).astype(BF16)
    row = lambda a: a.reshape(1, -1)
    for i in range(depth):
        u, q, k, v = _in_proj(x.reshape(b * s, d), row(norm_mix[i]), w_in[i], block_diag,
                              row(jnp.tile(q_norm[i], ATTN_HEADS)), row(jnp.tile(k_norm[i], ATTN_HEADS)))
        shp = lambda a: a.reshape(b, s, -1)
        y_attn = _natten(shp(q), shp(k), shp(v), _attn_bias_table(rpb[i], kh))
        x1, h, aff_t = _mix(x, shp(u), y_attn, w_pool[i], row(pool_scale[i]), w_out[i],
                            row(norm_ffn[i]), w_router[i].T)
        slot, slot_t, off = _route(aff_t, cap)
        off = off[:, :, :OFF_STRIDE].reshape(-1)
        xe, gate = _dispatch(off, slot, aff_t, h, cap)
        y = _experts(xe, gate, w_gate[i], w_up[i], w_down[i])
        x = _combine(off, x1, slot_t, y, p[i], row(norm_ple[i]), w_ple_gate[i], w_ple_proj[i],
                     row(norm_ple_post[i]), cap)
    return x
```

```python
import functools

import jax
import jax.numpy as jnp
from jax import lax
from jax.experimental import pallas as pl
from jax.experimental.pallas import tpu as pltpu

D_MODEL = 1024
GRID_W = 64
POOL_WINDOWS = (2, 4, 8, 16)
POOL_WIDTH = D_MODEL // 2
POOL_GROUP = POOL_WIDTH // len(POOL_WINDOWS)
ATTN_HEADS = 8
HEAD_DIM = (D_MODEL // 2) // ATTN_HEADS
ATTN_WIDTH = ATTN_HEADS * HEAD_DIM
WIN_ROWS_MAX = 8
WIN_COLS = 16
N_EXPERTS = 16
EC_CAPACITY = 2
D_EXPERT = 2 * D_MODEL
PLE_DIM = 256
RMS_EPS = 1e-6

LANES = 128
POOL_HALO = 8
MASK_BIAS = -1e30
NATTEN_UNROLL = 8
ROUTE_TILE = 256
ROUTE_WIN = 64
ROUTE_GROUP = 4
OFF_STRIDE = 16
NOT_SELECTED = -(1 << 20)
assert 2 * ROUTE_WIN == LANES and ROUTE_GROUP % 2 == 0
EXPERT_ROWS = 512
VMEM_LIMIT = 56 * 1024 * 1024

BF16 = jnp.bfloat16
F32 = jnp.float32


def _params(*sem):
    return pltpu.CompilerParams(dimension_semantics=sem, vmem_limit_bytes=VMEM_LIMIT)


def _rms(x, g):
    return x * lax.rsqrt(jnp.mean(x * x, axis=-1, keepdims=True) + RMS_EPS) * g


def _dot(a, b):
    return jnp.dot(a, b, preferred_element_type=F32)


def _dot_nt(a, b):
    return lax.dot_general(a, b, (((1,), (1,)), ((), ())), preferred_element_type=F32)


def _in_proj_kernel(x_ref, g_ref, w_ref, bd_ref, qg_ref, kg_ref,
                    u_ref, q_ref, k_ref, v_ref, wbf_ref):
    @pl.when(pl.program_id(0) == 0)
    def _():
        wbf_ref[...] = w_ref[...].astype(BF16)

    h = _rms(x_ref[...], g_ref[...]).astype(BF16)
    z = _dot(h, wbf_ref[...])
    u_ref[...] = z[:, :POOL_WIDTH]
    bd = bd_ref[...]

    def head_norm(t, g):
        ms = _dot((t * t).astype(BF16), bd) * (1.0 / HEAD_DIM)
        return t * lax.rsqrt(ms + RMS_EPS) * g

    q = z[:, POOL_WIDTH:POOL_WIDTH + ATTN_WIDTH]
    k = z[:, POOL_WIDTH + ATTN_WIDTH:POOL_WIDTH + 2 * ATTN_WIDTH]
    v = z[:, POOL_WIDTH + 2 * ATTN_WIDTH:]
    q_ref[...] = (head_norm(q, qg_ref[...]) * (HEAD_DIM ** -0.5)).astype(BF16)
    k_ref[...] = head_norm(k, kg_ref[...]).astype(BF16)
    v_ref[...] = v.astype(BF16)


def _in_proj(x2, g, w, bd, qg, kg, tm=512):
    n = x2.shape[0]
    zw = w.shape[1]
    row = lambda i: (i, 0)
    fixed = lambda i: (0, 0)
    return pl.pallas_call(
        _in_proj_kernel,
        grid=(n // tm,),
        in_specs=[
            pl.BlockSpec((tm, D_MODEL), row),
            pl.BlockSpec((1, D_MODEL), fixed),
            pl.BlockSpec((D_MODEL, zw), fixed),
            pl.BlockSpec((ATTN_WIDTH, ATTN_WIDTH), fixed),
            pl.BlockSpec((1, ATTN_WIDTH), fixed),
            pl.BlockSpec((1, ATTN_WIDTH), fixed),
        ],
        out_specs=[
            pl.BlockSpec((tm, POOL_WIDTH), row),
            pl.BlockSpec((tm, ATTN_WIDTH), row),
            pl.BlockSpec((tm, ATTN_WIDTH), row),
            pl.BlockSpec((tm, ATTN_WIDTH), row),
        ],
        out_shape=[
            jax.ShapeDtypeStruct((n, POOL_WIDTH), F32),
            jax.ShapeDtypeStruct((n, ATTN_WIDTH), BF16),
            jax.ShapeDtypeStruct((n, ATTN_WIDTH), BF16),
            jax.ShapeDtypeStruct((n, ATTN_WIDTH), BF16),
        ],
        scratch_shapes=[pltpu.VMEM((D_MODEL, zw), BF16)],
        compiler_params=_params("arbitrary"),
        name="in_proj",
    )(x2, g, w, bd, qg, kg)


def _natten_kernel(q_ref, k_ref, v_ref, bias_ref, o_ref, *, rows, kh):
    band = kh * GRID_W
    lane = lax.broadcasted_iota(jnp.int32, (GRID_W, LANES), 1)
    first = lane < HEAD_DIM

    def row_group(g, carry):
        geo = []
        for sub in range(NATTEN_UNROLL):
            r = g * NATTEN_UNROLL + sub
            rs = jnp.clip(r - kh // 2, 0, rows - kh)
            geo.append((rs - r + (WIN_ROWS_MAX - 1),
                        pl.multiple_of(r * GRID_W, GRID_W),
                        pl.multiple_of(rs * GRID_W, GRID_W)))
        scores = []
        for d0, q0, k0 in geo:
            qr = q_ref[0, pl.ds(q0, GRID_W), :]
            kb = k_ref[0, pl.ds(k0, band), :]
            for hh in range(2):
                sel = first if hh == 0 else jnp.logical_not(first)
                qm = jnp.where(sel, qr, jnp.zeros_like(qr))
                scores.append(_dot_nt(qm, kb) + bias_ref[hh, d0])
        probs = []
        for s in scores:
            e = jnp.exp(s - jnp.max(s, axis=-1, keepdims=True))
            probs.append((e.astype(BF16), jnp.sum(e, axis=-1, keepdims=True)))
        for idx, (d0, q0, k0) in enumerate(geo):
            vb = v_ref[0, pl.ds(k0, band), :]
            outs = [_dot(e, vb) / l for e, l in probs[2 * idx:2 * idx + 2]]
            o_ref[0, pl.ds(q0, GRID_W), :] = jnp.where(first, outs[0], outs[1]).astype(o_ref.dtype)
        return carry

    lax.fori_loop(0, rows // NATTEN_UNROLL, row_group, 0)


def _natten(q, k, v, bias):
    b, s, _ = q.shape
    rows = s // GRID_W
    kh = min(WIN_ROWS_MAX, rows)
    assert rows % NATTEN_UNROLL == 0
    pairs = ATTN_WIDTH // LANES
    blk = pl.BlockSpec((1, s, LANES), lambda bi, p: (bi, 0, p))
    return pl.pallas_call(
        functools.partial(_natten_kernel, rows=rows, kh=kh),
        grid=(b, pairs),
        in_specs=[blk, blk, blk,
                  pl.BlockSpec((2, WIN_ROWS_MAX, GRID_W, kh * GRID_W), lambda bi, p: (p, 0, 0, 0))],
        out_specs=blk,
        out_shape=jax.ShapeDtypeStruct((b, s, ATTN_WIDTH), BF16),
        compiler_params=_params("arbitrary", "arbitrary"),
        name="natten",
    )(q, k, v, bias)


def _attn_bias_table(rpb, kh):
    c = jnp.arange(GRID_W)
    cs = jnp.clip(c - WIN_COLS // 2, 0, GRID_W - WIN_COLS)
    j = jnp.arange(GRID_W)
    valid = (j[None, :] >= cs[:, None]) & (j[None, :] < cs[:, None] + WIN_COLS)
    dc = j[None, :] - c[:, None] + (WIN_COLS - 1)
    pick = ((dc[None] == jnp.arange(2 * WIN_COLS - 1)[:, None, None]) & valid[None]).astype(F32)
    t = jnp.einsum('hrd,dcj->hrcj', rpb.astype(F32), pick, precision=lax.Precision.HIGHEST)
    t = jnp.where(valid, t, MASK_BIAS)
    tbl = jnp.stack([t[:, d0:d0 + kh] for d0 in range(WIN_ROWS_MAX)], axis=1)
    return tbl.transpose(0, 1, 3, 2, 4).reshape(rpb.shape[0], WIN_ROWS_MAX, GRID_W, kh * GRID_W)


def _window_sum(upad, w, tm):
    n = upad.shape[0]
    fwd = upad
    span = 1
    while span < min(w, POOL_HALO):
        fwd = fwd + pltpu.roll(fwd, n - span, axis=0)
        span *= 2
    centre = slice(POOL_HALO, POOL_HALO + tm)
    if w == 2 * POOL_HALO:
        return fwd[0:tm] + fwd[centre]
    return pltpu.roll(fwd, w // 2, axis=0)[centre]


def _mix_kernel(x_ref, u_ref, up_ref, un_ref, ya_ref, ic_ref, wp_ref, ps_ref, wo_ref, g_ref, wr_ref,
                x1_ref, h_ref, aff_ref, upad_ref, wobf_ref, wpbf_ref, *, tm):
    i = pl.program_id(1)

    @pl.when((pl.program_id(0) == 0) & (i == 0))
    def _():
        wobf_ref[...] = wo_ref[...].astype(BF16)
        wpbf_ref[...] = wp_ref[...].astype(BF16)

    upad_ref[0:POOL_HALO, :] = jnp.where(i > 0, up_ref[0], 0.0)
    upad_ref[POOL_HALO:POOL_HALO + tm, :] = u_ref[0]
    upad_ref[POOL_HALO + tm:, :] = jnp.where(i < pl.num_programs(1) - 1, un_ref[0], 0.0)

    ypool = []
    for gi, w in enumerate(POOL_WINDOWS):
        cols = slice(gi * POOL_GROUP, (gi + 1) * POOL_GROUP)
        upad = upad_ref[:, cols]
        d = _window_sum(upad, w, tm) * ic_ref[:, cols] - upad[POOL_HALO:POOL_HALO + tm]
        ypool.append(_dot(d.astype(BF16), wpbf_ref[gi]) * ps_ref[:, cols])
    ypool = jnp.concatenate(ypool, axis=-1).astype(BF16)

    mix = _dot(ypool, wobf_ref[:POOL_WIDTH, :]) + _dot(ya_ref[0], wobf_ref[POOL_WIDTH:, :])
    x1 = x_ref[0] + mix
    x1_ref[0] = x1
    h = _rms(x1, g_ref[...]).astype(BF16)
    h_ref[0] = h
    logits = _dot_nt(wr_ref[...].astype(BF16), h)
    m = jnp.max(logits, axis=0, keepdims=True)
    e = jnp.exp(logits - m)
    aff_ref[0] = e / jnp.sum(e, axis=0, keepdims=True)


def _pool_inverse_counts(s):
    t = jnp.arange(s)
    cols = []
    for w in POOL_WINDOWS:
        lo = jnp.clip(t - w // 2, 0, s - 1)
        hi = jnp.clip(t + (w - w // 2) - 1, 0, s - 1)
        inv = 1.0 / (hi - lo + 1).astype(F32)
        cols.append(jnp.broadcast_to(inv[:, None], (s, POOL_GROUP)))
    return jnp.concatenate(cols, axis=-1)


def _mix(x, u, ya, wp, ps, wo, g, wr_t, tm=512):
    b, s, _ = x.shape
    nt = s // tm
    hb = tm // POOL_HALO
    tile = lambda w: pl.BlockSpec((1, tm, w), lambda bi, i: (bi, i, 0))
    fixed2 = lambda shape: pl.BlockSpec(shape, lambda bi, i: (0, 0))
    return pl.pallas_call(
        functools.partial(_mix_kernel, tm=tm),
        grid=(b, nt),
        in_specs=[
            tile(D_MODEL),
            tile(POOL_WIDTH),
            pl.BlockSpec((1, POOL_HALO, POOL_WIDTH), lambda bi, i: (bi, jnp.maximum(i * hb - 1, 0), 0)),
            pl.BlockSpec((1, POOL_HALO, POOL_WIDTH),
                         lambda bi, i: (bi, jnp.minimum((i + 1) * hb, s // POOL_HALO - 1), 0)),
            tile(ATTN_WIDTH),
            pl.BlockSpec((tm, POOL_WIDTH), lambda bi, i: (i, 0)),
            pl.BlockSpec(wp.shape, lambda bi, i: (0, 0, 0)),
            fixed2((1, POOL_WIDTH)),
            fixed2(wo.shape),
            fixed2((1, D_MODEL)),
            fixed2(wr_t.shape),
        ],
        out_specs=[
            tile(D_MODEL),
            tile(D_MODEL),
            pl.BlockSpec((1, N_EXPERTS, tm), lambda bi, i: (bi, 0, i)),
        ],
        out_shape=[
            jax.ShapeDtypeStruct((b, s, D_MODEL), F32),
            jax.ShapeDtypeStruct((b, s, D_MODEL), BF16),
            jax.ShapeDtypeStruct((b, N_EXPERTS, s), F32),
        ],
        scratch_shapes=[
            pltpu.VMEM((tm + 2 * POOL_HALO, POOL_WIDTH), F32),
            pltpu.VMEM(wo.shape, BF16),
            pltpu.VMEM(wp.shape, BF16),
        ],
        compiler_params=_params("arbitrary", "arbitrary"),
        name="mix",
    )(x, u, u, u, ya, _pool_inverse_counts(s), wp, ps, wo, g, wr_t)


def _lane_cumsum_exclusive(m):
    e, s = m.shape
    r = lax.broadcasted_iota(jnp.int32, (LANES, LANES), 0)
    c = lax.broadcasted_iota(jnp.int32, (LANES, LANES), 1)
    upper = (r < c).astype(BF16)
    carry = jnp.zeros((e, 1), F32)
    out = []
    for blk in range(s // LANES):
        piece = m[:, blk * LANES:(blk + 1) * LANES]
        out.append(_dot(piece.astype(BF16), upper) + carry)
        carry = carry + jnp.sum(piece, axis=-1, keepdims=True)
    return jnp.concatenate(out, axis=-1)


def _route_kernel(aff_ref, slot_ref, slot_t_ref, off_ref, *, cap):
    aff = aff_ref[0]
    bits = pltpu.bitcast(aff, jnp.int32)
    capf = jnp.float32(cap)

    def count_ge(cand):
        return jnp.sum((bits >= cand).astype(F32), axis=-1, keepdims=True)

    def search(step, ans):
        cand = ans | (jnp.int32(1) << (30 - step))
        return jnp.where(count_ge(cand) >= capf, cand, ans)

    thr = lax.fori_loop(0, 31, search, jnp.zeros((aff.shape[0], 1), jnp.int32))
    gt = bits > thr
    eq = bits == thr
    need = capf - jnp.sum(gt.astype(F32), axis=-1, keepdims=True)
    sel = gt | (eq & (_lane_cumsum_exclusive(eq.astype(F32)) < need))
    self = sel.astype(F32)
    slot = jnp.where(sel, _lane_cumsum_exclusive(self), float(NOT_SELECTED))
    slot_ref[0] = slot.astype(jnp.int32)
    pad = jnp.full((LANES - slot.shape[0], slot.shape[1]), float(NOT_SELECTED), F32)
    slot_t_ref[0] = jnp.concatenate([slot, pad], axis=0).T.astype(jnp.int32)
    lane = lax.broadcasted_iota(jnp.int32, (slot.shape[0], LANES), 1)
    off = jnp.zeros((slot.shape[0], LANES), F32)
    run = jnp.zeros((slot.shape[0], 1), F32)
    for i in range(1, slot.shape[1] // ROUTE_TILE + 1):
        run = run + jnp.sum(self[:, (i - 1) * ROUTE_TILE:i * ROUTE_TILE], axis=-1, keepdims=True)
        off = jnp.where(lane == i, run, off)
    off_ref[0] = off.astype(jnp.int32)


def _route(aff_t, cap):
    b, e, s = aff_t.shape
    assert s % ROUTE_TILE == 0 and s // ROUTE_TILE < OFF_STRIDE
    return pl.pallas_call(
        functools.partial(_route_kernel, cap=cap),
        grid=(b,),
        in_specs=[pl.BlockSpec((1, e, s), lambda bi: (bi, 0, 0))],
        out_specs=[
            pl.BlockSpec((1, e, s), lambda bi: (bi, 0, 0)),
            pl.BlockSpec((1, s, LANES), lambda bi: (bi, 0, 0)),
            pl.BlockSpec((1, e, LANES), lambda bi: (bi, 0, 0)),
        ],
        out_shape=[
            jax.ShapeDtypeStruct((b, e, s), jnp.int32),
            jax.ShapeDtypeStruct((b, s, LANES), jnp.int32),
            jax.ShapeDtypeStruct((b, e, LANES), jnp.int32),
        ],
        compiler_params=_params("arbitrary"),
        name="route",
    )(aff_t)


def _window_starts(off_ref, bi, ti, cap):
    starts = []
    ok = None
    for e in range(N_EXPERTS):
        base = (bi * N_EXPERTS + e) * OFF_STRIDE + ti
        lo = off_ref[base]
        hi = off_ref[base + 1]
        st = jnp.minimum(jnp.bitwise_and(lo, -16), cap - ROUTE_WIN)
        fits = hi <= st + ROUTE_WIN
        ok = fits if ok is None else jnp.logical_and(ok, fits)
        starts.append(pl.multiple_of(st, 16))
    return starts, ok


def _dispatch_kernel(off_ref, slot_ref, aff_ref, h_ref, xe_ref, gate_ref, *, cap):
    bi = pl.program_id(0)
    ti = pl.program_id(1)

    @pl.when(ti == 0)
    def _():
        xe_ref[...] = jnp.zeros_like(xe_ref)
        gate_ref[...] = jnp.zeros_like(gate_ref)

    slot = slot_ref[0]
    aff = aff_ref[0]
    h = h_ref[0]
    tile = slot.shape[-1]
    starts, ok = _window_starts(off_ref, bi, ti, cap)

    @pl.when(ok)
    def _():
        row = lax.broadcasted_iota(jnp.int32, (ROUTE_WIN, tile), 0)
        hits = [row == (slot[e:e + 1, :] - starts[e]) for e in range(N_EXPERTS)]
        onehot = jnp.concatenate([jnp.where(hh, 1.0, 0.0).astype(BF16) for hh in hits], axis=0)
        res = _dot(onehot, h)
        for e, hh in enumerate(hits):
            win = pl.ds(starts[e], ROUTE_WIN)
            xe_ref[e, win, :] += res[e * ROUTE_WIN:(e + 1) * ROUTE_WIN].astype(BF16)
            gate_ref[e, win, :] += jnp.sum(jnp.where(hh, aff[e:e + 1, :], 0.0), axis=-1, keepdims=True)

    @pl.when(jnp.logical_not(ok))
    def _():
        row = lax.broadcasted_iota(jnp.int32, (cap, tile), 0)
        for e in range(N_EXPERTS):
            hh = row == slot[e:e + 1, :]
            xe_ref[e] += _dot(jnp.where(hh, 1.0, 0.0).astype(BF16), h).astype(BF16)
            gate_ref[e] += jnp.sum(jnp.where(hh, aff[e:e + 1, :], 0.0), axis=-1, keepdims=True)


def _dispatch(off, slot, aff_t, h, cap):
    b, e, s = slot.shape
    rows = pl.BlockSpec((1, e, ROUTE_TILE), lambda bi, ti, off_ref: (bi, 0, ti))
    return pl.pallas_call(
        functools.partial(_dispatch_kernel, cap=cap),
        grid_spec=pltpu.PrefetchScalarGridSpec(
            num_scalar_prefetch=1,
            grid=(b, s // ROUTE_TILE),
            in_specs=[rows, rows,
                      pl.BlockSpec((1, ROUTE_TILE, D_MODEL), lambda bi, ti, off_ref: (bi, ti, 0))],
            out_specs=[
                pl.BlockSpec((e, cap, D_MODEL), lambda bi, ti, off_ref: (0, bi, 0)),
                pl.BlockSpec((e, cap, 1), lambda bi, ti, off_ref: (0, bi, 0)),
            ],
        ),
        out_shape=[
            jax.ShapeDtypeStruct((e, b * cap, D_MODEL), BF16),
            jax.ShapeDtypeStruct((e, b * cap, 1), F32),
        ],
        compiler_params=_params("arbitrary", "arbitrary"),
        name="dispatch",
    )(off, slot, aff_t, h)


def _experts_kernel(xe_ref, gate_ref, wg_ref, wu_ref, wd_ref, y_ref, acc_ref):
    f = pl.program_id(1)
    wg = wg_ref[0].astype(BF16)
    wu = wu_ref[0].astype(BF16)
    wd = wd_ref[0].astype(BF16)

    @pl.when(f == 0)
    def _():
        acc_ref[...] = jnp.zeros_like(acc_ref)

    for mb in range(xe_ref.shape[1] // EXPERT_ROWS):
        rows = slice(mb * EXPERT_ROWS, (mb + 1) * EXPERT_ROWS)
        xe = xe_ref[0, rows, :]
        a = _dot(xe, wg)
        b = _dot(xe, wu)
        hmid = (a * jax.nn.sigmoid(a) * b).astype(BF16)
        acc_ref[rows, :] += _dot(hmid, wd)

    @pl.when(f == pl.num_programs(1) - 1)
    def _():
        y_ref[0] = (acc_ref[...] * gate_ref[0]).astype(y_ref.dtype)


def _experts(xe, gate, wg, wu, wd, tf=512):
    e, m, _ = xe.shape
    nf = D_EXPERT // tf
    return pl.pallas_call(
        _experts_kernel,
        grid=(e, nf),
        in_specs=[
            pl.BlockSpec((1, m, D_MODEL), lambda ei, f: (ei, 0, 0)),
            pl.BlockSpec((1, m, 1), lambda ei, f: (ei, 0, 0)),
            pl.BlockSpec((1, D_MODEL, tf), lambda ei, f: (ei, 0, f)),
            pl.BlockSpec((1, D_MODEL, tf), lambda ei, f: (ei, 0, f)),
            pl.BlockSpec((1, tf, D_MODEL), lambda ei, f: (ei, f, 0)),
        ],
        out_specs=pl.BlockSpec((1, m, D_MODEL), lambda ei, f: (ei, 0, 0)),
        out_shape=jax.ShapeDtypeStruct((e, m, D_MODEL), BF16),
        scratch_shapes=[pltpu.VMEM((m, D_MODEL), F32)],
        compiler_params=_params("arbitrary", "arbitrary"),
        name="experts",
    )(xe, gate, wg, wu, wd)


def _combine_kernel(off_ref, x1_ref, st_ref, y_ref, p_ref, gn_ref, wg_ref, wp_ref, gp_ref,
                    o_ref, wgbf_ref, wpbf_ref, ffn_ref, *, cap):
    bi = pl.program_id(0)

    @pl.when((bi == 0) & (pl.program_id(1) == 0))
    def _():
        wgbf_ref[...] = wg_ref[...].astype(BF16)
        wpbf_ref[...] = wp_ref[...].astype(BF16)

    tm = x1_ref.shape[1]
    for sub in range(tm // ROUTE_TILE):
        rows = slice(sub * ROUTE_TILE, (sub + 1) * ROUTE_TILE)
        st = st_ref[0, rows, :]
        starts, ok = _window_starts(off_ref, bi, pl.program_id(1) * (tm // ROUTE_TILE) + sub, cap)

        @pl.when(ok)
        def _():
            lane = lax.broadcasted_iota(jnp.int32, (ROUTE_TILE, LANES), 1)
            low = lane < ROUTE_WIN
            total = None
            for g in range(N_EXPERTS // ROUTE_GROUP):
                halves = []
                wins = []
                for half in range(ROUTE_GROUP // 2):
                    e0 = g * ROUTE_GROUP + 2 * half
                    t0 = st[:, e0:e0 + 1] - starts[e0]
                    t1 = st[:, e0 + 1:e0 + 2] + (ROUTE_WIN - starts[e0 + 1])
                    halves.append(jnp.where(lane == jnp.where(low, t0, t1), 1.0, 0.0).astype(BF16))
                    wins.append(y_ref[e0, pl.ds(starts[e0], ROUTE_WIN), :])
                    wins.append(y_ref[e0 + 1, pl.ds(starts[e0 + 1], ROUTE_WIN), :])
                part = _dot(jnp.concatenate(halves, axis=1), jnp.concatenate(wins, axis=0))
                total = part if total is None else total + part
            ffn_ref[rows, :] = total

        @pl.when(jnp.logical_not(ok))
        def _():
            lane = lax.broadcasted_iota(jnp.int32, (ROUTE_TILE, cap), 1)
            total = None
            for e in range(N_EXPERTS):
                onehot = jnp.where(lane == st[:, e:e + 1], 1.0, 0.0).astype(BF16)
                part = _dot(onehot, y_ref[e])
                total = part if total is None else total + part
            ffn_ref[rows, :] = total

    for sub in range(tm // ROUTE_TILE):
        rows = slice(sub * ROUTE_TILE, (sub + 1) * ROUTE_TILE)
        x2 = x1_ref[0, rows, :] + ffn_ref[rows, :]
        g = jax.nn.sigmoid(_dot(_rms(x2, gn_ref[...]).astype(BF16), wgbf_ref[...]))
        emb = _rms(_dot(p_ref[0, rows, :].astype(BF16), wpbf_ref[...]), gp_ref[...])
        o_ref[0, rows, :] = x2 + g * emb


def _combine(off, x1, slot_t, y, p, gn, wg, wp, gp, cap, tm=512):
    b, s, _ = x1.shape
    assert tm % ROUTE_TILE == 0
    tile = lambda w: pl.BlockSpec((1, tm, w), lambda bi, i, off_ref: (bi, i, 0))
    fixed2 = lambda shape: pl.BlockSpec(shape, lambda bi, i, off_ref: (0, 0))
    return pl.pallas_call(
        functools.partial(_combine_kernel, cap=cap),
        grid_spec=pltpu.PrefetchScalarGridSpec(
            num_scalar_prefetch=1,
            grid=(b, s // tm),
            in_specs=[
                tile(D_MODEL),
                tile(LANES),
                pl.BlockSpec((N_EXPERTS, cap, D_MODEL), lambda bi, i, off_ref: (0, bi, 0)),
                tile(PLE_DIM),
                fixed2((1, D_MODEL)),
                fixed2(wg.shape),
                fixed2(wp.shape),
                fixed2((1, D_MODEL)),
            ],
            out_specs=tile(D_MODEL),
            scratch_shapes=[pltpu.VMEM(wg.shape, BF16), pltpu.VMEM(wp.shape, BF16),
                            pltpu.VMEM((tm, D_MODEL), F32)],
        ),
        out_shape=jax.ShapeDtypeStruct((b, s, D_MODEL), F32),
        compiler_params=_params("arbitrary", "arbitrary"),
        name="combine",
    )(off, x1, slot_t, y, p, gn, wg, wp, gp)


def kernel(x, p, norm_mix, w_in, w_pool, pool_scale, q_norm, k_norm, rpb, w_out, norm_ffn, w_router, w_gate, w_up, w_down, norm_ple, w_ple_gate, w_ple_proj, norm_ple_post):
    b, s, d = x.shape
    depth = w_in.shape[0]
    cap = EC_CAPACITY * s // N_EXPERTS
    kh = min(WIN_ROWS_MAX, s // GRID_W)
    head = jnp.arange(ATTN_WIDTH) // HEAD_DIM
    block_diag = (head[:, None] == head[None, :]).astype(BF16)
    row = lambda a: a.reshape(1, -1)
    for i in range(depth):
        u, q, k, v = _in_proj(x.reshape(b * s, d), row(norm_mix[i]), w_in[i], block_diag,
                              row(jnp.tile(q_norm[i], ATTN_HEADS)), row(jnp.tile(k_norm[i], ATTN_HEADS)))
        shp = lambda a: a.reshape(b, s, -1)
        y_attn = _natten(shp(q), shp(k), shp(v), _attn_bias_table(rpb[i], kh))
        x1, h, aff_t = _mix(x, shp(u), y_attn, w_pool[i], row(pool_scale[i]), w_out[i],
                            row(norm_ffn[i]), w_router[i].T)
        slot, slot_t, off = _route(aff_t, cap)
        off = off[:, :, :OFF_STRIDE].reshape(-1)
        xe, gate = _dispatch(off, slot, aff_t, h, cap)
        y = _experts(xe, gate, w_gate[i], w_up[i], w_down[i])
        x = _combine(off, x1, slot_t, y, p[i], row(norm_ple[i]), w_ple_gate[i], w_ple_proj[i],
                     row(norm_ple_post[i]), cap)
    return x
```

```python
import functools

import jax
import jax.numpy as jnp
from jax import lax
from jax.experimental import pallas as pl
from jax.experimental.pallas import tpu as pltpu

D_MODEL = 1024
GRID_W = 64
POOL_WINDOWS = (2, 4, 8, 16)
POOL_WIDTH = D_MODEL // 2
POOL_GROUP = POOL_WIDTH // len(POOL_WINDOWS)
ATTN_HEADS = 8
HEAD_DIM = (D_MODEL // 2) // ATTN_HEADS
ATTN_WIDTH = ATTN_HEADS * HEAD_DIM
WIN_ROWS_MAX = 8
WIN_COLS = 16
N_EXPERTS = 16
EC_CAPACITY = 2
D_EXPERT = 2 * D_MODEL
PLE_DIM = 256
RMS_EPS = 1e-6

LANES = 128
POOL_HALO = 8
MASK_BIAS = -1e30
NATTEN_UNROLL = 8
ROUTE_TILE = 256
ROUTE_WIN = 64
ROUTE_GROUP = 4
OFF_STRIDE = 16
NOT_SELECTED = -(1 << 20)
assert 2 * ROUTE_WIN == LANES and ROUTE_GROUP % 2 == 0
EXPERT_ROWS = 512
VMEM_LIMIT = 56 * 1024 * 1024

BF16 = jnp.bfloat16
F32 = jnp.float32


def _params(*sem):
    return pltpu.CompilerParams(dimension_semantics=sem, vmem_limit_bytes=VMEM_LIMIT)


def _rms(x, g):
    return x * lax.rsqrt(jnp.mean(x * x, axis=-1, keepdims=True) + RMS_EPS) * g


def _dot(a, b):
    return jnp.dot(a, b, preferred_element_type=F32)


def _dot_nt(a, b):
    return lax.dot_general(a, b, (((1,), (1,)), ((), ())), preferred_element_type=F32)


def _in_proj_kernel(x_ref, g_ref, w_ref, bd_ref, qg_ref, kg_ref,
                    u_ref, q_ref, k_ref, v_ref, wbf_ref):
    @pl.when(pl.program_id(0) == 0)
    def _():
        wbf_ref[...] = w_ref[...].astype(BF16)

    h = _rms(x_ref[...], g_ref[...]).astype(BF16)
    z = _dot(h, wbf_ref[...])
    u_ref[...] = z[:, :POOL_WIDTH]
    bd = bd_ref[...]

    def head_norm(t, g):
        ms = _dot((t * t).astype(BF16), bd) * (1.0 / HEAD_DIM)
        return t * lax.rsqrt(ms + RMS_EPS) * g

    q = z[:, POOL_WIDTH:POOL_WIDTH + ATTN_WIDTH]
    k = z[:, POOL_WIDTH + ATTN_WIDTH:POOL_WIDTH + 2 * ATTN_WIDTH]
    v = z[:, POOL_WIDTH + 2 * ATTN_WIDTH:]
    q_ref[...] = (head_norm(q, qg_ref[...]) * (HEAD_DIM ** -0.5)).astype(BF16)
    k_ref[...] = head_norm(k, kg_ref[...]).astype(BF16)
    v_ref[...] = v.astype(BF16)


def _in_proj(x2, g, w, bd, qg, kg, tm=512):
    n = x2.shape[0]
    zw = w.shape[1]
    row = lambda i: (i, 0)
    fixed = lambda i: (0, 0)
    return pl.pallas_call(
        _in_proj_kernel,
        grid=(n // tm,),
        in_specs=[
            pl.BlockSpec((tm, D_MODEL), row),
            pl.BlockSpec((1, D_MODEL), fixed),
            pl.BlockSpec((D_MODEL, zw), fixed),
            pl.BlockSpec((ATTN_WIDTH, ATTN_WIDTH), fixed),
            pl.BlockSpec((1, ATTN_WIDTH), fixed),
            pl.BlockSpec((1, ATTN_WIDTH), fixed),
        ],
        out_specs=[
            pl.BlockSpec((tm, POOL_WIDTH), row),
            pl.BlockSpec((tm, ATTN_WIDTH), row),
            pl.BlockSpec((tm, ATTN_WIDTH), row),
            pl.BlockSpec((tm, ATTN_WIDTH), row),
        ],
        out_shape=[
            jax.ShapeDtypeStruct((n, POOL_WIDTH), F32),
            jax.ShapeDtypeStruct((n, ATTN_WIDTH), BF16),
            jax.ShapeDtypeStruct((n, ATTN_WIDTH), BF16),
            jax.ShapeDtypeStruct((n, ATTN_WIDTH), BF16),
        ],
        scratch_shapes=[pltpu.VMEM((D_MODEL, zw), BF16)],
        compiler_params=_params("arbitrary"),
        name="in_proj",
    )(x2, g, w, bd, qg, kg)


def _natten_kernel(q_ref, k_ref, v_ref, tbl_ref, o_ref, bias_ref, *, rows, kh):
    band = kh * GRID_W
    lane = lax.broadcasted_iota(jnp.int32, (GRID_W, LANES), 1)
    first = lane < HEAD_DIM

    for hh in range(2):
        for d0 in range(WIN_ROWS_MAX):
            for kk in range(kh // 2):
                bias_ref[hh, d0, :, kk * LANES:(kk + 1) * LANES] = jnp.where(
                    lane < GRID_W, tbl_ref[hh, d0 + 2 * kk], tbl_ref[hh, d0 + 2 * kk + 1])

    def row_group(g, carry):
        geo = []
        for sub in range(NATTEN_UNROLL):
            r = g * NATTEN_UNROLL + sub
            rs = jnp.clip(r - kh // 2, 0, rows - kh)
            geo.append((rs - r + (WIN_ROWS_MAX - 1),
                        pl.multiple_of(r * GRID_W, GRID_W),
                        pl.multiple_of(rs * GRID_W, GRID_W)))
        scores = []
        for d0, q0, k0 in geo:
            qr = q_ref[0, pl.ds(q0, GRID_W), :]
            kb = k_ref[0, pl.ds(k0, band), :]
            for hh in range(2):
                sel = first if hh == 0 else jnp.logical_not(first)
                qm = jnp.where(sel, qr, jnp.zeros_like(qr))
                scores.append(_dot_nt(qm, kb) + bias_ref[hh, d0])
        probs = []
        for s in scores:
            e = jnp.exp(s - jnp.max(s, axis=-1, keepdims=True))
            probs.append((e.astype(BF16), jnp.sum(e, axis=-1, keepdims=True)))
        for idx, (d0, q0, k0) in enumerate(geo):
            vb = v_ref[0, pl.ds(k0, band), :]
            outs = [_dot(e, vb) / l for e, l in probs[2 * idx:2 * idx + 2]]
            o_ref[0, pl.ds(q0, GRID_W), :] = jnp.where(first, outs[0], outs[1]).astype(o_ref.dtype)
        return carry

    lax.fori_loop(0, rows // NATTEN_UNROLL, row_group, 0)


def _natten(q, k, v, bias):
    b, s, _ = q.shape
    rows = s // GRID_W
    kh = min(WIN_ROWS_MAX, rows)
    assert rows % NATTEN_UNROLL == 0 and kh % 2 == 0 and 2 * GRID_W == LANES
    pairs = ATTN_WIDTH // LANES
    blk = pl.BlockSpec((1, s, LANES), lambda bi, p: (bi, 0, p))
    return pl.pallas_call(
        functools.partial(_natten_kernel, rows=rows, kh=kh),
        grid=(b, pairs),
        in_specs=[blk, blk, blk,
                  pl.BlockSpec((2,) + bias.shape[1:], lambda bi, p: (p, 0, 0, 0))],
        out_specs=blk,
        out_shape=jax.ShapeDtypeStruct((b, s, ATTN_WIDTH), BF16),
        scratch_shapes=[pltpu.VMEM((2, WIN_ROWS_MAX, GRID_W, kh * GRID_W), F32)],
        compiler_params=_params("arbitrary", "arbitrary"),
        name="natten",
    )(q, k, v, bias)


def _attn_bias_table(rpb):
    c = jnp.arange(GRID_W)
    cs = jnp.clip(c - WIN_COLS // 2, 0, GRID_W - WIN_COLS)
    j = jnp.arange(GRID_W)
    valid = (j[None, :] >= cs[:, None]) & (j[None, :] < cs[:, None] + WIN_COLS)
    dc = j[None, :] - c[:, None] + (WIN_COLS - 1)
    pick = ((dc[None] == jnp.arange(2 * WIN_COLS - 1)[:, None, None]) & valid[None]).astype(F32)
    t = jnp.einsum('hrd,dcj->hrcj', rpb.astype(F32), pick, precision=lax.Precision.HIGHEST)
    t = jnp.where(valid, t, MASK_BIAS)
    return jnp.concatenate([t, t], axis=-1)


def _window_sum(upad, w, tm):
    n = upad.shape[0]
    fwd = upad
    span = 1
    while span < min(w, POOL_HALO):
        fwd = fwd + pltpu.roll(fwd, n - span, axis=0)
        span *= 2
    centre = slice(POOL_HALO, POOL_HALO + tm)
    if w == 2 * POOL_HALO:
        return fwd[0:tm] + fwd[centre]
    return pltpu.roll(fwd, w // 2, axis=0)[centre]


def _mix_kernel(x_ref, u_ref, up_ref, un_ref, ya_ref, ic_ref, wp_ref, ps_ref, wo_ref, g_ref, wr_ref,
                x1_ref, h_ref, aff_ref, upad_ref, wobf_ref, wpbf_ref, *, tm):
    i = pl.program_id(1)

    @pl.when((pl.program_id(0) == 0) & (i == 0))
    def _():
        wobf_ref[...] = wo_ref[...].astype(BF16)
        wpbf_ref[...] = wp_ref[...].astype(BF16)

    upad_ref[0:POOL_HALO, :] = jnp.where(i > 0, up_ref[0], 0.0)
    upad_ref[POOL_HALO:POOL_HALO + tm, :] = u_ref[0]
    upad_ref[POOL_HALO + tm:, :] = jnp.where(i < pl.num_programs(1) - 1, un_ref[0], 0.0)

    ypool = []
    for gi, w in enumerate(POOL_WINDOWS):
        cols = slice(gi * POOL_GROUP, (gi + 1) * POOL_GROUP)
        upad = upad_ref[:, cols]
        inv = jnp.concatenate([
            jnp.where(i == 0, ic_ref[0:POOL_HALO, cols], 1.0 / w),
            jnp.full((tm - 2 * POOL_HALO, POOL_GROUP), 1.0 / w, F32),
            jnp.where(i == pl.num_programs(1) - 1, ic_ref[POOL_HALO:, cols], 1.0 / w)], axis=0)
        d = _window_sum(upad, w, tm) * inv - upad[POOL_HALO:POOL_HALO + tm]
        ypool.append(_dot(d.astype(BF16), wpbf_ref[gi]) * ps_ref[:, cols])
    ypool = jnp.concatenate(ypool, axis=-1).astype(BF16)

    mix = _dot(ypool, wobf_ref[:POOL_WIDTH, :]) + _dot(ya_ref[0], wobf_ref[POOL_WIDTH:, :])
    x1 = x_ref[0] + mix
    x1_ref[0] = x1
    h = _rms(x1, g_ref[...]).astype(BF16)
    h_ref[0] = h
    logits = _dot_nt(wr_ref[...].astype(BF16), h)
    m = jnp.max(logits, axis=0, keepdims=True)
    e = jnp.exp(logits - m)
    aff_ref[0] = e / jnp.sum(e, axis=0, keepdims=True)


def _pool_edge_inverse_counts(s):
    t = jnp.concatenate([jnp.arange(POOL_HALO), jnp.arange(s - POOL_HALO, s)])
    cols = []
    for w in POOL_WINDOWS:
        lo = jnp.clip(t - w // 2, 0, s - 1)
        hi = jnp.clip(t + (w - w // 2) - 1, 0, s - 1)
        inv = 1.0 / (hi - lo + 1).astype(F32)
        cols.append(jnp.broadcast_to(inv[:, None], (2 * POOL_HALO, POOL_GROUP)))
    return jnp.concatenate(cols, axis=-1)


def _mix(x, u, ya, wp, ps, wo, g, wr_t, tm=512):
    b, s, _ = x.shape
    nt = s // tm
    hb = tm // POOL_HALO
    tile = lambda w: pl.BlockSpec((1, tm, w), lambda bi, i: (bi, i, 0))
    fixed2 = lambda shape: pl.BlockSpec(shape, lambda bi, i: (0, 0))
    return pl.pallas_call(
        functools.partial(_mix_kernel, tm=tm),
        grid=(b, nt),
        in_specs=[
            tile(D_MODEL),
            tile(POOL_WIDTH),
            pl.BlockSpec((1, POOL_HALO, POOL_WIDTH), lambda bi, i: (bi, jnp.maximum(i * hb - 1, 0), 0)),
            pl.BlockSpec((1, POOL_HALO, POOL_WIDTH),
                         lambda bi, i: (bi, jnp.minimum((i + 1) * hb, s // POOL_HALO - 1), 0)),
            tile(ATTN_WIDTH),
            fixed2((2 * POOL_HALO, POOL_WIDTH)),
            pl.BlockSpec(wp.shape, lambda bi, i: (0, 0, 0)),
            fixed2((1, POOL_WIDTH)),
            fixed2(wo.shape),
            fixed2((1, D_MODEL)),
            fixed2(wr_t.shape),
        ],
        out_specs=[
            tile(D_MODEL),
            tile(D_MODEL),
            pl.BlockSpec((1, N_EXPERTS, tm), lambda bi, i: (bi, 0, i)),
        ],
        out_shape=[
            jax.ShapeDtypeStruct((b, s, D_MODEL), F32),
            jax.ShapeDtypeStruct((b, s, D_MODEL), BF16),
            jax.ShapeDtypeStruct((b, N_EXPERTS, s), F32),
        ],
        scratch_shapes=[
            pltpu.VMEM((tm + 2 * POOL_HALO, POOL_WIDTH), F32),
            pltpu.VMEM(wo.shape, BF16),
            pltpu.VMEM(wp.shape, BF16),
        ],
        compiler_params=_params("arbitrary", "arbitrary"),
        name="mix",
    )(x, u, u, u, ya, _pool_edge_inverse_counts(s), wp, ps, wo, g, wr_t)


def _lane_cumsum_exclusive(m):
    e, s = m.shape
    r = lax.broadcasted_iota(jnp.int32, (LANES, LANES), 0)
    c = lax.broadcasted_iota(jnp.int32, (LANES, LANES), 1)
    upper = (r < c).astype(BF16)
    carry = jnp.zeros((e, 1), F32)
    out = []
    for blk in range(s // LANES):
        piece = m[:, blk * LANES:(blk + 1) * LANES]
        out.append(_dot(piece.astype(BF16), upper) + carry)
        carry = carry + jnp.sum(piece, axis=-1, keepdims=True)
    return jnp.concatenate(out, axis=-1)


def _route_kernel(aff_ref, slot_ref, slot_t_ref, off_ref, *, cap):
    aff = aff_ref[...]
    capf = jnp.float32(cap)

    def count_ge(cand_bits):
        return jnp.sum((aff >= pltpu.bitcast(cand_bits, F32)).astype(F32), axis=-1, keepdims=True)

    def search(step, ans):
        cand = ans | (jnp.int32(1) << (30 - step))
        return jnp.where(count_ge(cand) >= capf, cand, ans)

    thr = pltpu.bitcast(lax.fori_loop(0, 31, search, jnp.zeros((aff.shape[0], 1), jnp.int32)), F32)
    gt = aff > thr
    eq = aff == thr
    need = capf - jnp.sum(gt.astype(F32), axis=-1, keepdims=True)
    sel = gt | (eq & (_lane_cumsum_exclusive(eq.astype(F32)) < need))
    self = sel.astype(F32)
    slot = jnp.where(sel, _lane_cumsum_exclusive(self), float(NOT_SELECTED))
    slot_ref[...] = slot.astype(jnp.int32)
    pad = jnp.full((LANES - N_EXPERTS, slot.shape[1]), float(NOT_SELECTED), F32)
    for bi in range(slot_t_ref.shape[0]):
        mine = slot[bi * N_EXPERTS:(bi + 1) * N_EXPERTS]
        slot_t_ref[bi] = jnp.concatenate([mine, pad], axis=0).T.astype(jnp.int32)
    lane = lax.broadcasted_iota(jnp.int32, (slot.shape[0], LANES), 1)
    off = jnp.zeros((slot.shape[0], LANES), F32)
    run = jnp.zeros((slot.shape[0], 1), F32)
    for i in range(1, slot.shape[1] // ROUTE_TILE + 1):
        run = run + jnp.sum(self[:, (i - 1) * ROUTE_TILE:i * ROUTE_TILE], axis=-1, keepdims=True)
        off = jnp.where(lane == i, run, off)
    off_ref[...] = off.astype(jnp.int32)


def _route(aff_t, cap):
    b, e, s = aff_t.shape
    assert e == N_EXPERTS and s % ROUTE_TILE == 0 and s // ROUTE_TILE < OFF_STRIDE
    slot, slot_t, off = pl.pallas_call(
        functools.partial(_route_kernel, cap=cap),
        out_shape=[
            jax.ShapeDtypeStruct((b * e, s), jnp.int32),
            jax.ShapeDtypeStruct((b, s, LANES), jnp.int32),
            jax.ShapeDtypeStruct((b * e, LANES), jnp.int32),
        ],
        compiler_params=pltpu.CompilerParams(vmem_limit_bytes=VMEM_LIMIT),
        name="route",
    )(aff_t.reshape(b * e, s))
    return slot.reshape(b, e, s), slot_t, off


def _window_starts(off_ref, bi, ti, cap):
    starts = []
    ok = None
    for e in range(N_EXPERTS):
        base = (bi * N_EXPERTS + e) * OFF_STRIDE + ti
        lo = off_ref[base]
        hi = off_ref[base + 1]
        st = jnp.minimum(jnp.bitwise_and(lo, -16), cap - ROUTE_WIN)
        fits = hi <= st + ROUTE_WIN
        ok = fits if ok is None else jnp.logical_and(ok, fits)
        starts.append(pl.multiple_of(st, 16))
    return starts, ok


def _dispatch_kernel(off_ref, slot_ref, aff_ref, h_ref, xe_ref, gate_ref, *, cap):
    bi = pl.program_id(0)
    ti = pl.program_id(1)

    @pl.when(ti == 0)
    def _():
        xe_ref[...] = jnp.zeros_like(xe_ref)
        gate_ref[...] = jnp.zeros_like(gate_ref)

    slot = slot_ref[0]
    aff = aff_ref[0]
    h = h_ref[0]
    tile = slot.shape[-1]
    starts, ok = _window_starts(off_ref, bi, ti, cap)

    @pl.when(ok)
    def _():
        row = lax.broadcasted_iota(jnp.int32, (ROUTE_WIN, tile), 0)
        hits = [row == (slot[e:e + 1, :] - starts[e]) for e in range(N_EXPERTS)]
        onehot = jnp.concatenate([jnp.where(hh, 1.0, 0.0).astype(BF16) for hh in hits], axis=0)
        res = _dot(onehot, h)
        for e, hh in enumerate(hits):
            win = pl.ds(starts[e], ROUTE_WIN)
            xe_ref[e, win, :] += res[e * ROUTE_WIN:(e + 1) * ROUTE_WIN].astype(BF16)
            gate_ref[e, win, :] += jnp.sum(jnp.where(hh, aff[e:e + 1, :], 0.0), axis=-1, keepdims=True)

    @pl.when(jnp.logical_not(ok))
    def _():
        row = lax.broadcasted_iota(jnp.int32, (cap, tile), 0)
        for e in range(N_EXPERTS):
            hh = row == slot[e:e + 1, :]
            xe_ref[e] += _dot(jnp.where(hh, 1.0, 0.0).astype(BF16), h).astype(BF16)
            gate_ref[e] += jnp.sum(jnp.where(hh, aff[e:e + 1, :], 0.0), axis=-1, keepdims=True)


def _dispatch(off, slot, aff_t, h, cap):
    b, e, s = slot.shape
    rows = pl.BlockSpec((1, e, ROUTE_TILE), lambda bi, ti, off_ref: (bi, 0, ti))
    return pl.pallas_call(
        functools.partial(_dispatch_kernel, cap=cap),
        grid_spec=pltpu.PrefetchScalarGridSpec(
            num_scalar_prefetch=1,
            grid=(b, s // ROUTE_TILE),
            in_specs=[rows, rows,
                      pl.BlockSpec((1, ROUTE_TILE, D_MODEL), lambda bi, ti, off_ref: (bi, ti, 0))],
            out_specs=[
                pl.BlockSpec((e, cap, D_MODEL), lambda bi, ti, off_ref: (0, bi, 0)),
                pl.BlockSpec((e, cap, 1), lambda bi, ti, off_ref: (0, bi, 0)),
            ],
        ),
        out_shape=[
            jax.ShapeDtypeStruct((e, b * cap, D_MODEL), BF16),
            jax.ShapeDtypeStruct((e, b * cap, 1), F32),
        ],
        compiler_params=_params("arbitrary", "arbitrary"),
        name="dispatch",
    )(off, slot, aff_t, h)


def _experts_kernel(xe_ref, gate_ref, wg_ref, wu_ref, wd_ref, y_ref, acc_ref):
    f = pl.program_id(1)
    wg = wg_ref[0].astype(BF16)
    wu = wu_ref[0].astype(BF16)
    wd = wd_ref[0].astype(BF16)

    @pl.when(f == 0)
    def _():
        acc_ref[...] = jnp.zeros_like(acc_ref)

    for mb in range(xe_ref.shape[1] // EXPERT_ROWS):
        rows = slice(mb * EXPERT_ROWS, (mb + 1) * EXPERT_ROWS)
        xe = xe_ref[0, rows, :]
        a = _dot(xe, wg)
        b = _dot(xe, wu)
        hmid = (a * jax.nn.sigmoid(a) * b).astype(BF16)
        acc_ref[rows, :] += _dot(hmid, wd)

    @pl.when(f == pl.num_programs(1) - 1)
    def _():
        y_ref[0] = (acc_ref[...] * gate_ref[0]).astype(y_ref.dtype)


def _experts(xe, gate, wg, wu, wd, tf=512):
    e, m, _ = xe.shape
    nf = D_EXPERT // tf
    return pl.pallas_call(
        _experts_kernel,
        grid=(e, nf),
        in_specs=[
            pl.BlockSpec((1, m, D_MODEL), lambda ei, f: (ei, 0, 0)),
            pl.BlockSpec((1, m, 1), lambda ei, f: (ei, 0, 0)),
            pl.BlockSpec((1, D_MODEL, tf), lambda ei, f: (ei, 0, f)),
            pl.BlockSpec((1, D_MODEL, tf), lambda ei, f: (ei, 0, f)),
            pl.BlockSpec((1, tf, D_MODEL), lambda ei, f: (ei, f, 0)),
        ],
        out_specs=pl.BlockSpec((1, m, D_MODEL), lambda ei, f: (ei, 0, 0)),
        out_shape=jax.ShapeDtypeStruct((e, m, D_MODEL), BF16),
        scratch_shapes=[pltpu.VMEM((m, D_MODEL), F32)],
        compiler_params=_params("arbitrary", "arbitrary"),
        name="experts",
    )(xe, gate, wg, wu, wd)


def _combine_kernel(off_ref, x1_ref, st_ref, y_ref, p_ref, gn_ref, wg_ref, wp_ref, gp_ref,
                    o_ref, wgbf_ref, wpbf_ref, ffn_ref, *, cap):
    bi = pl.program_id(0)

    @pl.when((bi == 0) & (pl.program_id(1) == 0))
    def _():
        wgbf_ref[...] = wg_ref[...].astype(BF16)
        wpbf_ref[...] = wp_ref[...].astype(BF16)

    tm = x1_ref.shape[1]
    for sub in range(tm // ROUTE_TILE):
        rows = slice(sub * ROUTE_TILE, (sub + 1) * ROUTE_TILE)
        st = st_ref[0, rows, :]
        starts, ok = _window_starts(off_ref, bi, pl.program_id(1) * (tm // ROUTE_TILE) + sub, cap)

        @pl.when(ok)
        def _():
            lane = lax.broadcasted_iota(jnp.int32, (ROUTE_TILE, LANES), 1)
            low = lane < ROUTE_WIN
            total = None
            for g in range(N_EXPERTS // ROUTE_GROUP):
                halves = []
                wins = []
                for half in range(ROUTE_GROUP // 2):
                    e0 = g * ROUTE_GROUP + 2 * half
                    t0 = st[:, e0:e0 + 1] - starts[e0]
                    t1 = st[:, e0 + 1:e0 + 2] + (ROUTE_WIN - starts[e0 + 1])
                    halves.append(jnp.where(lane == jnp.where(low, t0, t1), 1.0, 0.0).astype(BF16))
                    wins.append(y_ref[e0, pl.ds(starts[e0], ROUTE_WIN), :])
                    wins.append(y_ref[e0 + 1, pl.ds(starts[e0 + 1], ROUTE_WIN), :])
                part = _dot(jnp.concatenate(halves, axis=1), jnp.concatenate(wins, axis=0))
                total = part if total is None else total + part
            ffn_ref[rows, :] = total

        @pl.when(jnp.logical_not(ok))
        def _():
            lane = lax.broadcasted_iota(jnp.int32, (ROUTE_TILE, cap), 1)
            total = None
            for e in range(N_EXPERTS):
                onehot = jnp.where(lane == st[:, e:e + 1], 1.0, 0.0).astype(BF16)
                part = _dot(onehot, y_ref[e])
                total = part if total is None else total + part
            ffn_ref[rows, :] = total

    for sub in range(tm // ROUTE_TILE):
        rows = slice(sub * ROUTE_TILE, (sub + 1) * ROUTE_TILE)
        x2 = x1_ref[0, rows, :] + ffn_ref[rows, :]
        g = jax.nn.sigmoid(_dot(_rms(x2, gn_ref[...]).astype(BF16), wgbf_ref[...]))
        emb = _rms(_dot(p_ref[0, rows, :].astype(BF16), wpbf_ref[...]), gp_ref[...])
        o_ref[0, rows, :] = x2 + g * emb


def _combine(off, x1, slot_t, y, p, gn, wg, wp, gp, cap, tm=512):
    b, s, _ = x1.shape
    assert tm % ROUTE_TILE == 0
    tile = lambda w: pl.BlockSpec((1, tm, w), lambda bi, i, off_ref: (bi, i, 0))
    fixed2 = lambda shape: pl.BlockSpec(shape, lambda bi, i, off_ref: (0, 0))
    return pl.pallas_call(
        functools.partial(_combine_kernel, cap=cap),
        grid_spec=pltpu.PrefetchScalarGridSpec(
            num_scalar_prefetch=1,
            grid=(b, s // tm),
            in_specs=[
                tile(D_MODEL),
                tile(LANES),
                pl.BlockSpec((N_EXPERTS, cap, D_MODEL), lambda bi, i, off_ref: (0, bi, 0)),
                tile(PLE_DIM),
                fixed2((1, D_MODEL)),
                fixed2(wg.shape),
                fixed2(wp.shape),
                fixed2((1, D_MODEL)),
            ],
            out_specs=tile(D_MODEL),
            scratch_shapes=[pltpu.VMEM(wg.shape, BF16), pltpu.VMEM(wp.shape, BF16),
                            pltpu.VMEM((tm, D_MODEL), F32)],
        ),
        out_shape=jax.ShapeDtypeStruct((b, s, D_MODEL), F32),
        compiler_params=_params("arbitrary", "arbitrary"),
        name="combine",
    )(off, x1, slot_t, y, p, gn, wg, wp, gp)


def kernel(x, p, norm_mix, w_in, w_pool, pool_scale, q_norm, k_norm, rpb, w_out, norm_ffn, w_router, w_gate, w_up, w_down, norm_ple, w_ple_gate, w_ple_proj, norm_ple_post):
    b, s, d = x.shape
    depth = w_in.shape[0]
    cap = EC_CAPACITY * s // N_EXPERTS
    head =jnp.arange(ATTN_WIDTH) // HEAD_DIM
    block_diag = (head[:, None] == head[None, :]).astype(BF16)
    row = lambda a: a.reshape(1, -1)
    for i in range(depth):
        u, q, k, v = _in_proj(x.reshape(b * s, d), row(norm_mix[i]), w_in[i], block_diag,
                              row(jnp.tile(q_norm[i], ATTN_HEADS)), row(jnp.tile(k_norm[i], ATTN_HEADS)))
        shp = lambda a: a.reshape(b, s, -1)
        y_attn = _natten(shp(q), shp(k), shp(v), _attn_bias_table(rpb[i]))
        x1, h, aff_t = _mix(x, shp(u), y_attn, w_pool[i], row(pool_scale[i]), w_out[i],
                            row(norm_ffn[i]), w_router[i].T)
        slot, slot_t, off = _route(aff_t, cap)
        off = off[:, :OFF_STRIDE].reshape(-1)
        xe, gate = _dispatch(off, slot, aff_t, h, cap)
        y = _experts(xe, gate, w_gate[i], w_up[i], w_down[i])
        x = _combine(off, x1, slot_t, y, p[i], row(norm_ple[i]), w_ple_gate[i], w_ple_proj[i],
                     row(norm_ple_post[i]), cap)
    return x
```

```python
import functools

import jax
import jax.numpy as jnp
from jax import lax
from jax.experimental import pallas as pl
from jax.experimental.pallas import tpu as pltpu

D_MODEL = 1024
GRID_W = 64
POOL_WINDOWS = (2, 4, 8, 16)
POOL_WIDTH = D_MODEL // 2
POOL_GROUP = POOL_WIDTH // len(POOL_WINDOWS)
ATTN_HEADS = 8
HEAD_DIM = (D_MODEL // 2) // ATTN_HEADS
ATTN_WIDTH = ATTN_HEADS * HEAD_DIM
WIN_ROWS_MAX = 8
WIN_COLS = 16
N_EXPERTS = 16
EC_CAPACITY = 2
D_EXPERT = 2 * D_MODEL
PLE_DIM = 256
RMS_EPS = 1e-6

LANES = 128
POOL_HALO = 8
MASK_BIAS = -1e30
NATTEN_UNROLL = 8
ROUTE_TILE = 256
ROUTE_WIN = 64
ROUTE_GROUP = 4
OFF_STRIDE = 16
NOT_SELECTED = -(1 << 20)
assert 2 * ROUTE_WIN == LANES and ROUTE_GROUP % 2 == 0
PROJ_ROWS = 256
EXPERT_ROWS = 512
VMEM_LIMIT = 56 * 1024 * 1024

BF16 = jnp.bfloat16
F32 = jnp.float32


def _params(*sem):
    return pltpu.CompilerParams(dimension_semantics=sem, vmem_limit_bytes=VMEM_LIMIT)


def _rms(x, g):
    return x * lax.rsqrt(jnp.mean(x * x, axis=-1, keepdims=True) + RMS_EPS) * g


def _dot(a, b):
    return jnp.dot(a, b, preferred_element_type=F32)


def _dot_nt(a, b):
    return lax.dot_general(a, b, (((1,), (1,)), ((), ())), preferred_element_type=F32)


def _in_proj_kernel(x_ref, g_ref, w_ref, bd_ref, qg_ref, kg_ref,
                    u_ref, q_ref, k_ref, v_ref, wbf_ref):
    @pl.when(pl.program_id(0) == 0)
    def _():
        wbf_ref[...] = w_ref[...].astype(BF16)

    bd = bd_ref[...]

    def head_norm(t, g):
        ms = _dot((t * t).astype(BF16), bd) * (1.0 / HEAD_DIM)
        return t * lax.rsqrt(ms + RMS_EPS) * g

    groups = [slice(sub * PROJ_ROWS, (sub + 1) * PROJ_ROWS) for sub in range(x_ref.shape[0] // PROJ_ROWS)]
    zs = [_dot(_rms(x_ref[rows, :], g_ref[...]).astype(BF16), wbf_ref[...]) for rows in groups]
    for rows, z in zip(groups, zs):
        u_ref[rows, :] = z[:, :POOL_WIDTH]
        q = z[:, POOL_WIDTH:POOL_WIDTH + ATTN_WIDTH]
        k = z[:, POOL_WIDTH + ATTN_WIDTH:POOL_WIDTH + 2 * ATTN_WIDTH]
        v = z[:, POOL_WIDTH + 2 * ATTN_WIDTH:]
        q_ref[rows, :] = (head_norm(q, qg_ref[...]) * (HEAD_DIM ** -0.5)).astype(BF16)
        k_ref[rows, :] = head_norm(k, kg_ref[...]).astype(BF16)
        v_ref[rows, :] = v.astype(BF16)


def _in_proj(x2, g, w, bd, qg, kg, tm=1024):
    n = x2.shape[0]
    zw = w.shape[1]
    row = lambda i: (i, 0)
    fixed = lambda i: (0, 0)
    return pl.pallas_call(
        _in_proj_kernel,
        grid=(n // tm,),
        in_specs=[
            pl.BlockSpec((tm, D_MODEL), row),
            pl.BlockSpec((1, D_MODEL), fixed),
            pl.BlockSpec((D_MODEL, zw), fixed, pipeline_mode=pl.Buffered(1)),
            pl.BlockSpec((ATTN_WIDTH, ATTN_WIDTH), fixed),
            pl.BlockSpec((1, ATTN_WIDTH), fixed),
            pl.BlockSpec((1, ATTN_WIDTH), fixed),
        ],
        out_specs=[
            pl.BlockSpec((tm, POOL_WIDTH), row),
            pl.BlockSpec((tm, ATTN_WIDTH), row),
            pl.BlockSpec((tm, ATTN_WIDTH), row),
            pl.BlockSpec((tm, ATTN_WIDTH), row),
        ],
        out_shape=[
            jax.ShapeDtypeStruct((n, POOL_WIDTH), F32),
            jax.ShapeDtypeStruct((n, ATTN_WIDTH), BF16),
            jax.ShapeDtypeStruct((n, ATTN_WIDTH), BF16),
            jax.ShapeDtypeStruct((n, ATTN_WIDTH), BF16),
        ],
        scratch_shapes=[pltpu.VMEM((D_MODEL, zw), BF16)],
        compiler_params=_params("arbitrary"),
        name="in_proj",
    )(x2, g, w, bd, qg, kg)


def _natten_kernel(q_ref, k_ref, v_ref, tbl_ref, o_ref, bias_ref, *, rows, kh):
    band = kh * GRID_W
    lane = lax.broadcasted_iota(jnp.int32, (GRID_W, LANES), 1)
    first = lane < HEAD_DIM

    for hh in range(2):
        for d0 in range(WIN_ROWS_MAX):
            for kk in range(kh // 2):
                bias_ref[hh, d0, :, kk * LANES:(kk + 1) * LANES] = jnp.where(
                    lane < GRID_W, tbl_ref[hh, d0 + 2 * kk], tbl_ref[hh, d0 + 2 * kk + 1])

    def row_group(g, carry):
        geo = []
        for sub in range(NATTEN_UNROLL):
            r = g * NATTEN_UNROLL + sub
            rs = jnp.clip(r - kh // 2, 0, rows - kh)
            geo.append((rs - r + (WIN_ROWS_MAX - 1),
                        pl.multiple_of(r * GRID_W, GRID_W),
                        pl.multiple_of(rs * GRID_W, GRID_W)))
        scores = []
        for d0, q0, k0 in geo:
            qr = q_ref[0, pl.ds(q0, GRID_W), :]
            kb = k_ref[0, pl.ds(k0, band), :]
            for hh in range(2):
                sel = first if hh == 0 else jnp.logical_not(first)
                qm = jnp.where(sel, qr, jnp.zeros_like(qr))
                scores.append(_dot_nt(qm, kb) + bias_ref[hh, d0])
        probs = []
        for s in scores:
            e = jnp.exp(s - jnp.max(s, axis=-1, keepdims=True))
            probs.append((e.astype(BF16), jnp.sum(e, axis=-1, keepdims=True)))
        for idx, (d0, q0, k0) in enumerate(geo):
            vb = v_ref[0, pl.ds(k0, band), :]
            outs = [_dot(e, vb) / l for e, l in probs[2 * idx:2 * idx + 2]]
            o_ref[0, pl.ds(q0, GRID_W), :] = jnp.where(first, outs[0], outs[1]).astype(o_ref.dtype)
        return carry

    lax.fori_loop(0, rows // NATTEN_UNROLL, row_group, 0)


def _natten(q, k, v, bias):
    b, s, _ = q.shape
    rows = s // GRID_W
    kh = min(WIN_ROWS_MAX, rows)
    assert rows % NATTEN_UNROLL == 0 and kh % 2 == 0 and 2 * GRID_W == LANES
    pairs = ATTN_WIDTH // LANES
    blk = pl.BlockSpec((1, s, LANES), lambda bi, p: (bi, 0, p))
    return pl.pallas_call(
        functools.partial(_natten_kernel, rows=rows, kh=kh),
        grid=(b, pairs),
        in_specs=[blk, blk, blk,
                  pl.BlockSpec((2,) + bias.shape[1:], lambda bi, p: (p, 0, 0, 0))],
        out_specs=blk,
        out_shape=jax.ShapeDtypeStruct((b, s, ATTN_WIDTH), BF16),
        scratch_shapes=[pltpu.VMEM((2, WIN_ROWS_MAX, GRID_W, kh * GRID_W), F32)],
        compiler_params=_params("arbitrary", "arbitrary"),
        name="natten",
    )(q, k, v, bias)


def _attn_bias_table(rpb):
    c = jnp.arange(GRID_W)
    cs = jnp.clip(c - WIN_COLS // 2, 0, GRID_W - WIN_COLS)
    j = jnp.arange(GRID_W)
    valid = (j[None, :] >= cs[:, None]) & (j[None, :] < cs[:, None] + WIN_COLS)
    dc = j[None, :] - c[:, None] + (WIN_COLS - 1)
    pick = ((dc[None] == jnp.arange(2 * WIN_COLS - 1)[:, None, None]) & valid[None]).astype(F32)
    t = jnp.einsum('hrd,dcj->hrcj', rpb.astype(F32), pick, precision=lax.Precision.HIGHEST)
    t = jnp.where(valid, t, MASK_BIAS)
    return jnp.concatenate([t, t], axis=-1)


def _window_sum(upad, w, tm):
    n = upad.shape[0]
    fwd = upad
    span = 1
    while span < min(w, POOL_HALO):
        fwd = fwd + pltpu.roll(fwd, n - span, axis=0)
        span *= 2
    centre = slice(POOL_HALO, POOL_HALO + tm)
    if w == 2 * POOL_HALO:
        return fwd[0:tm] + fwd[centre]
    return pltpu.roll(fwd, w // 2, axis=0)[centre]


def _mix_kernel(x_ref, u_ref, up_ref, un_ref, ya_ref, ic_ref, wp_ref, ps_ref, wo_ref, g_ref, wr_ref,
                x1_ref, h_ref, aff_ref, upad_ref, wobf_ref, wpbf_ref, *, tm):
    i = pl.program_id(1)

    @pl.when((pl.program_id(0) == 0) & (i == 0))
    def _():
        wobf_ref[...] = wo_ref[...].astype(BF16)
        wpbf_ref[...] = wp_ref[...].astype(BF16)

    upad_ref[0:POOL_HALO, :] = jnp.where(i > 0, up_ref[0], 0.0)
    upad_ref[POOL_HALO:POOL_HALO + tm, :] = u_ref[0]
    upad_ref[POOL_HALO + tm:, :] = jnp.where(i < pl.num_programs(1) - 1, un_ref[0], 0.0)

    ypool = []
    for gi, w in enumerate(POOL_WINDOWS):
        cols = slice(gi * POOL_GROUP, (gi + 1) * POOL_GROUP)
        upad = upad_ref[:, cols]
        inv = jnp.concatenate([
            jnp.where(i == 0, ic_ref[0:POOL_HALO, cols], 1.0 / w),
            jnp.full((tm - 2 * POOL_HALO, POOL_GROUP), 1.0 / w, F32),
            jnp.where(i == pl.num_programs(1) - 1, ic_ref[POOL_HALO:, cols], 1.0 / w)], axis=0)
        d = _window_sum(upad, w, tm) * inv - upad[POOL_HALO:POOL_HALO + tm]
        ypool.append(_dot(d.astype(BF16), wpbf_ref[gi]) * ps_ref[:, cols])
    ypool = jnp.concatenate(ypool, axis=-1).astype(BF16)

    mix = _dot(ypool, wobf_ref[:POOL_WIDTH, :]) + _dot(ya_ref[0], wobf_ref[POOL_WIDTH:, :])
    x1 = x_ref[0] + mix
    x1_ref[0] = x1
    h = _rms(x1, g_ref[...]).astype(BF16)
    h_ref[0] = h
    logits = _dot_nt(wr_ref[...].astype(BF16), h)
    m = jnp.max(logits, axis=0, keepdims=True)
    e = jnp.exp(logits - m)
    aff_ref[0] = e / jnp.sum(e, axis=0, keepdims=True)


def _pool_edge_inverse_counts(s):
    t = jnp.concatenate([jnp.arange(POOL_HALO), jnp.arange(s - POOL_HALO, s)])
    cols = []
    for w in POOL_WINDOWS:
        lo = jnp.clip(t - w // 2, 0, s - 1)
        hi = jnp.clip(t + (w - w // 2) - 1, 0, s - 1)
        inv = 1.0 / (hi - lo + 1).astype(F32)
        cols.append(jnp.broadcast_to(inv[:, None], (2 * POOL_HALO, POOL_GROUP)))
    return jnp.concatenate(cols, axis=-1)


def _mix(x, u, ya, wp, ps, wo, g, wr_t, tm=1024):
    b, s, _ = x.shape
    nt = s // tm
    hb = tm // POOL_HALO
    tile = lambda w: pl.BlockSpec((1, tm, w), lambda bi, i: (bi, i, 0))
    fixed2 = lambda shape: pl.BlockSpec(shape, lambda bi, i: (0, 0))
    return pl.pallas_call(
        functools.partial(_mix_kernel, tm=tm),
        grid=(b, nt),
        in_specs=[
            tile(D_MODEL),
            tile(POOL_WIDTH),
            pl.BlockSpec((1, POOL_HALO, POOL_WIDTH), lambda bi, i: (bi, jnp.maximum(i * hb - 1, 0), 0)),
            pl.BlockSpec((1, POOL_HALO, POOL_WIDTH),
                         lambda bi, i: (bi, jnp.minimum((i + 1) * hb, s // POOL_HALO - 1), 0)),
            tile(ATTN_WIDTH),
            fixed2((2 * POOL_HALO, POOL_WIDTH)),
            pl.BlockSpec(wp.shape, lambda bi, i: (0, 0, 0)),
            fixed2((1, POOL_WIDTH)),
            pl.BlockSpec(wo.shape, lambda bi, i: (0, 0), pipeline_mode=pl.Buffered(1)),
            fixed2((1, D_MODEL)),
            fixed2(wr_t.shape),
        ],
        out_specs=[
            tile(D_MODEL),
            tile(D_MODEL),
            pl.BlockSpec((1, N_EXPERTS, tm), lambda bi, i: (bi, 0, i)),
        ],
        out_shape=[
            jax.ShapeDtypeStruct((b, s, D_MODEL), F32),
            jax.ShapeDtypeStruct((b, s, D_MODEL), BF16),
            jax.ShapeDtypeStruct((b, N_EXPERTS, s), F32),
        ],
        scratch_shapes=[
            pltpu.VMEM((tm + 2 * POOL_HALO, POOL_WIDTH), F32),
            pltpu.VMEM(wo.shape, BF16),
            pltpu.VMEM(wp.shape, BF16),
        ],
        compiler_params=_params("arbitrary", "arbitrary"),
        name="mix",
    )(x, u, u, u, ya, _pool_edge_inverse_counts(s), wp, ps, wo, g, wr_t)


def _lane_cumsum_exclusive(m):
    e, s = m.shape
    r = lax.broadcasted_iota(jnp.int32, (LANES, LANES), 0)
    c = lax.broadcasted_iota(jnp.int32, (LANES, LANES), 1)
    upper = (r < c).astype(BF16)
    carry = jnp.zeros((e, 1), F32)
    out = []
    for blk in range(s // LANES):
        piece = m[:, blk * LANES:(blk + 1) * LANES]
        out.append(_dot(piece.astype(BF16), upper) + carry)
        carry = carry + jnp.sum(piece, axis=-1, keepdims=True)
    return jnp.concatenate(out, axis=-1)


def _route_kernel(aff_ref, slot_ref, slot_t_ref, off_ref, *, cap):
    aff = aff_ref[...]
    capf = jnp.float32(cap)

    def count_ge(cand_bits):
        return jnp.sum((aff >= pltpu.bitcast(cand_bits, F32)).astype(F32), axis=-1, keepdims=True)

    def search(step, ans):
        cand = ans | (jnp.int32(1) << (30 - step))
        return jnp.where(count_ge(cand) >= capf, cand, ans)

    thr = pltpu.bitcast(lax.fori_loop(0, 31, search, jnp.zeros((aff.shape[0], 1), jnp.int32)), F32)
    gt = aff > thr
    eq = aff == thr
    need = capf - jnp.sum(gt.astype(F32), axis=-1, keepdims=True)
    sel = gt | (eq & (_lane_cumsum_exclusive(eq.astype(F32)) < need))
    self = sel.astype(F32)
    slot = jnp.where(sel, _lane_cumsum_exclusive(self), float(NOT_SELECTED))
    slot_ref[...] = slot.astype(jnp.int32)
    pad = jnp.full((LANES - N_EXPERTS, slot.shape[1]), float(NOT_SELECTED), F32)
    for bi in range(slot_t_ref.shape[0]):
        mine = slot[bi * N_EXPERTS:(bi + 1) * N_EXPERTS]
        slot_t_ref[bi] = jnp.concatenate([mine, pad], axis=0).T.astype(jnp.int32)
    lane = lax.broadcasted_iota(jnp.int32, (slot.shape[0], LANES), 1)
    off = jnp.zeros((slot.shape[0], LANES), F32)
    run = jnp.zeros((slot.shape[0], 1), F32)
    for i in range(1, slot.shape[1] // ROUTE_TILE + 1):
        run = run + jnp.sum(self[:, (i - 1) * ROUTE_TILE:i * ROUTE_TILE], axis=-1, keepdims=True)
        off = jnp.where(lane == i, run, off)
    off_ref[...] = off.astype(jnp.int32)


def _route(aff_t, cap):
    b, e, s = aff_t.shape
    assert e == N_EXPERTS and s % ROUTE_TILE == 0 and s // ROUTE_TILE < OFF_STRIDE
    slot, slot_t, off = pl.pallas_call(
        functools.partial(_route_kernel, cap=cap),
        out_shape=[
            jax.ShapeDtypeStruct((b * e, s), jnp.int32),
            jax.ShapeDtypeStruct((b, s, LANES), jnp.int32),
            jax.ShapeDtypeStruct((b * e, LANES), jnp.int32),
        ],
        compiler_params=pltpu.CompilerParams(vmem_limit_bytes=VMEM_LIMIT),
        name="route",
    )(aff_t.reshape(b * e, s))
    return slot.reshape(b, e, s), slot_t, off


def _window_starts(off_ref, bi, ti, cap):
    starts = []
    ok = None
    for e in range(N_EXPERTS):
        base = (bi * N_EXPERTS + e) * OFF_STRIDE + ti
        lo = off_ref[base]
        hi = off_ref[base + 1]
        st = jnp.minimum(jnp.bitwise_and(lo, -16), cap - ROUTE_WIN)
        fits = hi <= st + ROUTE_WIN
        ok = fits if ok is None else jnp.logical_and(ok, fits)
        starts.append(pl.multiple_of(st, 16))
    return starts, ok


def _dispatch_kernel(off_ref, slot_ref, aff_ref, h_ref, xe_ref, gate_ref, *, cap):
    bi = pl.program_id(0)
    ti = pl.program_id(1)

    @pl.when(ti == 0)
    def _():
        xe_ref[...] = jnp.zeros_like(xe_ref)
        gate_ref[...] = jnp.zeros_like(gate_ref)

    slot = slot_ref[0]
    aff = aff_ref[0]
    h = h_ref[0]
    tile = slot.shape[-1]
    starts, ok = _window_starts(off_ref, bi, ti, cap)

    @pl.when(ok)
    def _():
        row = lax.broadcasted_iota(jnp.int32, (ROUTE_WIN, tile), 0)
        hits = [row == (slot[e:e + 1, :] - starts[e]) for e in range(N_EXPERTS)]
        onehot = jnp.concatenate([jnp.where(hh, 1.0, 0.0).astype(BF16) for hh in hits], axis=0)
        res = _dot(onehot, h)
        for e, hh in enumerate(hits):
            win = pl.ds(starts[e], ROUTE_WIN)
            xe_ref[e, win, :] += res[e * ROUTE_WIN:(e + 1) * ROUTE_WIN].astype(BF16)
            gate_ref[e, win, :] += jnp.sum(jnp.where(hh, aff[e:e + 1, :], 0.0), axis=-1, keepdims=True)

    @pl.when(jnp.logical_not(ok))
    def _():
        row = lax.broadcasted_iota(jnp.int32, (cap, tile), 0)
        for e in range(N_EXPERTS):
            hh = row == slot[e:e + 1, :]
            xe_ref[e] += _dot(jnp.where(hh, 1.0, 0.0).astype(BF16), h).astype(BF16)
            gate_ref[e] += jnp.sum(jnp.where(hh, aff[e:e + 1, :], 0.0), axis=-1, keepdims=True)


def _dispatch(off, slot, aff_t, h, cap):
    b, e, s = slot.shape
    rows = pl.BlockSpec((1, e, ROUTE_TILE), lambda bi, ti, off_ref: (bi, 0, ti))
    return pl.pallas_call(
        functools.partial(_dispatch_kernel, cap=cap),
        grid_spec=pltpu.PrefetchScalarGridSpec(
            num_scalar_prefetch=1,
            grid=(b, s // ROUTE_TILE),
            in_specs=[rows, rows,
                      pl.BlockSpec((1, ROUTE_TILE, D_MODEL), lambda bi, ti, off_ref: (bi, ti, 0))],
            out_specs=[
                pl.BlockSpec((e, cap, D_MODEL), lambda bi, ti, off_ref: (0, bi, 0)),
                pl.BlockSpec((e, cap, 1), lambda bi, ti, off_ref: (0, bi, 0)),
            ],
        ),
        out_shape=[
            jax.ShapeDtypeStruct((e, b * cap, D_MODEL), BF16),
            jax.ShapeDtypeStruct((e, b * cap, 1), F32),
        ],
        compiler_params=_params("arbitrary", "arbitrary"),
        name="dispatch",
    )(off, slot, aff_t, h)


def _experts_kernel(xe_ref, gate_ref, wg_ref, wu_ref, wd_ref, y_ref, acc_ref):
    f = pl.program_id(1)
    wg = wg_ref[0].astype(BF16)
    wu = wu_ref[0].astype(BF16)
    wd = wd_ref[0].astype(BF16)

    @pl.when(f == 0)
    def _():
        acc_ref[...] = jnp.zeros_like(acc_ref)

    for mb in range(xe_ref.shape[1] // EXPERT_ROWS):
        rows = slice(mb * EXPERT_ROWS, (mb + 1) * EXPERT_ROWS)
        xe = xe_ref[0, rows, :]
        a = _dot(xe, wg)
        b = _dot(xe, wu)
        hmid = (a * jax.nn.sigmoid(a) * b).astype(BF16)
        acc_ref[rows, :] += _dot(hmid, wd)

    @pl.when(f == pl.num_programs(1) - 1)
    def _():
        y_ref[0] = (acc_ref[...] * gate_ref[0]).astype(y_ref.dtype)


def _experts(xe, gate, wg, wu, wd, tf=512):
    e, m, _ = xe.shape
    nf = D_EXPERT // tf
    return pl.pallas_call(
        _experts_kernel,
        grid=(e, nf),
        in_specs=[
            pl.BlockSpec((1, m, D_MODEL), lambda ei, f: (ei, 0, 0)),
            pl.BlockSpec((1, m, 1), lambda ei, f: (ei, 0, 0)),
            pl.BlockSpec((1, D_MODEL, tf), lambda ei, f: (ei, 0, f)),
            pl.BlockSpec((1, D_MODEL, tf), lambda ei, f: (ei, 0, f)),
            pl.BlockSpec((1, tf, D_MODEL), lambda ei, f: (ei, f, 0)),
        ],
        out_specs=pl.BlockSpec((1, m, D_MODEL), lambda ei, f: (ei, 0, 0)),
        out_shape=jax.ShapeDtypeStruct((e, m, D_MODEL), BF16),
        scratch_shapes=[pltpu.VMEM((m, D_MODEL), F32)],
        compiler_params=_params("arbitrary", "arbitrary"),
        name="experts",
    )(xe, gate, wg, wu, wd)


def _combine_kernel(off_ref, x1_ref, st_ref, y_ref, p_ref, gn_ref, wg_ref, wp_ref, gp_ref,
                    o_ref, wgbf_ref, wpbf_ref, *, cap):
    bi = pl.program_id(0)

    @pl.when((bi == 0) & (pl.program_id(1) == 0))
    def _():
        wgbf_ref[...] = wg_ref[...].astype(BF16)
        wpbf_ref[...] = wp_ref[...].astype(BF16)

    nsub = x1_ref.shape[1] // ROUTE_TILE
    row_groups = [slice(sub * ROUTE_TILE, (sub + 1) * ROUTE_TILE) for sub in range(nsub)]
    geo = [_window_starts(off_ref, bi, pl.program_id(1) * nsub + sub, cap) for sub in range(nsub)]
    ok = functools.reduce(jnp.logical_and, [fits for _, fits in geo])

    def windowed_scatter(rows, starts):
        st = st_ref[0, rows, :]
        lane = lax.broadcasted_iota(jnp.int32, (ROUTE_TILE, LANES), 1)
        low = lane < ROUTE_WIN
        total = None
        for g in range(N_EXPERTS // ROUTE_GROUP):
            halves = []
            wins = []
            for half in range(ROUTE_GROUP // 2):
                e0 = g * ROUTE_GROUP + 2 * half
                t0 = st[:, e0:e0 + 1] - starts[e0]
                t1 = st[:, e0 + 1:e0 + 2] + (ROUTE_WIN - starts[e0 + 1])
                halves.append(jnp.where(lane == jnp.where(low, t0, t1), 1.0, 0.0).astype(BF16))
                wins.append(y_ref[e0, pl.ds(starts[e0], ROUTE_WIN), :])
                wins.append(y_ref[e0 + 1, pl.ds(starts[e0 + 1], ROUTE_WIN), :])
            part = _dot(jnp.concatenate(halves, axis=1), jnp.concatenate(wins, axis=0))
            total = part if total is None else total + part
        return total

    def dense_scatter(rows):
        st = st_ref[0, rows, :]
        lane = lax.broadcasted_iota(jnp.int32, (ROUTE_TILE, cap), 1)
        total = None
        for e in range(N_EXPERTS):
            onehot = jnp.where(lane == st[:, e:e + 1], 1.0, 0.0).astype(BF16)
            part = _dot(onehot, y_ref[e])
            total = part if total is None else total + part
        return total

    def finish(rows, ffn):
        x2 = x1_ref[0, rows, :] + ffn
        g = jax.nn.sigmoid(_dot(_rms(x2, gn_ref[...]).astype(BF16), wgbf_ref[...]))
        emb = _rms(_dot(p_ref[0, rows, :].astype(BF16), wpbf_ref[...]), gp_ref[...])
        o_ref[0, rows, :] = x2 + g * emb

    @pl.when(ok)
    def _():
        for rows, (starts, _) in zip(row_groups, geo):
            finish(rows, windowed_scatter(rows, starts))

    @pl.when(jnp.logical_not(ok))
    def _():
        for rows in row_groups:
            finish(rows, dense_scatter(rows))


def _combine(off, x1, slot_t, y, p, gn, wg, wp, gp, cap, tm=1024):
    b, s, _ = x1.shape
    assert tm % ROUTE_TILE == 0
    tile = lambda w: pl.BlockSpec((1, tm, w), lambda bi, i, off_ref: (bi, i, 0))
    fixed2 = lambda shape: pl.BlockSpec(shape, lambda bi, i, off_ref: (0, 0))
    return pl.pallas_call(
        functools.partial(_combine_kernel, cap=cap),
        grid_spec=pltpu.PrefetchScalarGridSpec(
            num_scalar_prefetch=1,
            grid=(b, s // tm),
            in_specs=[
                tile(D_MODEL),
                tile(LANES),
                pl.BlockSpec((N_EXPERTS, cap, D_MODEL), lambda bi, i, off_ref: (0, bi, 0)),
                tile(PLE_DIM),
                fixed2((1, D_MODEL)),
                pl.BlockSpec(wg.shape, lambda bi, i, off_ref: (0, 0), pipeline_mode=pl.Buffered(1)),
                fixed2(wp.shape),
                fixed2((1, D_MODEL)),
            ],
            out_specs=tile(D_MODEL),
            scratch_shapes=[pltpu.VMEM(wg.shape, BF16), pltpu.VMEM(wp.shape, BF16)],
        ),
        out_shape=jax.ShapeDtypeStruct((b, s, D_MODEL), F32),
        compiler_params=_params("arbitrary", "arbitrary"),
        name="combine",
    )(off, x1, slot_t, y, p, gn, wg, wp, gp)


def kernel(x, p, norm_mix, w_in, w_pool, pool_scale, q_norm, k_norm, rpb, w_out, norm_ffn, w_router, w_gate, w_up, w_down, norm_ple, w_ple_gate, w_ple_proj, norm_ple_post):
    b, s, d = x.shape
    depth = w_in.shape[0]
    cap = EC_CAPACITY * s // N_EXPERTS
    head =jnp.arange(ATTN_WIDTH) // HEAD_DIM
    block_diag = (head[:, None] == head[None, :]).astype(BF16)
    row = lambda a: a.reshape(1, -1)
    for i in range(depth):
        u, q, k, v = _in_proj(x.reshape(b * s, d), row(norm_mix[i]), w_in[i], block_diag,
                              row(jnp.tile(q_norm[i], ATTN_HEADS)), row(jnp.tile(k_norm[i], ATTN_HEADS)))
        shp = lambda a: a.reshape(b, s, -1)
        y_attn = _natten(shp(q), shp(k), shp(v), _attn_bias_table(rpb[i]))
        x1, h, aff_t = _mix(x, shp(u), y_attn, w_pool[i], row(pool_scale[i]), w_out[i],
                            row(norm_ffn[i]), w_router[i].T)
        slot, slot_t, off = _route(aff_t, cap)
        off = off[:, :OFF_STRIDE].reshape(-1)
        xe, gate = _dispatch(off, slot, aff_t, h, cap)
        y = _experts(xe, gate, w_gate[i], w_up[i], w_down[i])
        x = _combine(off, x1, slot_t, y, p[i], row(norm_ple[i]), w_ple_gate[i], w_ple_proj[i],
                     row(norm_ple_post[i]), cap)
    return x
```

```python
import functools

import jax
import jax.numpy as jnp
from jax import lax
from jax.experimental import pallas as pl
from jax.experimental.pallas import tpu as pltpu

D_MODEL = 1024
GRID_W = 64
POOL_WINDOWS = (2, 4, 8, 16)
POOL_WIDTH = D_MODEL // 2
POOL_GROUP = POOL_WIDTH // len(POOL_WINDOWS)
ATTN_HEADS = 8
HEAD_DIM = (D_MODEL // 2) // ATTN_HEADS
ATTN_WIDTH = ATTN_HEADS * HEAD_DIM
WIN_ROWS_MAX = 8
WIN_COLS = 16
N_EXPERTS = 16
EC_CAPACITY = 2
D_EXPERT = 2 * D_MODEL
PLE_DIM = 256
RMS_EPS = 1e-6

LANES = 128
POOL_HALO = 8
MASK_BIAS = -1e30
NATTEN_UNROLL = 8
ROUTE_TILE = 256
ROUTE_WIN = 64
ROUTE_GROUP = 4
OFF_STRIDE = 16
NOT_SELECTED = -(1 << 20)
assert 2 * ROUTE_WIN == LANES and ROUTE_GROUP % 2 == 0
PROJ_ROWS = 256
EXPERT_ROWS = 512
VMEM_LIMIT = 56 * 1024 * 1024

BF16 = jnp.bfloat16
F32 = jnp.float32


def _params(*sem):
    return pltpu.CompilerParams(dimension_semantics=sem, vmem_limit_bytes=VMEM_LIMIT)


def _rms(x, g):
    return x * lax.rsqrt(jnp.mean(x * x, axis=-1, keepdims=True) + RMS_EPS) * g


def _dot(a, b):
    return jnp.dot(a, b, preferred_element_type=F32)


def _dot_nt(a, b):
    return lax.dot_general(a, b, (((1,), (1,)), ((), ())), preferred_element_type=F32)


def _in_proj_kernel(x_ref, g_ref, w_ref, bd_ref, qg_ref, kg_ref,
                    u_ref, q_ref, k_ref, v_ref, wbf_ref):
    @pl.when(pl.program_id(0) == 0)
    def _():
        wbf_ref[...] = w_ref[...].astype(BF16)

    bd = bd_ref[...]

    def head_norm(t, g):
        ms = _dot((t * t).astype(BF16), bd) * (1.0 / HEAD_DIM)
        return t * lax.rsqrt(ms + RMS_EPS) * g

    groups = [slice(sub * PROJ_ROWS, (sub + 1) * PROJ_ROWS) for sub in range(x_ref.shape[0] // PROJ_ROWS)]
    zs = [_dot(_rms(x_ref[rows, :], g_ref[...]).astype(BF16), wbf_ref[...]) for rows in groups]
    for rows, z in zip(groups, zs):
        u_ref[rows, :] = z[:, :POOL_WIDTH]
        q = z[:, POOL_WIDTH:POOL_WIDTH + ATTN_WIDTH]
        k = z[:, POOL_WIDTH + ATTN_WIDTH:POOL_WIDTH + 2 * ATTN_WIDTH]
        v = z[:, POOL_WIDTH + 2 * ATTN_WIDTH:]
        q_ref[rows, :] = (head_norm(q, qg_ref[...]) * (HEAD_DIM ** -0.5)).astype(BF16)
        k_ref[rows, :] = head_norm(k, kg_ref[...]).astype(BF16)
        v_ref[rows, :] = v.astype(BF16)


def _in_proj(x2, g, w, bd, qg, kg, tm=1024):
    n = x2.shape[0]
    zw = w.shape[1]
    row = lambda i: (i, 0)
    fixed = lambda i: (0, 0)
    return pl.pallas_call(
        _in_proj_kernel,
        grid=(n // tm,),
        in_specs=[
            pl.BlockSpec((tm, D_MODEL), row),
            pl.BlockSpec((1, D_MODEL), fixed),
            pl.BlockSpec((D_MODEL, zw), fixed, pipeline_mode=pl.Buffered(1)),
            pl.BlockSpec((ATTN_WIDTH, ATTN_WIDTH), fixed),
            pl.BlockSpec((1, ATTN_WIDTH), fixed),
            pl.BlockSpec((1, ATTN_WIDTH), fixed),
        ],
        out_specs=[
            pl.BlockSpec((tm, POOL_WIDTH), row),
            pl.BlockSpec((tm, ATTN_WIDTH), row),
            pl.BlockSpec((tm, ATTN_WIDTH), row),
            pl.BlockSpec((tm, ATTN_WIDTH), row),
        ],
        out_shape=[
            jax.ShapeDtypeStruct((n, POOL_WIDTH), F32),
            jax.ShapeDtypeStruct((n, ATTN_WIDTH), BF16),
            jax.ShapeDtypeStruct((n, ATTN_WIDTH), BF16),
            jax.ShapeDtypeStruct((n, ATTN_WIDTH), BF16),
        ],
        scratch_shapes=[pltpu.VMEM((D_MODEL, zw), BF16)],
        compiler_params=_params("arbitrary"),
        name="in_proj",
    )(x2, g, w, bd, qg, kg)


def _natten_kernel(q_ref, k_ref, v_ref, tbl_ref, o_ref, bias_ref, *, rows, kh):
    band = kh * GRID_W
    lane = lax.broadcasted_iota(jnp.int32, (GRID_W, LANES), 1)
    first = lane < HEAD_DIM

    for hh in range(2):
        for d0 in range(WIN_ROWS_MAX):
            for kk in range(kh // 2):
                bias_ref[hh, d0, :, kk * LANES:(kk + 1) * LANES] = jnp.where(
                    lane < GRID_W, tbl_ref[hh, d0 + 2 * kk], tbl_ref[hh, d0 + 2 * kk + 1])

    def row_group(g, carry):
        geo = []
        for sub in range(NATTEN_UNROLL):
            r = g * NATTEN_UNROLL + sub
            rs = jnp.clip(r - kh // 2, 0, rows - kh)
            geo.append((rs - r + (WIN_ROWS_MAX - 1),
                        pl.multiple_of(r * GRID_W, GRID_W),
                        pl.multiple_of(rs * GRID_W, GRID_W)))
        scores = []
        for d0, q0, k0 in geo:
            qr = q_ref[0, pl.ds(q0, GRID_W), :]
            kb = k_ref[0, pl.ds(k0, band), :]
            for hh in range(2):
                sel = first if hh == 0 else jnp.logical_not(first)
                qm = jnp.where(sel, qr, jnp.zeros_like(qr))
                scores.append(_dot_nt(qm, kb) + bias_ref[hh, d0])
        probs = []
        for s in scores:
            e = jnp.exp(s - jnp.max(s, axis=-1, keepdims=True))
            probs.append((e.astype(BF16), jnp.sum(e, axis=-1, keepdims=True)))
        for idx, (d0, q0, k0) in enumerate(geo):
            vb = v_ref[0, pl.ds(k0, band), :]
            outs = [_dot(e, vb) / l for e, l in probs[2 * idx:2 * idx + 2]]
            o_ref[0, pl.ds(q0, GRID_W), :] = jnp.where(first, outs[0], outs[1]).astype(o_ref.dtype)
        return carry

    lax.fori_loop(0, rows // NATTEN_UNROLL, row_group, 0)


def _natten(q, k, v, bias):
    b, s, _ = q.shape
    rows = s // GRID_W
    kh = min(WIN_ROWS_MAX, rows)
    assert rows % NATTEN_UNROLL == 0 and kh % 2 == 0 and 2 * GRID_W == LANES
    pairs = ATTN_WIDTH // LANES
    blk = pl.BlockSpec((1, s, LANES), lambda bi, p: (bi, 0, p))
    return pl.pallas_call(
        functools.partial(_natten_kernel, rows=rows, kh=kh),
        grid=(b, pairs),
        in_specs=[blk, blk, blk,
                  pl.BlockSpec((2,) + bias.shape[1:], lambda bi, p: (p, 0, 0, 0))],
        out_specs=blk,
        out_shape=jax.ShapeDtypeStruct((b, s, ATTN_WIDTH), BF16),
        scratch_shapes=[pltpu.VMEM((2, WIN_ROWS_MAX, GRID_W, kh * GRID_W), F32)],
        compiler_params=_params("arbitrary", "arbitrary"),
        name="natten",
    )(q, k, v, bias)


def _attn_bias_table(rpb):
    c = jnp.arange(GRID_W)
    cs = jnp.clip(c - WIN_COLS // 2, 0, GRID_W - WIN_COLS)
    j = jnp.arange(GRID_W)
    valid = (j[None, :] >= cs[:, None]) & (j[None, :] < cs[:, None] + WIN_COLS)
    dc = j[None, :] - c[:, None] + (WIN_COLS - 1)
    pick = ((dc[None] == jnp.arange(2 * WIN_COLS - 1)[:, None, None]) & valid[None]).astype(F32)
    t = jnp.einsum('hrd,dcj->hrcj', rpb.astype(F32), pick, precision=lax.Precision.HIGHEST)
    t = jnp.where(valid, t, MASK_BIAS)
    return jnp.concatenate([t, t], axis=-1)


def _window_sum(upad, w, tm):
    n = upad.shape[0]
    fwd = upad
    span = 1
    while span < min(w, POOL_HALO):
        fwd = fwd + pltpu.roll(fwd, n - span, axis=0)
        span *= 2
    centre = slice(POOL_HALO, POOL_HALO + tm)
    if w == 2 * POOL_HALO:
        return fwd[0:tm] + fwd[centre]
    return pltpu.roll(fwd, w // 2, axis=0)[centre]


def _mix_kernel(x_ref, u_ref, up_ref, un_ref, ya_ref, ic_ref, wp_ref, ps_ref, wo_ref, g_ref, wr_ref,
                x1_ref, h_ref, aff_ref, upad_ref, wobf_ref, wpbf_ref, *, tm):
    i = pl.program_id(1)

    @pl.when((pl.program_id(0) == 0) & (i == 0))
    def _():
        wobf_ref[...] = wo_ref[...].astype(BF16)
        wpbf_ref[...] = wp_ref[...].astype(BF16)

    upad_ref[0:POOL_HALO, :] = jnp.where(i > 0, up_ref[0], 0.0)
    upad_ref[POOL_HALO:POOL_HALO + tm, :] = u_ref[0]
    upad_ref[POOL_HALO + tm:, :] = jnp.where(i < pl.num_programs(1) - 1, un_ref[0], 0.0)

    ypool = []
    for gi, w in enumerate(POOL_WINDOWS):
        cols = slice(gi * POOL_GROUP, (gi + 1) * POOL_GROUP)
        upad = upad_ref[:, cols]
        inv = jnp.concatenate([
            jnp.where(i == 0, ic_ref[0:POOL_HALO, cols], 1.0 / w),
            jnp.full((tm - 2 * POOL_HALO, POOL_GROUP), 1.0 / w, F32),
            jnp.where(i == pl.num_programs(1) - 1, ic_ref[POOL_HALO:, cols], 1.0 / w)], axis=0)
        d = _window_sum(upad, w, tm) * inv - upad[POOL_HALO:POOL_HALO + tm]
        ypool.append(_dot(d.astype(BF16), wpbf_ref[gi]) * ps_ref[:, cols])
    ypool = jnp.concatenate(ypool, axis=-1).astype(BF16)

    mix = _dot(ypool, wobf_ref[:POOL_WIDTH, :]) + _dot(ya_ref[0], wobf_ref[POOL_WIDTH:, :])
    x1 = x_ref[0] + mix
    x1_ref[0] = x1
    h = _rms(x1, g_ref[...]).astype(BF16)
    h_ref[0] = h
    logits = _dot_nt(wr_ref[...].astype(BF16), h)
    m = jnp.max(logits, axis=0, keepdims=True)
    e = jnp.exp(logits - m)
    aff_ref[0] = e / jnp.sum(e, axis=0, keepdims=True)


def _pool_edge_inverse_counts(s):
    t = jnp.concatenate([jnp.arange(POOL_HALO), jnp.arange(s - POOL_HALO, s)])
    cols = []
    for w in POOL_WINDOWS:
        lo = jnp.clip(t - w // 2, 0, s - 1)
        hi = jnp.clip(t + (w - w // 2) - 1, 0, s - 1)
        inv = 1.0 / (hi - lo + 1).astype(F32)
        cols.append(jnp.broadcast_to(inv[:, None], (2 * POOL_HALO, POOL_GROUP)))
    return jnp.concatenate(cols, axis=-1)


def _mix(x, u, ya, wp, ps, wo, g, wr_t, tm=1024):
    b, s, _ = x.shape
    nt = s // tm
    hb = tm // POOL_HALO
    tile = lambda w: pl.BlockSpec((1, tm, w), lambda bi, i: (bi, i, 0))
    fixed2 = lambda shape: pl.BlockSpec(shape, lambda bi, i: (0, 0))
    return pl.pallas_call(
        functools.partial(_mix_kernel, tm=tm),
        grid=(b, nt),
        in_specs=[
            tile(D_MODEL),
            tile(POOL_WIDTH),
            pl.BlockSpec((1, POOL_HALO, POOL_WIDTH), lambda bi, i: (bi, jnp.maximum(i * hb - 1, 0), 0)),
            pl.BlockSpec((1, POOL_HALO, POOL_WIDTH),
                         lambda bi, i: (bi, jnp.minimum((i + 1) * hb, s // POOL_HALO - 1), 0)),
            tile(ATTN_WIDTH),
            fixed2((2 * POOL_HALO, POOL_WIDTH)),
            pl.BlockSpec(wp.shape, lambda bi, i: (0, 0, 0)),
            fixed2((1, POOL_WIDTH)),
            pl.BlockSpec(wo.shape, lambda bi, i: (0, 0), pipeline_mode=pl.Buffered(1)),
            fixed2((1, D_MODEL)),
            fixed2(wr_t.shape),
        ],
        out_specs=[
            tile(D_MODEL),
            tile(D_MODEL),
            pl.BlockSpec((1, N_EXPERTS, tm), lambda bi, i: (bi, 0, i)),
        ],
        out_shape=[
            jax.ShapeDtypeStruct((b, s, D_MODEL), F32),
            jax.ShapeDtypeStruct((b, s, D_MODEL), BF16),
            jax.ShapeDtypeStruct((b, N_EXPERTS, s), F32),
        ],
        scratch_shapes=[
            pltpu.VMEM((tm + 2 * POOL_HALO, POOL_WIDTH), F32),
            pltpu.VMEM(wo.shape, BF16),
            pltpu.VMEM(wp.shape, BF16),
        ],
        compiler_params=_params("arbitrary", "arbitrary"),
        name="mix",
    )(x, u, u, u, ya, _pool_edge_inverse_counts(s), wp, ps, wo, g, wr_t)


def _lane_cumsum_exclusive(m):
    e, s = m.shape
    r = lax.broadcasted_iota(jnp.int32, (LANES, LANES), 0)
    c = lax.broadcasted_iota(jnp.int32, (LANES, LANES), 1)
    upper = (r < c).astype(BF16)
    carry = jnp.zeros((e, 1), F32)
    out = []
    for blk in range(s // LANES):
        piece = m[:, blk * LANES:(blk + 1) * LANES]
        out.append(_dot(piece.astype(BF16), upper) + carry)
        carry = carry + jnp.sum(piece, axis=-1, keepdims=True)
    return jnp.concatenate(out, axis=-1)


def _route_kernel(aff_ref, slot_ref, slot_t_ref, off_ref, *, cap):
    aff = aff_ref[...]
    capf = jnp.float32(cap)

    def count_ge(cand_bits):
        return jnp.sum((aff >= pltpu.bitcast(cand_bits, F32)).astype(F32), axis=-1, keepdims=True)

    def search(step, ans):
        cand = ans | (jnp.int32(1) << (30 - step))
        return jnp.where(count_ge(cand) >= capf, cand, ans)

    thr = pltpu.bitcast(lax.fori_loop(0, 31, search, jnp.zeros((aff.shape[0], 1), jnp.int32)), F32)
    gt = aff > thr
    eq = aff == thr
    need = capf - jnp.sum(gt.astype(F32), axis=-1, keepdims=True)
    sel = gt | (eq & (_lane_cumsum_exclusive(eq.astype(F32)) < need))
    self = sel.astype(F32)
    slot = jnp.where(sel, _lane_cumsum_exclusive(self), float(NOT_SELECTED))
    slot_ref[...] = slot.astype(jnp.int32)
    pad = jnp.full((LANES - N_EXPERTS, slot.shape[1]), float(NOT_SELECTED), F32)
    for bi in range(slot_t_ref.shape[0]):
        mine = slot[bi * N_EXPERTS:(bi + 1) * N_EXPERTS]
        slot_t_ref[bi] = jnp.concatenate([mine, pad], axis=0).T.astype(jnp.int32)
    lane = lax.broadcasted_iota(jnp.int32, (slot.shape[0], LANES), 1)
    off = jnp.zeros((slot.shape[0], LANES), F32)
    run = jnp.zeros((slot.shape[0], 1), F32)
    for i in range(1, slot.shape[1] // ROUTE_TILE + 1):
        run = run + jnp.sum(self[:, (i - 1) * ROUTE_TILE:i * ROUTE_TILE], axis=-1, keepdims=True)
        off = jnp.where(lane == i, run, off)
    off_ref[...] = off.astype(jnp.int32)


def _route(aff_t, cap):
    b, e, s = aff_t.shape
    assert e == N_EXPERTS and s % ROUTE_TILE == 0 and s // ROUTE_TILE < OFF_STRIDE
    slot, slot_t, off = pl.pallas_call(
        functools.partial(_route_kernel, cap=cap),
        out_shape=[
            jax.ShapeDtypeStruct((b * e, s), jnp.int32),
            jax.ShapeDtypeStruct((b, s, LANES), jnp.int32),
            jax.ShapeDtypeStruct((b * e, LANES), jnp.int32),
        ],
        compiler_params=pltpu.CompilerParams(vmem_limit_bytes=VMEM_LIMIT),
        name="route",
    )(aff_t.reshape(b * e, s))
    return slot.reshape(b, e, s), slot_t, off


def _window_starts(off_ref, bi, ti, cap):
    starts = []
    ok = None
    for e in range(N_EXPERTS):
        base = (bi * N_EXPERTS + e) * OFF_STRIDE + ti
        lo = off_ref[base]
        hi = off_ref[base + 1]
        st = jnp.minimum(jnp.bitwise_and(lo, -16), cap - ROUTE_WIN)
        fits = hi <= st + ROUTE_WIN
        ok = fits if ok is None else jnp.logical_and(ok, fits)
        starts.append(pl.multiple_of(st, 16))
    return starts, ok


def _dispatch_kernel(off_ref, slot_ref, aff_ref, h_ref, xe_ref, gate_ref, *, cap):
    bi = pl.program_id(0)
    ti = pl.program_id(1)

    @pl.when(ti == 0)
    def _():
        xe_ref[...] = jnp.zeros_like(xe_ref)
        gate_ref[...] = jnp.zeros_like(gate_ref)

    nsub = h_ref.shape[1] // ROUTE_TILE
    col_groups = [slice(sub * ROUTE_TILE, (sub + 1) * ROUTE_TILE) for sub in range(nsub)]
    geo = [_window_starts(off_ref, bi, ti * nsub + sub, cap) for sub in range(nsub)]
    ok = functools.reduce(jnp.logical_and, [fits for _, fits in geo])

    @pl.when(ok)
    def _():
        row = lax.broadcasted_iota(jnp.int32, (ROUTE_WIN, ROUTE_TILE), 0)
        for cols, (starts, _) in zip(col_groups, geo):
            slot = slot_ref[0, :, cols]
            aff = aff_ref[0, :, cols]
            hits = [row == (slot[e:e + 1, :] - starts[e]) for e in range(N_EXPERTS)]
            onehot = jnp.concatenate([jnp.where(hh, 1.0, 0.0).astype(BF16) for hh in hits], axis=0)
            res = _dot(onehot, h_ref[0, cols, :])
            for e, hh in enumerate(hits):
                win = pl.ds(starts[e], ROUTE_WIN)
                xe_ref[e, win, :] += res[e * ROUTE_WIN:(e + 1) * ROUTE_WIN].astype(BF16)
                gate_ref[e, win, :] += jnp.sum(jnp.where(hh, aff[e:e + 1, :], 0.0), axis=-1, keepdims=True)

    @pl.when(jnp.logical_not(ok))
    def _():
        row = lax.broadcasted_iota(jnp.int32, (cap, ROUTE_TILE), 0)
        for cols in col_groups:
            slot = slot_ref[0, :, cols]
            aff = aff_ref[0, :, cols]
            h = h_ref[0, cols, :]
            for e in range(N_EXPERTS):
                hh = row == slot[e:e + 1, :]
                xe_ref[e] += _dot(jnp.where(hh, 1.0, 0.0).astype(BF16), h).astype(BF16)
                gate_ref[e] += jnp.sum(jnp.where(hh, aff[e:e + 1, :], 0.0), axis=-1, keepdims=True)


def _dispatch(off, slot, aff_t, h, cap, tm=1024):
    b, e, s = slot.shape
    assert tm % ROUTE_TILE == 0
    rows = pl.BlockSpec((1, e, tm), lambda bi, ti, off_ref: (bi, 0, ti))
    return pl.pallas_call(
        functools.partial(_dispatch_kernel, cap=cap),
        grid_spec=pltpu.PrefetchScalarGridSpec(
            num_scalar_prefetch=1,
            grid=(b, s // tm),
            in_specs=[rows, rows,
                      pl.BlockSpec((1, tm, D_MODEL), lambda bi, ti, off_ref: (bi, ti, 0))],
            out_specs=[
                pl.BlockSpec((e, cap, D_MODEL), lambda bi, ti, off_ref: (0, bi, 0)),
                pl.BlockSpec((e, cap, 1), lambda bi, ti, off_ref: (0, bi, 0)),
            ],
        ),
        out_shape=[
            jax.ShapeDtypeStruct((e, b * cap, D_MODEL), BF16),
            jax.ShapeDtypeStruct((e, b * cap, 1), F32),
        ],
        compiler_params=_params("arbitrary", "arbitrary"),
        name="dispatch",
    )(off, slot, aff_t, h)


def _experts_kernel(xe_ref, gate_ref, wg_ref, wu_ref, wd_ref, y_ref, acc_ref):
    f = pl.program_id(1)

    def hidden_chunk(first):
        wg = wg_ref[0].astype(BF16)
        wu = wu_ref[0].astype(BF16)
        wd = wd_ref[0].astype(BF16)
        for mb in range(xe_ref.shape[1] // EXPERT_ROWS):
            rows = slice(mb * EXPERT_ROWS, (mb + 1) * EXPERT_ROWS)
            xe = xe_ref[0, rows, :]
            a = _dot(xe, wg)
            b = _dot(xe, wu)
            part = _dot((a * jax.nn.sigmoid(a) * b).astype(BF16), wd)
            acc_ref[rows, :] = part if first else acc_ref[rows, :] + part

    pl.when(f == 0)(functools.partial(hidden_chunk, True))
    pl.when(f > 0)(functools.partial(hidden_chunk, False))

    @pl.when(f == pl.num_programs(1) - 1)
    def _():
        y_ref[0] = (acc_ref[...] * gate_ref[0]).astype(y_ref.dtype)


def _experts(xe, gate, wg, wu, wd, tf=512):
    e, m, _ = xe.shape
    nf = D_EXPERT // tf
    return pl.pallas_call(
        _experts_kernel,
        grid=(e, nf),
        in_specs=[
            pl.BlockSpec((1, m, D_MODEL), lambda ei, f: (ei, 0, 0)),
            pl.BlockSpec((1, m, 1), lambda ei, f: (ei, 0, 0)),
            pl.BlockSpec((1, D_MODEL, tf), lambda ei, f: (ei, 0, f)),
            pl.BlockSpec((1, D_MODEL, tf), lambda ei, f: (ei, 0, f)),
            pl.BlockSpec((1, tf, D_MODEL), lambda ei, f: (ei, f, 0)),
        ],
        out_specs=pl.BlockSpec((1, m, D_MODEL), lambda ei, f: (ei, 0, 0)),
        out_shape=jax.ShapeDtypeStruct((e, m, D_MODEL), BF16),
        scratch_shapes=[pltpu.VMEM((m, D_MODEL), F32)],
        compiler_params=_params("arbitrary", "arbitrary"),
        name="experts",
    )(xe, gate, wg, wu, wd)


def _combine_kernel(off_ref, x1_ref, st_ref, y_ref, p_ref, gn_ref, wg_ref, wp_ref, gp_ref,
                    o_ref, wgbf_ref, wpbf_ref, *, cap):
    bi = pl.program_id(0)

    @pl.when((bi == 0) & (pl.program_id(1) == 0))
    def _():
        wgbf_ref[...] = wg_ref[...].astype(BF16)
        wpbf_ref[...] = wp_ref[...].astype(BF16)

    nsub = x1_ref.shape[1] // ROUTE_TILE
    row_groups = [slice(sub * ROUTE_TILE, (sub + 1) * ROUTE_TILE) for sub in range(nsub)]
    geo = [_window_starts(off_ref, bi, pl.program_id(1) * nsub + sub, cap) for sub in range(nsub)]
    ok = functools.reduce(jnp.logical_and, [fits for _, fits in geo])

    def windowed_scatter(rows, starts):
        st = st_ref[0, rows, :]
        lane = lax.broadcasted_iota(jnp.int32, (ROUTE_TILE, LANES), 1)
        low = lane < ROUTE_WIN
        total = None
        for g in range(N_EXPERTS // ROUTE_GROUP):
            halves = []
            wins = []
            for half in range(ROUTE_GROUP // 2):
                e0 = g * ROUTE_GROUP + 2 * half
                t0 = st[:, e0:e0 + 1] - starts[e0]
                t1 = st[:, e0 + 1:e0 + 2] + (ROUTE_WIN - starts[e0 + 1])
                halves.append(jnp.where(lane == jnp.where(low, t0, t1), 1.0, 0.0).astype(BF16))
                wins.append(y_ref[e0, pl.ds(starts[e0], ROUTE_WIN), :])
                wins.append(y_ref[e0 + 1, pl.ds(starts[e0 + 1], ROUTE_WIN), :])
            part = _dot(jnp.concatenate(halves, axis=1), jnp.concatenate(wins, axis=0))
            total = part if total is None else total + part
        return total

    def dense_scatter(rows):
        st = st_ref[0, rows, :]
        lane = lax.broadcasted_iota(jnp.int32, (ROUTE_TILE, cap), 1)
        total = None
        for e in range(N_EXPERTS):
            onehot = jnp.where(lane == st[:, e:e + 1], 1.0, 0.0).astype(BF16)
            part = _dot(onehot, y_ref[e])
            total = part if total is None else total + part
        return total

    def finish(rows, ffn):
        x2 = x1_ref[0, rows, :] + ffn
        g = jax.nn.sigmoid(_dot(_rms(x2, gn_ref[...]).astype(BF16), wgbf_ref[...]))
        emb = _rms(_dot(p_ref[0, rows, :].astype(BF16), wpbf_ref[...]), gp_ref[...])
        o_ref[0, rows, :] = x2 + g * emb

    @pl.when(ok)
    def _():
        for rows, (starts, _) in zip(row_groups, geo):
            finish(rows, windowed_scatter(rows, starts))

    @pl.when(jnp.logical_not(ok))
    def _():
        for rows in row_groups:
            finish(rows, dense_scatter(rows))


def _combine(off, x1, slot_t, y, p, gn, wg, wp, gp, cap, tm=1024):
    b, s, _ = x1.shape
    assert tm % ROUTE_TILE == 0
    tile = lambda w: pl.BlockSpec((1, tm, w), lambda bi, i, off_ref: (bi, i, 0))
    fixed2 = lambda shape: pl.BlockSpec(shape, lambda bi, i, off_ref: (0, 0))
    return pl.pallas_call(
        functools.partial(_combine_kernel, cap=cap),
        grid_spec=pltpu.PrefetchScalarGridSpec(
            num_scalar_prefetch=1,
            grid=(b, s // tm),
            in_specs=[
                tile(D_MODEL),
                tile(LANES),
                pl.BlockSpec((N_EXPERTS, cap, D_MODEL), lambda bi, i, off_ref: (0, bi, 0)),
                tile(PLE_DIM),
                fixed2((1, D_MODEL)),
                pl.BlockSpec(wg.shape, lambda bi, i, off_ref: (0, 0), pipeline_mode=pl.Buffered(1)),
                fixed2(wp.shape),
                fixed2((1, D_MODEL)),
            ],
            out_specs=tile(D_MODEL),
            scratch_shapes=[pltpu.VMEM(wg.shape, BF16), pltpu.VMEM(wp.shape, BF16)],
        ),
        out_shape=jax.ShapeDtypeStruct((b, s, D_MODEL), F32),
        compiler_params=_params("arbitrary", "arbitrary"),
        name="combine",
    )(off, x1, slot_t, y, p, gn, wg, wp, gp)


def kernel(x, p, norm_mix, w_in, w_pool, pool_scale, q_norm, k_norm, rpb, w_out, norm_ffn, w_router, w_gate, w_up, w_down, norm_ple, w_ple_gate, w_ple_proj, norm_ple_post):
    b, s, d = x.shape
    depth = w_in.shape[0]
    cap = EC_CAPACITY * s // N_EXPERTS
    head =jnp.arange(ATTN_WIDTH) // HEAD_DIM
    block_diag = (head[:, None] == head[None, :]).astype(BF16)
    row = lambda a: a.reshape(1, -1)
    for i in range(depth):
        u, q, k, v = _in_proj(x.reshape(b * s, d), row(norm_mix[i]), w_in[i], block_diag,
                              row(jnp.tile(q_norm[i], ATTN_HEADS)), row(jnp.tile(k_norm[i], ATTN_HEADS)))
        shp = lambda a: a.reshape(b, s, -1)
        y_attn = _natten(shp(q), shp(k), shp(v), _attn_bias_table(rpb[i]))
        x1, h, aff_t = _mix(x, shp(u), y_attn, w_pool[i], row(pool_scale[i]), w_out[i],
                            row(norm_ffn[i]), w_router[i].T)
        slot, slot_t, off = _route(aff_t, cap)
        off = off[:, :OFF_STRIDE].reshape(-1)
        xe, gate = _dispatch(off, slot, aff_t, h, cap)
        y = _experts(xe, gate, w_gate[i], w_up[i], w_down[i])
        x = _combine(off, x1, slot_t, y, p[i], row(norm_ple[i]), w_ple_gate[i], w_ple_proj[i],
                     row(norm_ple_post[i]), cap)
    return x
```

```python
import functools

import jax
import jax.numpy as jnp
from jax import lax
from jax.experimental import pallas as pl
from jax.experimental.pallas import tpu as pltpu

D_MODEL = 1024
GRID_W = 64
POOL_WINDOWS = (2, 4, 8, 16)
POOL_WIDTH = D_MODEL // 2
POOL_GROUP = POOL_WIDTH // len(POOL_WINDOWS)
ATTN_HEADS = 8
HEAD_DIM = (D_MODEL // 2) // ATTN_HEADS
ATTN_WIDTH = ATTN_HEADS * HEAD_DIM
WIN_ROWS_MAX = 8
WIN_COLS = 16
N_EXPERTS = 16
EC_CAPACITY = 2
D_EXPERT = 2 * D_MODEL
PLE_DIM = 256
RMS_EPS = 1e-6

LANES = 128
POOL_HALO = 8
MASK_BIAS = -1e30
NATTEN_UNROLL = 8
ROUTE_TILE = 256
ROUTE_WIN = 64
ROUTE_GROUP = 4
OFF_STRIDE = 16
NOT_SELECTED = -(1 << 20)
assert 2 * ROUTE_WIN == LANES and ROUTE_GROUP % 2 == 0
PROJ_ROWS = 256
EXPERT_ROWS = 512
VMEM_LIMIT = 56 * 1024 * 1024

BF16 = jnp.bfloat16
F32 = jnp.float32


def _params(*sem):
    return pltpu.CompilerParams(dimension_semantics=sem, vmem_limit_bytes=VMEM_LIMIT)


def _rms(x, g):
    return x * lax.rsqrt(jnp.mean(x * x, axis=-1, keepdims=True) + RMS_EPS) * g


def _dot(a, b):
    return jnp.dot(a, b, preferred_element_type=F32)


def _dot_nt(a, b):
    return lax.dot_general(a, b, (((1,), (1,)), ((), ())), preferred_element_type=F32)


def _in_proj_kernel(x_ref, g_ref, w_ref, bd_ref, qg_ref, kg_ref,
                    u_ref, q_ref, k_ref, v_ref, wbf_ref):
    @pl.when(pl.program_id(0) == 0)
    def _():
        wbf_ref[...] = w_ref[...].astype(BF16)

    bd = bd_ref[...]

    def head_norm(t, g):
        ms = _dot((t * t).astype(BF16), bd) * (1.0 / HEAD_DIM)
        return t * lax.rsqrt(ms + RMS_EPS) * g

    groups = [slice(sub * PROJ_ROWS, (sub + 1) * PROJ_ROWS) for sub in range(x_ref.shape[0] // PROJ_ROWS)]
    zs = [_dot(_rms(x_ref[rows, :], g_ref[...]).astype(BF16), wbf_ref[...]) for rows in groups]
    for rows, z in zip(groups, zs):
        u_ref[rows, :] = z[:, :POOL_WIDTH]
        q = z[:, POOL_WIDTH:POOL_WIDTH + ATTN_WIDTH]
        k = z[:, POOL_WIDTH + ATTN_WIDTH:POOL_WIDTH + 2 * ATTN_WIDTH]
        v = z[:, POOL_WIDTH + 2 * ATTN_WIDTH:]
        q_ref[rows, :] = (head_norm(q, qg_ref[...]) * (HEAD_DIM ** -0.5)).astype(BF16)
        k_ref[rows, :] = head_norm(k, kg_ref[...]).astype(BF16)
        v_ref[rows, :] = v.astype(BF16)


def _in_proj(x2, g, w, bd, qg, kg, tm=1024):
    n = x2.shape[0]
    zw = w.shape[1]
    row = lambda i: (i, 0)
    fixed = lambda i: (0, 0)
    return pl.pallas_call(
        _in_proj_kernel,
        grid=(n // tm,),
        in_specs=[
            pl.BlockSpec((tm, D_MODEL), row),
            pl.BlockSpec((1, D_MODEL), fixed),
            pl.BlockSpec((D_MODEL, zw), fixed, pipeline_mode=pl.Buffered(1)),
            pl.BlockSpec((ATTN_WIDTH, ATTN_WIDTH), fixed),
            pl.BlockSpec((1, ATTN_WIDTH), fixed),
            pl.BlockSpec((1, ATTN_WIDTH), fixed),
        ],
        out_specs=[
            pl.BlockSpec((tm, POOL_WIDTH), row),
            pl.BlockSpec((tm, ATTN_WIDTH), row),
            pl.BlockSpec((tm, ATTN_WIDTH), row),
            pl.BlockSpec((tm, ATTN_WIDTH), row),
        ],
        out_shape=[
            jax.ShapeDtypeStruct((n, POOL_WIDTH), F32),
            jax.ShapeDtypeStruct((n, ATTN_WIDTH), BF16),
            jax.ShapeDtypeStruct((n, ATTN_WIDTH), BF16),
            jax.ShapeDtypeStruct((n, ATTN_WIDTH), BF16),
        ],
        scratch_shapes=[pltpu.VMEM((D_MODEL, zw), BF16)],
        compiler_params=_params("arbitrary"),
        name="in_proj",
    )(x2, g, w, bd, qg, kg)


def _natten_kernel(q_ref, k_ref, v_ref, tbl_ref, o_ref, bias_ref, *, rows, kh):
    band = kh * GRID_W
    lane = lax.broadcasted_iota(jnp.int32, (GRID_W, LANES), 1)
    first = lane < HEAD_DIM

    for hh in range(2):
        for d0 in range(WIN_ROWS_MAX):
            for kk in range(kh // 2):
                bias_ref[hh, d0, :, kk * LANES:(kk + 1) * LANES] = jnp.where(
                    lane < GRID_W, tbl_ref[hh, d0 + 2 * kk], tbl_ref[hh, d0 + 2 * kk + 1])

    def geometry(g):
        geo = []
        for r in range(g * NATTEN_UNROLL, (g + 1) * NATTEN_UNROLL):
            rs = min(max(r - kh // 2, 0), rows - kh)
            geo.append((rs - r + (WIN_ROWS_MAX - 1), r * GRID_W, rs * GRID_W))
        return geo

    def score_stage(g):
        scores = []
        for d0, q0, k0 in geometry(g):
            qr = q_ref[0, q0:q0 + GRID_W, :]
            kb = k_ref[0, k0:k0 + band, :]
            for hh in range(2):
                sel = first if hh == 0 else jnp.logical_not(first)
                qm = jnp.where(sel, qr, jnp.zeros_like(qr))
                scores.append(_dot_nt(qm, kb) + bias_ref[hh, d0])
        return scores

    def softmax_stage(scores):
        probs = []
        for s in scores:
            e = jnp.exp(s - jnp.max(s, axis=-1, keepdims=True))
            probs.append((e.astype(BF16), jnp.sum(e, axis=-1, keepdims=True)))
        return probs

    def value_stage(g, probs):
        for idx, (d0, q0, k0) in enumerate(geometry(g)):
            vb = v_ref[0, k0:k0 + band, :]
            outs = [_dot(e, vb) / l for e, l in probs[2 * idx:2 * idx + 2]]
            o_ref[0, q0:q0 + GRID_W, :] = jnp.where(first, outs[0], outs[1]).astype(o_ref.dtype)

    ngroups = rows // NATTEN_UNROLL
    scores, probs = {}, {}
    for step in range(ngroups + 2):
        if step < ngroups:
            scores[step] = score_stage(step)
        if 0 <= step - 1 < ngroups:
            probs[step - 1] = softmax_stage(scores.pop(step - 1))
        if 0 <= step - 2 < ngroups:
            value_stage(step - 2, probs.pop(step - 2))


def _natten(q, k, v, bias):
    b, s, _ = q.shape
    rows = s // GRID_W
    kh = min(WIN_ROWS_MAX, rows)
    assert rows % NATTEN_UNROLL == 0 and kh % 2 == 0 and 2 * GRID_W == LANES
    pairs = ATTN_WIDTH // LANES
    blk = pl.BlockSpec((1, s, LANES), lambda bi, p: (bi, 0, p))
    return pl.pallas_call(
        functools.partial(_natten_kernel, rows=rows, kh=kh),
        grid=(b, pairs),
        in_specs=[blk, blk, blk,
                  pl.BlockSpec((2,) + bias.shape[1:], lambda bi, p: (p, 0, 0, 0))],
        out_specs=blk,
        out_shape=jax.ShapeDtypeStruct((b, s, ATTN_WIDTH), BF16),
        scratch_shapes=[pltpu.VMEM((2, WIN_ROWS_MAX, GRID_W, kh * GRID_W), F32)],
        compiler_params=_params("arbitrary", "arbitrary"),
        name="natten",
    )(q, k, v, bias)


def _attn_bias_table(rpb):
    c = jnp.arange(GRID_W)
    cs = jnp.clip(c - WIN_COLS // 2, 0, GRID_W - WIN_COLS)
    j = jnp.arange(GRID_W)
    valid = (j[None, :] >= cs[:, None]) & (j[None, :] < cs[:, None] + WIN_COLS)
    dc = j[None, :] - c[:, None] + (WIN_COLS - 1)
    pick = ((dc[None] == jnp.arange(2 * WIN_COLS - 1)[:, None, None]) & valid[None]).astype(F32)
    t = jnp.einsum('hrd,dcj->hrcj', rpb.astype(F32), pick, precision=lax.Precision.HIGHEST)
    t = jnp.where(valid, t, MASK_BIAS)
    return jnp.concatenate([t, t], axis=-1)


def _window_sum(upad, w, tm):
    n = upad.shape[0]
    fwd = upad
    span = 1
    while span < min(w, POOL_HALO):
        fwd = fwd + pltpu.roll(fwd, n - span, axis=0)
        span *= 2
    centre = slice(POOL_HALO, POOL_HALO + tm)
    if w == 2 * POOL_HALO:
        return fwd[0:tm] + fwd[centre]
    return pltpu.roll(fwd, w // 2, axis=0)[centre]


def _mix_kernel(x_ref, u_ref, up_ref, un_ref, ya_ref, ic_ref, wp_ref, ps_ref, wo_ref, g_ref, wr_ref,
                x1_ref, h_ref, aff_ref, upad_ref, wobf_ref, wpbf_ref, *, tm):
    i = pl.program_id(1)

    @pl.when((pl.program_id(0) == 0) & (i == 0))
    def _():
        wobf_ref[...] = wo_ref[...].astype(BF16)
        wpbf_ref[...] = wp_ref[...].astype(BF16)

    upad_ref[0:POOL_HALO, :] = jnp.where(i > 0, up_ref[0], 0.0)
    upad_ref[POOL_HALO:POOL_HALO + tm, :] = u_ref[0]
    upad_ref[POOL_HALO + tm:, :] = jnp.where(i < pl.num_programs(1) - 1, un_ref[0], 0.0)

    ypool = []
    for gi, w in enumerate(POOL_WINDOWS):
        cols = slice(gi * POOL_GROUP, (gi + 1) * POOL_GROUP)
        upad = upad_ref[:, cols]
        inv = jnp.concatenate([
            jnp.where(i == 0, ic_ref[0:POOL_HALO, cols], 1.0 / w),
            jnp.full((tm - 2 * POOL_HALO, POOL_GROUP), 1.0 / w, F32),
            jnp.where(i == pl.num_programs(1) - 1, ic_ref[POOL_HALO:, cols], 1.0 / w)], axis=0)
        d = _window_sum(upad, w, tm) * inv - upad[POOL_HALO:POOL_HALO + tm]
        ypool.append(_dot(d.astype(BF16), wpbf_ref[gi]) * ps_ref[:, cols])
    ypool = jnp.concatenate(ypool, axis=-1).astype(BF16)

    mix = _dot(ypool, wobf_ref[:POOL_WIDTH, :]) + _dot(ya_ref[0], wobf_ref[POOL_WIDTH:, :])
    x1 = x_ref[0] + mix
    x1_ref[0] = x1
    h = _rms(x1, g_ref[...]).astype(BF16)
    h_ref[0] = h
    logits = _dot_nt(wr_ref[...].astype(BF16), h)
    m = jnp.max(logits, axis=0, keepdims=True)
    e = jnp.exp(logits - m)
    aff_ref[0] = e / jnp.sum(e, axis=0, keepdims=True)


def _pool_edge_inverse_counts(s):
    t = jnp.concatenate([jnp.arange(POOL_HALO), jnp.arange(s - POOL_HALO, s)])
    cols = []
    for w in POOL_WINDOWS:
        lo = jnp.clip(t - w // 2, 0, s - 1)
        hi = jnp.clip(t + (w - w // 2) - 1, 0, s - 1)
        inv = 1.0 / (hi - lo + 1).astype(F32)
        cols.append(jnp.broadcast_to(inv[:, None], (2 * POOL_HALO, POOL_GROUP)))
    return jnp.concatenate(cols, axis=-1)


def _mix(x, u, ya, wp, ps, wo, g, wr_t, tm=1024):
    b, s, _ = x.shape
    nt = s // tm
    hb = tm // POOL_HALO
    tile = lambda w: pl.BlockSpec((1, tm, w), lambda bi, i: (bi, i, 0))
    fixed2 = lambda shape: pl.BlockSpec(shape, lambda bi, i: (0, 0))
    return pl.pallas_call(
        functools.partial(_mix_kernel, tm=tm),
        grid=(b, nt),
        in_specs=[
            tile(D_MODEL),
            tile(POOL_WIDTH),
            pl.BlockSpec((1, POOL_HALO, POOL_WIDTH), lambda bi, i: (bi, jnp.maximum(i * hb - 1, 0), 0)),
            pl.BlockSpec((1, POOL_HALO, POOL_WIDTH),
                         lambda bi, i: (bi, jnp.minimum((i + 1) * hb, s // POOL_HALO - 1), 0)),
            tile(ATTN_WIDTH),
            fixed2((2 * POOL_HALO, POOL_WIDTH)),
            pl.BlockSpec(wp.shape, lambda bi, i: (0, 0, 0)),
            fixed2((1, POOL_WIDTH)),
            pl.BlockSpec(wo.shape, lambda bi, i: (0, 0), pipeline_mode=pl.Buffered(1)),
            fixed2((1, D_MODEL)),
            fixed2(wr_t.shape),
        ],
        out_specs=[
            tile(D_MODEL),
            tile(D_MODEL),
            pl.BlockSpec((1, N_EXPERTS, tm), lambda bi, i: (bi, 0, i)),
        ],
        out_shape=[
            jax.ShapeDtypeStruct((b, s, D_MODEL), F32),
            jax.ShapeDtypeStruct((b, s, D_MODEL), BF16),
            jax.ShapeDtypeStruct((b, N_EXPERTS, s), F32),
        ],
        scratch_shapes=[
            pltpu.VMEM((tm + 2 * POOL_HALO, POOL_WIDTH), F32),
            pltpu.VMEM(wo.shape, BF16),
            pltpu.VMEM(wp.shape, BF16),
        ],
        compiler_params=_params("arbitrary", "arbitrary"),
        name="mix",
    )(x, u, u, u, ya, _pool_edge_inverse_counts(s), wp, ps, wo, g, wr_t)


def _lane_cumsum_exclusive(m):
    e, s = m.shape
    r = lax.broadcasted_iota(jnp.int32, (LANES, LANES), 0)
    c = lax.broadcasted_iota(jnp.int32, (LANES, LANES), 1)
    upper = (r < c).astype(BF16)
    carry = jnp.zeros((e, 1), F32)
    out = []
    for blk in range(s // LANES):
        piece = m[:, blk * LANES:(blk + 1) * LANES]
        out.append(_dot(piece.astype(BF16), upper) + carry)
        carry = carry + jnp.sum(piece, axis=-1, keepdims=True)
    return jnp.concatenate(out, axis=-1)


def _route_kernel(aff_ref, slot_ref, slot_t_ref, off_ref, *, cap):
    aff = aff_ref[...]
    capf = jnp.float32(cap)

    def count_ge(cand_bits):
        return jnp.sum((aff >= pltpu.bitcast(cand_bits, F32)).astype(F32), axis=-1, keepdims=True)

    def search(step, ans):
        cand = ans | (jnp.int32(1) << (30 - step))
        return jnp.where(count_ge(cand) >= capf, cand, ans)

    thr = pltpu.bitcast(lax.fori_loop(0, 31, search, jnp.zeros((aff.shape[0], 1), jnp.int32)), F32)
    gt = aff > thr
    eq = aff == thr
    need = capf - jnp.sum(gt.astype(F32), axis=-1, keepdims=True)
    sel = gt | (eq & (_lane_cumsum_exclusive(eq.astype(F32)) < need))
    self = sel.astype(F32)
    slot = jnp.where(sel, _lane_cumsum_exclusive(self), float(NOT_SELECTED))
    slot_ref[...] = slot.astype(jnp.int32)
    pad = jnp.full((LANES - N_EXPERTS, slot.shape[1]), float(NOT_SELECTED), F32)
    for bi in range(slot_t_ref.shape[0]):
        mine = slot[bi * N_EXPERTS:(bi + 1) * N_EXPERTS]
        slot_t_ref[bi] = jnp.concatenate([mine, pad], axis=0).T.astype(jnp.int32)
    lane = lax.broadcasted_iota(jnp.int32, (slot.shape[0], LANES), 1)
    off = jnp.zeros((slot.shape[0], LANES), F32)
    run = jnp.zeros((slot.shape[0], 1), F32)
    for i in range(1, slot.shape[1] // ROUTE_TILE + 1):
        run = run + jnp.sum(self[:, (i - 1) * ROUTE_TILE:i * ROUTE_TILE], axis=-1, keepdims=True)
        off = jnp.where(lane == i, run, off)
    off_ref[...] = off.astype(jnp.int32)


def _route(aff_t, cap):
    b, e, s = aff_t.shape
    assert e == N_EXPERTS and s % ROUTE_TILE == 0 and s // ROUTE_TILE < OFF_STRIDE
    slot, slot_t, off = pl.pallas_call(
        functools.partial(_route_kernel, cap=cap),
        out_shape=[
            jax.ShapeDtypeStruct((b * e, s), jnp.int32),
            jax.ShapeDtypeStruct((b, s, LANES), jnp.int32),
            jax.ShapeDtypeStruct((b * e, LANES), jnp.int32),
        ],
        compiler_params=pltpu.CompilerParams(vmem_limit_bytes=VMEM_LIMIT),
        name="route",
    )(aff_t.reshape(b * e, s))
    return slot.reshape(b, e, s), slot_t, off


def _window_starts(off_ref, bi, ti, cap):
    starts = []
    ok = None
    for e in range(N_EXPERTS):
        base = (bi * N_EXPERTS + e) * OFF_STRIDE + ti
        lo = off_ref[base]
        hi = off_ref[base + 1]
        st = jnp.minimum(jnp.bitwise_and(lo, -16), cap - ROUTE_WIN)
        fits = hi <= st + ROUTE_WIN
        ok = fits if ok is None else jnp.logical_and(ok, fits)
        starts.append(pl.multiple_of(st, 16))
    return starts, ok


def _dispatch_kernel(off_ref, slot_ref, aff_ref, h_ref, xe_ref, gate_ref, *, cap):
    bi = pl.program_id(0)
    ti = pl.program_id(1)

    @pl.when(ti == 0)
    def _():
        xe_ref[...] = jnp.zeros_like(xe_ref)
        gate_ref[...] = jnp.zeros_like(gate_ref)

    nsub = h_ref.shape[1] // ROUTE_TILE
    col_groups = [slice(sub * ROUTE_TILE, (sub + 1) * ROUTE_TILE) for sub in range(nsub)]
    geo = [_window_starts(off_ref, bi, ti * nsub + sub, cap) for sub in range(nsub)]
    ok = functools.reduce(jnp.logical_and, [fits for _, fits in geo])

    @pl.when(ok)
    def _():
        row = lax.broadcasted_iota(jnp.int32, (ROUTE_WIN, ROUTE_TILE), 0)
        for cols, (starts, _) in zip(col_groups, geo):
            slot = slot_ref[0, :, cols]
            aff = aff_ref[0, :, cols]
            hits = [row == (slot[e:e + 1, :] - starts[e]) for e in range(N_EXPERTS)]
            onehot = jnp.concatenate([jnp.where(hh, 1.0, 0.0).astype(BF16) for hh in hits], axis=0)
            res = _dot(onehot, h_ref[0, cols, :])
            for e, hh in enumerate(hits):
                win = pl.ds(starts[e], ROUTE_WIN)
                xe_ref[e, win, :] += res[e * ROUTE_WIN:(e + 1) * ROUTE_WIN].astype(BF16)
                gate_ref[e, win, :] += jnp.sum(jnp.where(hh, aff[e:e + 1, :], 0.0), axis=-1, keepdims=True)

    @pl.when(jnp.logical_not(ok))
    def _():
        row = lax.broadcasted_iota(jnp.int32, (cap, ROUTE_TILE), 0)
        for cols in col_groups:
            slot = slot_ref[0, :, cols]
            aff = aff_ref[0, :, cols]
            h = h_ref[0, cols, :]
            for e in range(N_EXPERTS):
                hh = row == slot[e:e + 1, :]
                xe_ref[e] += _dot(jnp.where(hh, 1.0, 0.0).astype(BF16), h).astype(BF16)
                gate_ref[e] += jnp.sum(jnp.where(hh, aff[e:e + 1, :], 0.0), axis=-1, keepdims=True)


def _dispatch(off, slot, aff_t, h, cap, tm=1024):
    b, e, s = slot.shape
    assert tm % ROUTE_TILE == 0
    rows = pl.BlockSpec((1, e, tm), lambda bi, ti, off_ref: (bi, 0, ti))
    return pl.pallas_call(
        functools.partial(_dispatch_kernel, cap=cap),
        grid_spec=pltpu.PrefetchScalarGridSpec(
            num_scalar_prefetch=1,
            grid=(b, s // tm),
            in_specs=[rows, rows,
                      pl.BlockSpec((1, tm, D_MODEL), lambda bi, ti, off_ref: (bi, ti, 0))],
            out_specs=[
                pl.BlockSpec((e, cap, D_MODEL), lambda bi, ti, off_ref: (0, bi, 0)),
                pl.BlockSpec((e, cap, 1), lambda bi, ti, off_ref: (0, bi, 0)),
            ],
        ),
        out_shape=[
            jax.ShapeDtypeStruct((e, b * cap, D_MODEL), BF16),
            jax.ShapeDtypeStruct((e, b * cap, 1), F32),
        ],
        compiler_params=_params("arbitrary", "arbitrary"),
        name="dispatch",
    )(off, slot, aff_t, h)


def _experts_kernel(xe_ref, gate_ref, wg_ref, wu_ref, wd_ref, y_ref, acc_ref):
    f = pl.program_id(1)

    def hidden_chunk(first):
        wg = wg_ref[0].astype(BF16)
        wu = wu_ref[0].astype(BF16)
        wd = wd_ref[0].astype(BF16)
        for mb in range(xe_ref.shape[1] // EXPERT_ROWS):
            rows = slice(mb * EXPERT_ROWS, (mb + 1) * EXPERT_ROWS)
            xe = xe_ref[0, rows, :]
            a = _dot(xe, wg)
            b = _dot(xe, wu)
            part = _dot((a * jax.nn.sigmoid(a) * b).astype(BF16), wd)
            acc_ref[rows, :] = part if first else acc_ref[rows, :] + part

    pl.when(f == 0)(functools.partial(hidden_chunk, True))
    pl.when(f > 0)(functools.partial(hidden_chunk, False))

    @pl.when(f == pl.num_programs(1) - 1)
    def _():
        y_ref[0] = (acc_ref[...] * gate_ref[0]).astype(y_ref.dtype)


def _experts(xe, gate, wg, wu, wd, tf=512):
    e, m, _ = xe.shape
    nf = D_EXPERT // tf
    return pl.pallas_call(
        _experts_kernel,
        grid=(e, nf),
        in_specs=[
            pl.BlockSpec((1, m, D_MODEL), lambda ei, f: (ei, 0, 0)),
            pl.BlockSpec((1, m, 1), lambda ei, f: (ei, 0, 0)),
            pl.BlockSpec((1, D_MODEL, tf), lambda ei, f: (ei, 0, f)),
            pl.BlockSpec((1, D_MODEL, tf), lambda ei, f: (ei, 0, f)),
            pl.BlockSpec((1, tf, D_MODEL), lambda ei, f: (ei, f, 0)),
        ],
        out_specs=pl.BlockSpec((1, m, D_MODEL), lambda ei, f: (ei, 0, 0)),
        out_shape=jax.ShapeDtypeStruct((e, m, D_MODEL), BF16),
        scratch_shapes=[pltpu.VMEM((m, D_MODEL), F32)],
        compiler_params=_params("arbitrary", "arbitrary"),
        name="experts",
    )(xe, gate, wg, wu, wd)


def _combine_kernel(off_ref, x1_ref, st_ref, y_ref, p_ref, gn_ref, wg_ref, wp_ref, gp_ref,
                    o_ref, wgbf_ref, wpbf_ref, *, cap):
    bi = pl.program_id(0)

    @pl.when((bi == 0) & (pl.program_id(1) == 0))
    def _():
        wgbf_ref[...] = wg_ref[...].astype(BF16)
        wpbf_ref[...] = wp_ref[...].astype(BF16)

    nsub = x1_ref.shape[1] // ROUTE_TILE
    row_groups = [slice(sub * ROUTE_TILE, (sub + 1) * ROUTE_TILE) for sub in range(nsub)]
    geo = [_window_starts(off_ref, bi, pl.program_id(1) * nsub + sub, cap) for sub in range(nsub)]
    ok = functools.reduce(jnp.logical_and, [fits for _, fits in geo])

    def windowed_scatter(rows, starts):
        st = st_ref[0, rows, :]
        lane = lax.broadcasted_iota(jnp.int32, (ROUTE_TILE, LANES), 1)
        low = lane < ROUTE_WIN
        total = None
        for g in range(N_EXPERTS // ROUTE_GROUP):
            halves = []
            wins = []
            for half in range(ROUTE_GROUP // 2):
                e0 = g * ROUTE_GROUP + 2 * half
                t0 = st[:, e0:e0 + 1] - starts[e0]
                t1 = st[:, e0 + 1:e0 + 2] + (ROUTE_WIN - starts[e0 + 1])
                halves.append(jnp.where(lane == jnp.where(low, t0, t1), 1.0, 0.0).astype(BF16))
                wins.append(y_ref[e0, pl.ds(starts[e0], ROUTE_WIN), :])
                wins.append(y_ref[e0 + 1, pl.ds(starts[e0 + 1], ROUTE_WIN), :])
            part = _dot(jnp.concatenate(halves, axis=1), jnp.concatenate(wins, axis=0))
            total = part if total is None else total + part
        return total

    def dense_scatter(rows):
        st = st_ref[0, rows, :]
        lane = lax.broadcasted_iota(jnp.int32, (ROUTE_TILE, cap), 1)
        total = None
        for e in range(N_EXPERTS):
            onehot = jnp.where(lane == st[:, e:e + 1], 1.0, 0.0).astype(BF16)
            part = _dot(onehot, y_ref[e])
            total = part if total is None else total + part
        return total

    def finish(rows, ffn):
        x2 = x1_ref[0, rows, :] + ffn
        g = jax.nn.sigmoid(_dot(_rms(x2, gn_ref[...]).astype(BF16), wgbf_ref[...]))
        emb = _rms(_dot(p_ref[0, rows, :].astype(BF16), wpbf_ref[...]), gp_ref[...])
        o_ref[0, rows, :] = x2 + g * emb

    @pl.when(ok)
    def _():
        for rows, (starts, _) in zip(row_groups, geo):
            finish(rows, windowed_scatter(rows, starts))

    @pl.when(jnp.logical_not(ok))
    def _():
        for rows in row_groups:
            finish(rows, dense_scatter(rows))


def _combine(off, x1, slot_t, y, p, gn, wg, wp, gp, cap, tm=1024):
    b, s, _ = x1.shape
    assert tm % ROUTE_TILE == 0
    tile = lambda w: pl.BlockSpec((1, tm, w), lambda bi, i, off_ref: (bi, i, 0))
    fixed2 = lambda shape: pl.BlockSpec(shape, lambda bi, i, off_ref: (0, 0))
    return pl.pallas_call(
        functools.partial(_combine_kernel, cap=cap),
        grid_spec=pltpu.PrefetchScalarGridSpec(
            num_scalar_prefetch=1,
            grid=(b, s // tm),
            in_specs=[
                tile(D_MODEL),
                tile(LANES),
                pl.BlockSpec((N_EXPERTS, cap, D_MODEL), lambda bi, i, off_ref: (0, bi, 0)),
                tile(PLE_DIM),
                fixed2((1, D_MODEL)),
                pl.BlockSpec(wg.shape, lambda bi, i, off_ref: (0, 0), pipeline_mode=pl.Buffered(1)),
                fixed2(wp.shape),
                fixed2((1, D_MODEL)),
            ],
            out_specs=tile(D_MODEL),
            scratch_shapes=[pltpu.VMEM(wg.shape, BF16), pltpu.VMEM(wp.shape, BF16)],
        ),
        out_shape=jax.ShapeDtypeStruct((b, s, D_MODEL), F32),
        compiler_params=_params("arbitrary", "arbitrary"),
        name="combine",
    )(off, x1, slot_t, y, p, gn, wg, wp, gp)


def kernel(x, p, norm_mix, w_in, w_pool, pool_scale, q_norm, k_norm, rpb, w_out, norm_ffn, w_router, w_gate, w_up, w_down, norm_ple, w_ple_gate, w_ple_proj, norm_ple_post):
    b, s, d = x.shape
    depth = w_in.shape[0]
    cap = EC_CAPACITY * s // N_EXPERTS
    head =jnp.arange(ATTN_WIDTH) // HEAD_DIM
    block_diag = (head[:, None] == head[None, :]).astype(BF16)
    row = lambda a: a.reshape(1, -1)
    for i in range(depth):
        u, q, k, v = _in_proj(x.reshape(b * s, d), row(norm_mix[i]), w_in[i], block_diag,
                              row(jnp.tile(q_norm[i], ATTN_HEADS)), row(jnp.tile(k_norm[i], ATTN_HEADS)))
        shp = lambda a: a.reshape(b, s, -1)
        y_attn = _natten(shp(q), shp(k), shp(v), _attn_bias_table(rpb[i]))
        x1, h, aff_t = _mix(x, shp(u), y_attn, w_pool[i], row(pool_scale[i]), w_out[i],
                            row(norm_ffn[i]), w_router[i].T)
        slot, slot_t, off = _route(aff_t, cap)
        off = off[:, :OFF_STRIDE].reshape(-1)
        xe, gate = _dispatch(off, slot, aff_t, h, cap)
        y = _experts(xe, gate, w_gate[i], w_up[i], w_down[i])
        x = _combine(off, x1, slot_t, y, p[i], row(norm_ple[i]), w_ple_gate[i], w_ple_proj[i],
                     row(norm_ple_post[i]), cap)
    return x
```

```python
import functools

import jax
import jax.numpy as jnp
from jax import lax
from jax.experimental import pallas as pl
from jax.experimental.pallas import tpu as pltpu

D_MODEL = 1024
GRID_W = 64
POOL_WINDOWS = (2, 4, 8, 16)
POOL_WIDTH = D_MODEL // 2
POOL_GROUP = POOL_WIDTH // len(POOL_WINDOWS)
ATTN_HEADS = 8
HEAD_DIM = (D_MODEL // 2) // ATTN_HEADS
ATTN_WIDTH = ATTN_HEADS * HEAD_DIM
WIN_ROWS_MAX = 8
WIN_COLS = 16
N_EXPERTS = 16
EC_CAPACITY = 2
D_EXPERT = 2 * D_MODEL
PLE_DIM = 256
RMS_EPS = 1e-6

LANES = 128
POOL_HALO = 8
LOG2E = 1.4426950408889634
MASK_BIAS = -1e30
NATTEN_UNROLL = 8
ROUTE_TILE = 256
ROUTE_WIN = 64
ROUTE_GROUP = 4
OFF_STRIDE = 16
NOT_SELECTED = -(1 << 20)
assert 2 * ROUTE_WIN == LANES and ROUTE_GROUP % 2 == 0
PROJ_ROWS = 256
EXPERT_ROWS = 512
VMEM_LIMIT = 56 * 1024 * 1024

BF16 = jnp.bfloat16
F32 = jnp.float32


def _params(*sem):
    return pltpu.CompilerParams(dimension_semantics=sem, vmem_limit_bytes=VMEM_LIMIT)


def _rms(x, g):
    return x * lax.rsqrt(jnp.mean(x * x, axis=-1, keepdims=True) + RMS_EPS) * g


def _dot(a, b):
    return jnp.dot(a, b, preferred_element_type=F32)


def _dot_nt(a, b):
    return lax.dot_general(a, b, (((1,), (1,)), ((), ())), preferred_element_type=F32)


def _in_proj_kernel(x_ref, g_ref, w_ref, bd_ref, qg_ref, kg_ref,
                    u_ref, q_ref, k_ref, v_ref, wbf_ref):
    @pl.when(pl.program_id(0) == 0)
    def _():
        wbf_ref[...] = w_ref[...].astype(BF16)

    bd = bd_ref[...]

    def head_norm(t, g):
        ms = _dot((t * t).astype(BF16), bd) * (1.0 / HEAD_DIM)
        return t * lax.rsqrt(ms + RMS_EPS) * g

    def norm_stage(rows):
        return _rms(x_ref[rows, :], g_ref[...]).astype(BF16)

    def proj_stage(h):
        return _dot(h, wbf_ref[...])

    def head_stage(rows, z):
        u_ref[rows, :] = z[:, :POOL_WIDTH]
        q = z[:, POOL_WIDTH:POOL_WIDTH + ATTN_WIDTH]
        k = z[:, POOL_WIDTH + ATTN_WIDTH:POOL_WIDTH + 2 * ATTN_WIDTH]
        v = z[:, POOL_WIDTH + 2 * ATTN_WIDTH:]
        q_ref[rows, :] = (head_norm(q, qg_ref[...]) * (HEAD_DIM ** -0.5 * LOG2E)).astype(BF16)
        k_ref[rows, :] = head_norm(k, kg_ref[...]).astype(BF16)
        v_ref[rows, :] = v.astype(BF16)

    groups = [slice(sub * PROJ_ROWS, (sub + 1) * PROJ_ROWS) for sub in range(x_ref.shape[0] // PROJ_ROWS)]
    zs = [proj_stage(norm_stage(rows)) for rows in groups]
    for rows, z in zip(groups, zs):
        head_stage(rows, z)


def _in_proj(x2, g, w, bd, qg, kg, tm=1024):
    n = x2.shape[0]
    zw = w.shape[1]
    row = lambda i: (i, 0)
    fixed = lambda i: (0, 0)
    return pl.pallas_call(
        _in_proj_kernel,
        grid=(n // tm,),
        in_specs=[
            pl.BlockSpec((tm, D_MODEL), row),
            pl.BlockSpec((1, D_MODEL), fixed),
            pl.BlockSpec((D_MODEL, zw), fixed, pipeline_mode=pl.Buffered(1)),
            pl.BlockSpec((ATTN_WIDTH, ATTN_WIDTH), fixed),
            pl.BlockSpec((1, ATTN_WIDTH), fixed),
            pl.BlockSpec((1, ATTN_WIDTH), fixed),
        ],
        out_specs=[
            pl.BlockSpec((tm, POOL_WIDTH), row),
            pl.BlockSpec((tm, ATTN_WIDTH), row),
            pl.BlockSpec((tm, ATTN_WIDTH), row),
            pl.BlockSpec((tm, ATTN_WIDTH), row),
        ],
        out_shape=[
            jax.ShapeDtypeStruct((n, POOL_WIDTH), F32),
            jax.ShapeDtypeStruct((n, ATTN_WIDTH), BF16),
            jax.ShapeDtypeStruct((n, ATTN_WIDTH), BF16),
            jax.ShapeDtypeStruct((n, ATTN_WIDTH), BF16),
        ],
        scratch_shapes=[pltpu.VMEM((D_MODEL, zw), BF16)],
        compiler_params=_params("arbitrary"),
        name="in_proj",
    )(x2, g, w, bd, qg, kg)


def _natten_kernel(q_ref, k_ref, v_ref, tbl_ref, o_ref, bias_ref, *, rows, kh):
    band = kh * GRID_W
    lane = lax.broadcasted_iota(jnp.int32, (GRID_W, LANES), 1)
    first = lane < HEAD_DIM

    for hh in range(2):
        for d0 in range(WIN_ROWS_MAX):
            for kk in range(kh // 2):
                bias_ref[hh, d0, :, kk * LANES:(kk + 1) * LANES] = jnp.where(
                    lane < GRID_W, tbl_ref[hh, d0 + 2 * kk], tbl_ref[hh, d0 + 2 * kk + 1])

    def geometry(g):
        geo = []
        for r in range(g * NATTEN_UNROLL, (g + 1) * NATTEN_UNROLL):
            rs = min(max(r - kh // 2, 0), rows - kh)
            geo.append((rs - r + (WIN_ROWS_MAX - 1), r * GRID_W, rs * GRID_W))
        return geo

    def score_stage(g):
        scores = []
        for d0, q0, k0 in geometry(g):
            qr = q_ref[0, q0:q0 + GRID_W, :]
            kb = k_ref[0, k0:k0 + band, :]
            for hh in range(2):
                sel = first if hh == 0 else jnp.logical_not(first)
                qm = jnp.where(sel, qr, jnp.zeros_like(qr))
                scores.append(_dot_nt(qm, kb) + bias_ref[hh, d0])
        return scores

    def softmax_stage(scores):
        probs = []
        for s in scores:
            e = jnp.exp2(s - jnp.max(s, axis=-1, keepdims=True))
            probs.append((e.astype(BF16), jnp.sum(e, axis=-1, keepdims=True)))
        return probs

    def value_stage(g, probs):
        for idx, (d0, q0, k0) in enumerate(geometry(g)):
            vb = v_ref[0, k0:k0 + band, :]
            outs = [_dot(e, vb) / l for e, l in probs[2 * idx:2 * idx + 2]]
            o_ref[0, q0:q0 + GRID_W, :] = jnp.where(first, outs[0], outs[1]).astype(o_ref.dtype)

    ngroups = rows // NATTEN_UNROLL
    scores, probs = {}, {}
    for step in range(ngroups + 2):
        if step < ngroups:
            scores[step] = score_stage(step)
        if 0 <= step - 1 < ngroups:
            probs[step - 1] = softmax_stage(scores.pop(step - 1))
        if 0 <= step - 2 < ngroups:
            value_stage(step - 2, probs.pop(step - 2))


def _natten(q, k, v, bias):
    b, s, _ = q.shape
    rows = s // GRID_W
    kh = min(WIN_ROWS_MAX, rows)
    assert rows % NATTEN_UNROLL == 0 and kh % 2 == 0 and 2 * GRID_W == LANES
    pairs = ATTN_WIDTH // LANES
    blk = pl.BlockSpec((1, s, LANES), lambda bi, p: (bi, 0, p))
    return pl.pallas_call(
        functools.partial(_natten_kernel, rows=rows, kh=kh),
        grid=(b, pairs),
        in_specs=[blk, blk, blk,
                  pl.BlockSpec((2,) + bias.shape[1:], lambda bi, p: (p, 0, 0, 0))],
        out_specs=blk,
        out_shape=jax.ShapeDtypeStruct((b, s, ATTN_WIDTH), BF16),
        scratch_shapes=[pltpu.VMEM((2, WIN_ROWS_MAX, GRID_W, kh * GRID_W), F32)],
        compiler_params=_params("arbitrary", "arbitrary"),
        name="natten",
    )(q, k, v, bias)


def _attn_bias_table(rpb):
    c = jnp.arange(GRID_W)
    cs = jnp.clip(c - WIN_COLS // 2, 0, GRID_W - WIN_COLS)
    j = jnp.arange(GRID_W)
    valid = (j[None, :] >= cs[:, None]) & (j[None, :] < cs[:, None] + WIN_COLS)
    dc = j[None, :] - c[:, None] + (WIN_COLS - 1)
    pick = ((dc[None] == jnp.arange(2 * WIN_COLS - 1)[:, None, None]) & valid[None]).astype(F32)
    t = jnp.einsum('hrd,dcj->hrcj', rpb.astype(F32), pick, precision=lax.Precision.HIGHEST)
    t = jnp.where(valid, t * LOG2E, MASK_BIAS)
    return jnp.concatenate([t, t], axis=-1)


def _window_sum(upad, w, tm):
    n = upad.shape[0]
    fwd = upad
    span = 1
    while span < min(w, POOL_HALO):
        fwd = fwd + pltpu.roll(fwd, n - span, axis=0)
        span *= 2
    centre = slice(POOL_HALO, POOL_HALO + tm)
    if w == 2 * POOL_HALO:
        return fwd[0:tm] + fwd[centre]
    return pltpu.roll(fwd, w // 2, axis=0)[centre]


def _mix_kernel(x_ref, u_ref, up_ref, un_ref, ya_ref, ic_ref, wp_ref, ps_ref, wo_ref, g_ref, wr_ref,
                x1_ref, h_ref, aff_ref, upad_ref, wobf_ref, wpbf_ref, *, tm):
    i = pl.program_id(1)

    @pl.when((pl.program_id(0) == 0) & (i == 0))
    def _():
        wobf_ref[...] = wo_ref[...].astype(BF16)
        wpbf_ref[...] = wp_ref[...].astype(BF16)

    upad_ref[0:POOL_HALO, :] = jnp.where(i > 0, up_ref[0], 0.0)
    upad_ref[POOL_HALO:POOL_HALO + tm, :] = u_ref[0]
    upad_ref[POOL_HALO + tm:, :] = jnp.where(i < pl.num_programs(1) - 1, un_ref[0], 0.0)

    ypool = []
    for gi, w in enumerate(POOL_WINDOWS):
        cols = slice(gi * POOL_GROUP, (gi + 1) * POOL_GROUP)
        upad = upad_ref[:, cols]
        inv = jnp.concatenate([
            jnp.where(i == 0, ic_ref[0:POOL_HALO, cols], 1.0 / w),
            jnp.full((tm - 2 * POOL_HALO, POOL_GROUP), 1.0 / w, F32),
            jnp.where(i == pl.num_programs(1) - 1, ic_ref[POOL_HALO:, cols], 1.0 / w)], axis=0)
        d = _window_sum(upad, w, tm) * inv - upad[POOL_HALO:POOL_HALO + tm]
        ypool.append(_dot(d.astype(BF16), wpbf_ref[gi]) * ps_ref[:, cols])
    ypool = jnp.concatenate(ypool, axis=-1).astype(BF16)

    mix = _dot(ypool, wobf_ref[:POOL_WIDTH, :]) + _dot(ya_ref[0], wobf_ref[POOL_WIDTH:, :])
    x1 = x_ref[0] + mix
    x1_ref[0] = x1
    h = _rms(x1, g_ref[...]).astype(BF16)
    h_ref[0] = h
    logits = _dot_nt(wr_ref[...].astype(BF16), h)
    m = jnp.max(logits, axis=0, keepdims=True)
    e = jnp.exp(logits - m)
    aff_ref[0] = e / jnp.sum(e, axis=0, keepdims=True)


def _pool_edge_inverse_counts(s):
    t = jnp.concatenate([jnp.arange(POOL_HALO), jnp.arange(s - POOL_HALO, s)])
    cols = []
    for w in POOL_WINDOWS:
        lo = jnp.clip(t - w // 2, 0, s - 1)
        hi = jnp.clip(t + (w - w // 2) - 1, 0, s - 1)
        inv = 1.0 / (hi - lo + 1).astype(F32)
        cols.append(jnp.broadcast_to(inv[:, None], (2 * POOL_HALO, POOL_GROUP)))
    return jnp.concatenate(cols, axis=-1)


def _mix(x, u, ya, wp, ps, wo, g, wr_t, tm=1024):
    b, s, _ = x.shape
    nt = s // tm
    hb = tm // POOL_HALO
    tile = lambda w: pl.BlockSpec((1, tm, w), lambda bi, i: (bi, i, 0))
    fixed2 = lambda shape: pl.BlockSpec(shape, lambda bi, i: (0, 0))
    return pl.pallas_call(
        functools.partial(_mix_kernel, tm=tm),
        grid=(b, nt),
        in_specs=[
            tile(D_MODEL),
            tile(POOL_WIDTH),
            pl.BlockSpec((1, POOL_HALO, POOL_WIDTH), lambda bi, i: (bi, jnp.maximum(i * hb - 1, 0), 0)),
            pl.BlockSpec((1, POOL_HALO, POOL_WIDTH),
                         lambda bi, i: (bi, jnp.minimum((i + 1) * hb, s // POOL_HALO - 1), 0)),
            tile(ATTN_WIDTH),
            fixed2((2 * POOL_HALO, POOL_WIDTH)),
            pl.BlockSpec(wp.shape, lambda bi, i: (0, 0, 0)),
            fixed2((1, POOL_WIDTH)),
            pl.BlockSpec(wo.shape, lambda bi, i: (0, 0), pipeline_mode=pl.Buffered(1)),
            fixed2((1, D_MODEL)),
            fixed2(wr_t.shape),
        ],
        out_specs=[
            tile(D_MODEL),
            tile(D_MODEL),
            pl.BlockSpec((1, N_EXPERTS, tm), lambda bi, i: (bi, 0, i)),
        ],
        out_shape=[
            jax.ShapeDtypeStruct((b, s, D_MODEL), F32),
            jax.ShapeDtypeStruct((b, s, D_MODEL), BF16),
            jax.ShapeDtypeStruct((b, N_EXPERTS, s), F32),
        ],
        scratch_shapes=[
            pltpu.VMEM((tm + 2 * POOL_HALO, POOL_WIDTH), F32),
            pltpu.VMEM(wo.shape, BF16),
            pltpu.VMEM(wp.shape, BF16),
        ],
        compiler_params=_params("arbitrary", "arbitrary"),
        name="mix",
    )(x, u, u, u, ya, _pool_edge_inverse_counts(s), wp, ps, wo, g, wr_t)


def _lane_cumsum_exclusive(m):
    e, s = m.shape
    r = lax.broadcasted_iota(jnp.int32, (LANES, LANES), 0)
    c = lax.broadcasted_iota(jnp.int32, (LANES, LANES), 1)
    upper = (r < c).astype(BF16)
    carry = jnp.zeros((e, 1), F32)
    out = []
    for blk in range(s // LANES):
        piece = m[:, blk * LANES:(blk + 1) * LANES]
        out.append(_dot(piece.astype(BF16), upper) + carry)
        carry = carry + jnp.sum(piece, axis=-1, keepdims=True)
    return jnp.concatenate(out, axis=-1)


def _route_kernel(aff_ref, slot_ref, slot_t_ref, off_ref, *, cap):
    aff = aff_ref[...]
    capf = jnp.float32(cap)

    def count_ge(cand_bits):
        return jnp.sum((aff >= pltpu.bitcast(cand_bits, F32)).astype(F32), axis=-1, keepdims=True)

    def search(step, ans):
        cand = ans | (jnp.int32(1) << (30 - step))
        return jnp.where(count_ge(cand) >= capf, cand, ans)

    thr = pltpu.bitcast(lax.fori_loop(0, 31, search, jnp.zeros((aff.shape[0], 1), jnp.int32)), F32)
    gt = aff > thr
    eq = aff == thr
    need = capf - jnp.sum(gt.astype(F32), axis=-1, keepdims=True)
    sel = gt | (eq & (_lane_cumsum_exclusive(eq.astype(F32)) < need))
    self = sel.astype(F32)
    slot = jnp.where(sel, _lane_cumsum_exclusive(self), float(NOT_SELECTED))
    slot_ref[...] = slot.astype(jnp.int32)
    pad = jnp.full((LANES - N_EXPERTS, slot.shape[1]), float(NOT_SELECTED), F32)
    for bi in range(slot_t_ref.shape[0]):
        mine = slot[bi * N_EXPERTS:(bi + 1) * N_EXPERTS]
        slot_t_ref[bi] = jnp.concatenate([mine, pad], axis=0).T.astype(jnp.int32)
    lane = lax.broadcasted_iota(jnp.int32, (slot.shape[0], LANES), 1)
    off = jnp.zeros((slot.shape[0], LANES), F32)
    run = jnp.zeros((slot.shape[0], 1), F32)
    for i in range(1, slot.shape[1] // ROUTE_TILE + 1):
        run = run + jnp.sum(self[:, (i - 1) * ROUTE_TILE:i * ROUTE_TILE], axis=-1, keepdims=True)
        off = jnp.where(lane == i, run, off)
    off_ref[...] = off.astype(jnp.int32)


def _route(aff_t, cap):
    b, e, s = aff_t.shape
    assert e == N_EXPERTS and s % ROUTE_TILE == 0 and s // ROUTE_TILE < OFF_STRIDE
    slot, slot_t, off = pl.pallas_call(
        functools.partial(_route_kernel, cap=cap),
        out_shape=[
            jax.ShapeDtypeStruct((b * e, s), jnp.int32),
            jax.ShapeDtypeStruct((b, s, LANES), jnp.int32),
            jax.ShapeDtypeStruct((b * e, LANES), jnp.int32),
        ],
        compiler_params=pltpu.CompilerParams(vmem_limit_bytes=VMEM_LIMIT),
        name="route",
    )(aff_t.reshape(b * e, s))
    return slot.reshape(b, e, s), slot_t, off


def _window_starts(off_ref, bi, ti, cap):
    starts = []
    ok = None
    for e in range(N_EXPERTS):
        base = (bi * N_EXPERTS + e) * OFF_STRIDE + ti
        lo = off_ref[base]
        hi = off_ref[base + 1]
        st = jnp.minimum(jnp.bitwise_and(lo, -16), cap - ROUTE_WIN)
        fits = hi <= st + ROUTE_WIN
        ok = fits if ok is None else jnp.logical_and(ok, fits)
        starts.append(pl.multiple_of(st, 16))
    return starts, ok


def _dispatch_kernel(off_ref, slot_ref, aff_ref, h_ref, xe_ref, gate_ref, *, cap):
    bi = pl.program_id(0)
    ti = pl.program_id(1)

    @pl.when(ti == 0)
    def _():
        xe_ref[...] = jnp.zeros_like(xe_ref)
        gate_ref[...] = jnp.zeros_like(gate_ref)

    nsub = h_ref.shape[1] // ROUTE_TILE
    col_groups = [slice(sub * ROUTE_TILE, (sub + 1) * ROUTE_TILE) for sub in range(nsub)]
    geo = [_window_starts(off_ref, bi, ti * nsub + sub, cap) for sub in range(nsub)]
    ok = functools.reduce(jnp.logical_and, [fits for _, fits in geo])

    @pl.when(ok)
    def _():
        row = lax.broadcasted_iota(jnp.int32, (ROUTE_WIN, ROUTE_TILE), 0)
        for cols, (starts, _) in zip(col_groups, geo):
            slot = slot_ref[0, :, cols]
            aff = aff_ref[0, :, cols]
            hits = [row == (slot[e:e + 1, :] - starts[e]) for e in range(N_EXPERTS)]
            onehot = jnp.concatenate([jnp.where(hh, 1.0, 0.0).astype(BF16) for hh in hits], axis=0)
            res = _dot(onehot, h_ref[0, cols, :])
            for e, hh in enumerate(hits):
                win = pl.ds(starts[e], ROUTE_WIN)
                xe_ref[e, win, :] += res[e * ROUTE_WIN:(e + 1) * ROUTE_WIN].astype(BF16)
                gate_ref[e, win, :] += jnp.sum(jnp.where(hh, aff[e:e + 1, :], 0.0), axis=-1, keepdims=True)

    @pl.when(jnp.logical_not(ok))
    def _():
        row = lax.broadcasted_iota(jnp.int32, (cap, ROUTE_TILE), 0)
        for cols in col_groups:
            slot = slot_ref[0, :, cols]
            aff = aff_ref[0, :, cols]
            h = h_ref[0, cols, :]
            for e in range(N_EXPERTS):
                hh = row == slot[e:e + 1, :]
                xe_ref[e] += _dot(jnp.where(hh, 1.0, 0.0).astype(BF16), h).astype(BF16)
                gate_ref[e] += jnp.sum(jnp.where(hh, aff[e:e + 1, :], 0.0), axis=-1, keepdims=True)


def _dispatch(off, slot, aff_t, h, cap, tm=1024):
    b, e, s = slot.shape
    assert tm % ROUTE_TILE == 0
    rows = pl.BlockSpec((1, e, tm), lambda bi, ti, off_ref: (bi, 0, ti))
    return pl.pallas_call(
        functools.partial(_dispatch_kernel, cap=cap),
        grid_spec=pltpu.PrefetchScalarGridSpec(
            num_scalar_prefetch=1,
            grid=(b, s // tm),
            in_specs=[rows, rows,
                      pl.BlockSpec((1, tm, D_MODEL), lambda bi, ti, off_ref: (bi, ti, 0))],
            out_specs=[
                pl.BlockSpec((e, cap, D_MODEL), lambda bi, ti, off_ref: (0, bi, 0)),
                pl.BlockSpec((e, cap, 1), lambda bi, ti, off_ref: (0, bi, 0)),
            ],
        ),
        out_shape=[
            jax.ShapeDtypeStruct((e, b * cap, D_MODEL), BF16),
            jax.ShapeDtypeStruct((e, b * cap, 1), F32),
        ],
        compiler_params=_params("arbitrary", "arbitrary"),
        name="dispatch",
    )(off, slot, aff_t, h)


def _experts_kernel(xe_ref, gate_ref, wg_ref, wu_ref, wd_ref, y_ref, acc_ref):
    f = pl.program_id(1)

    def hidden_chunk(first):
        wg = wg_ref[0].astype(BF16)
        wu = wu_ref[0].astype(BF16)
        wd = wd_ref[0].astype(BF16)
        for mb in range(xe_ref.shape[1] // EXPERT_ROWS):
            rows = slice(mb * EXPERT_ROWS, (mb + 1) * EXPERT_ROWS)
            xe = xe_ref[0, rows, :]
            a = _dot(xe, wg)
            b = _dot(xe, wu)
            part = _dot((a * jax.nn.sigmoid(a) * b).astype(BF16), wd)
            acc_ref[rows, :] = part if first else acc_ref[rows, :] + part

    pl.when(f == 0)(functools.partial(hidden_chunk, True))
    pl.when(f > 0)(functools.partial(hidden_chunk, False))

    @pl.when(f == pl.num_programs(1) - 1)
    def _():
        y_ref[0] = (acc_ref[...] * gate_ref[0]).astype(y_ref.dtype)


def _experts(xe, gate, wg, wu, wd, tf=512):
    e, m, _ = xe.shape
    nf = D_EXPERT // tf
    return pl.pallas_call(
        _experts_kernel,
        grid=(e, nf),
        in_specs=[
            pl.BlockSpec((1, m, D_MODEL), lambda ei, f: (ei, 0, 0)),
            pl.BlockSpec((1, m, 1), lambda ei, f: (ei, 0, 0)),
            pl.BlockSpec((1, D_MODEL, tf), lambda ei, f: (ei, 0, f)),
            pl.BlockSpec((1, D_MODEL, tf), lambda ei, f: (ei, 0, f)),
            pl.BlockSpec((1, tf, D_MODEL), lambda ei, f: (ei, f, 0)),
        ],
        out_specs=pl.BlockSpec((1, m, D_MODEL), lambda ei, f: (ei, 0, 0)),
        out_shape=jax.ShapeDtypeStruct((e, m, D_MODEL), BF16),
        scratch_shapes=[pltpu.VMEM((m, D_MODEL), F32)],
        compiler_params=_params("arbitrary", "arbitrary"),
        name="experts",
    )(xe, gate, wg, wu, wd)


def _combine_kernel(off_ref, x1_ref, st_ref, y_ref, p_ref, gn_ref, wgbf_ref, wpbf_ref, gp_ref,
                    o_ref, *, cap):
    bi = pl.program_id(0)
    nsub = x1_ref.shape[1] // ROUTE_TILE
    row_groups = [slice(sub * ROUTE_TILE, (sub + 1) * ROUTE_TILE) for sub in range(nsub)]
    geo = [_window_starts(off_ref, bi, pl.program_id(1) * nsub + sub, cap) for sub in range(nsub)]
    ok = functools.reduce(jnp.logical_and, [fits for _, fits in geo])

    def windowed_scatter(rows, starts):
        st = st_ref[0, rows, :]
        lane = lax.broadcasted_iota(jnp.int32, (ROUTE_TILE, LANES), 1)
        low = lane < ROUTE_WIN
        total = None
        for g in range(N_EXPERTS // ROUTE_GROUP):
            halves = []
            wins = []
            for half in range(ROUTE_GROUP // 2):
                e0 = g * ROUTE_GROUP + 2 * half
                t0 = st[:, e0:e0 + 1] - starts[e0]
                t1 = st[:, e0 + 1:e0 + 2] + (ROUTE_WIN - starts[e0 + 1])
                halves.append(jnp.where(lane == jnp.where(low, t0, t1), 1.0, 0.0).astype(BF16))
                wins.append(y_ref[e0, pl.ds(starts[e0], ROUTE_WIN), :])
                wins.append(y_ref[e0 + 1, pl.ds(starts[e0 + 1], ROUTE_WIN), :])
            part = _dot(jnp.concatenate(halves, axis=1), jnp.concatenate(wins, axis=0))
            total = part if total is None else total + part
        return total

    def dense_scatter(rows):
        st = st_ref[0, rows, :]
        lane = lax.broadcasted_iota(jnp.int32, (ROUTE_TILE, cap), 1)
        total = None
        for e in range(N_EXPERTS):
            onehot = jnp.where(lane == st[:, e:e + 1], 1.0, 0.0).astype(BF16)
            part = _dot(onehot, y_ref[e])
            total = part if total is None else total + part
        return total

    def gate_stage(rows, ffn):
        x2 = x1_ref[0, rows, :] + ffn
        return x2, _dot(_rms(x2, gn_ref[...]).astype(BF16), wgbf_ref[...])

    def output_stage(rows, x2, gate_logits):
        emb = _rms(_dot(p_ref[0, rows, :].astype(BF16), wpbf_ref[...]), gp_ref[...])
        o_ref[0, rows, :] = x2 + jax.nn.sigmoid(gate_logits) * emb

    def pipeline(scatter_stage):
        ffn, gated = {}, {}
        for step in range(nsub + 2):
            if step < nsub:
                ffn[step] = scatter_stage(step)
            if 0 <= step - 1 < nsub:
                gated[step - 1] = gate_stage(row_groups[step - 1], ffn.pop(step - 1))
            if 0 <= step - 2 < nsub:
                output_stage(row_groups[step - 2], *gated.pop(step - 2))

    pl.when(ok)(lambda: pipeline(lambda g: windowed_scatter(row_groups[g], geo[g][0])))
    pl.when(jnp.logical_not(ok))(lambda: pipeline(lambda g: dense_scatter(row_groups[g])))


def _combine(off, x1, slot_t, y, p, gn, wg, wp, gp, cap, tm=1024):
    b, s, _ = x1.shape
    assert tm % ROUTE_TILE == 0
    tile = lambda w: pl.BlockSpec((1, tm, w), lambda bi, i, off_ref: (bi, i, 0))
    fixed2 = lambda shape: pl.BlockSpec(shape, lambda bi, i, off_ref: (0, 0))
    return pl.pallas_call(
        functools.partial(_combine_kernel, cap=cap),
        grid_spec=pltpu.PrefetchScalarGridSpec(
            num_scalar_prefetch=1,
            grid=(b, s // tm),
            in_specs=[
                tile(D_MODEL),
                tile(LANES),
                pl.BlockSpec((N_EXPERTS, cap, D_MODEL), lambda bi, i, off_ref: (0, bi, 0)),
                tile(PLE_DIM),
                fixed2((1, D_MODEL)),
                pl.BlockSpec(wg.shape, lambda bi, i, off_ref: (0, 0), pipeline_mode=pl.Buffered(1)),
                pl.BlockSpec(wp.shape, lambda bi, i, off_ref: (0, 0), pipeline_mode=pl.Buffered(1)),
                fixed2((1, D_MODEL)),
            ],
            out_specs=tile(D_MODEL),
        ),
        out_shape=jax.ShapeDtypeStruct((b, s, D_MODEL), F32),
        compiler_params=_params("arbitrary", "arbitrary"),
        name="combine",
    )(off, x1, slot_t, y, p, gn, wg.astype(BF16), wp.astype(BF16), gp)


def kernel(x, p, norm_mix, w_in, w_pool, pool_scale, q_norm, k_norm, rpb, w_out, norm_ffn, w_router, w_gate, w_up, w_down, norm_ple, w_ple_gate, w_ple_proj, norm_ple_post):
    b, s, d = x.shape
    depth = w_in.shape[0]
    cap = EC_CAPACITY * s // N_EXPERTS
    head = jnp.arange(ATTN_WIDTH) // HEAD_DIM
    block_diag = (head[:, None] == head[None, :]).astype(BF16)
    row = lambda a: a.reshape(1, -1)
    for i in range(depth):
        u, q, k, v = _in_proj(x.reshape(b * s, d), row(norm_mix[i]), w_in[i], block_diag,
                              row(jnp.tile(q_norm[i], ATTN_HEADS)), row(jnp.tile(k_norm[i], ATTN_HEADS)))
        shp = lambda a: a.reshape(b, s, -1)
        y_attn = _natten(shp(q), shp(k), shp(v), _attn_bias_table(rpb[i]))
        x1, h, aff_t = _mix(x, shp(u), y_attn, w_pool[i], row(pool_scale[i]), w_out[i],
                            row(norm_ffn[i]), w_router[i].T)
        slot, slot_t, off = _route(aff_t, cap)
        off = off[:, :OFF_STRIDE].reshape(-1)
        xe, gate = _dispatch(off, slot, aff_t, h, cap)
        y = _experts(xe, gate, w_gate[i], w_up[i], w_down[i])
        x = _combine(off, x1, slot_t, y, p[i], row(norm_ple[i]), w_ple_gate[i], w_ple_proj[i],
                     row(norm_ple_post[i]), cap)
    return x
```

```python
import functools

import jax
import jax.numpy as jnp
from jax import lax
from jax.experimental import pallas as pl
from jax.experimental.pallas import tpu as pltpu

D_MODEL = 1024
GRID_W = 64
POOL_WINDOWS = (2, 4, 8, 16)
POOL_WIDTH = D_MODEL // 2
POOL_GROUP = POOL_WIDTH // len(POOL_WINDOWS)
ATTN_HEADS = 8
HEAD_DIM = (D_MODEL // 2) // ATTN_HEADS
ATTN_WIDTH = ATTN_HEADS * HEAD_DIM
WIN_ROWS_MAX = 8
WIN_COLS = 16
N_EXPERTS = 16
EC_CAPACITY = 2
D_EXPERT = 2 * D_MODEL
PLE_DIM = 256
RMS_EPS = 1e-6

LANES = 128
POOL_HALO = 8
LOG2E = 1.4426950408889634
MASK_BIAS = -1e30
NATTEN_UNROLL = 8
ROUTE_TILE = 256
ROUTE_WIN = 64
ROUTE_GROUP = 4
OFF_STRIDE = 16
NOT_SELECTED = -(1 << 20)
assert 2 * ROUTE_WIN == LANES and ROUTE_GROUP % 2 == 0
MIX_ROWS = 256
PROJ_ROWS = 256
EXPERT_ROWS = 512
VMEM_LIMIT = 56 * 1024 * 1024

BF16 = jnp.bfloat16
F32 = jnp.float32


def _params(*sem):
    return pltpu.CompilerParams(dimension_semantics=sem, vmem_limit_bytes=VMEM_LIMIT)


def _rms(x, g):
    return x * lax.rsqrt(jnp.mean(x * x, axis=-1, keepdims=True) + RMS_EPS) * g


def _dot(a, b):
    return jnp.dot(a, b, preferred_element_type=F32)


def _dot_nt(a, b):
    return lax.dot_general(a, b, (((1,), (1,)), ((), ())), preferred_element_type=F32)


def _in_proj_kernel(x_ref, g_ref, w_ref, bd_ref, qg_ref, kg_ref,
                    u_ref, q_ref, k_ref, v_ref, wbf_ref):
    @pl.when(pl.program_id(0) == 0)
    def _():
        wbf_ref[...] = w_ref[...].astype(BF16)

    bd = bd_ref[...]

    def head_norm(t, g):
        ms = _dot((t * t).astype(BF16), bd) * (1.0 / HEAD_DIM)
        return t * lax.rsqrt(ms + RMS_EPS) * g

    def norm_stage(rows):
        return _rms(x_ref[rows, :], g_ref[...]).astype(BF16)

    def proj_stage(h):
        return _dot(h, wbf_ref[...])

    def head_stage(rows, z):
        u_ref[rows, :] = z[:, :POOL_WIDTH]
        q = z[:, POOL_WIDTH:POOL_WIDTH + ATTN_WIDTH]
        k = z[:, POOL_WIDTH + ATTN_WIDTH:POOL_WIDTH + 2 * ATTN_WIDTH]
        v = z[:, POOL_WIDTH + 2 * ATTN_WIDTH:]
        q_ref[rows, :] = (head_norm(q, qg_ref[...]) * (HEAD_DIM ** -0.5 * LOG2E)).astype(BF16)
        k_ref[rows, :] = head_norm(k, kg_ref[...]).astype(BF16)
        v_ref[rows, :] = v.astype(BF16)

    groups = [slice(sub * PROJ_ROWS, (sub + 1) * PROJ_ROWS) for sub in range(x_ref.shape[0] // PROJ_ROWS)]
    zs = [proj_stage(norm_stage(rows)) for rows in groups]
    for rows, z in zip(groups, zs):
        head_stage(rows, z)


def _in_proj(x2, g, w, bd, qg, kg, tm=1024):
    n = x2.shape[0]
    zw = w.shape[1]
    row = lambda i: (i, 0)
    fixed = lambda i: (0, 0)
    return pl.pallas_call(
        _in_proj_kernel,
        grid=(n // tm,),
        in_specs=[
            pl.BlockSpec((tm, D_MODEL), row),
            pl.BlockSpec((1, D_MODEL), fixed),
            pl.BlockSpec((D_MODEL, zw), fixed, pipeline_mode=pl.Buffered(1)),
            pl.BlockSpec((ATTN_WIDTH, ATTN_WIDTH), fixed),
            pl.BlockSpec((1, ATTN_WIDTH), fixed),
            pl.BlockSpec((1, ATTN_WIDTH), fixed),
        ],
        out_specs=[
            pl.BlockSpec((tm, POOL_WIDTH), row),
            pl.BlockSpec((tm, ATTN_WIDTH), row),
            pl.BlockSpec((tm, ATTN_WIDTH), row),
            pl.BlockSpec((tm, ATTN_WIDTH), row),
        ],
        out_shape=[
            jax.ShapeDtypeStruct((n, POOL_WIDTH), F32),
            jax.ShapeDtypeStruct((n, ATTN_WIDTH), BF16),
            jax.ShapeDtypeStruct((n, ATTN_WIDTH), BF16),
            jax.ShapeDtypeStruct((n, ATTN_WIDTH), BF16),
        ],
        scratch_shapes=[pltpu.VMEM((D_MODEL, zw), BF16)],
        compiler_params=_params("arbitrary"),
        name="in_proj",
    )(x2, g, w, bd, qg, kg)


def _natten_kernel(q_ref, k_ref, v_ref, tbl_ref, o_ref, bias_ref, *, rows, kh):
    band = kh * GRID_W
    lane = lax.broadcasted_iota(jnp.int32, (GRID_W, LANES), 1)
    first = lane < HEAD_DIM

    for hh in range(2):
        for d0 in range(WIN_ROWS_MAX):
            for kk in range(kh // 2):
                bias_ref[hh, d0, :, kk * LANES:(kk + 1) * LANES] = jnp.where(
                    lane < GRID_W, tbl_ref[hh, d0 + 2 * kk], tbl_ref[hh, d0 + 2 * kk + 1])

    def geometry(g):
        geo = []
        for r in range(g * NATTEN_UNROLL, (g + 1) * NATTEN_UNROLL):
            rs = min(max(r - kh // 2, 0), rows - kh)
            geo.append((rs - r + (WIN_ROWS_MAX - 1), r * GRID_W, rs * GRID_W))
        return geo

    def score_stage(g):
        scores = []
        for d0, q0, k0 in geometry(g):
            qr = q_ref[0, q0:q0 + GRID_W, :]
            kb = k_ref[0, k0:k0 + band, :]
            for hh in range(2):
                sel = first if hh == 0 else jnp.logical_not(first)
                qm = jnp.where(sel, qr, jnp.zeros_like(qr))
                scores.append(_dot_nt(qm, kb) + bias_ref[hh, d0])
        return scores

    def softmax_stage(scores):
        probs = []
        for s in scores:
            e = jnp.exp2(s - jnp.max(s, axis=-1, keepdims=True))
            probs.append((e.astype(BF16), jnp.sum(e, axis=-1, keepdims=True)))
        return probs

    def value_stage(g, probs):
        for idx, (d0, q0, k0) in enumerate(geometry(g)):
            vb = v_ref[0, k0:k0 + band, :]
            outs = [_dot(e, vb) / l for e, l in probs[2 * idx:2 * idx + 2]]
            o_ref[0, q0:q0 + GRID_W, :] = jnp.where(first, outs[0], outs[1]).astype(o_ref.dtype)

    ngroups = rows // NATTEN_UNROLL
    scores, probs = {}, {}
    for step in range(ngroups + 2):
        if step < ngroups:
            scores[step] = score_stage(step)
        if 0 <= step - 1 < ngroups:
            probs[step - 1] = softmax_stage(scores.pop(step - 1))
        if 0 <= step - 2 < ngroups:
            value_stage(step - 2, probs.pop(step - 2))


def _natten(q, k, v, bias):
    b, s, _ = q.shape
    rows = s // GRID_W
    kh = min(WIN_ROWS_MAX, rows)
    assert rows % NATTEN_UNROLL == 0 and kh % 2 == 0 and 2 * GRID_W == LANES
    pairs = ATTN_WIDTH // LANES
    blk = pl.BlockSpec((1, s, LANES), lambda bi, p: (bi, 0, p))
    return pl.pallas_call(
        functools.partial(_natten_kernel, rows=rows, kh=kh),
        grid=(b, pairs),
        in_specs=[blk, blk, blk,
                  pl.BlockSpec((2,) + bias.shape[1:], lambda bi, p: (p, 0, 0, 0))],
        out_specs=blk,
        out_shape=jax.ShapeDtypeStruct((b, s, ATTN_WIDTH), BF16),
        scratch_shapes=[pltpu.VMEM((2, WIN_ROWS_MAX, GRID_W, kh * GRID_W), F32)],
        compiler_params=_params("arbitrary", "arbitrary"),
        name="natten",
    )(q, k, v, bias)


def _attn_bias_table(rpb):
    c = jnp.arange(GRID_W)
    cs = jnp.clip(c - WIN_COLS // 2, 0, GRID_W - WIN_COLS)
    j = jnp.arange(GRID_W)
    valid = (j[None, :] >= cs[:, None]) & (j[None, :] < cs[:, None] + WIN_COLS)
    dc = j[None, :] - c[:, None] + (WIN_COLS - 1)
    pick = ((dc[None] == jnp.arange(2 * WIN_COLS - 1)[:, None, None]) & valid[None]).astype(F32)
    t = jnp.einsum('hrd,dcj->hrcj', rpb.astype(F32), pick, precision=lax.Precision.HIGHEST)
    t = jnp.where(valid, t * LOG2E, MASK_BIAS)
    return jnp.concatenate([t, t], axis=-1)


def _window_sum(upad, w, tm):
    n = upad.shape[0]
    fwd = upad
    span = 1
    while span < min(w, POOL_HALO):
        fwd = fwd + pltpu.roll(fwd, n - span, axis=0)
        span *= 2
    centre = slice(POOL_HALO, POOL_HALO + tm)
    if w == 2 * POOL_HALO:
        return fwd[0:tm] + fwd[centre]
    return pltpu.roll(fwd, w // 2, axis=0)[centre]


def _mix_kernel(x_ref, u_ref, up_ref, un_ref, ya_ref, ic_ref, wp_ref, ps_ref, wo_ref, g_ref, wr_ref,
                x1_ref, h_ref, aff_ref, upad_ref, wobf_ref, wpbf_ref, *, tm):
    i = pl.program_id(1)

    @pl.when((pl.program_id(0) == 0) & (i == 0))
    def _():
        wobf_ref[...] = wo_ref[...].astype(BF16)
        wpbf_ref[...] = wp_ref[...].astype(BF16)

    upad_ref[0:POOL_HALO, :] = jnp.where(i > 0, up_ref[0], 0.0)
    upad_ref[POOL_HALO:POOL_HALO + tm, :] = u_ref[0]
    upad_ref[POOL_HALO + tm:, :] = jnp.where(i < pl.num_programs(1) - 1, un_ref[0], 0.0)

    ngroups = tm // MIX_ROWS
    group_cols = [slice(gi * POOL_GROUP, (gi + 1) * POOL_GROUP) for gi in range(len(POOL_WINDOWS))]

    def pool_stage(g):
        ds = []
        for cols, w in zip(group_cols, POOL_WINDOWS):
            upad = upad_ref[g * MIX_ROWS:(g + 1) * MIX_ROWS + 2 * POOL_HALO, cols]
            edge = jnp.full((POOL_HALO, POOL_GROUP), 1.0 / w, F32)
            head = jnp.where(i == 0, ic_ref[0:POOL_HALO, cols], 1.0 / w) if g == 0 else edge
            tail = (jnp.where(i == pl.num_programs(1) - 1, ic_ref[POOL_HALO:, cols], 1.0 / w)
                    if g == ngroups - 1 else edge)
            inv = jnp.concatenate(
                [head, jnp.full((MIX_ROWS - 2 * POOL_HALO, POOL_GROUP), 1.0 / w, F32), tail], axis=0)
            d = _window_sum(upad, w, MIX_ROWS) * inv - upad[POOL_HALO:POOL_HALO + MIX_ROWS]
            ds.append(d.astype(BF16))
        return ds

    def proj_stage(g, ds):
        rows = slice(g * MIX_ROWS, (g + 1) * MIX_ROWS)
        ypool = [_dot(d, wpbf_ref[gi]) * ps_ref[:, cols] for gi, (d, cols) in enumerate(zip(ds, group_cols))]
        ypool = jnp.concatenate(ypool, axis=-1).astype(BF16)
        return _dot(ypool, wobf_ref[:POOL_WIDTH, :]) + _dot(ya_ref[0, rows, :], wobf_ref[POOL_WIDTH:, :])

    def route_stage(g, mix):
        rows = slice(g * MIX_ROWS, (g + 1) * MIX_ROWS)
        x1 = x_ref[0, rows, :] + mix
        x1_ref[0, rows, :] = x1
        h = _rms(x1, g_ref[...]).astype(BF16)
        h_ref[0, rows, :] = h
        logits = _dot_nt(wr_ref[...].astype(BF16), h)
        m = jnp.max(logits, axis=0, keepdims=True)
        e = jnp.exp(logits - m)
        aff_ref[0, :, rows] = e / jnp.sum(e, axis=0, keepdims=True)

    pooled, mixed = {}, {}
    for step in range(ngroups + 2):
        if step < ngroups:
            pooled[step] = pool_stage(step)
        if 0 <= step - 1 < ngroups:
            mixed[step - 1] = proj_stage(step - 1, pooled.pop(step - 1))
        if 0 <= step - 2 < ngroups:
            route_stage(step - 2, mixed.pop(step - 2))


def _pool_edge_inverse_counts(s):
    t = jnp.concatenate([jnp.arange(POOL_HALO), jnp.arange(s - POOL_HALO, s)])
    cols = []
    for w in POOL_WINDOWS:
        lo = jnp.clip(t - w // 2, 0, s - 1)
        hi = jnp.clip(t + (w - w // 2) - 1, 0, s - 1)
        inv = 1.0 / (hi - lo + 1).astype(F32)
        cols.append(jnp.broadcast_to(inv[:, None], (2 * POOL_HALO, POOL_GROUP)))
    return jnp.concatenate(cols, axis=-1)


def _mix(x, u, ya, wp, ps, wo, g, wr_t, tm=1024):
    b, s, _ = x.shape
    nt = s // tm
    hb = tm // POOL_HALO
    tile = lambda w: pl.BlockSpec((1, tm, w), lambda bi, i: (bi, i, 0))
    fixed2 = lambda shape: pl.BlockSpec(shape, lambda bi, i: (0, 0))
    return pl.pallas_call(
        functools.partial(_mix_kernel, tm=tm),
        grid=(b, nt),
        in_specs=[
            tile(D_MODEL),
            tile(POOL_WIDTH),
            pl.BlockSpec((1, POOL_HALO, POOL_WIDTH), lambda bi, i: (bi, jnp.maximum(i * hb - 1, 0), 0)),
            pl.BlockSpec((1, POOL_HALO, POOL_WIDTH),
                         lambda bi, i: (bi, jnp.minimum((i + 1) * hb, s // POOL_HALO - 1), 0)),
            tile(ATTN_WIDTH),
            fixed2((2 * POOL_HALO, POOL_WIDTH)),
            pl.BlockSpec(wp.shape, lambda bi, i: (0, 0, 0)),
            fixed2((1, POOL_WIDTH)),
            pl.BlockSpec(wo.shape, lambda bi, i: (0, 0), pipeline_mode=pl.Buffered(1)),
            fixed2((1, D_MODEL)),
            fixed2(wr_t.shape),
        ],
        out_specs=[
            tile(D_MODEL),
            tile(D_MODEL),
            pl.BlockSpec((1, N_EXPERTS, tm), lambda bi, i: (bi, 0, i)),
        ],
        out_shape=[
            jax.ShapeDtypeStruct((b, s, D_MODEL), F32),
            jax.ShapeDtypeStruct((b, s, D_MODEL), BF16),
            jax.ShapeDtypeStruct((b, N_EXPERTS, s), F32),
        ],
        scratch_shapes=[
            pltpu.VMEM((tm + 2 * POOL_HALO, POOL_WIDTH), F32),
            pltpu.VMEM(wo.shape, BF16),
            pltpu.VMEM(wp.shape, BF16),
        ],
        compiler_params=_params("arbitrary", "arbitrary"),
        name="mix",
    )(x, u, u, u, ya, _pool_edge_inverse_counts(s), wp, ps, wo, g, wr_t)


def _lane_cumsum_exclusive(m):
    e, s = m.shape
    r = lax.broadcasted_iota(jnp.int32, (LANES, LANES), 0)
    c = lax.broadcasted_iota(jnp.int32, (LANES, LANES), 1)
    upper = (r < c).astype(BF16)
    carry = jnp.zeros((e, 1), F32)
    out = []
    for blk in range(s // LANES):
        piece = m[:, blk * LANES:(blk + 1) * LANES]
        out.append(_dot(piece.astype(BF16), upper) + carry)
        carry = carry + jnp.sum(piece, axis=-1, keepdims=True)
    return jnp.concatenate(out, axis=-1)


def _route_kernel(aff_ref, slot_ref, slot_t_ref, off_ref, *, cap):
    aff = aff_ref[...]
    capf = jnp.float32(cap)

    def count_ge(cand_bits):
        return jnp.sum((aff >= pltpu.bitcast(cand_bits, F32)).astype(F32), axis=-1, keepdims=True)

    def search(step, ans):
        cand = ans | (jnp.int32(1) << (30 - step))
        return jnp.where(count_ge(cand) >= capf, cand, ans)

    thr = pltpu.bitcast(lax.fori_loop(0, 31, search, jnp.zeros((aff.shape[0], 1), jnp.int32)), F32)
    gt = aff > thr
    eq = aff == thr
    need = capf - jnp.sum(gt.astype(F32), axis=-1, keepdims=True)
    sel = gt | (eq & (_lane_cumsum_exclusive(eq.astype(F32)) < need))
    self = sel.astype(F32)
    slot = jnp.where(sel, _lane_cumsum_exclusive(self), float(NOT_SELECTED))
    slot_ref[...] = slot.astype(jnp.int32)
    pad = jnp.full((LANES - N_EXPERTS, slot.shape[1]), float(NOT_SELECTED), F32)
    for bi in range(slot_t_ref.shape[0]):
        mine = slot[bi * N_EXPERTS:(bi + 1) * N_EXPERTS]
        slot_t_ref[bi] = jnp.concatenate([mine, pad], axis=0).T.astype(jnp.int32)
    lane = lax.broadcasted_iota(jnp.int32, (slot.shape[0], LANES), 1)
    off = jnp.zeros((slot.shape[0], LANES), F32)
    run = jnp.zeros((slot.shape[0], 1), F32)
    for i in range(1, slot.shape[1] // ROUTE_TILE + 1):
        run = run + jnp.sum(self[:, (i - 1) * ROUTE_TILE:i * ROUTE_TILE], axis=-1, keepdims=True)
        off = jnp.where(lane == i, run, off)
    off_ref[...] = off.astype(jnp.int32)


def _route(aff_t, cap):
    b, e, s = aff_t.shape
    assert e == N_EXPERTS and s % ROUTE_TILE == 0 and s // ROUTE_TILE < OFF_STRIDE
    slot, slot_t, off = pl.pallas_call(
        functools.partial(_route_kernel, cap=cap),
        out_shape=[
            jax.ShapeDtypeStruct((b * e, s), jnp.int32),
            jax.ShapeDtypeStruct((b, s, LANES), jnp.int32),
            jax.ShapeDtypeStruct((b * e, LANES), jnp.int32),
        ],
        compiler_params=pltpu.CompilerParams(vmem_limit_bytes=VMEM_LIMIT),
        name="route",
    )(aff_t.reshape(b * e, s))
    return slot.reshape(b, e, s), slot_t, off


def _window_starts(off_ref, bi, ti, cap):
    starts = []
    ok = None
    for e in range(N_EXPERTS):
        base = (bi * N_EXPERTS + e) * OFF_STRIDE + ti
        lo = off_ref[base]
        hi = off_ref[base + 1]
        st = jnp.minimum(jnp.bitwise_and(lo, -16), cap - ROUTE_WIN)
        fits = hi <= st + ROUTE_WIN
        ok = fits if ok is None else jnp.logical_and(ok, fits)
        starts.append(pl.multiple_of(st, 16))
    return starts, ok


def _dispatch_kernel(off_ref, slot_ref, aff_ref, h_ref, xe_ref, gate_ref, *, cap):
    bi = pl.program_id(0)
    ti = pl.program_id(1)

    @pl.when(ti == 0)
    def _():
        xe_ref[...] = jnp.zeros_like(xe_ref)
        gate_ref[...] = jnp.zeros_like(gate_ref)

    nsub = h_ref.shape[1] // ROUTE_TILE
    col_groups = [slice(sub * ROUTE_TILE, (sub + 1) * ROUTE_TILE) for sub in range(nsub)]
    geo = [_window_starts(off_ref, bi, ti * nsub + sub, cap) for sub in range(nsub)]
    ok = functools.reduce(jnp.logical_and, [fits for _, fits in geo])

    @pl.when(ok)
    def _():
        row = lax.broadcasted_iota(jnp.int32, (ROUTE_WIN, ROUTE_TILE), 0)
        for cols, (starts, _) in zip(col_groups, geo):
            slot = slot_ref[0, :, cols]
            aff = aff_ref[0, :, cols]
            hits = [row == (slot[e:e + 1, :] - starts[e]) for e in range(N_EXPERTS)]
            onehot = jnp.concatenate([jnp.where(hh, 1.0, 0.0).astype(BF16) for hh in hits], axis=0)
            res = _dot(onehot, h_ref[0, cols, :])
            for e, hh in enumerate(hits):
                win = pl.ds(starts[e], ROUTE_WIN)
                xe_ref[e, win, :] += res[e * ROUTE_WIN:(e + 1) * ROUTE_WIN].astype(BF16)
                gate_ref[e, win, :] += jnp.sum(jnp.where(hh, aff[e:e + 1, :], 0.0), axis=-1, keepdims=True)

    @pl.when(jnp.logical_not(ok))
    def _():
        row = lax.broadcasted_iota(jnp.int32, (cap, ROUTE_TILE), 0)
        for cols in col_groups:
            slot = slot_ref[0, :, cols]
            aff = aff_ref[0, :, cols]
            h = h_ref[0, cols, :]
            for e in range(N_EXPERTS):
                hh = row == slot[e:e + 1, :]
                xe_ref[e] += _dot(jnp.where(hh, 1.0, 0.0).astype(BF16), h).astype(BF16)
                gate_ref[e] += jnp.sum(jnp.where(hh, aff[e:e + 1, :], 0.0), axis=-1, keepdims=True)


def _dispatch(off, slot, aff_t, h, cap, tm=1024):
    b, e, s = slot.shape
    assert tm % ROUTE_TILE == 0
    rows = pl.BlockSpec((1, e, tm), lambda bi, ti, off_ref: (bi, 0, ti))
    return pl.pallas_call(
        functools.partial(_dispatch_kernel, cap=cap),
        grid_spec=pltpu.PrefetchScalarGridSpec(
            num_scalar_prefetch=1,
            grid=(b, s // tm),
            in_specs=[rows, rows,
                      pl.BlockSpec((1, tm, D_MODEL), lambda bi, ti, off_ref: (bi, ti, 0))],
            out_specs=[
                pl.BlockSpec((e, cap, D_MODEL), lambda bi, ti, off_ref: (0, bi, 0)),
                pl.BlockSpec((e, cap, 1), lambda bi, ti, off_ref: (0, bi, 0)),
            ],
        ),
        out_shape=[
            jax.ShapeDtypeStruct((e, b * cap, D_MODEL), BF16),
            jax.ShapeDtypeStruct((e, b * cap, 1), F32),
        ],
        compiler_params=_params("arbitrary", "arbitrary"),
        name="dispatch",
    )(off, slot, aff_t, h)


def _experts_kernel(xe_ref, gate_ref, wg_ref, wu_ref, wd_ref, y_ref, acc_ref):
    f = pl.program_id(1)

    def hidden_chunk(first):
        wg = wg_ref[0].astype(BF16)
        wu = wu_ref[0].astype(BF16)
        wd = wd_ref[0].astype(BF16)
        for mb in range(xe_ref.shape[1] // EXPERT_ROWS):
            rows = slice(mb * EXPERT_ROWS, (mb + 1) * EXPERT_ROWS)
            xe = xe_ref[0, rows, :]
            a = _dot(xe, wg)
            b = _dot(xe, wu)
            part = _dot((a * jax.nn.sigmoid(a) * b).astype(BF16), wd)
            acc_ref[rows, :] = part if first else acc_ref[rows, :] + part

    pl.when(f == 0)(functools.partial(hidden_chunk, True))
    pl.when(f > 0)(functools.partial(hidden_chunk, False))

    @pl.when(f == pl.num_programs(1) - 1)
    def _():
        y_ref[0] = (acc_ref[...] * gate_ref[0]).astype(y_ref.dtype)


def _experts(xe, gate, wg, wu, wd, tf=512):
    e, m, _ = xe.shape
    nf = D_EXPERT // tf
    return pl.pallas_call(
        _experts_kernel,
        grid=(e, nf),
        in_specs=[
            pl.BlockSpec((1, m, D_MODEL), lambda ei, f: (ei, 0, 0)),
            pl.BlockSpec((1, m, 1), lambda ei, f: (ei, 0, 0)),
            pl.BlockSpec((1, D_MODEL, tf), lambda ei, f: (ei, 0, f)),
            pl.BlockSpec((1, D_MODEL, tf), lambda ei, f: (ei, 0, f)),
            pl.BlockSpec((1, tf, D_MODEL), lambda ei, f: (ei, f, 0)),
        ],
        out_specs=pl.BlockSpec((1, m, D_MODEL), lambda ei, f: (ei, 0, 0)),
        out_shape=jax.ShapeDtypeStruct((e, m, D_MODEL), BF16),
        scratch_shapes=[pltpu.VMEM((m, D_MODEL), F32)],
        compiler_params=_params("arbitrary", "arbitrary"),
        name="experts",
    )(xe, gate, wg, wu, wd)


def _combine_kernel(off_ref, x1_ref, st_ref, y_ref, p_ref, gn_ref, wgbf_ref, wpbf_ref, gp_ref,
                    o_ref, *, cap):
    bi = pl.program_id(0)
    nsub = x1_ref.shape[1] // ROUTE_TILE
    row_groups = [slice(sub * ROUTE_TILE, (sub + 1) * ROUTE_TILE) for sub in range(nsub)]
    geo = [_window_starts(off_ref, bi, pl.program_id(1) * nsub + sub, cap) for sub in range(nsub)]
    ok = functools.reduce(jnp.logical_and, [fits for _, fits in geo])

    def windowed_scatter(rows, starts):
        st = st_ref[0, rows, :]
        lane = lax.broadcasted_iota(jnp.int32, (ROUTE_TILE, LANES), 1)
        low = lane < ROUTE_WIN
        total = None
        for g in range(N_EXPERTS // ROUTE_GROUP):
            halves = []
            wins = []
            for half in range(ROUTE_GROUP // 2):
                e0 = g * ROUTE_GROUP + 2 * half
                t0 = st[:, e0:e0 + 1] - starts[e0]
                t1 = st[:, e0 + 1:e0 + 2] + (ROUTE_WIN - starts[e0 + 1])
                halves.append(jnp.where(lane == jnp.where(low, t0, t1), 1.0, 0.0).astype(BF16))
                wins.append(y_ref[e0, pl.ds(starts[e0], ROUTE_WIN), :])
                wins.append(y_ref[e0 + 1, pl.ds(starts[e0 + 1], ROUTE_WIN), :])
            part = _dot(jnp.concatenate(halves, axis=1), jnp.concatenate(wins, axis=0))
            total = part if total is None else total + part
        return total

    def dense_scatter(rows):
        st = st_ref[0, rows, :]
        lane = lax.broadcasted_iota(jnp.int32, (ROUTE_TILE, cap), 1)
        total = None
        for e in range(N_EXPERTS):
            onehot = jnp.where(lane == st[:, e:e + 1], 1.0, 0.0).astype(BF16)
            part = _dot(onehot, y_ref[e])
            total = part if total is None else total + part
        return total

    def gate_stage(rows, ffn):
        x2 = x1_ref[0, rows, :] + ffn
        return x2, _dot(_rms(x2, gn_ref[...]).astype(BF16), wgbf_ref[...])

    def output_stage(rows, x2, gate_logits):
        emb = _rms(_dot(p_ref[0, rows, :].astype(BF16), wpbf_ref[...]), gp_ref[...])
        o_ref[0, rows, :] = x2 + jax.nn.sigmoid(gate_logits) * emb

    def pipeline(scatter_stage):
        ffn, gated = {}, {}
        for step in range(nsub + 2):
            if step < nsub:
                ffn[step] = scatter_stage(step)
            if 0 <= step - 1 < nsub:
                gated[step - 1] = gate_stage(row_groups[step - 1], ffn.pop(step - 1))
            if 0 <= step - 2 < nsub:
                output_stage(row_groups[step - 2], *gated.pop(step - 2))

    pl.when(ok)(lambda: pipeline(lambda g: windowed_scatter(row_groups[g], geo[g][0])))
    pl.when(jnp.logical_not(ok))(lambda: pipeline(lambda g: dense_scatter(row_groups[g])))


def _combine(off, x1, slot_t, y, p, gn, wg, wp, gp, cap, tm=1024):
    b, s, _ = x1.shape
    assert tm % ROUTE_TILE == 0
    tile = lambda w: pl.BlockSpec((1, tm, w), lambda bi, i, off_ref: (bi, i, 0))
    fixed2 = lambda shape: pl.BlockSpec(shape, lambda bi, i, off_ref: (0, 0))
    return pl.pallas_call(
        functools.partial(_combine_kernel, cap=cap),
        grid_spec=pltpu.PrefetchScalarGridSpec(
            num_scalar_prefetch=1,
            grid=(b, s // tm),
            in_specs=[
                tile(D_MODEL),
                tile(LANES),
                pl.BlockSpec((N_EXPERTS, cap, D_MODEL), lambda bi, i, off_ref: (0, bi, 0)),
                tile(PLE_DIM),
                fixed2((1, D_MODEL)),
                pl.BlockSpec(wg.shape, lambda bi, i, off_ref: (0, 0), pipeline_mode=pl.Buffered(1)),
                pl.BlockSpec(wp.shape, lambda bi, i, off_ref: (0, 0), pipeline_mode=pl.Buffered(1)),
                fixed2((1, D_MODEL)),
            ],
            out_specs=tile(D_MODEL),
        ),
        out_shape=jax.ShapeDtypeStruct((b, s, D_MODEL), F32),
        compiler_params=_params("arbitrary", "arbitrary"),
        name="combine",
    )(off, x1, slot_t, y, p, gn, wg.astype(BF16), wp.astype(BF16), gp)


def kernel(x, p, norm_mix, w_in, w_pool, pool_scale, q_norm, k_norm, rpb, w_out, norm_ffn, w_router, w_gate, w_up, w_down, norm_ple, w_ple_gate, w_ple_proj, norm_ple_post):
    b, s, d = x.shape
    depth = w_in.shape[0]
    cap = EC_CAPACITY * s // N_EXPERTS
    head = jnp.arange(ATTN_WIDTH) // HEAD_DIM
    block_diag = (head[:, None] == head[None, :]).astype(BF16)
    row = lambda a: a.reshape(1, -1)
    for i in range(depth):
        u, q, k, v = _in_proj(x.reshape(b * s, d), row(norm_mix[i]), w_in[i], block_diag,
                              row(jnp.tile(q_norm[i], ATTN_HEADS)), row(jnp.tile(k_norm[i], ATTN_HEADS)))
        shp = lambda a: a.reshape(b, s, -1)
        y_attn = _natten(shp(q), shp(k), shp(v), _attn_bias_table(rpb[i]))
        x1, h, aff_t = _mix(x, shp(u), y_attn, w_pool[i], row(pool_scale[i]), w_out[i],
                            row(norm_ffn[i]), w_router[i].T)
        slot, slot_t, off = _route(aff_t, cap)
        off = off[:, :OFF_STRIDE].reshape(-1)
        xe, gate = _dispatch(off, slot, aff_t, h, cap)
        y = _experts(xe, gate, w_gate[i], w_up[i], w_down[i])
        x = _combine(off, x1, slot_t, y, p[i], row(norm_ple[i]), w_ple_gate[i], w_ple_proj[i],
                     row(norm_ple_post[i]), cap)
    return x
```

```python
import functools

import jax
import jax.numpy as jnp
from jax import lax
from jax.experimental import pallas as pl
from jax.experimental.pallas import tpu as pltpu

D_MODEL = 1024
GRID_W = 64
POOL_WINDOWS = (2, 4, 8, 16)
POOL_WIDTH = D_MODEL // 2
POOL_GROUP = POOL_WIDTH // len(POOL_WINDOWS)
ATTN_HEADS = 8
HEAD_DIM = (D_MODEL // 2) // ATTN_HEADS
ATTN_WIDTH = ATTN_HEADS * HEAD_DIM
WIN_ROWS_MAX = 8
WIN_COLS = 16
N_EXPERTS = 16
EC_CAPACITY = 2
D_EXPERT = 2 * D_MODEL
PLE_DIM = 256
RMS_EPS = 1e-6

LANES = 128
POOL_HALO = 8
LOG2E = 1.4426950408889634
MASK_BIAS = -1e30
NATTEN_UNROLL = 8
ROUTE_TILE = 256
ROUTE_WIN = 64
ROUTE_GROUP = 4
OFF_STRIDE = 16
NOT_SELECTED = -(1 << 20)
assert 2 * ROUTE_WIN == LANES and ROUTE_GROUP % 2 == 0
MIX_ROWS = 256
PROJ_ROWS = 256
EXPERT_ROWS = 512
VMEM_LIMIT = 56 * 1024 * 1024

BF16 = jnp.bfloat16
F32 = jnp.float32


def _params(*sem):
    return pltpu.CompilerParams(dimension_semantics=sem, vmem_limit_bytes=VMEM_LIMIT)


def _rms(x, g):
    return x * lax.rsqrt(jnp.mean(x * x, axis=-1, keepdims=True) + RMS_EPS) * g


def _dot(a, b):
    return jnp.dot(a, b, preferred_element_type=F32)


def _dot_nt(a, b):
    return lax.dot_general(a, b, (((1,), (1,)), ((), ())), preferred_element_type=F32)


def _in_proj_kernel(x_ref, g_ref, w_ref, qg_ref, kg_ref,
                    u_ref, q_ref, k_ref, v_ref, wbf_ref):
    @pl.when(pl.program_id(0) == 0)
    def _():
        wbf_ref[...] = w_ref[...].astype(BF16)

    def head_norm(t, g):
        low = lax.broadcasted_iota(jnp.int32, (t.shape[0], LANES), 1) < HEAD_DIM
        out = []
        for j in range(t.shape[1] // LANES):
            tile = t[:, j * LANES:(j + 1) * LANES]
            sq = tile * tile
            sa = jnp.sum(jnp.where(low, sq, 0.0), axis=-1, keepdims=True)
            sb = jnp.sum(sq, axis=-1, keepdims=True) - sa
            ra = lax.rsqrt(sa * (1.0 / HEAD_DIM) + RMS_EPS)
            rb = lax.rsqrt(sb * (1.0 / HEAD_DIM) + RMS_EPS)
            out.append(tile * jnp.where(low, ra, rb) * g[:, j * LANES:(j + 1) * LANES])
        return jnp.concatenate(out, axis=-1)

    def norm_stage(rows):
        return _rms(x_ref[rows, :], g_ref[...]).astype(BF16)

    def proj_stage(h):
        return _dot(h, wbf_ref[...])

    def head_stage(rows, z):
        u_ref[rows, :] = z[:, :POOL_WIDTH]
        q = z[:, POOL_WIDTH:POOL_WIDTH + ATTN_WIDTH]
        k = z[:, POOL_WIDTH + ATTN_WIDTH:POOL_WIDTH + 2 * ATTN_WIDTH]
        v = z[:, POOL_WIDTH + 2 * ATTN_WIDTH:]
        q_ref[rows, :] = (head_norm(q, qg_ref[...]) * (HEAD_DIM ** -0.5 * LOG2E)).astype(BF16)
        k_ref[rows, :] = head_norm(k, kg_ref[...]).astype(BF16)
        v_ref[rows, :] = v.astype(BF16)

    groups = [slice(sub * PROJ_ROWS, (sub + 1) * PROJ_ROWS) for sub in range(x_ref.shape[0] // PROJ_ROWS)]
    zs = [proj_stage(norm_stage(rows)) for rows in groups]
    for rows, z in zip(groups, zs):
        head_stage(rows, z)


def _in_proj(x2, g, w, qg, kg, tm=1024):
    n = x2.shape[0]
    zw = w.shape[1]
    assert 2 * HEAD_DIM == LANES
    row = lambda i: (i, 0)
    fixed = lambda i: (0, 0)
    return pl.pallas_call(
        _in_proj_kernel,
        grid=(n // tm,),
        in_specs=[
            pl.BlockSpec((tm, D_MODEL), row),
            pl.BlockSpec((1, D_MODEL), fixed),
            pl.BlockSpec((D_MODEL, zw), fixed, pipeline_mode=pl.Buffered(1)),
            pl.BlockSpec((1, ATTN_WIDTH), fixed),
            pl.BlockSpec((1, ATTN_WIDTH), fixed),
        ],
        out_specs=[
            pl.BlockSpec((tm, POOL_WIDTH), row),
            pl.BlockSpec((tm, ATTN_WIDTH), row),
            pl.BlockSpec((tm, ATTN_WIDTH), row),
            pl.BlockSpec((tm, ATTN_WIDTH), row),
        ],
        out_shape=[
            jax.ShapeDtypeStruct((n, POOL_WIDTH), F32),
            jax.ShapeDtypeStruct((n, ATTN_WIDTH), BF16),
            jax.ShapeDtypeStruct((n, ATTN_WIDTH), BF16),
            jax.ShapeDtypeStruct((n, ATTN_WIDTH), BF16),
        ],
        scratch_shapes=[pltpu.VMEM((D_MODEL, zw), BF16)],
        compiler_params=_params("arbitrary"),
        name="in_proj",
    )(x2, g, w, qg, kg)


def _natten_kernel(q_ref, k_ref, v_ref, tbl_ref, o_ref, bias_ref, *, rows, kh):
    band = kh * GRID_W
    lane = lax.broadcasted_iota(jnp.int32, (GRID_W, LANES), 1)
    first = lane < HEAD_DIM

    for hh in range(2):
        for d0 in range(WIN_ROWS_MAX):
            for kk in range(kh // 2):
                bias_ref[hh, d0, :, kk * LANES:(kk + 1) * LANES] = jnp.where(
                    lane < GRID_W, tbl_ref[hh, d0 + 2 * kk], tbl_ref[hh, d0 + 2 * kk + 1])

    def geometry(g):
        geo = []
        for r in range(g * NATTEN_UNROLL, (g + 1) * NATTEN_UNROLL):
            rs = min(max(r - kh // 2, 0), rows - kh)
            geo.append((rs - r + (WIN_ROWS_MAX - 1), r * GRID_W, rs * GRID_W))
        return geo

    def score_stage(g):
        scores = []
        for d0, q0, k0 in geometry(g):
            qr = q_ref[0, q0:q0 + GRID_W, :]
            kb = k_ref[0, k0:k0 + band, :]
            for hh in range(2):
                sel = first if hh == 0 else jnp.logical_not(first)
                qm = jnp.where(sel, qr, jnp.zeros_like(qr))
                scores.append(_dot_nt(qm, kb) + bias_ref[hh, d0])
        return scores

    def softmax_stage(scores):
        probs = []
        for s in scores:
            e = jnp.exp2(s - jnp.max(s, axis=-1, keepdims=True))
            probs.append((e.astype(BF16), jnp.sum(e, axis=-1, keepdims=True)))
        return probs

    def value_stage(g, probs):
        for idx, (d0, q0, k0) in enumerate(geometry(g)):
            vb = v_ref[0, k0:k0 + band, :]
            outs = [_dot(e, vb) / l for e, l in probs[2 * idx:2 * idx + 2]]
            o_ref[0, q0:q0 + GRID_W, :] = jnp.where(first, outs[0], outs[1]).astype(o_ref.dtype)

    ngroups = rows // NATTEN_UNROLL
    scores, probs = {}, {}
    for step in range(ngroups + 2):
        if step < ngroups:
            scores[step] = score_stage(step)
        if 0 <= step - 1 < ngroups:
            probs[step - 1] = softmax_stage(scores.pop(step - 1))
        if 0 <= step - 2 < ngroups:
            value_stage(step - 2, probs.pop(step - 2))


def _natten(q, k, v, bias):
    b, s, _ = q.shape
    rows = s // GRID_W
    kh = min(WIN_ROWS_MAX, rows)
    assert rows % NATTEN_UNROLL == 0 and kh % 2 == 0 and 2 * GRID_W == LANES
    pairs = ATTN_WIDTH // LANES
    blk = pl.BlockSpec((1, s, LANES), lambda bi, p: (bi, 0, p))
    return pl.pallas_call(
        functools.partial(_natten_kernel, rows=rows, kh=kh),
        grid=(b, pairs),
        in_specs=[blk, blk, blk,
                  pl.BlockSpec((2,) + bias.shape[1:], lambda bi, p: (p, 0, 0, 0))],
        out_specs=blk,
        out_shape=jax.ShapeDtypeStruct((b, s, ATTN_WIDTH), BF16),
        scratch_shapes=[pltpu.VMEM((2, WIN_ROWS_MAX, GRID_W, kh * GRID_W), F32)],
        compiler_params=_params("arbitrary", "arbitrary"),
        name="natten",
    )(q, k, v, bias)


def _attn_bias_table(rpb):
    c = jnp.arange(GRID_W)
    cs = jnp.clip(c - WIN_COLS // 2, 0, GRID_W - WIN_COLS)
    j = jnp.arange(GRID_W)
    valid = (j[None, :] >= cs[:, None]) & (j[None, :] < cs[:, None] + WIN_COLS)
    dc = j[None, :] - c[:, None] + (WIN_COLS - 1)
    pick = ((dc[None] == jnp.arange(2 * WIN_COLS - 1)[:, None, None]) & valid[None]).astype(F32)
    t = jnp.einsum('hrd,dcj->hrcj', rpb.astype(F32), pick, precision=lax.Precision.HIGHEST)
    t = jnp.where(valid, t * LOG2E, MASK_BIAS)
    return jnp.concatenate([t, t], axis=-1)


def _window_sum(upad, w, tm):
    n = upad.shape[0]
    fwd = upad
    span = 1
    while span < min(w, POOL_HALO):
        fwd = fwd + pltpu.roll(fwd, n - span, axis=0)
        span *= 2
    centre = slice(POOL_HALO, POOL_HALO + tm)
    if w == 2 * POOL_HALO:
        return fwd[0:tm] + fwd[centre]
    return pltpu.roll(fwd, w // 2, axis=0)[centre]


def _mix_kernel(x_ref, u_ref, up_ref, un_ref, ya_ref, ic_ref, wp_ref, ps_ref, wo_ref, g_ref, wr_ref,
                x1_ref, h_ref, aff_ref, upad_ref, wobf_ref, wpbf_ref, *, tm):
    i = pl.program_id(1)

    @pl.when((pl.program_id(0) == 0) & (i == 0))
    def _():
        wobf_ref[...] = wo_ref[...].astype(BF16)
        wpbf_ref[...] = wp_ref[...].astype(BF16)

    upad_ref[0:POOL_HALO, :] = jnp.where(i > 0, up_ref[0], 0.0)
    upad_ref[POOL_HALO:POOL_HALO + tm, :] = u_ref[0]
    upad_ref[POOL_HALO + tm:, :] = jnp.where(i < pl.num_programs(1) - 1, un_ref[0], 0.0)

    ngroups = tm // MIX_ROWS
    group_cols = [slice(gi * POOL_GROUP, (gi + 1) * POOL_GROUP) for gi in range(len(POOL_WINDOWS))]

    def pool_stage(g):
        ds = []
        for cols, w in zip(group_cols, POOL_WINDOWS):
            upad = upad_ref[g * MIX_ROWS:(g + 1) * MIX_ROWS + 2 * POOL_HALO, cols]
            edge = jnp.full((POOL_HALO, POOL_GROUP), 1.0 / w, F32)
            head = jnp.where(i == 0, ic_ref[0:POOL_HALO, cols], 1.0 / w) if g == 0 else edge
            tail = (jnp.where(i == pl.num_programs(1) - 1, ic_ref[POOL_HALO:, cols], 1.0 / w)
                    if g == ngroups - 1 else edge)
            inv = jnp.concatenate(
                [head, jnp.full((MIX_ROWS - 2 * POOL_HALO, POOL_GROUP), 1.0 / w, F32), tail], axis=0)
            d = _window_sum(upad, w, MIX_ROWS) * inv - upad[POOL_HALO:POOL_HALO + MIX_ROWS]
            ds.append(d.astype(BF16))
        return ds

    def proj_stage(g, ds):
        rows = slice(g * MIX_ROWS, (g + 1) * MIX_ROWS)
        ypool = [_dot(d, wpbf_ref[gi]) * ps_ref[:, cols] for gi, (d, cols) in enumerate(zip(ds, group_cols))]
        ypool = jnp.concatenate(ypool, axis=-1).astype(BF16)
        return _dot(ypool, wobf_ref[:POOL_WIDTH, :]) + _dot(ya_ref[0, rows, :], wobf_ref[POOL_WIDTH:, :])

    def route_stage(g, mix):
        rows = slice(g * MIX_ROWS, (g + 1) * MIX_ROWS)
        x1 = x_ref[0, rows, :] + mix
        x1_ref[0, rows, :] = x1
        h = _rms(x1, g_ref[...]).astype(BF16)
        h_ref[0, rows, :] = h
        logits = _dot_nt(wr_ref[...].astype(BF16), h)
        m = jnp.max(logits, axis=0, keepdims=True)
        e = jnp.exp(logits - m)
        aff_ref[0, :, rows] = e / jnp.sum(e, axis=0, keepdims=True)

    pooled, mixed = {}, {}
    for step in range(ngroups + 2):
        if step < ngroups:
            pooled[step] = pool_stage(step)
        if 0 <= step - 1 < ngroups:
            mixed[step - 1] = proj_stage(step - 1, pooled.pop(step - 1))
        if 0 <= step - 2 < ngroups:
            route_stage(step - 2, mixed.pop(step - 2))


def _pool_edge_inverse_counts(s):
    t = jnp.concatenate([jnp.arange(POOL_HALO), jnp.arange(s - POOL_HALO, s)])
    cols = []
    for w in POOL_WINDOWS:
        lo = jnp.clip(t - w // 2, 0, s - 1)
        hi = jnp.clip(t + (w - w // 2) - 1, 0, s - 1)
        inv = 1.0 / (hi - lo + 1).astype(F32)
        cols.append(jnp.broadcast_to(inv[:, None], (2 * POOL_HALO, POOL_GROUP)))
    return jnp.concatenate(cols, axis=-1)


def _mix(x, u, ya, wp, ps, wo, g, wr_t, tm=1024):
    b, s, _ = x.shape
    nt = s // tm
    hb = tm // POOL_HALO
    tile = lambda w: pl.BlockSpec((1, tm, w), lambda bi, i: (bi, i, 0))
    fixed2 = lambda shape: pl.BlockSpec(shape, lambda bi, i: (0, 0))
    return pl.pallas_call(
        functools.partial(_mix_kernel, tm=tm),
        grid=(b, nt),
        in_specs=[
            tile(D_MODEL),
            tile(POOL_WIDTH),
            pl.BlockSpec((1, POOL_HALO, POOL_WIDTH), lambda bi, i: (bi, jnp.maximum(i * hb - 1, 0), 0)),
            pl.BlockSpec((1, POOL_HALO, POOL_WIDTH),
                         lambda bi, i: (bi, jnp.minimum((i + 1) * hb, s // POOL_HALO - 1), 0)),
            tile(ATTN_WIDTH),
            fixed2((2 * POOL_HALO, POOL_WIDTH)),
            pl.BlockSpec(wp.shape, lambda bi, i: (0, 0, 0)),
            fixed2((1, POOL_WIDTH)),
            pl.BlockSpec(wo.shape, lambda bi, i: (0, 0), pipeline_mode=pl.Buffered(1)),
            fixed2((1, D_MODEL)),
            fixed2(wr_t.shape),
        ],
        out_specs=[
            tile(D_MODEL),
            tile(D_MODEL),
            pl.BlockSpec((1, N_EXPERTS, tm), lambda bi, i: (bi, 0, i)),
        ],
        out_shape=[
            jax.ShapeDtypeStruct((b, s, D_MODEL), F32),
            jax.ShapeDtypeStruct((b, s, D_MODEL), BF16),
            jax.ShapeDtypeStruct((b, N_EXPERTS, s), F32),
        ],
        scratch_shapes=[
            pltpu.VMEM((tm + 2 * POOL_HALO, POOL_WIDTH), F32),
            pltpu.VMEM(wo.shape, BF16),
            pltpu.VMEM(wp.shape, BF16),
        ],
        compiler_params=_params("arbitrary", "arbitrary"),
        name="mix",
    )(x, u, u, u, ya, _pool_edge_inverse_counts(s), wp, ps, wo, g, wr_t)


def _lane_cumsum_exclusive(m):
    e, s = m.shape
    r = lax.broadcasted_iota(jnp.int32, (LANES, LANES), 0)
    c = lax.broadcasted_iota(jnp.int32, (LANES, LANES), 1)
    upper = (r < c).astype(BF16)
    carry = jnp.zeros((e, 1), F32)
    out = []
    for blk in range(s // LANES):
        piece = m[:, blk * LANES:(blk + 1) * LANES]
        out.append(_dot(piece.astype(BF16), upper) + carry)
        carry = carry + jnp.sum(piece, axis=-1, keepdims=True)
    return jnp.concatenate(out, axis=-1)


def _route_kernel(aff_ref, slot_ref, slot_t_ref, off_ref, *, cap):
    aff = aff_ref[...]
    capf = jnp.float32(cap)

    def count_ge(cand_bits):
        return jnp.sum((aff >= pltpu.bitcast(cand_bits, F32)).astype(F32), axis=-1, keepdims=True)

    def search(step, ans):
        cand = ans | (jnp.int32(1) << (30 - step))
        return jnp.where(count_ge(cand) >= capf, cand, ans)

    thr = pltpu.bitcast(lax.fori_loop(0, 31, search, jnp.zeros((aff.shape[0], 1), jnp.int32)), F32)
    gt = aff > thr
    eq = aff == thr
    need = capf - jnp.sum(gt.astype(F32), axis=-1, keepdims=True)
    sel = gt | (eq & (_lane_cumsum_exclusive(eq.astype(F32)) < need))
    self = sel.astype(F32)
    slot = jnp.where(sel, _lane_cumsum_exclusive(self), float(NOT_SELECTED))
    slot_ref[...] = slot.astype(jnp.int32)
    pad = jnp.full((LANES - N_EXPERTS, slot.shape[1]), float(NOT_SELECTED), F32)
    for bi in range(slot_t_ref.shape[0]):
        mine = slot[bi * N_EXPERTS:(bi + 1) * N_EXPERTS]
        slot_t_ref[bi] = jnp.concatenate([mine, pad], axis=0).T.astype(jnp.int32)
    lane = lax.broadcasted_iota(jnp.int32, (slot.shape[0], LANES), 1)
    off = jnp.zeros((slot.shape[0], LANES), F32)
    run = jnp.zeros((slot.shape[0], 1), F32)
    for i in range(1, slot.shape[1] // ROUTE_TILE + 1):
        run = run + jnp.sum(self[:, (i - 1) * ROUTE_TILE:i * ROUTE_TILE], axis=-1, keepdims=True)
        off = jnp.where(lane == i, run, off)
    off_ref[...] = off.astype(jnp.int32)


def _route(aff_t, cap):
    b, e, s = aff_t.shape
    assert e == N_EXPERTS and s % ROUTE_TILE == 0 and s // ROUTE_TILE < OFF_STRIDE
    slot, slot_t, off = pl.pallas_call(
        functools.partial(_route_kernel, cap=cap),
        out_shape=[
            jax.ShapeDtypeStruct((b * e, s), jnp.int32),
            jax.ShapeDtypeStruct((b, s, LANES), jnp.int32),
            jax.ShapeDtypeStruct((b * e, LANES), jnp.int32),
        ],
        compiler_params=pltpu.CompilerParams(vmem_limit_bytes=VMEM_LIMIT),
        name="route",
    )(aff_t.reshape(b * e, s))
    return slot.reshape(b, e, s), slot_t, off


def _window_starts(off_ref, bi, ti, cap):
    starts = []
    ok = None
    for e in range(N_EXPERTS):
        base = (bi * N_EXPERTS + e) * OFF_STRIDE + ti
        lo = off_ref[base]
        hi = off_ref[base + 1]
        st = jnp.minimum(jnp.bitwise_and(lo, -16), cap - ROUTE_WIN)
        fits = hi <= st + ROUTE_WIN
        ok = fits if ok is None else jnp.logical_and(ok, fits)
        starts.append(pl.multiple_of(st, 16))
    return starts, ok


def _dispatch_kernel(off_ref, slot_ref, aff_ref, h_ref, xe_ref, gate_ref, *, cap):
    bi = pl.program_id(0)
    ti = pl.program_id(1)

    @pl.when(ti == 0)
    def _():
        xe_ref[...] = jnp.zeros_like(xe_ref)
        gate_ref[...] = jnp.zeros_like(gate_ref)

    nsub = h_ref.shape[1] // ROUTE_TILE
    col_groups = [slice(sub * ROUTE_TILE, (sub + 1) * ROUTE_TILE) for sub in range(nsub)]
    geo = [_window_starts(off_ref, bi, ti * nsub + sub, cap) for sub in range(nsub)]
    ok = functools.reduce(jnp.logical_and, [fits for _, fits in geo])

    @pl.when(ok)
    def _():
        row = lax.broadcasted_iota(jnp.int32, (ROUTE_WIN, ROUTE_TILE), 0)
        for cols, (starts, _) in zip(col_groups, geo):
            slot = slot_ref[0, :, cols]
            aff = aff_ref[0, :, cols]
            hits = [row == (slot[e:e + 1, :] - starts[e]) for e in range(N_EXPERTS)]
            onehot = jnp.concatenate([jnp.where(hh, 1.0, 0.0).astype(BF16) for hh in hits], axis=0)
            res = _dot(onehot, h_ref[0, cols, :])
            for e, hh in enumerate(hits):
                win = pl.ds(starts[e], ROUTE_WIN)
                xe_ref[e, win, :] += res[e * ROUTE_WIN:(e + 1) * ROUTE_WIN].astype(BF16)
                gate_ref[e, win, :] += jnp.sum(jnp.where(hh, aff[e:e + 1, :], 0.0), axis=-1, keepdims=True)

    @pl.when(jnp.logical_not(ok))
    def _():
        row = lax.broadcasted_iota(jnp.int32, (cap, ROUTE_TILE), 0)
        for cols in col_groups:
            slot = slot_ref[0, :, cols]
            aff = aff_ref[0, :, cols]
            h = h_ref[0, cols, :]
            for e in range(N_EXPERTS):
                hh = row == slot[e:e + 1, :]
                xe_ref[e] += _dot(jnp.where(hh, 1.0, 0.0).astype(BF16), h).astype(BF16)
                gate_ref[e] += jnp.sum(jnp.where(hh, aff[e:e + 1, :], 0.0), axis=-1, keepdims=True)


def _dispatch(off, slot, aff_t, h, cap, tm=1024):
    b, e, s = slot.shape
    assert tm % ROUTE_TILE == 0
    rows = pl.BlockSpec((1, e, tm), lambda bi, ti, off_ref: (bi, 0, ti))
    return pl.pallas_call(
        functools.partial(_dispatch_kernel, cap=cap),
        grid_spec=pltpu.PrefetchScalarGridSpec(
            num_scalar_prefetch=1,
            grid=(b, s // tm),
            in_specs=[rows, rows,
                      pl.BlockSpec((1, tm, D_MODEL), lambda bi, ti, off_ref: (bi, ti, 0))],
            out_specs=[
                pl.BlockSpec((e, cap, D_MODEL), lambda bi, ti, off_ref: (0, bi, 0)),
                pl.BlockSpec((e, cap, 1), lambda bi, ti, off_ref: (0, bi, 0)),
            ],
        ),
        out_shape=[
            jax.ShapeDtypeStruct((e, b * cap, D_MODEL), BF16),
            jax.ShapeDtypeStruct((e, b * cap, 1), F32),
        ],
        compiler_params=_params("arbitrary", "arbitrary"),
        name="dispatch",
    )(off, slot, aff_t, h)


def _experts_kernel(xe_ref, gate_ref, wg_ref, wu_ref, wd_ref, y_ref, acc_ref):
    f = pl.program_id(1)

    def hidden_chunk(first):
        wg = wg_ref[0].astype(BF16)
        wu = wu_ref[0].astype(BF16)
        wd = wd_ref[0].astype(BF16)
        for mb in range(xe_ref.shape[1] // EXPERT_ROWS):
            rows = slice(mb * EXPERT_ROWS, (mb + 1) * EXPERT_ROWS)
            xe = xe_ref[0, rows, :]
            a = _dot(xe, wg)
            b = _dot(xe, wu)
            part = _dot((a * jax.nn.sigmoid(a) * b).astype(BF16), wd)
            acc_ref[rows, :] = part if first else acc_ref[rows, :] + part

    pl.when(f == 0)(functools.partial(hidden_chunk, True))
    pl.when(f > 0)(functools.partial(hidden_chunk, False))

    @pl.when(f == pl.num_programs(1) - 1)
    def _():
        y_ref[0] = (acc_ref[...] * gate_ref[0]).astype(y_ref.dtype)


def _experts(xe, gate, wg, wu, wd, tf=512):
    e, m, _ = xe.shape
    nf = D_EXPERT // tf
    return pl.pallas_call(
        _experts_kernel,
        grid=(e, nf),
        in_specs=[
            pl.BlockSpec((1, m, D_MODEL), lambda ei, f: (ei, 0, 0)),
            pl.BlockSpec((1, m, 1), lambda ei, f: (ei, 0, 0)),
            pl.BlockSpec((1, D_MODEL, tf), lambda ei, f: (ei, 0, f)),
            pl.BlockSpec((1, D_MODEL, tf), lambda ei, f: (ei, 0, f)),
            pl.BlockSpec((1, tf, D_MODEL), lambda ei, f: (ei, f, 0)),
        ],
        out_specs=pl.BlockSpec((1, m, D_MODEL), lambda ei, f: (ei, 0, 0)),
        out_shape=jax.ShapeDtypeStruct((e, m, D_MODEL), BF16),
        scratch_shapes=[pltpu.VMEM((m, D_MODEL), F32)],
        compiler_params=_params("arbitrary", "arbitrary"),
        name="experts",
    )(xe, gate, wg, wu, wd)


def _combine_kernel(off_ref, x1_ref, st_ref, y_ref, p_ref, gn_ref, wgbf_ref, wpbf_ref, gp_ref,
                    o_ref, *, cap):
    bi = pl.program_id(0)
    nsub = x1_ref.shape[1] // ROUTE_TILE
    row_groups = [slice(sub * ROUTE_TILE, (sub + 1) * ROUTE_TILE) for sub in range(nsub)]
    geo = [_window_starts(off_ref, bi, pl.program_id(1) * nsub + sub, cap) for sub in range(nsub)]
    ok = functools.reduce(jnp.logical_and, [fits for _, fits in geo])

    def windowed_scatter(rows, starts):
        st = st_ref[0, rows, :]
        lane = lax.broadcasted_iota(jnp.int32, (ROUTE_TILE, LANES), 1)
        low = lane < ROUTE_WIN
        total = None
        for g in range(N_EXPERTS // ROUTE_GROUP):
            halves = []
            wins = []
            for half in range(ROUTE_GROUP // 2):
                e0 = g * ROUTE_GROUP + 2 * half
                t0 = st[:, e0:e0 + 1] - starts[e0]
                t1 = st[:, e0 + 1:e0 + 2] + (ROUTE_WIN - starts[e0 + 1])
                halves.append(jnp.where(lane == jnp.where(low, t0, t1), 1.0, 0.0).astype(BF16))
                wins.append(y_ref[e0, pl.ds(starts[e0], ROUTE_WIN), :])
                wins.append(y_ref[e0 + 1, pl.ds(starts[e0 + 1], ROUTE_WIN), :])
            part = _dot(jnp.concatenate(halves, axis=1), jnp.concatenate(wins, axis=0))
            total = part if total is None else total + part
        return total

    def dense_scatter(rows):
        st = st_ref[0, rows, :]
        lane = lax.broadcasted_iota(jnp.int32, (ROUTE_TILE, cap), 1)
        total = None
        for e in range(N_EXPERTS):
            onehot = jnp.where(lane == st[:, e:e + 1], 1.0, 0.0).astype(BF16)
            part = _dot(onehot, y_ref[e])
            total = part if total is None else total + part
        return total

    def gate_stage(rows, ffn):
        x2 = x1_ref[0, rows, :] + ffn
        return x2, _dot(_rms(x2, gn_ref[...]).astype(BF16), wgbf_ref[...])

    def output_stage(rows, x2, gate_logits):
        emb = _rms(_dot(p_ref[0, rows, :].astype(BF16), wpbf_ref[...]), gp_ref[...])
        o_ref[0, rows, :] = x2 + jax.nn.sigmoid(gate_logits) * emb

    def pipeline(scatter_stage):
        ffn, gated = {}, {}
        for step in range(nsub + 2):
            if step < nsub:
                ffn[step] = scatter_stage(step)
            if 0 <= step - 1 < nsub:
                gated[step - 1] = gate_stage(row_groups[step - 1], ffn.pop(step - 1))
            if 0 <= step - 2 < nsub:
                output_stage(row_groups[step - 2], *gated.pop(step - 2))

    pl.when(ok)(lambda: pipeline(lambda g: windowed_scatter(row_groups[g], geo[g][0])))
    pl.when(jnp.logical_not(ok))(lambda: pipeline(lambda g: dense_scatter(row_groups[g])))


def _combine(off, x1, slot_t, y, p, gn, wg, wp, gp, cap, tm=1024):
    b, s, _ = x1.shape
    assert tm % ROUTE_TILE == 0
    tile = lambda w: pl.BlockSpec((1, tm, w), lambda bi, i, off_ref: (bi, i, 0))
    fixed2 = lambda shape: pl.BlockSpec(shape, lambda bi, i, off_ref: (0, 0))
    return pl.pallas_call(
        functools.partial(_combine_kernel, cap=cap),
        grid_spec=pltpu.PrefetchScalarGridSpec(
            num_scalar_prefetch=1,
            grid=(b, s // tm),
            in_specs=[
                tile(D_MODEL),
                tile(LANES),
                pl.BlockSpec((N_EXPERTS, cap, D_MODEL), lambda bi, i, off_ref: (0, bi, 0)),
                tile(PLE_DIM),
                fixed2((1, D_MODEL)),
                pl.BlockSpec(wg.shape, lambda bi, i, off_ref: (0, 0), pipeline_mode=pl.Buffered(1)),
                pl.BlockSpec(wp.shape, lambda bi, i, off_ref: (0, 0), pipeline_mode=pl.Buffered(1)),
                fixed2((1, D_MODEL)),
            ],
            out_specs=tile(D_MODEL),
        ),
        out_shape=jax.ShapeDtypeStruct((b, s, D_MODEL), F32),
        compiler_params=_params("arbitrary", "arbitrary"),
        name="combine",
    )(off, x1, slot_t, y, p, gn, wg.astype(BF16), wp.astype(BF16), gp)


def kernel(x, p, norm_mix, w_in, w_pool, pool_scale, q_norm, k_norm, rpb, w_out, norm_ffn, w_router, w_gate, w_up, w_down, norm_ple, w_ple_gate, w_ple_proj, norm_ple_post):
    b, s, d = x.shape
    depth = w_in.shape[0]
    cap = EC_CAPACITY * s // N_EXPERTS
    row = lambda a: a.reshape(1, -1)
    for i in range(depth):
        u, q, k, v = _in_proj(x.reshape(b * s, d), row(norm_mix[i]), w_in[i],
                              row(jnp.tile(q_norm[i], ATTN_HEADS)), row(jnp.tile(k_norm[i], ATTN_HEADS)))
        shp = lambda a: a.reshape(b, s, -1)
        y_attn = _natten(shp(q), shp(k), shp(v), _attn_bias_table(rpb[i]))
        x1, h, aff_t = _mix(x, shp(u), y_attn, w_pool[i], row(pool_scale[i]), w_out[i],
                            row(norm_ffn[i]), w_router[i].T)
        slot, slot_t, off = _route(aff_t, cap)
        off = off[:, :OFF_STRIDE].reshape(-1)
        xe, gate = _dispatch(off, slot, aff_t, h, cap)
        y = _experts(xe, gate, w_gate[i], w_up[i], w_down[i])
        x = _combine(off, x1, slot_t, y, p[i], row(norm_ple[i]), w_ple_gate[i], w_ple_proj[i],
                     row(norm_ple_post[i]), cap)
    return x
```

```python
import functools

import jax
import jax.numpy as jnp
from jax import lax
from jax.experimental import pallas as pl
from jax.experimental.pallas import tpu as pltpu

D_MODEL = 1024
GRID_W = 64
POOL_WINDOWS = (2, 4, 8, 16)
POOL_WIDTH = D_MODEL // 2
POOL_GROUP = POOL_WIDTH // len(POOL_WINDOWS)
ATTN_HEADS = 8
HEAD_DIM = (D_MODEL // 2) // ATTN_HEADS
ATTN_WIDTH = ATTN_HEADS * HEAD_DIM
WIN_ROWS_MAX = 8
WIN_COLS = 16
N_EXPERTS = 16
EC_CAPACITY = 2
D_EXPERT = 2 * D_MODEL
PLE_DIM = 256
RMS_EPS = 1e-6

LANES = 128
POOL_HALO = 8
LOG2E = 1.4426950408889634
MASK_BIAS = -1e30
NATTEN_UNROLL = 8
ROUTE_TILE = 256
ROUTE_WIN = 64
ROUTE_GROUP = 4
OFF_STRIDE = 16
NOT_SELECTED = -(1 << 20)
assert 2 * ROUTE_WIN == LANES and ROUTE_GROUP % 2 == 0
MIX_ROWS = 256
PROJ_ROWS = 256
EXPERT_ROWS = 512
VMEM_LIMIT = 56 * 1024 * 1024

BF16 = jnp.bfloat16
F32 = jnp.float32


def _params(*sem):
    return pltpu.CompilerParams(dimension_semantics=sem, vmem_limit_bytes=VMEM_LIMIT)


def _rms(x, g):
    return x * lax.rsqrt(jnp.mean(x * x, axis=-1, keepdims=True) + RMS_EPS) * g


def _dot(a, b):
    return jnp.dot(a, b, preferred_element_type=F32)


def _dot_nt(a, b):
    return lax.dot_general(a, b, (((1,), (1,)), ((), ())), preferred_element_type=F32)


def _in_proj_kernel(x_ref, g_ref, w_ref, qg_ref, kg_ref,
                    u_ref, q_ref, k_ref, v_ref, wbf_ref):
    @pl.when(pl.program_id(0) == 0)
    def _():
        wbf_ref[...] = w_ref[...].astype(BF16)

    def head_norm(t, g):
        low = lax.broadcasted_iota(jnp.int32, (t.shape[0], LANES), 1) < HEAD_DIM
        out = []
        for j in range(t.shape[1] // LANES):
            tile = t[:, j * LANES:(j + 1) * LANES]
            sq = tile * tile
            sa = jnp.sum(jnp.where(low, sq, 0.0), axis=-1, keepdims=True)
            sb = jnp.sum(sq, axis=-1, keepdims=True) - sa
            ra = lax.rsqrt(sa * (1.0 / HEAD_DIM) + RMS_EPS)
            rb = lax.rsqrt(sb * (1.0 / HEAD_DIM) + RMS_EPS)
            out.append(tile * jnp.where(low, ra, rb) * g[:, j * LANES:(j + 1) * LANES])
        return jnp.concatenate(out, axis=-1)

    def norm_stage(rows):
        return _rms(x_ref[rows, :], g_ref[...]).astype(BF16)

    def proj_stage(h):
        return _dot(h, wbf_ref[...])

    def head_stage(rows, z):
        u_ref[rows, :] = z[:, :POOL_WIDTH]
        q = z[:, POOL_WIDTH:POOL_WIDTH + ATTN_WIDTH]
        k = z[:, POOL_WIDTH + ATTN_WIDTH:POOL_WIDTH + 2 * ATTN_WIDTH]
        v = z[:, POOL_WIDTH + 2 * ATTN_WIDTH:]
        q_ref[rows, :] = (head_norm(q, qg_ref[...]) * (HEAD_DIM ** -0.5 * LOG2E)).astype(BF16)
        k_ref[rows, :] = head_norm(k, kg_ref[...]).astype(BF16)
        v_ref[rows, :] = v.astype(BF16)

    groups = [slice(sub * PROJ_ROWS, (sub + 1) * PROJ_ROWS) for sub in range(x_ref.shape[0] // PROJ_ROWS)]
    zs = [proj_stage(norm_stage(rows)) for rows in groups]
    for rows, z in zip(groups, zs):
        head_stage(rows, z)


def _in_proj(x2, g, w, qg, kg, tm=1024):
    n = x2.shape[0]
    zw = w.shape[1]
    assert 2 * HEAD_DIM == LANES
    row = lambda i: (i, 0)
    fixed = lambda i: (0, 0)
    return pl.pallas_call(
        _in_proj_kernel,
        grid=(n // tm,),
        in_specs=[
            pl.BlockSpec((tm, D_MODEL), row),
            pl.BlockSpec((1, D_MODEL), fixed),
            pl.BlockSpec((D_MODEL, zw), fixed, pipeline_mode=pl.Buffered(1)),
            pl.BlockSpec((1, ATTN_WIDTH), fixed),
            pl.BlockSpec((1, ATTN_WIDTH), fixed),
        ],
        out_specs=[
            pl.BlockSpec((tm, POOL_WIDTH), row),
            pl.BlockSpec((tm, ATTN_WIDTH), row),
            pl.BlockSpec((tm, ATTN_WIDTH), row),
            pl.BlockSpec((tm, ATTN_WIDTH), row),
        ],
        out_shape=[
            jax.ShapeDtypeStruct((n, POOL_WIDTH), F32),
            jax.ShapeDtypeStruct((n, ATTN_WIDTH), BF16),
            jax.ShapeDtypeStruct((n, ATTN_WIDTH), BF16),
            jax.ShapeDtypeStruct((n, ATTN_WIDTH), BF16),
        ],
        scratch_shapes=[pltpu.VMEM((D_MODEL, zw), BF16)],
        compiler_params=_params("arbitrary"),
        name="in_proj",
    )(x2, g, w, qg, kg)


def _natten_kernel(q_ref, k_ref, v_ref, tbl_ref, o_ref, bias_ref, *, rows, kh):
    band = kh * GRID_W
    lane = lax.broadcasted_iota(jnp.int32, (GRID_W, LANES), 1)
    first = lane < HEAD_DIM

    for hh in range(2):
        for d0 in range(WIN_ROWS_MAX):
            for kk in range(kh // 2):
                bias_ref[d0, hh * GRID_W:(hh + 1) * GRID_W, kk * LANES:(kk + 1) * LANES] = jnp.where(
                    lane < GRID_W, tbl_ref[hh, d0 + 2 * kk], tbl_ref[hh, d0 + 2 * kk + 1])

    def geometry(g):
        geo = []
        for r in range(g * NATTEN_UNROLL, (g + 1) * NATTEN_UNROLL):
            rs = min(max(r - kh // 2, 0), rows - kh)
            geo.append((rs - r + (WIN_ROWS_MAX - 1), r * GRID_W, rs * GRID_W))
        return geo

    def score_stage(g):
        scores = []
        for d0, q0, k0 in geometry(g):
            qr = q_ref[0, q0:q0 + GRID_W, :]
            zero = jnp.zeros_like(qr)
            q2 = jnp.concatenate([jnp.where(first, qr, zero), jnp.where(first, zero, qr)], axis=0)
            s = _dot_nt(q2, k_ref[0, k0:k0 + band, :]) + bias_ref[d0]
            scores.append((s, jnp.max(s, axis=-1, keepdims=True)))
        return scores

    def softmax_stage(scores):
        probs = []
        for s, m in scores:
            e = jnp.exp2(s - m)
            probs.append((e.astype(BF16), jnp.sum(e, axis=-1, keepdims=True)))
        return probs

    def value_stage(g, probs):
        for (d0, q0, k0), (e, l) in zip(geometry(g), probs):
            o = _dot(e, v_ref[0, k0:k0 + band, :]) / l
            o_ref[0, q0:q0 + GRID_W, :] = jnp.where(first, o[:GRID_W], o[GRID_W:]).astype(o_ref.dtype)

    ngroups = rows // NATTEN_UNROLL
    scores, probs = {}, {}
    for step in range(ngroups + 2):
        if step < ngroups:
            scores[step] = score_stage(step)
        if 0 <= step - 1 < ngroups:
            probs[step - 1] = softmax_stage(scores.pop(step - 1))
        if 0 <= step - 2 < ngroups:
            value_stage(step - 2, probs.pop(step - 2))


def _natten(q, k, v, bias):
    b, s, _ = q.shape
    rows = s // GRID_W
    kh = min(WIN_ROWS_MAX, rows)
    assert rows % NATTEN_UNROLL == 0 and kh % 2 == 0 and 2 * GRID_W == LANES
    pairs = ATTN_WIDTH // LANES
    blk = pl.BlockSpec((1, s, LANES), lambda bi, p: (bi, 0, p))
    return pl.pallas_call(
        functools.partial(_natten_kernel, rows=rows, kh=kh),
        grid=(b, pairs),
        in_specs=[blk, blk, blk,
                  pl.BlockSpec((2,) + bias.shape[1:], lambda bi, p: (p, 0, 0, 0))],
        out_specs=blk,
        out_shape=jax.ShapeDtypeStruct((b, s, ATTN_WIDTH), BF16),
        scratch_shapes=[pltpu.VMEM((WIN_ROWS_MAX, 2 * GRID_W, kh * GRID_W), F32)],
        compiler_params=_params("arbitrary", "arbitrary"),
        name="natten",
    )(q, k, v, bias)


def _attn_bias_table(rpb):
    c = jnp.arange(GRID_W)
    cs = jnp.clip(c - WIN_COLS // 2, 0, GRID_W - WIN_COLS)
    j = jnp.arange(GRID_W)
    valid = (j[None, :] >= cs[:, None]) & (j[None, :] < cs[:, None] + WIN_COLS)
    dc = j[None, :] - c[:, None] + (WIN_COLS - 1)
    pick = ((dc[None] == jnp.arange(2 * WIN_COLS - 1)[:, None, None]) & valid[None]).astype(F32)
    t = jnp.einsum('hrd,dcj->hrcj', rpb.astype(F32), pick, precision=lax.Precision.HIGHEST)
    t = jnp.where(valid, t * LOG2E, MASK_BIAS)
    return jnp.concatenate([t, t], axis=-1)


def _window_sum(upad, w, tm):
    n = upad.shape[0]
    fwd = upad
    span = 1
    while span < min(w, POOL_HALO):
        fwd = fwd + pltpu.roll(fwd, n - span, axis=0)
        span *= 2
    centre = slice(POOL_HALO, POOL_HALO + tm)
    if w == 2 * POOL_HALO:
        return fwd[0:tm] + fwd[centre]
    return pltpu.roll(fwd, w // 2, axis=0)[centre]


def _mix_kernel(x_ref, u_ref, up_ref, un_ref, ya_ref, ic_ref, wp_ref, ps_ref, wo_ref, g_ref, wr_ref,
                x1_ref, h_ref, aff_ref, upad_ref, wobf_ref, wpbf_ref, *, tm):
    i = pl.program_id(1)

    @pl.when((pl.program_id(0) == 0) & (i == 0))
    def _():
        wobf_ref[...] = wo_ref[...].astype(BF16)
        wpbf_ref[...] = wp_ref[...].astype(BF16)

    upad_ref[0:POOL_HALO, :] = jnp.where(i > 0, up_ref[0], 0.0)
    upad_ref[POOL_HALO:POOL_HALO + tm, :] = u_ref[0]
    upad_ref[POOL_HALO + tm:, :] = jnp.where(i < pl.num_programs(1) - 1, un_ref[0], 0.0)

    ngroups = tm // MIX_ROWS
    group_cols = [slice(gi * POOL_GROUP, (gi + 1) * POOL_GROUP) for gi in range(len(POOL_WINDOWS))]

    def pool_stage(g):
        ds = []
        for cols, w in zip(group_cols, POOL_WINDOWS):
            upad = upad_ref[g * MIX_ROWS:(g + 1) * MIX_ROWS + 2 * POOL_HALO, cols]
            edge = jnp.full((POOL_HALO, POOL_GROUP), 1.0 / w, F32)
            head = jnp.where(i == 0, ic_ref[0:POOL_HALO, cols], 1.0 / w) if g == 0 else edge
            tail = (jnp.where(i == pl.num_programs(1) - 1, ic_ref[POOL_HALO:, cols], 1.0 / w)
                    if g == ngroups - 1 else edge)
            inv = jnp.concatenate(
                [head, jnp.full((MIX_ROWS - 2 * POOL_HALO, POOL_GROUP), 1.0 / w, F32), tail], axis=0)
            d = _window_sum(upad, w, MIX_ROWS) * inv - upad[POOL_HALO:POOL_HALO + MIX_ROWS]
            ds.append(d.astype(BF16))
        return ds

    def proj_stage(g, ds):
        rows = slice(g * MIX_ROWS, (g + 1) * MIX_ROWS)
        ypool = [_dot(d, wpbf_ref[gi]) * ps_ref[:, cols] for gi, (d, cols) in enumerate(zip(ds, group_cols))]
        ypool = jnp.concatenate(ypool, axis=-1).astype(BF16)
        return _dot(ypool, wobf_ref[:POOL_WIDTH, :]) + _dot(ya_ref[0, rows, :], wobf_ref[POOL_WIDTH:, :])

    def route_stage(g, mix):
        rows = slice(g * MIX_ROWS, (g + 1) * MIX_ROWS)
        x1 = x_ref[0, rows, :] + mix
        x1_ref[0, rows, :] = x1
        h = _rms(x1, g_ref[...]).astype(BF16)
        h_ref[0, rows, :] = h
        logits = _dot_nt(wr_ref[...].astype(BF16), h)
        m = jnp.max(logits, axis=0, keepdims=True)
        e = jnp.exp(logits - m)
        aff_ref[0, :, rows] = e / jnp.sum(e, axis=0, keepdims=True)

    pooled, mixed = {}, {}
    for step in range(ngroups + 2):
        if step < ngroups:
            pooled[step] = pool_stage(step)
        if 0 <= step - 1 < ngroups:
            mixed[step - 1] = proj_stage(step - 1, pooled.pop(step - 1))
        if 0 <= step - 2 < ngroups:
            route_stage(step - 2, mixed.pop(step - 2))


def _pool_edge_inverse_counts(s):
    t = jnp.concatenate([jnp.arange(POOL_HALO), jnp.arange(s - POOL_HALO, s)])
    cols = []
    for w in POOL_WINDOWS:
        lo = jnp.clip(t - w // 2, 0, s - 1)
        hi = jnp.clip(t + (w - w // 2) - 1, 0, s - 1)
        inv = 1.0 / (hi - lo + 1).astype(F32)
        cols.append(jnp.broadcast_to(inv[:, None], (2 * POOL_HALO, POOL_GROUP)))
    return jnp.concatenate(cols, axis=-1)


def _mix(x, u, ya, wp, ps, wo, g, wr_t, tm=1024):
    b, s, _ = x.shape
    nt = s // tm
    hb = tm // POOL_HALO
    tile = lambda w: pl.BlockSpec((1, tm, w), lambda bi, i: (bi, i, 0))
    fixed2 = lambda shape: pl.BlockSpec(shape, lambda bi, i: (0, 0))
    return pl.pallas_call(
        functools.partial(_mix_kernel, tm=tm),
        grid=(b, nt),
        in_specs=[
            tile(D_MODEL),
            tile(POOL_WIDTH),
            pl.BlockSpec((1, POOL_HALO, POOL_WIDTH), lambda bi, i: (bi, jnp.maximum(i * hb - 1, 0), 0)),
            pl.BlockSpec((1, POOL_HALO, POOL_WIDTH),
                         lambda bi, i: (bi, jnp.minimum((i + 1) * hb, s // POOL_HALO - 1), 0)),
            tile(ATTN_WIDTH),
            fixed2((2 * POOL_HALO, POOL_WIDTH)),
            pl.BlockSpec(wp.shape, lambda bi, i: (0, 0, 0)),
            fixed2((1, POOL_WIDTH)),
            pl.BlockSpec(wo.shape, lambda bi, i: (0, 0), pipeline_mode=pl.Buffered(1)),
            fixed2((1, D_MODEL)),
            fixed2(wr_t.shape),
        ],
        out_specs=[
            tile(D_MODEL),
            tile(D_MODEL),
            pl.BlockSpec((1, N_EXPERTS, tm), lambda bi, i: (bi, 0, i)),
        ],
        out_shape=[
            jax.ShapeDtypeStruct((b, s, D_MODEL), F32),
            jax.ShapeDtypeStruct((b, s, D_MODEL), BF16),
            jax.ShapeDtypeStruct((b, N_EXPERTS, s), F32),
        ],
        scratch_shapes=[
            pltpu.VMEM((tm + 2 * POOL_HALO, POOL_WIDTH), F32),
            pltpu.VMEM(wo.shape, BF16),
            pltpu.VMEM(wp.shape, BF16),
        ],
        compiler_params=_params("arbitrary", "arbitrary"),
        name="mix",
    )(x, u, u, u, ya, _pool_edge_inverse_counts(s), wp, ps, wo, g, wr_t)


def _lane_cumsum_exclusive(m):
    e, s = m.shape
    r = lax.broadcasted_iota(jnp.int32, (LANES, LANES), 0)
    c = lax.broadcasted_iota(jnp.int32, (LANES, LANES), 1)
    upper = (r < c).astype(BF16)
    carry = jnp.zeros((e, 1), F32)
    out = []
    for blk in range(s // LANES):
        piece = m[:, blk * LANES:(blk + 1) * LANES]
        out.append(_dot(piece.astype(BF16), upper) + carry)
        carry = carry + jnp.sum(piece, axis=-1, keepdims=True)
    return jnp.concatenate(out, axis=-1)


def _route_kernel(aff_ref, slot_ref, slot_t_ref, off_ref, *, cap):
    aff = aff_ref[...]
    capf = jnp.float32(cap)

    def count_ge(cand_bits):
        return jnp.sum((aff >= pltpu.bitcast(cand_bits, F32)).astype(F32), axis=-1, keepdims=True)

    def search(step, ans):
        cand = ans | (jnp.int32(1) << (30 - step))
        return jnp.where(count_ge(cand) >= capf, cand, ans)

    thr = pltpu.bitcast(lax.fori_loop(0, 31, search, jnp.zeros((aff.shape[0], 1), jnp.int32)), F32)
    gt = aff > thr
    eq = aff == thr
    need = capf - jnp.sum(gt.astype(F32), axis=-1, keepdims=True)
    sel = gt | (eq & (_lane_cumsum_exclusive(eq.astype(F32)) < need))
    self = sel.astype(F32)
    slot = jnp.where(sel, _lane_cumsum_exclusive(self), float(NOT_SELECTED))
    slot_ref[...] = slot.astype(jnp.int32)
    pad = jnp.full((LANES - N_EXPERTS, slot.shape[1]), float(NOT_SELECTED), F32)
    for bi in range(slot_t_ref.shape[0]):
        mine = slot[bi * N_EXPERTS:(bi + 1) * N_EXPERTS]
        slot_t_ref[bi] = jnp.concatenate([mine, pad], axis=0).T.astype(jnp.int32)
    lane = lax.broadcasted_iota(jnp.int32, (slot.shape[0], LANES), 1)
    off = jnp.zeros((slot.shape[0], LANES), F32)
    run = jnp.zeros((slot.shape[0], 1), F32)
    for i in range(1, slot.shape[1] // ROUTE_TILE + 1):
        run = run + jnp.sum(self[:, (i - 1) * ROUTE_TILE:i * ROUTE_TILE], axis=-1, keepdims=True)
        off = jnp.where(lane == i, run, off)
    off_ref[...] = off.astype(jnp.int32)


def _route(aff_t, cap):
    b, e, s = aff_t.shape
    assert e == N_EXPERTS and s % ROUTE_TILE == 0 and s // ROUTE_TILE < OFF_STRIDE
    slot, slot_t, off = pl.pallas_call(
        functools.partial(_route_kernel, cap=cap),
        out_shape=[
            jax.ShapeDtypeStruct((b * e, s), jnp.int32),
            jax.ShapeDtypeStruct((b, s, LANES), jnp.int32),
            jax.ShapeDtypeStruct((b * e, LANES), jnp.int32),
        ],
        compiler_params=pltpu.CompilerParams(vmem_limit_bytes=VMEM_LIMIT),
        name="route",
    )(aff_t.reshape(b * e, s))
    return slot.reshape(b, e, s), slot_t, off


def _window_starts(off_ref, bi, ti, cap):
    starts = []
    ok = None
    for e in range(N_EXPERTS):
        base = (bi * N_EXPERTS + e) * OFF_STRIDE + ti
        lo = off_ref[base]
        hi = off_ref[base + 1]
        st = jnp.minimum(jnp.bitwise_and(lo, -16), cap - ROUTE_WIN)
        fits = hi <= st + ROUTE_WIN
        ok = fits if ok is None else jnp.logical_and(ok, fits)
        starts.append(pl.multiple_of(st, 16))
    return starts, ok


def _dispatch_kernel(off_ref, slot_ref, aff_ref, h_ref, xe_ref, gate_ref, *, cap):
    bi = pl.program_id(0)
    ti = pl.program_id(1)

    @pl.when(ti == 0)
    def _():
        xe_ref[...] = jnp.zeros_like(xe_ref)
        gate_ref[...] = jnp.zeros_like(gate_ref)

    nsub = h_ref.shape[1] // ROUTE_TILE
    col_groups = [slice(sub * ROUTE_TILE, (sub + 1) * ROUTE_TILE) for sub in range(nsub)]
    geo = [_window_starts(off_ref, bi, ti * nsub + sub, cap) for sub in range(nsub)]
    ok = functools.reduce(jnp.logical_and, [fits for _, fits in geo])

    @pl.when(ok)
    def _():
        row = lax.broadcasted_iota(jnp.int32, (ROUTE_WIN, ROUTE_TILE), 0)
        for cols, (starts, _) in zip(col_groups, geo):
            slot = slot_ref[0, :, cols]
            aff = aff_ref[0, :, cols]
            hits = [row == (slot[e:e + 1, :] - starts[e]) for e in range(N_EXPERTS)]
            onehot = jnp.concatenate([jnp.where(hh, 1.0, 0.0).astype(BF16) for hh in hits], axis=0)
            res = _dot(onehot, h_ref[0, cols, :])
            for e, hh in enumerate(hits):
                win = pl.ds(starts[e], ROUTE_WIN)
                xe_ref[e, win, :] += res[e * ROUTE_WIN:(e + 1) * ROUTE_WIN].astype(BF16)
                gate_ref[e, win, :] += jnp.sum(jnp.where(hh, aff[e:e + 1, :], 0.0), axis=-1, keepdims=True)

    @pl.when(jnp.logical_not(ok))
    def _():
        row = lax.broadcasted_iota(jnp.int32, (cap, ROUTE_TILE), 0)
        for cols in col_groups:
            slot = slot_ref[0, :, cols]
            aff = aff_ref[0, :, cols]
            h = h_ref[0, cols, :]
            for e in range(N_EXPERTS):
                hh = row == slot[e:e + 1, :]
                xe_ref[e] += _dot(jnp.where(hh, 1.0, 0.0).astype(BF16), h).astype(BF16)
                gate_ref[e] += jnp.sum(jnp.where(hh, aff[e:e + 1, :], 0.0), axis=-1, keepdims=True)


def _dispatch(off, slot, aff_t, h, cap, tm=1024):
    b, e, s = slot.shape
    assert tm % ROUTE_TILE == 0
    rows = pl.BlockSpec((1, e, tm), lambda bi, ti, off_ref: (bi, 0, ti))
    return pl.pallas_call(
        functools.partial(_dispatch_kernel, cap=cap),
        grid_spec=pltpu.PrefetchScalarGridSpec(
            num_scalar_prefetch=1,
            grid=(b, s // tm),
            in_specs=[rows, rows,
                      pl.BlockSpec((1, tm, D_MODEL), lambda bi, ti, off_ref: (bi, ti, 0))],
            out_specs=[
                pl.BlockSpec((e, cap, D_MODEL), lambda bi, ti, off_ref: (0, bi, 0)),
                pl.BlockSpec((e, cap, 1), lambda bi, ti, off_ref: (0, bi, 0)),
            ],
        ),
        out_shape=[
            jax.ShapeDtypeStruct((e, b * cap, D_MODEL), BF16),
            jax.ShapeDtypeStruct((e, b * cap, 1), F32),
        ],
        compiler_params=_params("arbitrary", "arbitrary"),
        name="dispatch",
    )(off, slot, aff_t, h)


def _experts_kernel(xe_ref, gate_ref, wg_ref, wu_ref, wd_ref, y_ref, acc_ref):
    f = pl.program_id(1)

    def hidden_chunk(first):
        wg = wg_ref[0].astype(BF16)
        wu = wu_ref[0].astype(BF16)
        wd = wd_ref[0].astype(BF16)
        for mb in range(xe_ref.shape[1] // EXPERT_ROWS):
            rows = slice(mb * EXPERT_ROWS, (mb + 1) * EXPERT_ROWS)
            xe = xe_ref[0, rows, :]
            a = _dot(xe, wg)
            b = _dot(xe, wu)
            part = _dot((a * jax.nn.sigmoid(a) * b).astype(BF16), wd)
            acc_ref[rows, :] = part if first else acc_ref[rows, :] + part

    pl.when(f == 0)(functools.partial(hidden_chunk, True))
    pl.when(f > 0)(functools.partial(hidden_chunk, False))

    @pl.when(f == pl.num_programs(1) - 1)
    def _():
        y_ref[0] = (acc_ref[...] * gate_ref[0]).astype(y_ref.dtype)


def _experts(xe, gate, wg, wu, wd, tf=512):
    e, m, _ = xe.shape
    nf = D_EXPERT // tf
    return pl.pallas_call(
        _experts_kernel,
        grid=(e, nf),
        in_specs=[
            pl.BlockSpec((1, m, D_MODEL), lambda ei, f: (ei, 0, 0)),
            pl.BlockSpec((1, m, 1), lambda ei, f: (ei, 0, 0)),
            pl.BlockSpec((1, D_MODEL, tf), lambda ei, f: (ei, 0, f)),
            pl.BlockSpec((1, D_MODEL, tf), lambda ei, f: (ei, 0, f)),
            pl.BlockSpec((1, tf, D_MODEL), lambda ei, f: (ei, f, 0)),
        ],
        out_specs=pl.BlockSpec((1, m, D_MODEL), lambda ei, f: (ei, 0, 0)),
        out_shape=jax.ShapeDtypeStruct((e, m, D_MODEL), BF16),
        scratch_shapes=[pltpu.VMEM((m, D_MODEL), F32)],
        compiler_params=_params("arbitrary", "arbitrary"),
        name="experts",
    )(xe, gate, wg, wu, wd)


def _combine_kernel(off_ref, x1_ref, st_ref, y_ref, p_ref, gn_ref, wgbf_ref, wpbf_ref, gp_ref,
                    o_ref, *, cap):
    bi = pl.program_id(0)
    nsub = x1_ref.shape[1] // ROUTE_TILE
    row_groups = [slice(sub * ROUTE_TILE, (sub + 1) * ROUTE_TILE) for sub in range(nsub)]
    geo = [_window_starts(off_ref, bi, pl.program_id(1) * nsub + sub, cap) for sub in range(nsub)]
    ok = functools.reduce(jnp.logical_and, [fits for _, fits in geo])

    def windowed_scatter(rows, starts):
        st = st_ref[0, rows, :]
        lane = lax.broadcasted_iota(jnp.int32, (ROUTE_TILE, LANES), 1)
        low = lane < ROUTE_WIN
        total = None
        for g in range(N_EXPERTS // ROUTE_GROUP):
            halves = []
            wins = []
            for half in range(ROUTE_GROUP // 2):
                e0 = g * ROUTE_GROUP + 2 * half
                t0 = st[:, e0:e0 + 1] - starts[e0]
                t1 = st[:, e0 + 1:e0 + 2] + (ROUTE_WIN - starts[e0 + 1])
                halves.append(jnp.where(lane == jnp.where(low, t0, t1), 1.0, 0.0).astype(BF16))
                wins.append(y_ref[e0, pl.ds(starts[e0], ROUTE_WIN), :])
                wins.append(y_ref[e0 + 1, pl.ds(starts[e0 + 1], ROUTE_WIN), :])
            part = _dot(jnp.concatenate(halves, axis=1), jnp.concatenate(wins, axis=0))
            total = part if total is None else total + part
        return total

    def dense_scatter(rows):
        st = st_ref[0, rows, :]
        lane = lax.broadcasted_iota(jnp.int32, (ROUTE_TILE, cap), 1)
        total = None
        for e in range(N_EXPERTS):
            onehot = jnp.where(lane == st[:, e:e + 1], 1.0, 0.0).astype(BF16)
            part = _dot(onehot, y_ref[e])
            total = part if total is None else total + part
        return total

    def gate_stage(rows, ffn):
        x2 = x1_ref[0, rows, :] + ffn
        return x2, _dot(_rms(x2, gn_ref[...]).astype(BF16), wgbf_ref[...])

    def output_stage(rows, x2, gate_logits):
        emb = _rms(_dot(p_ref[0, rows, :].astype(BF16), wpbf_ref[...]), gp_ref[...])
        o_ref[0, rows, :] = x2 + jax.nn.sigmoid(gate_logits) * emb

    def pipeline(scatter_stage):
        ffn, gated = {}, {}
        for step in range(nsub + 2):
            if step < nsub:
                ffn[step] = scatter_stage(step)
            if 0 <= step - 1 < nsub:
                gated[step - 1] = gate_stage(row_groups[step - 1], ffn.pop(step - 1))
            if 0 <= step - 2 < nsub:
                output_stage(row_groups[step - 2], *gated.pop(step - 2))

    pl.when(ok)(lambda: pipeline(lambda g: windowed_scatter(row_groups[g], geo[g][0])))
    pl.when(jnp.logical_not(ok))(lambda: pipeline(lambda g: dense_scatter(row_groups[g])))


def _combine(off, x1, slot_t, y, p, gn, wg, wp, gp, cap, tm=1024):
    b, s, _ = x1.shape
    assert tm % ROUTE_TILE == 0
    tile = lambda w: pl.BlockSpec((1, tm, w), lambda bi, i, off_ref: (bi, i, 0))
    fixed2 = lambda shape: pl.BlockSpec(shape, lambda bi, i, off_ref: (0, 0))
    return pl.pallas_call(
        functools.partial(_combine_kernel, cap=cap),
        grid_spec=pltpu.PrefetchScalarGridSpec(
            num_scalar_prefetch=1,
            grid=(b, s // tm),
            in_specs=[
                tile(D_MODEL),
                tile(LANES),
                pl.BlockSpec((N_EXPERTS, cap, D_MODEL), lambda bi, i, off_ref: (0, bi, 0)),
                tile(PLE_DIM),
                fixed2((1, D_MODEL)),
                pl.BlockSpec(wg.shape, lambda bi, i, off_ref: (0, 0), pipeline_mode=pl.Buffered(1)),
                pl.BlockSpec(wp.shape, lambda bi, i, off_ref: (0, 0), pipeline_mode=pl.Buffered(1)),
                fixed2((1, D_MODEL)),
            ],
            out_specs=tile(D_MODEL),
        ),
        out_shape=jax.ShapeDtypeStruct((b, s, D_MODEL), F32),
        compiler_params=_params("arbitrary", "arbitrary"),
        name="combine",
    )(off, x1, slot_t, y, p, gn, wg.astype(BF16), wp.astype(BF16), gp)


def kernel(x, p, norm_mix, w_in, w_pool, pool_scale, q_norm, k_norm, rpb, w_out, norm_ffn, w_router, w_gate, w_up, w_down, norm_ple, w_ple_gate, w_ple_proj, norm_ple_post):
    b, s, d = x.shape
    depth = w_in.shape[0]
    cap = EC_CAPACITY * s // N_EXPERTS
    row = lambda a: a.reshape(1, -1)
    for i in range(depth):
        u, q, k, v = _in_proj(x.reshape(b * s, d), row(norm_mix[i]), w_in[i],
                              row(jnp.tile(q_norm[i], ATTN_HEADS)), row(jnp.tile(k_norm[i], ATTN_HEADS)))
        shp = lambda a: a.reshape(b, s, -1)
        y_attn = _natten(shp(q), shp(k), shp(v), _attn_bias_table(rpb[i]))
        x1, h, aff_t = _mix(x, shp(u), y_attn, w_pool[i], row(pool_scale[i]), w_out[i],
                            row(norm_ffn[i]), w_router[i].T)
        slot, slot_t, off = _route(aff_t, cap)
        off = off[:, :OFF_STRIDE].reshape(-1)
        xe, gate = _dispatch(off, slot, aff_t, h, cap)
        y = _experts(xe, gate, w_gate[i], w_up[i], w_down[i])
        x = _combine(off, x1, slot_t, y, p[i], row(norm_ple[i]), w_ple_gate[i], w_ple_proj[i],
                     row(norm_ple_post[i]), cap)
    return x
```

```python
import functools

import jax
import jax.numpy as jnp
from jax import lax
from jax.experimental import pallas as pl
from jax.experimental.pallas import tpu as pltpu

D_MODEL = 1024
GRID_W = 64
POOL_WINDOWS = (2, 4, 8, 16)
POOL_WIDTH = D_MODEL // 2
POOL_GROUP = POOL_WIDTH // len(POOL_WINDOWS)
ATTN_HEADS = 8
HEAD_DIM = (D_MODEL // 2) // ATTN_HEADS
ATTN_WIDTH = ATTN_HEADS * HEAD_DIM
WIN_ROWS_MAX = 8
WIN_COLS = 16
N_EXPERTS = 16
EC_CAPACITY = 2
D_EXPERT = 2 * D_MODEL
PLE_DIM = 256
RMS_EPS = 1e-6

LANES = 128
POOL_HALO = 8
LOG2E = 1.4426950408889634
MASK_BIAS = -1e30
NATTEN_UNROLL = 8
ROUTE_TILE = 256
ROUTE_WIN = 64
ROUTE_GROUP = 4
OFF_STRIDE = 16
NOT_SELECTED = -(1 << 20)
assert 2 * ROUTE_WIN == LANES and ROUTE_GROUP % 2 == 0
PLE_ROWS = 512
MIX_ROWS = 512
PROJ_ROWS = 512
EXPERT_ROWS = 1024
VMEM_LIMIT = 56 * 1024 * 1024

BF16 = jnp.bfloat16
F32 = jnp.float32


def _params(*sem):
    return pltpu.CompilerParams(dimension_semantics=sem, vmem_limit_bytes=VMEM_LIMIT)


def _rms(x, g):
    return x * lax.rsqrt(jnp.mean(x * x, axis=-1, keepdims=True) + RMS_EPS) * g


def _dot(a, b):
    return jnp.dot(a, b, preferred_element_type=F32)


def _dot_nt(a, b):
    return lax.dot_general(a, b, (((1,), (1,)), ((), ())), preferred_element_type=F32)


def _in_proj_kernel(x_ref, g_ref, w_ref, qg_ref, kg_ref,
                    u_ref, q_ref, k_ref, v_ref, wbf_ref):
    @pl.when(pl.program_id(0) == 0)
    def _():
        wbf_ref[...] = w_ref[...].astype(BF16)

    def head_norm(t, g):
        low = lax.broadcasted_iota(jnp.int32, (t.shape[0], LANES), 1) < HEAD_DIM
        out = []
        for j in range(t.shape[1] // LANES):
            tile = t[:, j * LANES:(j + 1) * LANES]
            sq = tile * tile
            sa = jnp.sum(jnp.where(low, sq, 0.0), axis=-1, keepdims=True)
            sb = jnp.sum(sq, axis=-1, keepdims=True) - sa
            ra = lax.rsqrt(sa * (1.0 / HEAD_DIM) + RMS_EPS)
            rb = lax.rsqrt(sb * (1.0 / HEAD_DIM) + RMS_EPS)
            out.append(tile * jnp.where(low, ra, rb) * g[:, j * LANES:(j + 1) * LANES])
        return jnp.concatenate(out, axis=-1)

    def norm_stage(rows):
        return _rms(x_ref[rows, :], g_ref[...]).astype(BF16)

    def proj_stage(h):
        return _dot(h, wbf_ref[...])

    def head_stage(rows, z):
        u_ref[rows, :] = z[:, :POOL_WIDTH]
        q = z[:, POOL_WIDTH:POOL_WIDTH + ATTN_WIDTH]
        k = z[:, POOL_WIDTH + ATTN_WIDTH:POOL_WIDTH + 2 * ATTN_WIDTH]
        v = z[:, POOL_WIDTH + 2 * ATTN_WIDTH:]
        q_ref[rows, :] = (head_norm(q, qg_ref[...]) * (HEAD_DIM ** -0.5 * LOG2E)).astype(BF16)
        k_ref[rows, :] = head_norm(k, kg_ref[...]).astype(BF16)
        v_ref[rows, :] = v.astype(BF16)

    groups = [slice(sub * PROJ_ROWS, (sub + 1) * PROJ_ROWS) for sub in range(x_ref.shape[0] // PROJ_ROWS)]
    zs = [proj_stage(norm_stage(rows)) for rows in groups]
    for rows, z in zip(groups, zs):
        head_stage(rows, z)


def _in_proj(x2, g, w, qg, kg, tm=1024):
    n = x2.shape[0]
    zw = w.shape[1]
    assert 2 * HEAD_DIM == LANES
    row = lambda i: (i, 0)
    fixed = lambda i: (0, 0)
    return pl.pallas_call(
        _in_proj_kernel,
        grid=(n // tm,),
        in_specs=[
            pl.BlockSpec((tm, D_MODEL), row),
            pl.BlockSpec((1, D_MODEL), fixed),
            pl.BlockSpec((D_MODEL, zw), fixed, pipeline_mode=pl.Buffered(1)),
            pl.BlockSpec((1, ATTN_WIDTH), fixed),
            pl.BlockSpec((1, ATTN_WIDTH), fixed),
        ],
        out_specs=[
            pl.BlockSpec((tm, POOL_WIDTH), row),
            pl.BlockSpec((tm, ATTN_WIDTH), row),
            pl.BlockSpec((tm, ATTN_WIDTH), row),
            pl.BlockSpec((tm, ATTN_WIDTH), row),
        ],
        out_shape=[
            jax.ShapeDtypeStruct((n, POOL_WIDTH), F32),
            jax.ShapeDtypeStruct((n, ATTN_WIDTH), BF16),
            jax.ShapeDtypeStruct((n, ATTN_WIDTH), BF16),
            jax.ShapeDtypeStruct((n, ATTN_WIDTH), BF16),
        ],
        scratch_shapes=[pltpu.VMEM((D_MODEL, zw), BF16)],
        compiler_params=_params("arbitrary"),
        name="in_proj",
    )(x2, g, w, qg, kg)


def _natten_kernel(q_ref, k_ref, v_ref, tbl_ref, o_ref, bias_ref, *, rows, kh):
    band = kh * GRID_W
    lane = lax.broadcasted_iota(jnp.int32, (GRID_W, LANES), 1)
    first = lane < HEAD_DIM

    for hh in range(2):
        for d0 in range(WIN_ROWS_MAX):
            for kk in range(kh // 2):
                bias_ref[d0, hh * GRID_W:(hh + 1) * GRID_W, kk * LANES:(kk + 1) * LANES] = jnp.where(
                    lane < GRID_W, tbl_ref[hh, d0 + 2 * kk], tbl_ref[hh, d0 + 2 * kk + 1])

    def geometry(g):
        geo = []
        for r in range(g * NATTEN_UNROLL, (g + 1) * NATTEN_UNROLL):
            rs = min(max(r - kh // 2, 0), rows - kh)
            geo.append((rs - r + (WIN_ROWS_MAX - 1), r * GRID_W, rs * GRID_W))
        return geo

    def score_stage(g):
        scores = []
        for d0, q0, k0 in geometry(g):
            qr = q_ref[0, q0:q0 + GRID_W, :]
            zero = jnp.zeros_like(qr)
            q2 = jnp.concatenate([jnp.where(first, qr, zero), jnp.where(first, zero, qr)], axis=0)
            s = _dot_nt(q2, k_ref[0, k0:k0 + band, :]) + bias_ref[d0]
            scores.append((s, jnp.max(s, axis=-1, keepdims=True)))
        return scores

    def softmax_stage(scores):
        probs = []
        for s, m in scores:
            e = jnp.exp2(s - m)
            probs.append((e.astype(BF16), jnp.sum(e, axis=-1, keepdims=True)))
        return probs

    def value_stage(g, probs):
        for (d0, q0, k0), (e, l) in zip(geometry(g), probs):
            o = _dot(e, v_ref[0, k0:k0 + band, :]) / l
            o_ref[0, q0:q0 + GRID_W, :] = jnp.where(first, o[:GRID_W], o[GRID_W:]).astype(o_ref.dtype)

    ngroups = rows // NATTEN_UNROLL
    scores, probs = {}, {}
    for step in range(ngroups + 2):
        if step < ngroups:
            scores[step] = score_stage(step)
        if 0 <= step - 1 < ngroups:
            probs[step - 1] = softmax_stage(scores.pop(step - 1))
        if 0 <= step - 2 < ngroups:
            value_stage(step - 2, probs.pop(step - 2))


def _natten(q, k, v, bias):
    b, s, _ = q.shape
    rows = s // GRID_W
    kh = min(WIN_ROWS_MAX, rows)
    assert rows % NATTEN_UNROLL == 0 and kh % 2 == 0 and 2 * GRID_W == LANES
    pairs = ATTN_WIDTH // LANES
    blk = pl.BlockSpec((1, s, LANES), lambda bi, p: (bi, 0, p))
    return pl.pallas_call(
        functools.partial(_natten_kernel, rows=rows, kh=kh),
        grid=(b, pairs),
        in_specs=[blk, blk, blk,
                  pl.BlockSpec((2,) + bias.shape[1:], lambda bi, p: (p, 0, 0, 0))],
        out_specs=blk,
        out_shape=jax.ShapeDtypeStruct((b, s, ATTN_WIDTH), BF16),
        scratch_shapes=[pltpu.VMEM((WIN_ROWS_MAX, 2 * GRID_W, kh * GRID_W), F32)],
        compiler_params=_params("arbitrary", "arbitrary"),
        name="natten",
    )(q, k, v, bias)


def _attn_bias_table(rpb):
    c = jnp.arange(GRID_W)
    cs = jnp.clip(c - WIN_COLS // 2, 0, GRID_W - WIN_COLS)
    j = jnp.arange(GRID_W)
    valid = (j[None, :] >= cs[:, None]) & (j[None, :] < cs[:, None] + WIN_COLS)
    dc = j[None, :] - c[:, None] + (WIN_COLS - 1)
    pick = ((dc[None] == jnp.arange(2 * WIN_COLS - 1)[:, None, None]) & valid[None]).astype(F32)
    t = jnp.einsum('hrd,dcj->hrcj', rpb.astype(F32), pick, precision=lax.Precision.HIGHEST)
    t = jnp.where(valid, t * LOG2E, MASK_BIAS)
    return jnp.concatenate([t, t], axis=-1)


def _window_sum(upad, w, tm):
    n = upad.shape[0]
    fwd = upad
    span = 1
    while span < min(w, POOL_HALO):
        fwd = fwd + pltpu.roll(fwd, n - span, axis=0)
        span *= 2
    centre = slice(POOL_HALO, POOL_HALO + tm)
    if w == 2 * POOL_HALO:
        return fwd[0:tm] + fwd[centre]
    return pltpu.roll(fwd, w // 2, axis=0)[centre]


def _mix_kernel(x_ref, u_ref, up_ref, un_ref, ya_ref, ic_ref, wp_ref, ps_ref, wo_ref, g_ref, wr_ref,
                x1_ref, h_ref, aff_ref, upad_ref, wobf_ref, wpbf_ref, *, tm):
    i = pl.program_id(1)

    @pl.when((pl.program_id(0) == 0) & (i == 0))
    def _():
        wobf_ref[...] = wo_ref[...].astype(BF16)
        wpbf_ref[...] = wp_ref[...].astype(BF16)

    upad_ref[0:POOL_HALO, :] = jnp.where(i > 0, up_ref[0], 0.0)
    upad_ref[POOL_HALO:POOL_HALO + tm, :] = u_ref[0]
    upad_ref[POOL_HALO + tm:, :] = jnp.where(i < pl.num_programs(1) - 1, un_ref[0], 0.0)

    ngroups = tm // MIX_ROWS
    group_cols = [slice(gi * POOL_GROUP, (gi + 1) * POOL_GROUP) for gi in range(len(POOL_WINDOWS))]

    def pool_stage(g):
        ds = []
        for cols, w in zip(group_cols, POOL_WINDOWS):
            upad = upad_ref[g * MIX_ROWS:(g + 1) * MIX_ROWS + 2 * POOL_HALO, cols]
            edge = jnp.full((POOL_HALO, POOL_GROUP), 1.0 / w, F32)
            head = jnp.where(i == 0, ic_ref[0:POOL_HALO, cols], 1.0 / w) if g == 0 else edge
            tail = (jnp.where(i == pl.num_programs(1) - 1, ic_ref[POOL_HALO:, cols], 1.0 / w)
                    if g == ngroups - 1 else edge)
            inv = jnp.concatenate(
                [head, jnp.full((MIX_ROWS - 2 * POOL_HALO, POOL_GROUP), 1.0 / w, F32), tail], axis=0)
            d = _window_sum(upad, w, MIX_ROWS) * inv - upad[POOL_HALO:POOL_HALO + MIX_ROWS]
            ds.append(d.astype(BF16))
        return ds

    def proj_stage(g, ds):
        rows = slice(g * MIX_ROWS, (g + 1) * MIX_ROWS)
        ypool = [_dot(d, wpbf_ref[gi]) * ps_ref[:, cols] for gi, (d, cols) in enumerate(zip(ds, group_cols))]
        ypool = jnp.concatenate(ypool, axis=-1).astype(BF16)
        return _dot(ypool, wobf_ref[:POOL_WIDTH, :]) + _dot(ya_ref[0, rows, :], wobf_ref[POOL_WIDTH:, :])

    def route_stage(g, mix):
        rows = slice(g * MIX_ROWS, (g + 1) * MIX_ROWS)
        x1 = x_ref[0, rows, :] + mix
        x1_ref[0, rows, :] = x1
        h = _rms(x1, g_ref[...]).astype(BF16)
        h_ref[0, rows, :] = h
        logits = _dot_nt(wr_ref[...].astype(BF16), h)
        m = jnp.max(logits, axis=0, keepdims=True)
        e = jnp.exp(logits - m)
        aff_ref[0, :, rows] = e / jnp.sum(e, axis=0, keepdims=True)

    pooled, mixed = {}, {}
    for step in range(ngroups + 2):
        if step < ngroups:
            pooled[step] = pool_stage(step)
        if 0 <= step - 1 < ngroups:
            mixed[step - 1] = proj_stage(step - 1, pooled.pop(step - 1))
        if 0 <= step - 2 < ngroups:
            route_stage(step - 2, mixed.pop(step - 2))


def _pool_edge_inverse_counts(s):
    t = jnp.concatenate([jnp.arange(POOL_HALO), jnp.arange(s - POOL_HALO, s)])
    cols = []
    for w in POOL_WINDOWS:
        lo = jnp.clip(t - w // 2, 0, s - 1)
        hi = jnp.clip(t + (w - w // 2) - 1, 0, s - 1)
        inv = 1.0 / (hi - lo + 1).astype(F32)
        cols.append(jnp.broadcast_to(inv[:, None], (2 * POOL_HALO, POOL_GROUP)))
    return jnp.concatenate(cols, axis=-1)


def _mix(x, u, ya, wp, ps, wo, g, wr_t, tm=1024):
    b, s, _ = x.shape
    nt = s // tm
    hb = tm // POOL_HALO
    tile = lambda w: pl.BlockSpec((1, tm, w), lambda bi, i: (bi, i, 0))
    fixed2 = lambda shape: pl.BlockSpec(shape, lambda bi, i: (0, 0))
    return pl.pallas_call(
        functools.partial(_mix_kernel, tm=tm),
        grid=(b, nt),
        in_specs=[
            tile(D_MODEL),
            tile(POOL_WIDTH),
            pl.BlockSpec((1, POOL_HALO, POOL_WIDTH), lambda bi, i: (bi, jnp.maximum(i * hb - 1, 0), 0)),
            pl.BlockSpec((1, POOL_HALO, POOL_WIDTH),
                         lambda bi, i: (bi, jnp.minimum((i + 1) * hb, s // POOL_HALO - 1), 0)),
            tile(ATTN_WIDTH),
            fixed2((2 * POOL_HALO, POOL_WIDTH)),
            pl.BlockSpec(wp.shape, lambda bi, i: (0, 0, 0)),
            fixed2((1, POOL_WIDTH)),
            pl.BlockSpec(wo.shape, lambda bi, i: (0, 0), pipeline_mode=pl.Buffered(1)),
            fixed2((1, D_MODEL)),
            fixed2(wr_t.shape),
        ],
        out_specs=[
            tile(D_MODEL),
            tile(D_MODEL),
            pl.BlockSpec((1, N_EXPERTS, tm), lambda bi, i: (bi, 0, i)),
        ],
        out_shape=[
            jax.ShapeDtypeStruct((b, s, D_MODEL), F32),
            jax.ShapeDtypeStruct((b, s, D_MODEL), BF16),
            jax.ShapeDtypeStruct((b, N_EXPERTS, s), F32),
        ],
        scratch_shapes=[
            pltpu.VMEM((tm + 2 * POOL_HALO, POOL_WIDTH), F32),
            pltpu.VMEM(wo.shape, BF16),
            pltpu.VMEM(wp.shape, BF16),
        ],
        compiler_params=_params("arbitrary", "arbitrary"),
        name="mix",
    )(x, u, u, u, ya, _pool_edge_inverse_counts(s), wp, ps, wo, g, wr_t)


def _lane_cumsum_exclusive(m):
    e, s = m.shape
    r = lax.broadcasted_iota(jnp.int32, (LANES, LANES), 0)
    c = lax.broadcasted_iota(jnp.int32, (LANES, LANES), 1)
    upper = (r < c).astype(BF16)
    carry = jnp.zeros((e, 1), F32)
    out = []
    for blk in range(s // LANES):
        piece = m[:, blk * LANES:(blk + 1) * LANES]
        out.append(_dot(piece.astype(BF16), upper) + carry)
        carry = carry + jnp.sum(piece, axis=-1, keepdims=True)
    return jnp.concatenate(out, axis=-1)


def _route_kernel(aff_ref, slot_ref, slot_t_ref, off_ref, *, cap):
    aff = aff_ref[...]
    capf = jnp.float32(cap)

    def count_ge(cand_bits):
        return jnp.sum((aff >= pltpu.bitcast(cand_bits, F32)).astype(F32), axis=-1, keepdims=True)

    def search(step, ans):
        cand = ans | (jnp.int32(1) << (30 - step))
        return jnp.where(count_ge(cand) >= capf, cand, ans)

    thr = pltpu.bitcast(lax.fori_loop(0, 31, search, jnp.zeros((aff.shape[0], 1), jnp.int32)), F32)
    gt = aff > thr
    eq = aff == thr
    need = capf - jnp.sum(gt.astype(F32), axis=-1, keepdims=True)
    sel = gt | (eq & (_lane_cumsum_exclusive(eq.astype(F32)) < need))
    self = sel.astype(F32)
    slot = jnp.where(sel, _lane_cumsum_exclusive(self), float(NOT_SELECTED))
    slot_ref[...] = slot.astype(jnp.int32)
    pad = jnp.full((LANES - N_EXPERTS, slot.shape[1]), float(NOT_SELECTED), F32)
    for bi in range(slot_t_ref.shape[0]):
        mine = slot[bi * N_EXPERTS:(bi + 1) * N_EXPERTS]
        slot_t_ref[bi] = jnp.concatenate([mine, pad], axis=0).T.astype(jnp.int32)
    lane = lax.broadcasted_iota(jnp.int32, (slot.shape[0], LANES), 1)
    off = jnp.zeros((slot.shape[0], LANES), F32)
    run = jnp.zeros((slot.shape[0], 1), F32)
    for i in range(1, slot.shape[1] // ROUTE_TILE + 1):
        run = run + jnp.sum(self[:, (i - 1) * ROUTE_TILE:i * ROUTE_TILE], axis=-1, keepdims=True)
        off = jnp.where(lane == i, run, off)
    off_ref[...] = off.astype(jnp.int32)


def _route(aff_t, cap):
    b, e, s = aff_t.shape
    assert e == N_EXPERTS and s % ROUTE_TILE == 0 and s // ROUTE_TILE < OFF_STRIDE
    slot, slot_t, off = pl.pallas_call(
        functools.partial(_route_kernel, cap=cap),
        out_shape=[
            jax.ShapeDtypeStruct((b * e, s), jnp.int32),
            jax.ShapeDtypeStruct((b, s, LANES), jnp.int32),
            jax.ShapeDtypeStruct((b * e, LANES), jnp.int32),
        ],
        compiler_params=pltpu.CompilerParams(vmem_limit_bytes=VMEM_LIMIT),
        name="route",
    )(aff_t.reshape(b * e, s))
    return slot.reshape(b, e, s), slot_t, off


def _window_starts(off_ref, bi, ti, cap):
    starts = []
    ok = None
    for e in range(N_EXPERTS):
        base = (bi * N_EXPERTS + e) * OFF_STRIDE + ti
        lo = off_ref[base]
        hi = off_ref[base + 1]
        st = jnp.minimum(jnp.bitwise_and(lo, -16), cap - ROUTE_WIN)
        fits = hi <= st + ROUTE_WIN
        ok = fits if ok is None else jnp.logical_and(ok, fits)
        starts.append(pl.multiple_of(st, 16))
    return starts, ok


def _dispatch_kernel(off_ref, slot_ref, aff_ref, h_ref, xe_ref, gate_ref, *, cap):
    bi = pl.program_id(0)
    ti = pl.program_id(1)

    @pl.when(ti == 0)
    def _():
        xe_ref[...] = jnp.zeros_like(xe_ref)
        gate_ref[...] = jnp.zeros_like(gate_ref)

    nsub = h_ref.shape[1] // ROUTE_TILE
    col_groups = [slice(sub * ROUTE_TILE, (sub + 1) * ROUTE_TILE) for sub in range(nsub)]
    geo = [_window_starts(off_ref, bi, ti * nsub + sub, cap) for sub in range(nsub)]
    ok = functools.reduce(jnp.logical_and, [fits for _, fits in geo])

    @pl.when(ok)
    def _():
        row = lax.broadcasted_iota(jnp.int32, (ROUTE_WIN, ROUTE_TILE), 0)
        for cols, (starts, _) in zip(col_groups, geo):
            slot = slot_ref[0, :, cols]
            aff = aff_ref[0, :, cols]
            hits = [row == (slot[e:e + 1, :] - starts[e]) for e in range(N_EXPERTS)]
            onehot = jnp.concatenate([jnp.where(hh, 1.0, 0.0).astype(BF16) for hh in hits], axis=0)
            res = _dot(onehot, h_ref[0, cols, :])
            for e, hh in enumerate(hits):
                win = pl.ds(starts[e], ROUTE_WIN)
                xe_ref[e, win, :] += res[e * ROUTE_WIN:(e + 1) * ROUTE_WIN].astype(BF16)
                gate_ref[e, win, :] += jnp.sum(jnp.where(hh, aff[e:e + 1, :], 0.0), axis=-1, keepdims=True)

    @pl.when(jnp.logical_not(ok))
    def _():
        row = lax.broadcasted_iota(jnp.int32, (cap, ROUTE_TILE), 0)
        for cols in col_groups:
            slot = slot_ref[0, :, cols]
            aff = aff_ref[0, :, cols]
            h = h_ref[0, cols, :]
            for e in range(N_EXPERTS):
                hh = row == slot[e:e + 1, :]
                xe_ref[e] += _dot(jnp.where(hh, 1.0, 0.0).astype(BF16), h).astype(BF16)
                gate_ref[e] += jnp.sum(jnp.where(hh, aff[e:e + 1, :], 0.0), axis=-1, keepdims=True)


def _dispatch(off, slot, aff_t, h, cap, tm=1024):
    b, e, s = slot.shape
    assert tm % ROUTE_TILE == 0
    rows = pl.BlockSpec((1, e, tm), lambda bi, ti, off_ref: (bi, 0, ti))
    return pl.pallas_call(
        functools.partial(_dispatch_kernel, cap=cap),
        grid_spec=pltpu.PrefetchScalarGridSpec(
            num_scalar_prefetch=1,
            grid=(b, s // tm),
            in_specs=[rows, rows,
                      pl.BlockSpec((1, tm, D_MODEL), lambda bi, ti, off_ref: (bi, ti, 0))],
            out_specs=[
                pl.BlockSpec((e, cap, D_MODEL), lambda bi, ti, off_ref: (0, bi, 0)),
                pl.BlockSpec((e, cap, 1), lambda bi, ti, off_ref: (0, bi, 0)),
            ],
        ),
        out_shape=[
            jax.ShapeDtypeStruct((e, b * cap, D_MODEL), BF16),
            jax.ShapeDtypeStruct((e, b * cap, 1), F32),
        ],
        compiler_params=_params("arbitrary", "arbitrary"),
        name="dispatch",
    )(off, slot, aff_t, h)


def _experts_kernel(xe_ref, gate_ref, wg_ref, wu_ref, wd_ref, y_ref, acc_ref):
    f = pl.program_id(1)

    def hidden_chunk(first):
        wg = wg_ref[0].astype(BF16)
        wu = wu_ref[0].astype(BF16)
        wd = wd_ref[0].astype(BF16)
        for mb in range(xe_ref.shape[1] // EXPERT_ROWS):
            rows = slice(mb * EXPERT_ROWS, (mb + 1) * EXPERT_ROWS)
            xe = xe_ref[0, rows, :]
            a = _dot(xe, wg)
            b = _dot(xe, wu)
            part = _dot((a * jax.nn.sigmoid(a) * b).astype(BF16), wd)
            acc_ref[rows, :] = part if first else acc_ref[rows, :] + part

    pl.when(f == 0)(functools.partial(hidden_chunk, True))
    pl.when(f > 0)(functools.partial(hidden_chunk, False))

    @pl.when(f == pl.num_programs(1) - 1)
    def _():
        y_ref[0] = (acc_ref[...] * gate_ref[0]).astype(y_ref.dtype)


def _experts(xe, gate, wg, wu, wd, tf=512):
    e, m, _ = xe.shape
    nf = D_EXPERT // tf
    return pl.pallas_call(
        _experts_kernel,
        grid=(e, nf),
        in_specs=[
            pl.BlockSpec((1, m, D_MODEL), lambda ei, f: (ei, 0, 0)),
            pl.BlockSpec((1, m, 1), lambda ei, f: (ei, 0, 0)),
            pl.BlockSpec((1, D_MODEL, tf), lambda ei, f: (ei, 0, f)),
            pl.BlockSpec((1, D_MODEL, tf), lambda ei, f: (ei, 0, f)),
            pl.BlockSpec((1, tf, D_MODEL), lambda ei, f: (ei, f, 0)),
        ],
        out_specs=pl.BlockSpec((1, m, D_MODEL), lambda ei, f: (ei, 0, 0)),
        out_shape=jax.ShapeDtypeStruct((e, m, D_MODEL), BF16),
        scratch_shapes=[pltpu.VMEM((m, D_MODEL), F32)],
        compiler_params=_params("arbitrary", "arbitrary"),
        name="experts",
    )(xe, gate, wg, wu, wd)


def _combine_kernel(off_ref, x1_ref, st_ref, y_ref, p_ref, gn_ref, wgbf_ref, wpbf_ref, gp_ref,
                    o_ref, *, cap):
    bi = pl.program_id(0)
    nsub = x1_ref.shape[1] // ROUTE_TILE
    row_groups = [slice(sub * ROUTE_TILE, (sub + 1) * ROUTE_TILE) for sub in range(nsub)]
    geo = [_window_starts(off_ref, bi, pl.program_id(1) * nsub + sub, cap) for sub in range(nsub)]
    ok = functools.reduce(jnp.logical_and, [fits for _, fits in geo])

    def windowed_scatter(rows, starts):
        st = st_ref[0, rows, :]
        lane = lax.broadcasted_iota(jnp.int32, (ROUTE_TILE, LANES), 1)
        low = lane < ROUTE_WIN
        total = None
        for g in range(N_EXPERTS // ROUTE_GROUP):
            halves = []
            wins = []
            for half in range(ROUTE_GROUP // 2):
                e0 = g * ROUTE_GROUP + 2 * half
                t0 = st[:, e0:e0 + 1] - starts[e0]
                t1 = st[:, e0 + 1:e0 + 2] + (ROUTE_WIN - starts[e0 + 1])
                halves.append(jnp.where(lane == jnp.where(low, t0, t1), 1.0, 0.0).astype(BF16))
                wins.append(y_ref[e0, pl.ds(starts[e0], ROUTE_WIN), :])
                wins.append(y_ref[e0 + 1, pl.ds(starts[e0 + 1], ROUTE_WIN), :])
            part = _dot(jnp.concatenate(halves, axis=1), jnp.concatenate(wins, axis=0))
            total = part if total is None else total + part
        return total

    def dense_scatter(rows):
        st = st_ref[0, rows, :]
        lane = lax.broadcasted_iota(jnp.int32, (ROUTE_TILE, cap), 1)
        total = None
        for e in range(N_EXPERTS):
            onehot = jnp.where(lane == st[:, e:e + 1], 1.0, 0.0).astype(BF16)
            part = _dot(onehot, y_ref[e])
            total = part if total is None else total + part
        return total

    def gate_stage(rows, ffn):
        x2 = x1_ref[0, rows, :] + ffn
        return x2, _dot(_rms(x2, gn_ref[...]).astype(BF16), wgbf_ref[...])

    def output_stage(rows, x2, gate_logits):
        emb = _rms(_dot(p_ref[0, rows, :].astype(BF16), wpbf_ref[...]), gp_ref[...])
        o_ref[0, rows, :] = x2 + jax.nn.sigmoid(gate_logits) * emb

    def pipeline(scatter_stage):
        per = PLE_ROWS // ROUTE_TILE
        groups = [slice(g * PLE_ROWS, (g + 1) * PLE_ROWS) for g in range(nsub // per)]
        ffn, gated = {}, {}
        for step in range(len(groups) + 2):
            if step < len(groups):
                ffn[step] = jnp.concatenate([scatter_stage(step * per + j) for j in range(per)], axis=0)
            if 0 <= step - 1 < len(groups):
                gated[step - 1] = gate_stage(groups[step - 1], ffn.pop(step - 1))
            if 0 <= step - 2 < len(groups):
                output_stage(groups[step - 2], *gated.pop(step - 2))

    pl.when(ok)(lambda: pipeline(lambda g: windowed_scatter(row_groups[g], geo[g][0])))
    pl.when(jnp.logical_not(ok))(lambda: pipeline(lambda g: dense_scatter(row_groups[g])))


def _combine(off, x1, slot_t, y, p, gn, wg, wp, gp, cap, tm=1024):
    b, s, _ = x1.shape
    assert tm % PLE_ROWS == 0 and PLE_ROWS % ROUTE_TILE == 0
    tile = lambda w: pl.BlockSpec((1, tm, w), lambda bi, i, off_ref: (bi, i, 0))
    fixed2 = lambda shape: pl.BlockSpec(shape, lambda bi, i, off_ref: (0, 0))
    return pl.pallas_call(
        functools.partial(_combine_kernel, cap=cap),
        grid_spec=pltpu.PrefetchScalarGridSpec(
            num_scalar_prefetch=1,
            grid=(b, s // tm),
            in_specs=[
                tile(D_MODEL),
                tile(LANES),
                pl.BlockSpec((N_EXPERTS, cap, D_MODEL), lambda bi, i, off_ref: (0, bi, 0)),
                tile(PLE_DIM),
                fixed2((1, D_MODEL)),
                pl.BlockSpec(wg.shape, lambda bi, i, off_ref: (0, 0), pipeline_mode=pl.Buffered(1)),
                pl.BlockSpec(wp.shape, lambda bi, i, off_ref: (0, 0), pipeline_mode=pl.Buffered(1)),
                fixed2((1, D_MODEL)),
            ],
            out_specs=tile(D_MODEL),
        ),
        out_shape=jax.ShapeDtypeStruct((b, s, D_MODEL), F32),
        compiler_params=_params("arbitrary", "arbitrary"),
        name="combine",
    )(off, x1, slot_t, y, p, gn, wg.astype(BF16), wp.astype(BF16), gp)


def kernel(x, p, norm_mix, w_in, w_pool, pool_scale, q_norm, k_norm, rpb, w_out, norm_ffn, w_router, w_gate, w_up, w_down, norm_ple, w_ple_gate, w_ple_proj, norm_ple_post):
    b, s, d = x.shape
    depth = w_in.shape[0]
    cap = EC_CAPACITY * s // N_EXPERTS
    row = lambda a: a.reshape(1, -1)
    for i in range(depth):
        u, q, k, v = _in_proj(x.reshape(b * s, d), row(norm_mix[i]), w_in[i],
                              row(jnp.tile(q_norm[i], ATTN_HEADS)), row(jnp.tile(k_norm[i], ATTN_HEADS)))
        shp = lambda a: a.reshape(b, s, -1)
        y_attn = _natten(shp(q), shp(k), shp(v), _attn_bias_table(rpb[i]))
        x1, h, aff_t = _mix(x, shp(u), y_attn, w_pool[i], row(pool_scale[i]), w_out[i],
                            row(norm_ffn[i]), w_router[i].T)
        slot, slot_t, off = _route(aff_t, cap)
        off = off[:, :OFF_STRIDE].reshape(-1)
        xe, gate = _dispatch(off, slot, aff_t, h, cap)
        y = _experts(xe, gate, w_gate[i], w_up[i], w_down[i])
        x = _combine(off, x1, slot_t, y, p[i], row(norm_ple[i]), w_ple_gate[i], w_ple_proj[i],
                     row(norm_ple_post[i]), cap)
    return x
```

```python
import functools

import jax
import jax.numpy as jnp
from jax import lax
from jax.experimental import pallas as pl
from jax.experimental.pallas import tpu as pltpu

D_MODEL = 1024
GRID_W = 64
POOL_WINDOWS = (2, 4, 8, 16)
POOL_WIDTH = D_MODEL // 2
POOL_GROUP = POOL_WIDTH // len(POOL_WINDOWS)
ATTN_HEADS = 8
HEAD_DIM = (D_MODEL // 2) // ATTN_HEADS
ATTN_WIDTH = ATTN_HEADS * HEAD_DIM
WIN_ROWS_MAX = 8
WIN_COLS = 16
N_EXPERTS = 16
EC_CAPACITY = 2
D_EXPERT = 2 * D_MODEL
PLE_DIM = 256
RMS_EPS = 1e-6

LANES = 128
POOL_HALO = 8
LOG2E = 1.4426950408889634
MASK_BIAS = -1e30
NATTEN_UNROLL = 8
ROUTE_TILE = 256
ROUTE_WIN = 64
ROUTE_GROUP = 4
OFF_STRIDE = 16
NOT_SELECTED = -(1 << 20)
assert 2 * ROUTE_WIN == LANES and ROUTE_GROUP % 2 == 0
PLE_ROWS = 256
MIX_ROWS = 512
PROJ_ROWS = 512
EXPERT_ROWS = 1024
VMEM_LIMIT = 56 * 1024 * 1024

BF16 = jnp.bfloat16
F32 = jnp.float32


def _params(*sem):
    return pltpu.CompilerParams(dimension_semantics=sem, vmem_limit_bytes=VMEM_LIMIT)


def _rms(x, g):
    return x * lax.rsqrt(jnp.mean(x * x, axis=-1, keepdims=True) + RMS_EPS) * g


def _dot(a, b):
    return jnp.dot(a, b, preferred_element_type=F32)


def _dot_nt(a, b):
    return lax.dot_general(a, b, (((1,), (1,)), ((), ())), preferred_element_type=F32)


def _in_proj_kernel(x_ref, g_ref, w_ref, qg_ref, kg_ref,
                    u_ref, q_ref, kt_ref, v_ref, wbf_ref, wkt_ref):
    k_lo = POOL_WIDTH + ATTN_WIDTH

    @pl.when(pl.program_id(0) == 0)
    def _():
        wbf_ref[:, :k_lo] = w_ref[:, :k_lo].astype(BF16)
        wbf_ref[:, k_lo:] = w_ref[:, k_lo + ATTN_WIDTH:].astype(BF16)
        wkt_ref[...] = w_ref[:, k_lo:k_lo + ATTN_WIDTH].T.astype(BF16)

    def head_norm(t, g):
        low = lax.broadcasted_iota(jnp.int32, (t.shape[0], LANES), 1) < HEAD_DIM
        out = []
        for j in range(t.shape[1] // LANES):
            tile = t[:, j * LANES:(j + 1) * LANES]
            sq = tile * tile
            sa = jnp.sum(jnp.where(low, sq, 0.0), axis=-1, keepdims=True)
            sb = jnp.sum(sq, axis=-1, keepdims=True) - sa
            ra = lax.rsqrt(sa * (1.0 / HEAD_DIM) + RMS_EPS)
            rb = lax.rsqrt(sb * (1.0 / HEAD_DIM) + RMS_EPS)
            out.append(tile * jnp.where(low, ra, rb) * g[:, j * LANES:(j + 1) * LANES])
        return jnp.concatenate(out, axis=-1)

    def head_norm_t(t, g):
        out = []
        for hd in range(ATTN_HEADS):
            blk = t[hd * HEAD_DIM:(hd + 1) * HEAD_DIM, :]
            ms = jnp.mean(blk * blk, axis=0, keepdims=True)
            out.append(blk * lax.rsqrt(ms + RMS_EPS) * g[hd * HEAD_DIM:(hd + 1) * HEAD_DIM, :])
        return jnp.concatenate(out, axis=0)

    def norm_stage(rows):
        return _rms(x_ref[rows, :], g_ref[...]).astype(BF16)

    def proj_stage(h):
        return _dot(h, wbf_ref[...]), _dot_nt(wkt_ref[...], h)

    def head_stage(rows, z, kt):
        u_ref[rows, :] = z[:, :POOL_WIDTH]
        q = z[:, POOL_WIDTH:k_lo]
        v = z[:, k_lo:]
        q_ref[rows, :] = (head_norm(q, qg_ref[...]) * (HEAD_DIM ** -0.5 * LOG2E)).astype(BF16)
        kt_ref[:, rows] = head_norm_t(kt, kg_ref[...]).astype(BF16)
        v_ref[rows, :] = v.astype(BF16)

    groups = [slice(sub * PROJ_ROWS, (sub + 1) * PROJ_ROWS) for sub in range(x_ref.shape[0] // PROJ_ROWS)]
    zs = [proj_stage(norm_stage(rows)) for rows in groups]
    for rows, (z, kt) in zip(groups, zs):
        head_stage(rows, z, kt)


def _in_proj(x2, g, w, qg, kg, tm=1024):
    n = x2.shape[0]
    zw = w.shape[1]
    assert 2 * HEAD_DIM == LANES
    row = lambda i: (i, 0)
    fixed = lambda i: (0, 0)
    return pl.pallas_call(
        _in_proj_kernel,
        grid=(n // tm,),
        in_specs=[
            pl.BlockSpec((tm, D_MODEL), row),
            pl.BlockSpec((1, D_MODEL), fixed),
            pl.BlockSpec((D_MODEL, zw), fixed, pipeline_mode=pl.Buffered(1)),
            pl.BlockSpec((1, ATTN_WIDTH), fixed),
            pl.BlockSpec((ATTN_WIDTH, 1), fixed),
        ],
        out_specs=[
            pl.BlockSpec((tm, POOL_WIDTH), row),
            pl.BlockSpec((tm, ATTN_WIDTH), row),
            pl.BlockSpec((ATTN_WIDTH, tm), lambda i: (0, i)),
            pl.BlockSpec((tm, ATTN_WIDTH), row),
        ],
        out_shape=[
            jax.ShapeDtypeStruct((n, POOL_WIDTH), F32),
            jax.ShapeDtypeStruct((n, ATTN_WIDTH), BF16),
            jax.ShapeDtypeStruct((ATTN_WIDTH, n), BF16),
            jax.ShapeDtypeStruct((n, ATTN_WIDTH), BF16),
        ],
        scratch_shapes=[pltpu.VMEM((D_MODEL, zw - ATTN_WIDTH), BF16), pltpu.VMEM((ATTN_WIDTH, D_MODEL), BF16)],
        compiler_params=_params("arbitrary"),
        name="in_proj",
    )(x2, g, w, qg, kg)


def _natten_kernel(q_ref, kt_ref, v_ref, tbl_ref, o_ref, bias_ref, kts_ref, *, rows, kh):
    band = kh * GRID_W
    lane = lax.broadcasted_iota(jnp.int32, (GRID_W, LANES), 1)
    first = lane < HEAD_DIM

    @pl.when(pl.program_id(1) == 0)
    def _():
        for hh in range(2):
            for d0 in range(WIN_ROWS_MAX):
                for kk in range(kh // 2):
                    bias_ref[d0, hh * GRID_W:(hh + 1) * GRID_W, kk * LANES:(kk + 1) * LANES] = jnp.where(
                        lane < GRID_W, tbl_ref[hh, d0 + 2 * kk], tbl_ref[hh, d0 + 2 * kk + 1])

    seq = kt_ref.shape[1]
    kts_ref[...] = pltpu.bitcast(pltpu.roll(pltpu.bitcast(kt_ref[...], jnp.int32), seq - GRID_W, axis=1), BF16)

    def geometry(g):
        geo = []
        for r in range(g * NATTEN_UNROLL, (g + 1) * NATTEN_UNROLL):
            rs = min(max(r - kh // 2, 0), rows - kh)
            geo.append((rs - r + (WIN_ROWS_MAX - 1), r * GRID_W, rs * GRID_W))
        return geo

    def key_band(k0):
        if k0 % LANES == 0:
            return kt_ref[:, k0:k0 + band]
        return kts_ref[:, k0 - GRID_W:k0 - GRID_W + band]

    def score_stage(g):
        scores = []
        for d0, q0, k0 in geometry(g):
            qr = q_ref[0, q0:q0 + GRID_W, :]
            zero = jnp.zeros_like(qr)
            q2 = jnp.concatenate([jnp.where(first, qr, zero), jnp.where(first, zero, qr)], axis=0)
            s = _dot(q2, key_band(k0)) + bias_ref[d0]
            scores.append((s, jnp.max(s, axis=-1, keepdims=True)))
        return scores

    def softmax_stage(scores):
        probs = []
        for s, m in scores:
            e = jnp.exp2(s - m)
            probs.append((e.astype(BF16), jnp.sum(e, axis=-1, keepdims=True)))
        return probs

    def value_stage(g, probs):
        for (d0, q0, k0), (e, l) in zip(geometry(g), probs):
            o = _dot(e, v_ref[0, k0:k0 + band, :]) / l
            o_ref[0, q0:q0 + GRID_W, :] = jnp.where(first, o[:GRID_W], o[GRID_W:]).astype(o_ref.dtype)

    ngroups = rows // NATTEN_UNROLL
    scores, probs = {}, {}
    for step in range(ngroups + 2):
        if step < ngroups:
            scores[step] = score_stage(step)
        if 0 <= step - 1 < ngroups:
            probs[step - 1] = softmax_stage(scores.pop(step - 1))
        if 0 <= step - 2 < ngroups:
            value_stage(step - 2, probs.pop(step - 2))


def _natten(q, kt, v, bias):
    b, s, _ = q.shape
    rows = s // GRID_W
    kh = min(WIN_ROWS_MAX, rows)
    assert rows % NATTEN_UNROLL == 0 and kh % 2 == 0 and 2 * GRID_W == LANES
    pairs = ATTN_WIDTH // LANES
    blk = pl.BlockSpec((1, s, LANES), lambda p, bi: (bi, 0, p))
    return pl.pallas_call(
        functools.partial(_natten_kernel, rows=rows, kh=kh),
        grid=(pairs, b),
        in_specs=[blk, pl.BlockSpec((LANES, s), lambda p, bi: (p, bi)), blk,
                  pl.BlockSpec((2,) + bias.shape[1:], lambda p, bi: (p, 0, 0, 0))],
        out_specs=blk,
        out_shape=jax.ShapeDtypeStruct((b, s, ATTN_WIDTH), BF16),
        scratch_shapes=[pltpu.VMEM((WIN_ROWS_MAX, 2 * GRID_W, kh * GRID_W), F32),
                        pltpu.VMEM((LANES, s), BF16)],
        compiler_params=_params("arbitrary", "arbitrary"),
        name="natten",
    )(q, kt, v, bias)


def _attn_bias_table(rpb):
    c = jnp.arange(GRID_W)
    cs = jnp.clip(c - WIN_COLS // 2, 0, GRID_W - WIN_COLS)
    j = jnp.arange(GRID_W)
    valid = (j[None, :] >= cs[:, None]) & (j[None, :] < cs[:, None] + WIN_COLS)
    dc = j[None, :] - c[:, None] + (WIN_COLS - 1)
    pick = ((dc[None] == jnp.arange(2 * WIN_COLS - 1)[:, None, None]) & valid[None]).astype(F32)
    t = jnp.einsum('hrd,dcj->hrcj', rpb.astype(F32), pick, precision=lax.Precision.HIGHEST)
    t = jnp.where(valid, t * LOG2E, MASK_BIAS)
    return jnp.concatenate([t, t], axis=-1)


def _window_sum(upad, w, tm):
    n = upad.shape[0]
    fwd = upad
    span = 1
    while span < min(w, POOL_HALO):
        fwd = fwd + pltpu.roll(fwd, n - span, axis=0)
        span *= 2
    centre = slice(POOL_HALO, POOL_HALO + tm)
    if w == 2 * POOL_HALO:
        return fwd[0:tm] + fwd[centre]
    return pltpu.roll(fwd, w // 2, axis=0)[centre]


def _mix_kernel(x_ref, u_ref, up_ref, un_ref, ya_ref, ic_ref, wp_ref, ps_ref, wo_ref, g_ref, wr_ref,
                x1_ref, h_ref, aff_ref, upad_ref, wobf_ref, wpbf_ref, *, tm):
    i = pl.program_id(1)

    @pl.when((pl.program_id(0) == 0) & (i == 0))
    def _():
        wobf_ref[...] = wo_ref[...].astype(BF16)
        wpbf_ref[...] = wp_ref[...].astype(BF16)

    upad_ref[0:POOL_HALO, :] = jnp.where(i > 0, up_ref[0], 0.0)
    upad_ref[POOL_HALO:POOL_HALO + tm, :] = u_ref[0]
    upad_ref[POOL_HALO + tm:, :] = jnp.where(i < pl.num_programs(1) - 1, un_ref[0], 0.0)

    ngroups = tm // MIX_ROWS
    group_cols = [slice(gi * POOL_GROUP, (gi + 1) * POOL_GROUP) for gi in range(len(POOL_WINDOWS))]

    def pool_stage(g):
        ds = []
        for cols, w in zip(group_cols, POOL_WINDOWS):
            upad = upad_ref[g * MIX_ROWS:(g + 1) * MIX_ROWS + 2 * POOL_HALO, cols]
            edge = jnp.full((POOL_HALO, POOL_GROUP), 1.0 / w, F32)
            head = jnp.where(i == 0, ic_ref[0:POOL_HALO, cols], 1.0 / w) if g == 0 else edge
            tail = (jnp.where(i == pl.num_programs(1) - 1, ic_ref[POOL_HALO:, cols], 1.0 / w)
                    if g == ngroups - 1 else edge)
            inv = jnp.concatenate(
                [head, jnp.full((MIX_ROWS - 2 * POOL_HALO, POOL_GROUP), 1.0 / w, F32), tail], axis=0)
            d = _window_sum(upad, w, MIX_ROWS) * inv - upad[POOL_HALO:POOL_HALO + MIX_ROWS]
            ds.append(d.astype(BF16))
        return ds

    def proj_stage(g, ds):
        rows = slice(g * MIX_ROWS, (g + 1) * MIX_ROWS)
        ypool = [_dot(d, wpbf_ref[gi]) * ps_ref[:, cols] for gi, (d, cols) in enumerate(zip(ds, group_cols))]
        ypool = jnp.concatenate(ypool, axis=-1).astype(BF16)
        return _dot(ypool, wobf_ref[:POOL_WIDTH, :]) + _dot(ya_ref[0, rows, :], wobf_ref[POOL_WIDTH:, :])

    def route_stage(g, mix):
        rows = slice(g * MIX_ROWS, (g + 1) * MIX_ROWS)
        x1 = x_ref[0, rows, :] + mix
        x1_ref[0, rows, :] = x1
        h = _rms(x1, g_ref[...]).astype(BF16)
        h_ref[0, rows, :] = h
        logits = _dot_nt(wr_ref[...].astype(BF16), h)
        m = jnp.max(logits, axis=0, keepdims=True)
        e = jnp.exp(logits - m)
        aff_ref[0, :, rows] = e / jnp.sum(e, axis=0, keepdims=True)

    pooled, mixed = {}, {}
    for step in range(ngroups + 2):
        if step < ngroups:
            pooled[step] = pool_stage(step)
        if 0 <= step - 1 < ngroups:
            mixed[step - 1] = proj_stage(step - 1, pooled.pop(step - 1))
        if 0 <= step - 2 < ngroups:
            route_stage(step - 2, mixed.pop(step - 2))


def _pool_edge_inverse_counts(s):
    t = jnp.concatenate([jnp.arange(POOL_HALO), jnp.arange(s - POOL_HALO, s)])
    cols = []
    for w in POOL_WINDOWS:
        lo = jnp.clip(t - w // 2, 0, s - 1)
        hi = jnp.clip(t + (w - w // 2) - 1, 0, s - 1)
        inv = 1.0 / (hi - lo + 1).astype(F32)
        cols.append(jnp.broadcast_to(inv[:, None], (2 * POOL_HALO, POOL_GROUP)))
    return jnp.concatenate(cols, axis=-1)


def _mix(x, u, ya, wp, ps, wo, g, wr_t, tm=1024):
    b, s, _ = x.shape
    nt = s // tm
    hb = tm // POOL_HALO
    tile = lambda w: pl.BlockSpec((1, tm, w), lambda bi, i: (bi, i, 0))
    fixed2 = lambda shape: pl.BlockSpec(shape, lambda bi, i: (0, 0))
    return pl.pallas_call(
        functools.partial(_mix_kernel, tm=tm),
        grid=(b, nt),
        in_specs=[
            tile(D_MODEL),
            tile(POOL_WIDTH),
            pl.BlockSpec((1, POOL_HALO, POOL_WIDTH), lambda bi, i: (bi, jnp.maximum(i * hb - 1, 0), 0)),
            pl.BlockSpec((1, POOL_HALO, POOL_WIDTH),
                         lambda bi, i: (bi, jnp.minimum((i + 1) * hb, s // POOL_HALO - 1), 0)),
            tile(ATTN_WIDTH),
            fixed2((2 * POOL_HALO, POOL_WIDTH)),
            pl.BlockSpec(wp.shape, lambda bi, i: (0, 0, 0)),
            fixed2((1, POOL_WIDTH)),
            pl.BlockSpec(wo.shape, lambda bi, i: (0, 0), pipeline_mode=pl.Buffered(1)),
            fixed2((1, D_MODEL)),
            fixed2(wr_t.shape),
        ],
        out_specs=[
            tile(D_MODEL),
            tile(D_MODEL),
            pl.BlockSpec((1, N_EXPERTS, tm), lambda bi, i: (bi, 0, i)),
        ],
        out_shape=[
            jax.ShapeDtypeStruct((b, s, D_MODEL), F32),
            jax.ShapeDtypeStruct((b, s, D_MODEL), BF16),
            jax.ShapeDtypeStruct((b, N_EXPERTS, s), F32),
        ],
        scratch_shapes=[
            pltpu.VMEM((tm + 2 * POOL_HALO, POOL_WIDTH), F32),
            pltpu.VMEM(wo.shape, BF16),
            pltpu.VMEM(wp.shape, BF16),
        ],
        compiler_params=_params("arbitrary", "arbitrary"),
        name="mix",
    )(x, u, u, u, ya, _pool_edge_inverse_counts(s), wp, ps, wo, g, wr_t)


def _lane_cumsum_exclusive(m):
    e, s = m.shape
    r = lax.broadcasted_iota(jnp.int32, (LANES, LANES), 0)
    c = lax.broadcasted_iota(jnp.int32, (LANES, LANES), 1)
    upper = (r < c).astype(BF16)
    carry = jnp.zeros((e, 1), F32)
    out = []
    for blk in range(s // LANES):
        piece = m[:, blk * LANES:(blk + 1) * LANES]
        out.append(_dot(piece.astype(BF16), upper) + carry)
        carry = carry + jnp.sum(piece, axis=-1, keepdims=True)
    return jnp.concatenate(out, axis=-1)


def _route_kernel(aff_ref, slot_ref, slot_t_ref, off_ref, *, cap):
    aff = aff_ref[...]
    capf = jnp.float32(cap)

    def count_ge(cand_bits):
        return jnp.sum((aff >= pltpu.bitcast(cand_bits, F32)).astype(F32), axis=-1, keepdims=True)

    def search(step, ans):
        cand = ans | (jnp.int32(1) << (30 - step))
        return jnp.where(count_ge(cand) >= capf, cand, ans)

    thr = pltpu.bitcast(lax.fori_loop(0, 31, search, jnp.zeros((aff.shape[0], 1), jnp.int32)), F32)
    gt = aff > thr
    eq = aff == thr
    need = capf - jnp.sum(gt.astype(F32), axis=-1, keepdims=True)
    sel = gt | (eq & (_lane_cumsum_exclusive(eq.astype(F32)) < need))
    self = sel.astype(F32)
    slot = jnp.where(sel, _lane_cumsum_exclusive(self), float(NOT_SELECTED))
    slot_ref[...] = slot.astype(jnp.int32)
    pad = jnp.full((LANES - N_EXPERTS, slot.shape[1]), float(NOT_SELECTED), F32)
    for bi in range(slot_t_ref.shape[0]):
        mine = slot[bi * N_EXPERTS:(bi + 1) * N_EXPERTS]
        slot_t_ref[bi] = jnp.concatenate([mine, pad], axis=0).T.astype(jnp.int32)
    lane = lax.broadcasted_iota(jnp.int32, (slot.shape[0], LANES), 1)
    off = jnp.zeros((slot.shape[0], LANES), F32)
    run = jnp.zeros((slot.shape[0], 1), F32)
    for i in range(1, slot.shape[1] // ROUTE_TILE + 1):
        run = run + jnp.sum(self[:, (i - 1) * ROUTE_TILE:i * ROUTE_TILE], axis=-1, keepdims=True)
        off = jnp.where(lane == i, run, off)
    off_ref[...] = off.astype(jnp.int32)


def _route(aff_t, cap):
    b, e, s = aff_t.shape
    assert e == N_EXPERTS and s % ROUTE_TILE == 0 and s // ROUTE_TILE < OFF_STRIDE
    slot, slot_t, off = pl.pallas_call(
        functools.partial(_route_kernel, cap=cap),
        out_shape=[
            jax.ShapeDtypeStruct((b * e, s), jnp.int32),
            jax.ShapeDtypeStruct((b, s, LANES), jnp.int32),
            jax.ShapeDtypeStruct((b * e, LANES), jnp.int32),
        ],
        compiler_params=pltpu.CompilerParams(vmem_limit_bytes=VMEM_LIMIT),
        name="route",
    )(aff_t.reshape(b * e, s))
    return slot.reshape(b, e, s), slot_t, off


def _window_starts(off_ref, bi, ti, cap):
    starts = []
    ok = None
    for e in range(N_EXPERTS):
        base = (bi * N_EXPERTS + e) * OFF_STRIDE + ti
        lo = off_ref[base]
        hi = off_ref[base + 1]
        st = jnp.minimum(jnp.bitwise_and(lo, -16), cap - ROUTE_WIN)
        fits = hi <= st + ROUTE_WIN
        ok = fits if ok is None else jnp.logical_and(ok, fits)
        starts.append(pl.multiple_of(st, 16))
    return starts, ok


def _dispatch_kernel(off_ref, slot_ref, aff_ref, h_ref, xe_ref, gate_ref, *, cap):
    bi = pl.program_id(0)
    ti = pl.program_id(1)

    @pl.when(ti == 0)
    def _():
        xe_ref[...] = jnp.zeros_like(xe_ref)
        gate_ref[...] = jnp.zeros_like(gate_ref)

    nsub = h_ref.shape[1] // ROUTE_TILE
    col_groups = [slice(sub * ROUTE_TILE, (sub + 1) * ROUTE_TILE) for sub in range(nsub)]
    geo = [_window_starts(off_ref, bi, ti * nsub + sub, cap) for sub in range(nsub)]
    ok = functools.reduce(jnp.logical_and, [fits for _, fits in geo])

    @pl.when(ok)
    def _():
        row = lax.broadcasted_iota(jnp.int32, (ROUTE_WIN, ROUTE_TILE), 0)
        for cols, (starts, _) in zip(col_groups, geo):
            slot = slot_ref[0, :, cols]
            aff = aff_ref[0, :, cols]
            hits = [row == (slot[e:e + 1, :] - starts[e]) for e in range(N_EXPERTS)]
            onehot = jnp.concatenate([jnp.where(hh, 1.0, 0.0).astype(BF16) for hh in hits], axis=0)
            res = _dot(onehot, h_ref[0, cols, :])
            for e, hh in enumerate(hits):
                win = pl.ds(starts[e], ROUTE_WIN)
                xe_ref[e, win, :] += res[e * ROUTE_WIN:(e + 1) * ROUTE_WIN].astype(BF16)
                gate_ref[e, win, :] += jnp.sum(jnp.where(hh, aff[e:e + 1, :], 0.0), axis=-1, keepdims=True)

    @pl.when(jnp.logical_not(ok))
    def _():
        row = lax.broadcasted_iota(jnp.int32, (cap, ROUTE_TILE), 0)
        for cols in col_groups:
            slot = slot_ref[0, :, cols]
            aff = aff_ref[0, :, cols]
            h = h_ref[0, cols, :]
            for e in range(N_EXPERTS):
                hh = row == slot[e:e + 1, :]
                xe_ref[e] += _dot(jnp.where(hh, 1.0, 0.0).astype(BF16), h).astype(BF16)
                gate_ref[e] += jnp.sum(jnp.where(hh, aff[e:e + 1, :], 0.0), axis=-1, keepdims=True)


def _dispatch(off, slot, aff_t, h, cap, tm=1024):
    b, e, s = slot.shape
    assert tm % ROUTE_TILE == 0
    rows = pl.BlockSpec((1, e, tm), lambda bi, ti, off_ref: (bi, 0, ti))
    return pl.pallas_call(
        functools.partial(_dispatch_kernel, cap=cap),
        grid_spec=pltpu.PrefetchScalarGridSpec(
            num_scalar_prefetch=1,
            grid=(b, s // tm),
            in_specs=[rows, rows,
                      pl.BlockSpec((1, tm, D_MODEL), lambda bi, ti, off_ref: (bi, ti, 0))],
            out_specs=[
                pl.BlockSpec((e, cap, D_MODEL), lambda bi, ti, off_ref: (0, bi, 0)),
                pl.BlockSpec((e, cap, 1), lambda bi, ti, off_ref: (0, bi, 0)),
            ],
        ),
        out_shape=[
            jax.ShapeDtypeStruct((e, b * cap, D_MODEL), BF16),
            jax.ShapeDtypeStruct((e, b * cap, 1), F32),
        ],
        compiler_params=_params("arbitrary", "arbitrary"),
        name="dispatch",
    )(off, slot, aff_t, h)


def _experts_kernel(xe_ref, gate_ref, wg_ref, wu_ref, wd_ref, y_ref, acc_ref):
    f = pl.program_id(1)

    def hidden_chunk(first):
        wg = wg_ref[0].astype(BF16)
        wu = wu_ref[0].astype(BF16)
        wd = wd_ref[0].astype(BF16)
        for mb in range(xe_ref.shape[1] // EXPERT_ROWS):
            rows = slice(mb * EXPERT_ROWS, (mb + 1) * EXPERT_ROWS)
            xe = xe_ref[0, rows, :]
            a = _dot(xe, wg)
            b = _dot(xe, wu)
            part = _dot((a * jax.nn.sigmoid(a) * b).astype(BF16), wd)
            acc_ref[rows, :] = part if first else acc_ref[rows, :] + part

    pl.when(f == 0)(functools.partial(hidden_chunk, True))
    pl.when(f > 0)(functools.partial(hidden_chunk, False))

    @pl.when(f == pl.num_programs(1) - 1)
    def _():
        y_ref[0] = (acc_ref[...] * gate_ref[0]).astype(y_ref.dtype)


def _experts(xe, gate, wg, wu, wd, tf=512):
    e, m, _ = xe.shape
    nf = D_EXPERT // tf
    return pl.pallas_call(
        _experts_kernel,
        grid=(e, nf),
        in_specs=[
            pl.BlockSpec((1, m, D_MODEL), lambda ei, f: (ei, 0, 0)),
            pl.BlockSpec((1, m, 1), lambda ei, f: (ei, 0, 0)),
            pl.BlockSpec((1, D_MODEL, tf), lambda ei, f: (ei, 0, f)),
            pl.BlockSpec((1, D_MODEL, tf), lambda ei, f: (ei, 0, f)),
            pl.BlockSpec((1, tf, D_MODEL), lambda ei, f: (ei, f, 0)),
        ],
        out_specs=pl.BlockSpec((1, m, D_MODEL), lambda ei, f: (ei, 0, 0)),
        out_shape=jax.ShapeDtypeStruct((e, m, D_MODEL), BF16),
        scratch_shapes=[pltpu.VMEM((m, D_MODEL), F32)],
        compiler_params=_params("arbitrary", "arbitrary"),
        name="experts",
    )(xe, gate, wg, wu, wd)


def _combine_kernel(off_ref, x1_ref, st_ref, y_ref, p_ref, gn_ref, wgbf_ref, wpbf_ref, gp_ref,
                    o_ref, *, cap):
    bi = pl.program_id(0)
    nsub = x1_ref.shape[1] // ROUTE_TILE
    row_groups = [slice(sub * ROUTE_TILE, (sub + 1) * ROUTE_TILE) for sub in range(nsub)]
    geo = [_window_starts(off_ref, bi, pl.program_id(1) * nsub + sub, cap) for sub in range(nsub)]
    ok = functools.reduce(jnp.logical_and, [fits for _, fits in geo])

    def token_slots(rows):
        return st_ref[0, rows, :]

    def windowed_scatter(rows, starts):
        st = token_slots(rows)
        lane = lax.broadcasted_iota(jnp.int32, (ROUTE_TILE, LANES), 1)
        low = lane < ROUTE_WIN
        total = None
        for g in range(N_EXPERTS // ROUTE_GROUP):
            halves = []
            wins = []
            for half in range(ROUTE_GROUP // 2):
                e0 = g * ROUTE_GROUP + 2 * half
                t0 = st[:, e0:e0 + 1] - starts[e0]
                t1 = st[:, e0 + 1:e0 + 2] + (ROUTE_WIN - starts[e0 + 1])
                halves.append(jnp.where(lane == jnp.where(low, t0, t1), 1.0, 0.0).astype(BF16))
                wins.append(y_ref[e0, pl.ds(starts[e0], ROUTE_WIN), :])
                wins.append(y_ref[e0 + 1, pl.ds(starts[e0 + 1], ROUTE_WIN), :])
            part = _dot(jnp.concatenate(halves, axis=1), jnp.concatenate(wins, axis=0))
            total = part if total is None else total + part
        return total

    def dense_scatter(rows):
        st = token_slots(rows)
        lane = lax.broadcasted_iota(jnp.int32, (ROUTE_TILE, cap), 1)
        total = None
        for e in range(N_EXPERTS):
            onehot = jnp.where(lane == st[:, e:e + 1], 1.0, 0.0).astype(BF16)
            part = _dot(onehot, y_ref[e])
            total = part if total is None else total + part
        return total

    def gate_stage(rows, ffn):
        x2 = x1_ref[0, rows, :] + ffn
        return x2, _dot(_rms(x2, gn_ref[...]).astype(BF16), wgbf_ref[...])

    def output_stage(rows, x2, gate_logits):
        emb = _rms(_dot(p_ref[0, rows, :].astype(BF16), wpbf_ref[...]), gp_ref[...])
        o_ref[0, rows, :] = x2 + jax.nn.sigmoid(gate_logits) * emb

    def pipeline(scatter_stage):
        per = PLE_ROWS // ROUTE_TILE
        groups = [slice(g * PLE_ROWS, (g + 1) * PLE_ROWS) for g in range(nsub // per)]
        ffn, gated = {}, {}
        for step in range(len(groups) + 2):
            if step < len(groups):
                ffn[step] = jnp.concatenate([scatter_stage(step * per + j) for j in range(per)], axis=0)
            if 0 <= step - 1 < len(groups):
                gated[step - 1] = gate_stage(groups[step - 1], ffn.pop(step - 1))
            if 0 <= step - 2 < len(groups):
                output_stage(groups[step - 2], *gated.pop(step - 2))

    pl.when(ok)(lambda: pipeline(lambda g: windowed_scatter(row_groups[g], geo[g][0])))
    pl.when(jnp.logical_not(ok))(lambda: pipeline(lambda g: dense_scatter(row_groups[g])))


def _combine(off, x1, slot_t, y, p, gn, wg, wp, gp, cap, tm=1024):
    b, s, _ = x1.shape
    assert tm % PLE_ROWS == 0 and PLE_ROWS % ROUTE_TILE == 0
    tile = lambda w: pl.BlockSpec((1, tm, w), lambda bi, i, off_ref: (bi, i, 0))
    fixed2 = lambda shape: pl.BlockSpec(shape, lambda bi, i, off_ref: (0, 0))
    return pl.pallas_call(
        functools.partial(_combine_kernel, cap=cap),
        grid_spec=pltpu.PrefetchScalarGridSpec(
            num_scalar_prefetch=1,
            grid=(b, s // tm),
            in_specs=[
                tile(D_MODEL),
                tile(LANES),
                pl.BlockSpec((N_EXPERTS, cap, D_MODEL), lambda bi, i, off_ref: (0, bi, 0)),
                tile(PLE_DIM),
                fixed2((1, D_MODEL)),
                pl.BlockSpec(wg.shape, lambda bi, i, off_ref: (0, 0), pipeline_mode=pl.Buffered(1)),
                pl.BlockSpec(wp.shape, lambda bi, i, off_ref: (0, 0), pipeline_mode=pl.Buffered(1)),
                fixed2((1, D_MODEL)),
            ],
            out_specs=tile(D_MODEL),
        ),
        out_shape=jax.ShapeDtypeStruct((b, s, D_MODEL), F32),
        compiler_params=_params("arbitrary", "arbitrary"),
        name="combine",
    )(off, x1, slot_t, y, p, gn, wg.astype(BF16), wp.astype(BF16), gp)


def kernel(x, p, norm_mix, w_in, w_pool, pool_scale, q_norm, k_norm, rpb, w_out, norm_ffn, w_router, w_gate, w_up, w_down, norm_ple, w_ple_gate, w_ple_proj, norm_ple_post):
    b, s, d = x.shape
    depth = w_in.shape[0]
    cap = EC_CAPACITY * s // N_EXPERTS
    row = lambda a: a.reshape(1, -1)
    for i in range(depth):
        u, q, kt, v = _in_proj(x.reshape(b * s, d), row(norm_mix[i]), w_in[i],
                               row(jnp.tile(q_norm[i], ATTN_HEADS)),
                               jnp.tile(k_norm[i], ATTN_HEADS).reshape(-1, 1))
        shp = lambda a: a.reshape(b, s, -1)
        y_attn = _natten(shp(q), kt, shp(v), _attn_bias_table(rpb[i]))
        x1, h, aff_t = _mix(x, shp(u), y_attn, w_pool[i], row(pool_scale[i]), w_out[i],
                            row(norm_ffn[i]), w_router[i].T)
        slot, slot_t, off = _route(aff_t, cap)
        off = off[:, :OFF_STRIDE].reshape(-1)
        xe, gate = _dispatch(off, slot, aff_t, h, cap)
        y = _experts(xe, gate, w_gate[i], w_up[i], w_down[i])
        x = _combine(off, x1, slot_t, y, p[i], row(norm_ple[i]), w_ple_gate[i], w_ple_proj[i],
                     row(norm_ple_post[i]), cap)
    return x
```

```python
import functools

import jax
import jax.numpy as jnp
from jax import lax
from jax.experimental import pallas as pl
from jax.experimental.pallas import tpu as pltpu

D_MODEL = 1024
GRID_W = 64
POOL_WINDOWS = (2, 4, 8, 16)
POOL_WIDTH = D_MODEL // 2
POOL_GROUP = POOL_WIDTH // len(POOL_WINDOWS)
ATTN_HEADS = 8
HEAD_DIM = (D_MODEL // 2) // ATTN_HEADS
ATTN_WIDTH = ATTN_HEADS * HEAD_DIM
WIN_ROWS_MAX = 8
WIN_COLS = 16
N_EXPERTS = 16
EC_CAPACITY = 2
D_EXPERT = 2 * D_MODEL
PLE_DIM = 256
RMS_EPS = 1e-6

LANES = 128
POOL_HALO = 8
LOG2E = 1.4426950408889634
MASK_BIAS = -1e30
NATTEN_UNROLL = 8
ROUTE_TILE = 256
ROUTE_WIN = 64
ROUTE_GROUP = 4
OFF_STRIDE = 16
NOT_SELECTED = -(1 << 20)
assert 2 * ROUTE_WIN == LANES and ROUTE_GROUP % 2 == 0
PLE_ROWS = 256
MIX_ROWS = 512
PROJ_ROWS = 512
EXPERT_ROWS = 2048
VMEM_LIMIT = 56 * 1024 * 1024

BF16 = jnp.bfloat16
F32 = jnp.float32


def _params(*sem):
    return pltpu.CompilerParams(dimension_semantics=sem, vmem_limit_bytes=VMEM_LIMIT)


def _rms(x, g):
    return x * lax.rsqrt(jnp.mean(x * x, axis=-1, keepdims=True) + RMS_EPS) * g


def _dot(a, b):
    return jnp.dot(a, b, preferred_element_type=F32)


def _dot_nt(a, b):
    return lax.dot_general(a, b, (((1,), (1,)), ((), ())), preferred_element_type=F32)


def _in_proj_kernel(x_ref, g_ref, w_ref, qg_ref, kg_ref,
                    u_ref, q_ref, k_ref, v_ref, wbf_ref):
    @pl.when(pl.program_id(0) == 0)
    def _():
        wbf_ref[...] = w_ref[...].astype(BF16)

    def head_norm(t, g):
        low = lax.broadcasted_iota(jnp.int32, (t.shape[0], LANES), 1) < HEAD_DIM
        out = []
        for j in range(t.shape[1] // LANES):
            tile = t[:, j * LANES:(j + 1) * LANES]
            sq = tile * tile
            sa = jnp.sum(jnp.where(low, sq, 0.0), axis=-1, keepdims=True)
            sb = jnp.sum(sq, axis=-1, keepdims=True) - sa
            ra = lax.rsqrt(sa * (1.0 / HEAD_DIM) + RMS_EPS)
            rb = lax.rsqrt(sb * (1.0 / HEAD_DIM) + RMS_EPS)
            out.append(tile * jnp.where(low, ra, rb) * g[:, j * LANES:(j + 1) * LANES])
        return jnp.concatenate(out, axis=-1)

    def norm_stage(rows):
        return _rms(x_ref[rows, :], g_ref[...]).astype(BF16)

    def proj_stage(h):
        return _dot(h, wbf_ref[...])

    def head_stage(rows, z):
        u_ref[rows, :] = z[:, :POOL_WIDTH]
        q = z[:, POOL_WIDTH:POOL_WIDTH + ATTN_WIDTH]
        k = z[:, POOL_WIDTH + ATTN_WIDTH:POOL_WIDTH + 2 * ATTN_WIDTH]
        v = z[:, POOL_WIDTH + 2 * ATTN_WIDTH:]
        q_ref[rows, :] = (head_norm(q, qg_ref[...]) * (HEAD_DIM ** -0.5 * LOG2E)).astype(BF16)
        k_ref[rows, :] = head_norm(k, kg_ref[...]).astype(BF16)
        v_ref[rows, :] = v.astype(BF16)

    groups = [slice(sub * PROJ_ROWS, (sub + 1) * PROJ_ROWS) for sub in range(x_ref.shape[0] // PROJ_ROWS)]
    zs = [proj_stage(norm_stage(rows)) for rows in groups]
    for rows, z in zip(groups, zs):
        head_stage(rows, z)


def _in_proj(x2, g, w, qg, kg, tm=1024):
    n = x2.shape[0]
    zw = w.shape[1]
    assert 2 * HEAD_DIM == LANES
    row = lambda i: (i, 0)
    fixed = lambda i: (0, 0)
    return pl.pallas_call(
        _in_proj_kernel,
        grid=(n // tm,),
        in_specs=[
            pl.BlockSpec((tm, D_MODEL), row),
            pl.BlockSpec((1, D_MODEL), fixed),
            pl.BlockSpec((D_MODEL, zw), fixed, pipeline_mode=pl.Buffered(1)),
            pl.BlockSpec((1, ATTN_WIDTH), fixed),
            pl.BlockSpec((1, ATTN_WIDTH), fixed),
        ],
        out_specs=[
            pl.BlockSpec((tm, POOL_WIDTH), row),
            pl.BlockSpec((tm, ATTN_WIDTH), row),
            pl.BlockSpec((tm, ATTN_WIDTH), row),
            pl.BlockSpec((tm, ATTN_WIDTH), row),
        ],
        out_shape=[
            jax.ShapeDtypeStruct((n, POOL_WIDTH), F32),
            jax.ShapeDtypeStruct((n, ATTN_WIDTH), BF16),
            jax.ShapeDtypeStruct((n, ATTN_WIDTH), BF16),
            jax.ShapeDtypeStruct((n, ATTN_WIDTH), BF16),
        ],
        scratch_shapes=[pltpu.VMEM((D_MODEL, zw), BF16)],
        compiler_params=_params("arbitrary"),
        name="in_proj",
    )(x2, g, w, qg, kg)


def _natten_kernel(q_ref, k_ref, v_ref, tbl_ref, o_ref, bias_ref, *, rows, kh):
    band = kh * GRID_W
    lane = lax.broadcasted_iota(jnp.int32, (GRID_W, LANES), 1)
    first = lane < HEAD_DIM

    @pl.when(pl.program_id(1) == 0)
    def _():
        for hh in range(2):
            for d0 in range(WIN_ROWS_MAX):
                for kk in range(kh // 2):
                    bias_ref[d0, hh * GRID_W:(hh + 1) * GRID_W, kk * LANES:(kk + 1) * LANES] = jnp.where(
                        lane < GRID_W, tbl_ref[hh, d0 + 2 * kk], tbl_ref[hh, d0 + 2 * kk + 1])

    def geometry(g):
        geo = []
        for r in range(g * NATTEN_UNROLL, (g + 1) * NATTEN_UNROLL):
            rs = min(max(r - kh // 2, 0), rows - kh)
            geo.append((rs - r + (WIN_ROWS_MAX - 1), r * GRID_W, rs * GRID_W))
        return geo

    def score_stage(g):
        scores = []
        for d0, q0, k0 in geometry(g):
            qr = q_ref[0, q0:q0 + GRID_W, :]
            zero = jnp.zeros_like(qr)
            q2 = jnp.concatenate([jnp.where(first, qr, zero), jnp.where(first, zero, qr)], axis=0)
            s = _dot_nt(q2, k_ref[0, k0:k0 + band, :]) + bias_ref[d0]
            scores.append((s, jnp.max(s, axis=-1, keepdims=True)))
        return scores

    def softmax_stage(scores):
        probs = []
        for s, m in scores:
            e = jnp.exp2(s - m)
            probs.append((e.astype(BF16), jnp.sum(e, axis=-1, keepdims=True)))
        return probs

    def value_stage(g, probs):
        for (d0, q0, k0), (e, l) in zip(geometry(g), probs):
            o = _dot(e, v_ref[0, k0:k0 + band, :]) / l
            o_ref[0, q0:q0 + GRID_W, :] = jnp.where(first, o[:GRID_W], o[GRID_W:]).astype(o_ref.dtype)

    ngroups = rows // NATTEN_UNROLL
    scores, probs = {}, {}
    for step in range(ngroups + 2):
        if step < ngroups:
            scores[step] = score_stage(step)
        if 0 <= step - 1 < ngroups:
            probs[step - 1] = softmax_stage(scores.pop(step - 1))
        if 0 <= step - 2 < ngroups:
            value_stage(step - 2, probs.pop(step - 2))


def _natten(q, k, v, bias):
    b, s, _ = q.shape
    rows = s // GRID_W
    kh = min(WIN_ROWS_MAX, rows)
    assert rows % NATTEN_UNROLL == 0 and kh % 2 == 0 and 2 * GRID_W == LANES
    pairs = ATTN_WIDTH // LANES
    blk = pl.BlockSpec((1, s, LANES), lambda p, bi: (bi, 0, p))
    return pl.pallas_call(
        functools.partial(_natten_kernel, rows=rows, kh=kh),
        grid=(pairs, b),
        in_specs=[blk, blk, blk,
                  pl.BlockSpec((2,) + bias.shape[1:], lambda p, bi: (p, 0, 0, 0))],
        out_specs=blk,
        out_shape=jax.ShapeDtypeStruct((b, s, ATTN_WIDTH), BF16),
        scratch_shapes=[pltpu.VMEM((WIN_ROWS_MAX, 2 * GRID_W, kh * GRID_W), F32)],
        compiler_params=_params("arbitrary", "arbitrary"),
        name="natten",
    )(q, k, v, bias)


def _attn_bias_table(rpb):
    c = jnp.arange(GRID_W)
    cs = jnp.clip(c - WIN_COLS // 2, 0, GRID_W - WIN_COLS)
    j = jnp.arange(GRID_W)
    valid = (j[None, :] >= cs[:, None]) & (j[None, :] < cs[:, None] + WIN_COLS)
    dc = j[None, :] - c[:, None] + (WIN_COLS - 1)
    pick = ((dc[None] == jnp.arange(2 * WIN_COLS - 1)[:, None, None]) & valid[None]).astype(F32)
    t = jnp.einsum('hrd,dcj->hrcj', rpb.astype(F32), pick, precision=lax.Precision.HIGHEST)
    t = jnp.where(valid, t * LOG2E, MASK_BIAS)
    return jnp.concatenate([t, t], axis=-1)


def _window_sum(upad, w, tm):
    n = upad.shape[0]
    fwd = upad
    span = 1
    while span < min(w, POOL_HALO):
        fwd = fwd + pltpu.roll(fwd, n - span, axis=0)
        span *= 2
    centre = slice(POOL_HALO, POOL_HALO + tm)
    if w == 2 * POOL_HALO:
        return fwd[0:tm] + fwd[centre]
    return pltpu.roll(fwd, w // 2, axis=0)[centre]


def _mix_kernel(x_ref, u_ref, up_ref, un_ref, ya_ref, ic_ref, wp_ref, ps_ref, wo_ref, g_ref, wr_ref,
                x1_ref, h_ref, aff_ref, upad_ref, wobf_ref, wpbf_ref, *, tm):
    i = pl.program_id(1)

    @pl.when((pl.program_id(0) == 0) & (i == 0))
    def _():
        wobf_ref[...] = wo_ref[...].astype(BF16)
        wpbf_ref[...] = wp_ref[...].astype(BF16)

    upad_ref[0:POOL_HALO, :] = jnp.where(i > 0, up_ref[0], 0.0)
    upad_ref[POOL_HALO:POOL_HALO + tm, :] = u_ref[0]
    upad_ref[POOL_HALO + tm:, :] = jnp.where(i < pl.num_programs(1) - 1, un_ref[0], 0.0)

    ngroups = tm // MIX_ROWS
    group_cols = [slice(gi * POOL_GROUP, (gi + 1) * POOL_GROUP) for gi in range(len(POOL_WINDOWS))]

    def pool_stage(g):
        ds = []
        for cols, w in zip(group_cols, POOL_WINDOWS):
            upad = upad_ref[g * MIX_ROWS:(g + 1) * MIX_ROWS + 2 * POOL_HALO, cols]
            edge = jnp.full((POOL_HALO, POOL_GROUP), 1.0 / w, F32)
            head = jnp.where(i == 0, ic_ref[0:POOL_HALO, cols], 1.0 / w) if g == 0 else edge
            tail = (jnp.where(i == pl.num_programs(1) - 1, ic_ref[POOL_HALO:, cols], 1.0 / w)
                    if g == ngroups - 1 else edge)
            inv = jnp.concatenate(
                [head, jnp.full((MIX_ROWS - 2 * POOL_HALO, POOL_GROUP), 1.0 / w, F32), tail], axis=0)
            d = _window_sum(upad, w, MIX_ROWS) * inv - upad[POOL_HALO:POOL_HALO + MIX_ROWS]
            ds.append(d.astype(BF16))
        return ds

    def proj_stage(g, ds):
        rows = slice(g * MIX_ROWS, (g + 1) * MIX_ROWS)
        ypool = [_dot(d, wpbf_ref[gi]) * ps_ref[:, cols] for gi, (d, cols) in enumerate(zip(ds, group_cols))]
        ypool = jnp.concatenate(ypool, axis=-1).astype(BF16)
        return _dot(ypool, wobf_ref[:POOL_WIDTH, :]) + _dot(ya_ref[0, rows, :], wobf_ref[POOL_WIDTH:, :])

    def route_stage(g, mix):
        rows = slice(g * MIX_ROWS, (g + 1) * MIX_ROWS)
        x1 = x_ref[0, rows, :] + mix
        x1_ref[0, rows, :] = x1
        h = _rms(x1, g_ref[...]).astype(BF16)
        h_ref[0, rows, :] = h
        logits = _dot_nt(wr_ref[...].astype(BF16), h)
        m = jnp.max(logits, axis=0, keepdims=True)
        e = jnp.exp(logits - m)
        aff_ref[0, :, rows] = e / jnp.sum(e, axis=0, keepdims=True)

    pooled, mixed = {}, {}
    for step in range(ngroups + 2):
        if step < ngroups:
            pooled[step] = pool_stage(step)
        if 0 <= step - 1 < ngroups:
            mixed[step - 1] = proj_stage(step - 1, pooled.pop(step - 1))
        if 0 <= step - 2 < ngroups:
            route_stage(step - 2, mixed.pop(step - 2))


def _pool_edge_inverse_counts(s):
    t = jnp.concatenate([jnp.arange(POOL_HALO), jnp.arange(s - POOL_HALO, s)])
    cols = []
    for w in POOL_WINDOWS:
        lo = jnp.clip(t - w // 2, 0, s - 1)
        hi = jnp.clip(t + (w - w // 2) - 1, 0, s - 1)
        inv = 1.0 / (hi - lo + 1).astype(F32)
        cols.append(jnp.broadcast_to(inv[:, None], (2 * POOL_HALO, POOL_GROUP)))
    return jnp.concatenate(cols, axis=-1)


def _mix(x, u, ya, wp, ps, wo, g, wr_t, tm=1024):
    b, s, _ = x.shape
    nt = s // tm
    hb = tm // POOL_HALO
    tile = lambda w: pl.BlockSpec((1, tm, w), lambda bi, i: (bi, i, 0))
    fixed2 = lambda shape: pl.BlockSpec(shape, lambda bi, i: (0, 0))
    return pl.pallas_call(
        functools.partial(_mix_kernel, tm=tm),
        grid=(b, nt),
        in_specs=[
            tile(D_MODEL),
            tile(POOL_WIDTH),
            pl.BlockSpec((1, POOL_HALO, POOL_WIDTH), lambda bi, i: (bi, jnp.maximum(i * hb - 1, 0), 0)),
            pl.BlockSpec((1, POOL_HALO, POOL_WIDTH),
                         lambda bi, i: (bi, jnp.minimum((i + 1) * hb, s // POOL_HALO - 1), 0)),
            tile(ATTN_WIDTH),
            fixed2((2 * POOL_HALO, POOL_WIDTH)),
            pl.BlockSpec(wp.shape, lambda bi, i: (0, 0, 0)),
            fixed2((1, POOL_WIDTH)),
            pl.BlockSpec(wo.shape, lambda bi, i: (0, 0), pipeline_mode=pl.Buffered(1)),
            fixed2((1, D_MODEL)),
            fixed2(wr_t.shape),
        ],
        out_specs=[
            tile(D_MODEL),
            tile(D_MODEL),
            pl.BlockSpec((1, N_EXPERTS, tm), lambda bi, i: (bi, 0, i)),
        ],
        out_shape=[
            jax.ShapeDtypeStruct((b, s, D_MODEL), F32),
            jax.ShapeDtypeStruct((b, s, D_MODEL), BF16),
            jax.ShapeDtypeStruct((b, N_EXPERTS, s), F32),
        ],
        scratch_shapes=[
            pltpu.VMEM((tm + 2 * POOL_HALO, POOL_WIDTH), F32),
            pltpu.VMEM(wo.shape, BF16),
            pltpu.VMEM(wp.shape, BF16),
        ],
        compiler_params=_params("arbitrary", "arbitrary"),
        name="mix",
    )(x, u, u, u, ya, _pool_edge_inverse_counts(s), wp, ps, wo, g, wr_t)


def _lane_cumsum_exclusive(m):
    e, s = m.shape
    r = lax.broadcasted_iota(jnp.int32, (LANES, LANES), 0)
    c = lax.broadcasted_iota(jnp.int32, (LANES, LANES), 1)
    upper = (r < c).astype(BF16)
    carry = jnp.zeros((e, 1), F32)
    out = []
    for blk in range(s // LANES):
        piece = m[:, blk * LANES:(blk + 1) * LANES]
        out.append(_dot(piece.astype(BF16), upper) + carry)
        carry = carry + jnp.sum(piece, axis=-1, keepdims=True)
    return jnp.concatenate(out, axis=-1)


def _route_kernel(aff_ref, slot_ref, slot_t_ref, off_ref, *, cap):
    aff = aff_ref[...]
    capf = jnp.float32(cap)

    def count_ge(cand_bits):
        return jnp.sum((aff >= pltpu.bitcast(cand_bits, F32)).astype(F32), axis=-1, keepdims=True)

    def search(step, ans):
        cand = ans | (jnp.int32(1) << (30 - step))
        return jnp.where(count_ge(cand) >= capf, cand, ans)

    thr = pltpu.bitcast(lax.fori_loop(0, 31, search, jnp.zeros((aff.shape[0], 1), jnp.int32)), F32)
    gt = aff > thr
    eq = aff == thr
    need = capf - jnp.sum(gt.astype(F32), axis=-1, keepdims=True)
    sel = gt | (eq & (_lane_cumsum_exclusive(eq.astype(F32)) < need))
    self = sel.astype(F32)
    slot = jnp.where(sel, _lane_cumsum_exclusive(self), float(NOT_SELECTED))
    slot_ref[...] = slot.astype(jnp.int32)
    pad = jnp.full((LANES - N_EXPERTS, slot.shape[1]), float(NOT_SELECTED), F32)
    for bi in range(slot_t_ref.shape[0]):
        mine = slot[bi * N_EXPERTS:(bi + 1) * N_EXPERTS]
        slot_t_ref[bi] = jnp.concatenate([mine, pad], axis=0).T.astype(jnp.int32)
    lane = lax.broadcasted_iota(jnp.int32, (slot.shape[0], LANES), 1)
    off = jnp.zeros((slot.shape[0], LANES), F32)
    run = jnp.zeros((slot.shape[0], 1), F32)
    for i in range(1, slot.shape[1] // ROUTE_TILE + 1):
        run = run + jnp.sum(self[:, (i - 1) * ROUTE_TILE:i * ROUTE_TILE], axis=-1, keepdims=True)
        off = jnp.where(lane == i, run, off)
    off_ref[...] = off.astype(jnp.int32)


def _route(aff_t, cap):
    b, e, s = aff_t.shape
    assert e == N_EXPERTS and s % ROUTE_TILE == 0 and s // ROUTE_TILE < OFF_STRIDE
    slot, slot_t, off = pl.pallas_call(
        functools.partial(_route_kernel, cap=cap),
        out_shape=[
            jax.ShapeDtypeStruct((b * e, s), jnp.int32),
            jax.ShapeDtypeStruct((b, s, LANES), jnp.int32),
            jax.ShapeDtypeStruct((b * e, LANES), jnp.int32),
        ],
        compiler_params=pltpu.CompilerParams(vmem_limit_bytes=VMEM_LIMIT),
        name="route",
    )(aff_t.reshape(b * e, s))
    return slot.reshape(b, e, s), slot_t, off


def _window_starts(off_ref, bi, ti, cap):
    starts = []
    ok = None
    for e in range(N_EXPERTS):
        base = (bi * N_EXPERTS + e) * OFF_STRIDE + ti
        lo = off_ref[base]
        hi = off_ref[base + 1]
        st = jnp.minimum(jnp.bitwise_and(lo, -16), cap - ROUTE_WIN)
        fits = hi <= st + ROUTE_WIN
        ok = fits if ok is None else jnp.logical_and(ok, fits)
        starts.append(pl.multiple_of(st, 16))
    return starts, ok


def _dispatch_kernel(off_ref, slot_ref, aff_ref, h_ref, xe_ref, gate_ref, *, cap):
    bi = pl.program_id(0)
    ti = pl.program_id(1)

    @pl.when(ti == 0)
    def _():
        xe_ref[...] = jnp.zeros_like(xe_ref)
        gate_ref[...] = jnp.zeros_like(gate_ref)

    nsub = h_ref.shape[1] // ROUTE_TILE
    col_groups = [slice(sub * ROUTE_TILE, (sub + 1) * ROUTE_TILE) for sub in range(nsub)]
    geo = [_window_starts(off_ref, bi, ti * nsub + sub, cap) for sub in range(nsub)]
    ok = functools.reduce(jnp.logical_and, [fits for _, fits in geo])

    @pl.when(ok)
    def _():
        row = lax.broadcasted_iota(jnp.int32, (ROUTE_WIN, ROUTE_TILE), 0)
        for cols, (starts, _) in zip(col_groups, geo):
            slot = slot_ref[0, :, cols]
            aff = aff_ref[0, :, cols]
            hits = [row == (slot[e:e + 1, :] - starts[e]) for e in range(N_EXPERTS)]
            onehot = jnp.concatenate([jnp.where(hh, 1.0, 0.0).astype(BF16) for hh in hits], axis=0)
            res = _dot(onehot, h_ref[0, cols, :])
            for e, hh in enumerate(hits):
                win = pl.ds(starts[e], ROUTE_WIN)
                xe_ref[e, win, :] += res[e * ROUTE_WIN:(e + 1) * ROUTE_WIN].astype(BF16)
                gate_ref[e, win, :] += jnp.sum(jnp.where(hh, aff[e:e + 1, :], 0.0), axis=-1, keepdims=True)

    @pl.when(jnp.logical_not(ok))
    def _():
        row = lax.broadcasted_iota(jnp.int32, (cap, ROUTE_TILE), 0)
        for cols in col_groups:
            slot = slot_ref[0, :, cols]
            aff = aff_ref[0, :, cols]
            h = h_ref[0, cols, :]
            for e in range(N_EXPERTS):
                hh = row == slot[e:e + 1, :]
                xe_ref[e] += _dot(jnp.where(hh, 1.0, 0.0).astype(BF16), h).astype(BF16)
                gate_ref[e] += jnp.sum(jnp.where(hh, aff[e:e + 1, :], 0.0), axis=-1, keepdims=True)


def _dispatch(off, slot, aff_t, h, cap, tm=512):
    b, e, s = slot.shape
    assert tm % ROUTE_TILE == 0
    rows = pl.BlockSpec((1, e, tm), lambda bi, ti, off_ref: (bi, 0, ti))
    return pl.pallas_call(
        functools.partial(_dispatch_kernel, cap=cap),
        grid_spec=pltpu.PrefetchScalarGridSpec(
            num_scalar_prefetch=1,
            grid=(b, s // tm),
            in_specs=[rows, rows,
                      pl.BlockSpec((1, tm, D_MODEL), lambda bi, ti, off_ref: (bi, ti, 0))],
            out_specs=[
                pl.BlockSpec((e, cap, D_MODEL), lambda bi, ti, off_ref: (0, bi, 0)),
                pl.BlockSpec((e, cap, 1), lambda bi, ti, off_ref: (0, bi, 0)),
            ],
        ),
        out_shape=[
            jax.ShapeDtypeStruct((e, b * cap, D_MODEL), BF16),
            jax.ShapeDtypeStruct((e, b * cap, 1), F32),
        ],
        compiler_params=_params("arbitrary", "arbitrary"),
        name="dispatch",
    )(off, slot, aff_t, h)


def _experts_kernel(xe_ref, gate_ref, wg_ref, wu_ref, wd_ref, y_ref, acc_ref):
    f = pl.program_id(1)

    def hidden_chunk(first):
        wg = wg_ref[0].astype(BF16)
        wu = wu_ref[0].astype(BF16)
        wd = wd_ref[0].astype(BF16)
        for mb in range(xe_ref.shape[1] // EXPERT_ROWS):
            rows = slice(mb * EXPERT_ROWS, (mb + 1) * EXPERT_ROWS)
            xe = xe_ref[0, rows, :]
            a = _dot(xe, wg)
            b = _dot(xe, wu)
            part = _dot((a * jax.nn.sigmoid(a) * b).astype(BF16), wd)
            acc_ref[rows, :] = part if first else acc_ref[rows, :] + part

    pl.when(f == 0)(functools.partial(hidden_chunk, True))
    pl.when(f > 0)(functools.partial(hidden_chunk, False))

    @pl.when(f == pl.num_programs(1) - 1)
    def _():
        y_ref[0] = (acc_ref[...] * gate_ref[0]).astype(y_ref.dtype)


def _experts(xe, gate, wg, wu, wd, tf=256):
    e, m, _ = xe.shape
    nf = D_EXPERT // tf
    return pl.pallas_call(
        _experts_kernel,
        grid=(e, nf),
        in_specs=[
            pl.BlockSpec((1, m, D_MODEL), lambda ei, f: (ei, 0, 0)),
            pl.BlockSpec((1, m, 1), lambda ei, f: (ei, 0, 0)),
            pl.BlockSpec((1, D_MODEL, tf), lambda ei, f: (ei, 0, f)),
            pl.BlockSpec((1, D_MODEL, tf), lambda ei, f: (ei, 0, f)),
            pl.BlockSpec((1, tf, D_MODEL), lambda ei, f: (ei, f, 0)),
        ],
        out_specs=pl.BlockSpec((1, m, D_MODEL), lambda ei, f: (ei, 0, 0)),
        out_shape=jax.ShapeDtypeStruct((e, m, D_MODEL), BF16),
        scratch_shapes=[pltpu.VMEM((m, D_MODEL), F32)],
        compiler_params=_params("arbitrary", "arbitrary"),
        name="experts",
    )(xe, gate, wg, wu, wd)


def _combine_kernel(off_ref, x1_ref, st_ref, y_ref, p_ref, gn_ref, wgbf_ref, wpbf_ref, gp_ref,
                    o_ref, *, cap):
    bi = pl.program_id(0)
    nsub = x1_ref.shape[1] // ROUTE_TILE
    row_groups = [slice(sub * ROUTE_TILE, (sub + 1) * ROUTE_TILE) for sub in range(nsub)]
    geo = [_window_starts(off_ref, bi, pl.program_id(1) * nsub + sub, cap) for sub in range(nsub)]
    ok = functools.reduce(jnp.logical_and, [fits for _, fits in geo])

    def token_slots(rows):
        return st_ref[0, rows, :]

    def windowed_scatter(rows, starts):
        st = token_slots(rows)
        lane = lax.broadcasted_iota(jnp.int32, (ROUTE_TILE, LANES), 1)
        low = lane < ROUTE_WIN
        total = None
        for g in range(N_EXPERTS // ROUTE_GROUP):
            halves = []
            wins = []
            for half in range(ROUTE_GROUP // 2):
                e0 = g * ROUTE_GROUP + 2 * half
                t0 = st[:, e0:e0 + 1] - starts[e0]
                t1 = st[:, e0 + 1:e0 + 2] + (ROUTE_WIN - starts[e0 + 1])
                halves.append(jnp.where(lane == jnp.where(low, t0, t1), 1.0, 0.0).astype(BF16))
                wins.append(y_ref[e0, pl.ds(starts[e0], ROUTE_WIN), :])
                wins.append(y_ref[e0 + 1, pl.ds(starts[e0 + 1], ROUTE_WIN), :])
            part = _dot(jnp.concatenate(halves, axis=1), jnp.concatenate(wins, axis=0))
            total = part if total is None else total + part
        return total

    def dense_scatter(rows):
        st = token_slots(rows)
        lane = lax.broadcasted_iota(jnp.int32, (ROUTE_TILE, cap), 1)
        total = None
        for e in range(N_EXPERTS):
            onehot = jnp.where(lane == st[:, e:e + 1], 1.0, 0.0).astype(BF16)
            part = _dot(onehot, y_ref[e])
            total = part if total is None else total + part
        return total

    def gate_stage(rows, ffn):
        x2 = x1_ref[0, rows, :] + ffn
        return x2, _dot(_rms(x2, gn_ref[...]).astype(BF16), wgbf_ref[...])

    def output_stage(rows, x2, gate_logits):
        emb = _rms(_dot(p_ref[0, rows, :].astype(BF16), wpbf_ref[...]), gp_ref[...])
        o_ref[0, rows, :] = x2 + jax.nn.sigmoid(gate_logits) * emb

    def pipeline(scatter_stage):
        per = PLE_ROWS // ROUTE_TILE
        groups = [slice(g * PLE_ROWS, (g + 1) * PLE_ROWS) for g in range(nsub // per)]
        ffn, gated = {}, {}
        for step in range(len(groups) + 2):
            if step < len(groups):
                ffn[step] = jnp.concatenate([scatter_stage(step * per + j) for j in range(per)], axis=0)
            if 0 <= step - 1 < len(groups):
                gated[step - 1] = gate_stage(groups[step - 1], ffn.pop(step - 1))
            if 0 <= step - 2 < len(groups):
                output_stage(groups[step - 2], *gated.pop(step - 2))

    pl.when(ok)(lambda: pipeline(lambda g: windowed_scatter(row_groups[g], geo[g][0])))
    pl.when(jnp.logical_not(ok))(lambda: pipeline(lambda g: dense_scatter(row_groups[g])))


def _combine(off, x1, slot_t, y, p, gn, wg, wp, gp, cap, tm=512):
    b, s, _ = x1.shape
    assert tm % PLE_ROWS == 0 and PLE_ROWS % ROUTE_TILE == 0
    tile = lambda w: pl.BlockSpec((1, tm, w), lambda bi, i, off_ref: (bi, i, 0))
    fixed2 = lambda shape: pl.BlockSpec(shape, lambda bi, i, off_ref: (0, 0))
    return pl.pallas_call(
        functools.partial(_combine_kernel, cap=cap),
        grid_spec=pltpu.PrefetchScalarGridSpec(
            num_scalar_prefetch=1,
            grid=(b, s // tm),
            in_specs=[
                tile(D_MODEL),
                tile(LANES),
                pl.BlockSpec((N_EXPERTS, cap, D_MODEL), lambda bi, i, off_ref: (0, bi, 0)),
                tile(PLE_DIM),
                fixed2((1, D_MODEL)),
                pl.BlockSpec(wg.shape, lambda bi, i, off_ref: (0, 0), pipeline_mode=pl.Buffered(1)),
                pl.BlockSpec(wp.shape, lambda bi, i, off_ref: (0, 0), pipeline_mode=pl.Buffered(1)),
                fixed2((1, D_MODEL)),
            ],
            out_specs=tile(D_MODEL),
        ),
        out_shape=jax.ShapeDtypeStruct((b, s, D_MODEL), F32),
        compiler_params=_params("arbitrary", "arbitrary"),
        name="combine",
    )(off, x1, slot_t, y, p, gn, wg.astype(BF16), wp.astype(BF16), gp)


def kernel(x, p, norm_mix, w_in, w_pool, pool_scale, q_norm, k_norm, rpb, w_out, norm_ffn, w_router, w_gate, w_up, w_down, norm_ple, w_ple_gate, w_ple_proj, norm_ple_post):
    b, s, d = x.shape
    depth = w_in.shape[0]
    cap = EC_CAPACITY * s // N_EXPERTS
    row = lambda a: a.reshape(1, -1)
    for i in range(depth):
        u, q, k, v = _in_proj(x.reshape(b * s, d), row(norm_mix[i]), w_in[i],
                              row(jnp.tile(q_norm[i], ATTN_HEADS)), row(jnp.tile(k_norm[i], ATTN_HEADS)))
        shp = lambda a: a.reshape(b, s, -1)
        y_attn = _natten(shp(q), shp(k), shp(v), _attn_bias_table(rpb[i]))
        x1, h, aff_t = _mix(x, shp(u), y_attn, w_pool[i], row(pool_scale[i]), w_out[i],
                            row(norm_ffn[i]), w_router[i].T)
        slot, slot_t, off = _route(aff_t, cap)
        off = off[:, :OFF_STRIDE].reshape(-1)
        xe, gate = _dispatch(off, slot, aff_t, h, cap)
        y = _experts(xe, gate, w_gate[i], w_up[i], w_down[i])
        x = _combine(off, x1, slot_t, y, p[i], row(norm_ple[i]), w_ple_gate[i], w_ple_proj[i],
                     row(norm_ple_post[i]), cap)
    return x
```

```python
import functools

import jax
import jax.numpy as jnp
from jax import lax
from jax.experimental import pallas as pl
from jax.experimental.pallas import tpu as pltpu

D_MODEL = 1024
GRID_W = 64
POOL_WINDOWS = (2, 4, 8, 16)
POOL_WIDTH = D_MODEL // 2
POOL_GROUP = POOL_WIDTH // len(POOL_WINDOWS)
ATTN_HEADS = 8
HEAD_DIM = (D_MODEL // 2) // ATTN_HEADS
ATTN_WIDTH = ATTN_HEADS * HEAD_DIM
WIN_ROWS_MAX = 8
WIN_COLS = 16
N_EXPERTS = 16
EC_CAPACITY = 2
D_EXPERT = 2 * D_MODEL
PLE_DIM = 256
RMS_EPS = 1e-6

LANES = 128
POOL_HALO = 8
LOG2E = 1.4426950408889634
MASK_BIAS = -1e30
NATTEN_UNROLL = 8
ROUTE_TILE = 256
ROUTE_WIN = 64
ROUTE_GROUP = 4
OFF_STRIDE = 16
NOT_SELECTED = -(1 << 20)
assert 2 * ROUTE_WIN == LANES and ROUTE_GROUP % 2 == 0
PLE_ROWS = 256
MIX_ROWS = 512
PROJ_ROWS = 512
EXPERT_ROWS = 1024
VMEM_LIMIT = 56 * 1024 * 1024

BF16 = jnp.bfloat16
F32 = jnp.float32


def _params(*sem):
    return pltpu.CompilerParams(dimension_semantics=sem, vmem_limit_bytes=VMEM_LIMIT)


def _rms(x, g):
    return x * lax.rsqrt(jnp.mean(x * x, axis=-1, keepdims=True) + RMS_EPS) * g


def _dot(a, b):
    return jnp.dot(a, b, preferred_element_type=F32)


def _dot_nt(a, b):
    return lax.dot_general(a, b, (((1,), (1,)), ((), ())), preferred_element_type=F32)


def _in_proj_kernel(x_ref, g_ref, w_ref, qg_ref, kg_ref,
                    u_ref, q_ref, k_ref, v_ref, wbf_ref):
    @pl.when(pl.program_id(0) == 0)
    def _():
        wbf_ref[...] = w_ref[...].astype(BF16)

    def head_norm(t, g):
        low = lax.broadcasted_iota(jnp.int32, (t.shape[0], LANES), 1) < HEAD_DIM
        out = []
        for j in range(t.shape[1] // LANES):
            tile = t[:, j * LANES:(j + 1) * LANES]
            sq = tile * tile
            sa = jnp.sum(jnp.where(low, sq, 0.0), axis=-1, keepdims=True)
            sb = jnp.sum(sq, axis=-1, keepdims=True) - sa
            ra = lax.rsqrt(sa * (1.0 / HEAD_DIM) + RMS_EPS)
            rb = lax.rsqrt(sb * (1.0 / HEAD_DIM) + RMS_EPS)
            out.append(tile * jnp.where(low, ra, rb) * g[:, j * LANES:(j + 1) * LANES])
        return jnp.concatenate(out, axis=-1)

    def norm_stage(rows):
        return _rms(x_ref[rows, :], g_ref[...]).astype(BF16)

    def proj_stage(h):
        return _dot(h, wbf_ref[...])

    def head_stage(rows, z):
        u_ref[rows, :] = z[:, :POOL_WIDTH]
        q = z[:, POOL_WIDTH:POOL_WIDTH + ATTN_WIDTH]
        k = z[:, POOL_WIDTH + ATTN_WIDTH:POOL_WIDTH + 2 * ATTN_WIDTH]
        v = z[:, POOL_WIDTH + 2 * ATTN_WIDTH:]
        q_ref[rows, :] = (head_norm(q, qg_ref[...]) * (HEAD_DIM ** -0.5 * LOG2E)).astype(BF16)
        k_ref[rows, :] = head_norm(k, kg_ref[...]).astype(BF16)
        v_ref[rows, :] = v.astype(BF16)

    groups = [slice(sub * PROJ_ROWS, (sub + 1) * PROJ_ROWS) for sub in range(x_ref.shape[0] // PROJ_ROWS)]
    zs = [proj_stage(norm_stage(rows)) for rows in groups]
    for rows, z in zip(groups, zs):
        head_stage(rows, z)


def _in_proj(x2, g, w, qg, kg, tm=1024):
    n = x2.shape[0]
    zw = w.shape[1]
    assert 2 * HEAD_DIM == LANES
    row = lambda i: (i, 0)
    fixed = lambda i: (0, 0)
    return pl.pallas_call(
        _in_proj_kernel,
        grid=(n // tm,),
        in_specs=[
            pl.BlockSpec((tm, D_MODEL), row),
            pl.BlockSpec((1, D_MODEL), fixed),
            pl.BlockSpec((D_MODEL, zw), fixed, pipeline_mode=pl.Buffered(1)),
            pl.BlockSpec((1, ATTN_WIDTH), fixed),
            pl.BlockSpec((1, ATTN_WIDTH), fixed),
        ],
        out_specs=[
            pl.BlockSpec((tm, POOL_WIDTH), row),
            pl.BlockSpec((tm, ATTN_WIDTH), row),
            pl.BlockSpec((tm, ATTN_WIDTH), row),
            pl.BlockSpec((tm, ATTN_WIDTH), row),
        ],
        out_shape=[
            jax.ShapeDtypeStruct((n, POOL_WIDTH), F32),
            jax.ShapeDtypeStruct((n, ATTN_WIDTH), BF16),
            jax.ShapeDtypeStruct((n, ATTN_WIDTH), BF16),
            jax.ShapeDtypeStruct((n, ATTN_WIDTH), BF16),
        ],
        scratch_shapes=[pltpu.VMEM((D_MODEL, zw), BF16)],
        compiler_params=_params("arbitrary"),
        name="in_proj",
    )(x2, g, w, qg, kg)


def _natten_kernel(q_ref, k_ref, v_ref, tbl_ref, o_ref, bias_ref, *, rows, kh):
    band = kh * GRID_W
    lane = lax.broadcasted_iota(jnp.int32, (GRID_W, LANES), 1)
    first = lane < HEAD_DIM

    @pl.when(pl.program_id(1) == 0)
    def _():
        for hh in range(2):
            for d0 in range(WIN_ROWS_MAX):
                for kk in range(kh // 2):
                    bias_ref[d0, hh * GRID_W:(hh + 1) * GRID_W, kk * LANES:(kk + 1) * LANES] = jnp.where(
                        lane < GRID_W, tbl_ref[hh, d0 + 2 * kk], tbl_ref[hh, d0 + 2 * kk + 1])

    def geometry(g):
        geo = []
        for r in range(g * NATTEN_UNROLL, (g + 1) * NATTEN_UNROLL):
            rs = min(max(r - kh // 2, 0), rows - kh)
            geo.append((rs - r + (WIN_ROWS_MAX - 1), r * GRID_W, rs * GRID_W))
        return geo

    def score_stage(g):
        scores = []
        for d0, q0, k0 in geometry(g):
            qr = q_ref[0, q0:q0 + GRID_W, :]
            zero = jnp.zeros_like(qr)
            q2 = jnp.concatenate([jnp.where(first, qr, zero), jnp.where(first, zero, qr)], axis=0)
            s = _dot_nt(q2, k_ref[0, k0:k0 + band, :]) + bias_ref[d0]
            scores.append((s, jnp.max(s, axis=-1, keepdims=True)))
        return scores

    def softmax_stage(scores):
        probs = []
        for s, m in scores:
            e = jnp.exp2(s - m)
            probs.append((e.astype(BF16), jnp.sum(e, axis=-1, keepdims=True)))
        return probs

    def value_stage(g, probs):
        for (d0, q0, k0), (e, l) in zip(geometry(g), probs):
            o = _dot(e, v_ref[0, k0:k0 + band, :]) / l
            o_ref[0, q0:q0 + GRID_W, :] = jnp.where(first, o[:GRID_W], o[GRID_W:]).astype(o_ref.dtype)

    ngroups = rows // NATTEN_UNROLL
    scores, probs = {}, {}
    for step in range(ngroups + 2):
        if step < ngroups:
            scores[step] = score_stage(step)
        if 0 <= step - 1 < ngroups:
            probs[step - 1] = softmax_stage(scores.pop(step - 1))
        if 0 <= step - 2 < ngroups:
            value_stage(step - 2, probs.pop(step - 2))


def _natten(q, k, v, bias):
    b, s, _ = q.shape
    rows = s // GRID_W
    kh = min(WIN_ROWS_MAX, rows)
    assert rows % NATTEN_UNROLL == 0 and kh % 2 == 0 and 2 * GRID_W == LANES
    pairs = ATTN_WIDTH // LANES
    blk = pl.BlockSpec((1, s, LANES), lambda p, bi: (bi, 0, p))
    return pl.pallas_call(
        functools.partial(_natten_kernel, rows=rows, kh=kh),
        grid=(pairs, b),
        in_specs=[blk, blk, blk,
                  pl.BlockSpec((2,) + bias.shape[1:], lambda p, bi: (p, 0, 0, 0))],
        out_specs=blk,
        out_shape=jax.ShapeDtypeStruct((b, s, ATTN_WIDTH), BF16),
        scratch_shapes=[pltpu.VMEM((WIN_ROWS_MAX, 2 * GRID_W, kh * GRID_W), F32)],
        compiler_params=_params("arbitrary", "arbitrary"),
        name="natten",
    )(q, k, v, bias)


def _attn_bias_table(rpb):
    c = jnp.arange(GRID_W)
    cs = jnp.clip(c - WIN_COLS // 2, 0, GRID_W - WIN_COLS)
    j = jnp.arange(GRID_W)
    valid = (j[None, :] >= cs[:, None]) & (j[None, :] < cs[:, None] + WIN_COLS)
    dc = j[None, :] - c[:, None] + (WIN_COLS - 1)
    pick = ((dc[None] == jnp.arange(2 * WIN_COLS - 1)[:, None, None]) & valid[None]).astype(F32)
    t = jnp.einsum('hrd,dcj->hrcj', rpb.astype(F32), pick, precision=lax.Precision.HIGHEST)
    t = jnp.where(valid, t * LOG2E, MASK_BIAS)
    return jnp.concatenate([t, t], axis=-1)


def _window_sum(upad, w, tm):
    n = upad.shape[0]
    fwd = upad
    span = 1
    while span < min(w, POOL_HALO):
        fwd = fwd + pltpu.roll(fwd, n - span, axis=0)
        span *= 2
    centre = slice(POOL_HALO, POOL_HALO + tm)
    if w == 2 * POOL_HALO:
        return fwd[0:tm] + fwd[centre]
    return pltpu.roll(fwd, w // 2, axis=0)[centre]


def _mix_kernel(x_ref, u_ref, up_ref, un_ref, ya_ref, ic_ref, wp_ref, ps_ref, wo_ref, g_ref, wr_ref,
                x1_ref, h_ref, aff_ref, upad_ref, wobf_ref, wpbf_ref, *, tm):
    i = pl.program_id(1)

    @pl.when((pl.program_id(0) == 0) & (i == 0))
    def _():
        wobf_ref[...] = wo_ref[...].astype(BF16)
        wpbf_ref[...] = wp_ref[...].astype(BF16)

    upad_ref[0:POOL_HALO, :] = jnp.where(i > 0, up_ref[0], 0.0)
    upad_ref[POOL_HALO:POOL_HALO + tm, :] = u_ref[0]
    upad_ref[POOL_HALO + tm:, :] = jnp.where(i < pl.num_programs(1) - 1, un_ref[0], 0.0)

    ngroups = tm // MIX_ROWS
    group_cols = [slice(gi * POOL_GROUP, (gi + 1) * POOL_GROUP) for gi in range(len(POOL_WINDOWS))]

    def pool_stage(g):
        ds = []
        for cols, w in zip(group_cols, POOL_WINDOWS):
            upad = upad_ref[g * MIX_ROWS:(g + 1) * MIX_ROWS + 2 * POOL_HALO, cols]
            edge = jnp.full((POOL_HALO, POOL_GROUP), 1.0 / w, F32)
            head = jnp.where(i == 0, ic_ref[0:POOL_HALO, cols], 1.0 / w) if g == 0 else edge
            tail = (jnp.where(i == pl.num_programs(1) - 1, ic_ref[POOL_HALO:, cols], 1.0 / w)
                    if g == ngroups - 1 else edge)
            inv = jnp.concatenate(
                [head, jnp.full((MIX_ROWS - 2 * POOL_HALO, POOL_GROUP), 1.0 / w, F32), tail], axis=0)
            d = _window_sum(upad, w, MIX_ROWS) * inv - upad[POOL_HALO:POOL_HALO + MIX_ROWS]
            ds.append(d.astype(BF16))
        return ds

    def proj_stage(g, ds):
        rows = slice(g * MIX_ROWS, (g + 1) * MIX_ROWS)
        ypool = [_dot(d, wpbf_ref[gi]) * ps_ref[:, cols] for gi, (d, cols) in enumerate(zip(ds, group_cols))]
        ypool = jnp.concatenate(ypool, axis=-1).astype(BF16)
        return _dot(ypool, wobf_ref[:POOL_WIDTH, :]) + _dot(ya_ref[0, rows, :], wobf_ref[POOL_WIDTH:, :])

    def route_stage(g, mix):
        rows = slice(g * MIX_ROWS, (g + 1) * MIX_ROWS)
        x1 = x_ref[0, rows, :] + mix
        x1_ref[0, rows, :] = x1
        h = _rms(x1, g_ref[...]).astype(BF16)
        h_ref[0, rows, :] = h
        logits = _dot_nt(wr_ref[...].astype(BF16), h)
        m = jnp.max(logits, axis=0, keepdims=True)
        e = jnp.exp(logits - m)
        aff_ref[0, :, rows] = e / jnp.sum(e, axis=0, keepdims=True)

    pooled, mixed = {}, {}
    for step in range(ngroups + 2):
        if step < ngroups:
            pooled[step] = pool_stage(step)
        if 0 <= step - 1 < ngroups:
            mixed[step - 1] = proj_stage(step - 1, pooled.pop(step - 1))
        if 0 <= step - 2 < ngroups:
            route_stage(step - 2, mixed.pop(step - 2))


def _pool_edge_inverse_counts(s):
    t = jnp.concatenate([jnp.arange(POOL_HALO), jnp.arange(s - POOL_HALO, s)])
    cols = []
    for w in POOL_WINDOWS:
        lo = jnp.clip(t - w // 2, 0, s - 1)
        hi = jnp.clip(t + (w - w // 2) - 1, 0, s - 1)
        inv = 1.0 / (hi - lo + 1).astype(F32)
        cols.append(jnp.broadcast_to(inv[:, None], (2 * POOL_HALO, POOL_GROUP)))
    return jnp.concatenate(cols, axis=-1)


def _mix(x, u, ya, wp, ps, wo, g, wr_t, tm=1024):
    b, s, _ = x.shape
    nt = s // tm
    hb = tm // POOL_HALO
    tile = lambda w: pl.BlockSpec((1, tm, w), lambda bi, i: (bi, i, 0))
    fixed2 = lambda shape: pl.BlockSpec(shape, lambda bi, i: (0, 0))
    return pl.pallas_call(
        functools.partial(_mix_kernel, tm=tm),
        grid=(b, nt),
        in_specs=[
            tile(D_MODEL),
            tile(POOL_WIDTH),
            pl.BlockSpec((1, POOL_HALO, POOL_WIDTH), lambda bi, i: (bi, jnp.maximum(i * hb - 1, 0), 0)),
            pl.BlockSpec((1, POOL_HALO, POOL_WIDTH),
                         lambda bi, i: (bi, jnp.minimum((i + 1) * hb, s // POOL_HALO - 1), 0)),
            tile(ATTN_WIDTH),
            fixed2((2 * POOL_HALO, POOL_WIDTH)),
            pl.BlockSpec(wp.shape, lambda bi, i: (0, 0, 0)),
            fixed2((1, POOL_WIDTH)),
            pl.BlockSpec(wo.shape, lambda bi, i: (0, 0), pipeline_mode=pl.Buffered(1)),
            fixed2((1, D_MODEL)),
            fixed2(wr_t.shape),
        ],
        out_specs=[
            tile(D_MODEL),
            tile(D_MODEL),
            pl.BlockSpec((1, N_EXPERTS, tm), lambda bi, i: (bi, 0, i)),
        ],
        out_shape=[
            jax.ShapeDtypeStruct((b, s, D_MODEL), F32),
            jax.ShapeDtypeStruct((b, s, D_MODEL), BF16),
            jax.ShapeDtypeStruct((b, N_EXPERTS, s), F32),
        ],
        scratch_shapes=[
            pltpu.VMEM((tm + 2 * POOL_HALO, POOL_WIDTH), F32),
            pltpu.VMEM(wo.shape, BF16),
            pltpu.VMEM(wp.shape, BF16),
        ],
        compiler_params=_params("arbitrary", "arbitrary"),
        name="mix",
    )(x, u, u, u, ya, _pool_edge_inverse_counts(s), wp, ps, wo, g, wr_t)


def _lane_cumsum_exclusive(m):
    e, s = m.shape
    r = lax.broadcasted_iota(jnp.int32, (LANES, LANES), 0)
    c = lax.broadcasted_iota(jnp.int32, (LANES, LANES), 1)
    upper = (r < c).astype(BF16)
    carry = jnp.zeros((e, 1), F32)
    out = []
    for blk in range(s // LANES):
        piece = m[:, blk * LANES:(blk + 1) * LANES]
        out.append(_dot(piece.astype(BF16), upper) + carry)
        carry = carry + jnp.sum(piece, axis=-1, keepdims=True)
    return jnp.concatenate(out, axis=-1)


def _route_kernel(aff_ref, slot_ref, slot_t_ref, off_ref, *, cap):
    aff = aff_ref[...]
    capf = jnp.float32(cap)

    def count_ge(cand_bits):
        return jnp.sum((aff >= pltpu.bitcast(cand_bits, F32)).astype(F32), axis=-1, keepdims=True)

    def search(step, ans):
        cand = ans | (jnp.int32(1) << (30 - step))
        return jnp.where(count_ge(cand) >= capf, cand, ans)

    thr = pltpu.bitcast(lax.fori_loop(0, 31, search, jnp.zeros((aff.shape[0], 1), jnp.int32)), F32)
    gt = aff > thr
    eq = aff == thr
    need = capf - jnp.sum(gt.astype(F32), axis=-1, keepdims=True)
    sel = gt | (eq & (_lane_cumsum_exclusive(eq.astype(F32)) < need))
    self = sel.astype(F32)
    slot = jnp.where(sel, _lane_cumsum_exclusive(self), float(NOT_SELECTED))
    slot_ref[...] = slot.astype(jnp.int32)
    pad = jnp.full((LANES - N_EXPERTS, slot.shape[1]), float(NOT_SELECTED), F32)
    for bi in range(slot_t_ref.shape[0]):
        mine = slot[bi * N_EXPERTS:(bi + 1) * N_EXPERTS]
        slot_t_ref[bi] = jnp.concatenate([mine, pad], axis=0).T.astype(jnp.int32)
    lane = lax.broadcasted_iota(jnp.int32, (slot.shape[0], LANES), 1)
    off = jnp.zeros((slot.shape[0], LANES), F32)
    run = jnp.zeros((slot.shape[0], 1), F32)
    for i in range(1, slot.shape[1] // ROUTE_TILE + 1):
        run = run + jnp.sum(self[:, (i - 1) * ROUTE_TILE:i * ROUTE_TILE], axis=-1, keepdims=True)
        off = jnp.where(lane == i, run, off)
    off_ref[...] = off.astype(jnp.int32)


def _route(aff_t, cap):
    b, e, s = aff_t.shape
    assert e == N_EXPERTS and s % ROUTE_TILE == 0 and s // ROUTE_TILE < OFF_STRIDE
    slot, slot_t, off = pl.pallas_call(
        functools.partial(_route_kernel, cap=cap),
        out_shape=[
            jax.ShapeDtypeStruct((b * e, s), jnp.int32),
            jax.ShapeDtypeStruct((b, s, LANES), jnp.int32),
            jax.ShapeDtypeStruct((b * e, LANES), jnp.int32),
        ],
        compiler_params=pltpu.CompilerParams(vmem_limit_bytes=VMEM_LIMIT),
        name="route",
    )(aff_t.reshape(b * e, s))
    return slot.reshape(b, e, s), slot_t, off


def _window_starts(off_ref, bi, ti, cap):
    starts = []
    ok = None
    for e in range(N_EXPERTS):
        base = (bi * N_EXPERTS + e) * OFF_STRIDE + ti
        lo = off_ref[base]
        hi = off_ref[base + 1]
        st = jnp.minimum(jnp.bitwise_and(lo, -16), cap - ROUTE_WIN)
        fits = hi <= st + ROUTE_WIN
        ok = fits if ok is None else jnp.logical_and(ok, fits)
        starts.append(pl.multiple_of(st, 16))
    return starts, ok


def _dispatch_kernel(off_ref, slot_ref, aff_ref, h_ref, xe_ref, gate_ref, *, cap):
    bi = pl.program_id(0)
    ti = pl.program_id(1)

    @pl.when(ti == 0)
    def _():
        xe_ref[...] = jnp.zeros_like(xe_ref)
        gate_ref[...] = jnp.zeros_like(gate_ref)

    nsub = h_ref.shape[1] // ROUTE_TILE
    col_groups = [slice(sub * ROUTE_TILE, (sub + 1) * ROUTE_TILE) for sub in range(nsub)]
    geo = [_window_starts(off_ref, bi, ti * nsub + sub, cap) for sub in range(nsub)]
    ok = functools.reduce(jnp.logical_and, [fits for _, fits in geo])

    @pl.when(ok)
    def _():
        row = lax.broadcasted_iota(jnp.int32, (ROUTE_WIN, ROUTE_TILE), 0)
        for cols, (starts, _) in zip(col_groups, geo):
            slot = slot_ref[0, :, cols]
            aff = aff_ref[0, :, cols]
            hits = [row == (slot[e:e + 1, :] - starts[e]) for e in range(N_EXPERTS)]
            onehot = jnp.concatenate([jnp.where(hh, 1.0, 0.0).astype(BF16) for hh in hits], axis=0)
            res = _dot(onehot, h_ref[0, cols, :])
            for e, hh in enumerate(hits):
                win = pl.ds(starts[e], ROUTE_WIN)
                xe_ref[e, win, :] += res[e * ROUTE_WIN:(e + 1) * ROUTE_WIN].astype(BF16)
                gate_ref[e, win, :] += jnp.sum(jnp.where(hh, aff[e:e + 1, :], 0.0), axis=-1, keepdims=True)

    @pl.when(jnp.logical_not(ok))
    def _():
        row = lax.broadcasted_iota(jnp.int32, (cap, ROUTE_TILE), 0)
        for cols in col_groups:
            slot = slot_ref[0, :, cols]
            aff = aff_ref[0, :, cols]
            h = h_ref[0, cols, :]
            for e in range(N_EXPERTS):
                hh = row == slot[e:e + 1, :]
                xe_ref[e] += _dot(jnp.where(hh, 1.0, 0.0).astype(BF16), h).astype(BF16)
                gate_ref[e] += jnp.sum(jnp.where(hh, aff[e:e + 1, :], 0.0), axis=-1, keepdims=True)


def _dispatch(off, slot, aff_t, h, cap, tm=1024):
    b, e, s = slot.shape
    assert tm % ROUTE_TILE == 0
    rows = pl.BlockSpec((1, e, tm), lambda bi, ti, off_ref: (bi, 0, ti))
    return pl.pallas_call(
        functools.partial(_dispatch_kernel, cap=cap),
        grid_spec=pltpu.PrefetchScalarGridSpec(
            num_scalar_prefetch=1,
            grid=(b, s // tm),
            in_specs=[rows, rows,
                      pl.BlockSpec((1, tm, D_MODEL), lambda bi, ti, off_ref: (bi, ti, 0))],
            out_specs=[
                pl.BlockSpec((e, cap, D_MODEL), lambda bi, ti, off_ref: (0, bi, 0)),
                pl.BlockSpec((e, cap, 1), lambda bi, ti, off_ref: (0, bi, 0)),
            ],
        ),
        out_shape=[
            jax.ShapeDtypeStruct((e, b * cap, D_MODEL), BF16),
            jax.ShapeDtypeStruct((e, b * cap, 1), F32),
        ],
        compiler_params=_params("arbitrary", "arbitrary"),
        name="dispatch",
    )(off, slot, aff_t, h)


def _experts_kernel(xe_ref, gate_ref, wg_ref, wu_ref, wd_ref, y_ref, acc_ref):
    f = pl.program_id(1)

    def hidden_chunk(first, last):
        wg = wg_ref[0].astype(BF16)
        wu = wu_ref[0].astype(BF16)
        wd = wd_ref[0].astype(BF16)
        for mb in range(xe_ref.shape[1] // EXPERT_ROWS):
            rows = slice(mb * EXPERT_ROWS, (mb + 1) * EXPERT_ROWS)
            xe = xe_ref[0, rows, :]
            a = _dot(xe, wg)
            b = _dot(xe, wu)
            part = _dot((a * jax.nn.sigmoid(a) * b).astype(BF16), wd)
            total = part if first else acc_ref[rows, :] + part
            if last:
                y_ref[0, rows, :] = (total * gate_ref[0, rows, :]).astype(y_ref.dtype)
            else:
                acc_ref[rows, :] = total

    nf = pl.num_programs(1)
    pl.when(f == 0)(functools.partial(hidden_chunk, True, False))
    pl.when((f > 0) & (f < nf - 1))(functools.partial(hidden_chunk, False, False))
    pl.when(f == nf - 1)(functools.partial(hidden_chunk, False, True))


def _experts(xe, gate, wg, wu, wd, tf=512):
    e, m, _ = xe.shape
    nf = D_EXPERT // tf
    assert nf >= 2
    return pl.pallas_call(
        _experts_kernel,
        grid=(e, nf),
        in_specs=[
            pl.BlockSpec((1, m, D_MODEL), lambda ei, f: (ei, 0, 0)),
            pl.BlockSpec((1, m, 1), lambda ei, f: (ei, 0, 0)),
            pl.BlockSpec((1, D_MODEL, tf), lambda ei, f: (ei, 0, f)),
            pl.BlockSpec((1, D_MODEL, tf), lambda ei, f: (ei, 0, f)),
            pl.BlockSpec((1, tf, D_MODEL), lambda ei, f: (ei, f, 0)),
        ],
        out_specs=pl.BlockSpec((1, m, D_MODEL), lambda ei, f: (ei, 0, 0)),
        out_shape=jax.ShapeDtypeStruct((e, m, D_MODEL), BF16),
        scratch_shapes=[pltpu.VMEM((m, D_MODEL), F32)],
        compiler_params=_params("arbitrary", "arbitrary"),
        name="experts",
    )(xe, gate, wg, wu, wd)


def _combine_kernel(off_ref, x1_ref, st_ref, y_ref, p_ref, gn_ref, wgbf_ref, wpbf_ref, gp_ref,
                    o_ref, *, cap):
    bi = pl.program_id(0)
    nsub = x1_ref.shape[1] // ROUTE_TILE
    row_groups = [slice(sub * ROUTE_TILE, (sub + 1) * ROUTE_TILE) for sub in range(nsub)]
    geo = [_window_starts(off_ref, bi, pl.program_id(1) * nsub + sub, cap) for sub in range(nsub)]
    ok = functools.reduce(jnp.logical_and, [fits for _, fits in geo])

    def token_slots(rows):
        return st_ref[0, rows, :]

    def windowed_scatter(rows, starts):
        st = token_slots(rows)
        lane = lax.broadcasted_iota(jnp.int32, (ROUTE_TILE, LANES), 1)
        low = lane < ROUTE_WIN
        total = None
        for g in range(N_EXPERTS // ROUTE_GROUP):
            halves = []
            wins = []
            for half in range(ROUTE_GROUP // 2):
                e0 = g * ROUTE_GROUP + 2 * half
                t0 = st[:, e0:e0 + 1] - starts[e0]
                t1 = st[:, e0 + 1:e0 + 2] + (ROUTE_WIN - starts[e0 + 1])
                halves.append(jnp.where(lane == jnp.where(low, t0, t1), 1.0, 0.0).astype(BF16))
                wins.append(y_ref[e0, pl.ds(starts[e0], ROUTE_WIN), :])
                wins.append(y_ref[e0 + 1, pl.ds(starts[e0 + 1], ROUTE_WIN), :])
            part = _dot(jnp.concatenate(halves, axis=1), jnp.concatenate(wins, axis=0))
            total = part if total is None else total + part
        return total

    def dense_scatter(rows):
        st = token_slots(rows)
        lane = lax.broadcasted_iota(jnp.int32, (ROUTE_TILE, cap), 1)
        total = None
        for e in range(N_EXPERTS):
            onehot = jnp.where(lane == st[:, e:e + 1], 1.0, 0.0).astype(BF16)
            part = _dot(onehot, y_ref[e])
            total = part if total is None else total + part
        return total

    def gate_stage(rows, ffn):
        x2 = x1_ref[0, rows, :] + ffn
        return x2, _dot(_rms(x2, gn_ref[...]).astype(BF16), wgbf_ref[...])

    def output_stage(rows, x2, gate_logits):
        emb = _rms(_dot(p_ref[0, rows, :].astype(BF16), wpbf_ref[...]), gp_ref[...])
        o_ref[0, rows, :] = x2 + jax.nn.sigmoid(gate_logits) * emb

    def pipeline(scatter_stage):
        per = PLE_ROWS // ROUTE_TILE
        groups = [slice(g * PLE_ROWS, (g + 1) * PLE_ROWS) for g in range(nsub // per)]
        ffn, gated = {}, {}
        for step in range(len(groups) + 2):
            if step < len(groups):
                ffn[step] = jnp.concatenate([scatter_stage(step * per + j) for j in range(per)], axis=0)
            if 0 <= step - 1 < len(groups):
                gated[step - 1] = gate_stage(groups[step - 1], ffn.pop(step - 1))
            if 0 <= step - 2 < len(groups):
                output_stage(groups[step - 2], *gated.pop(step - 2))

    pl.when(ok)(lambda: pipeline(lambda g: windowed_scatter(row_groups[g], geo[g][0])))
    pl.when(jnp.logical_not(ok))(lambda: pipeline(lambda g: dense_scatter(row_groups[g])))


def _combine(off, x1, slot_t, y, p, gn, wg, wp, gp, cap, tm=1024):
    b, s, _ = x1.shape
    assert tm % PLE_ROWS == 0 and PLE_ROWS % ROUTE_TILE == 0
    tile = lambda w: pl.BlockSpec((1, tm, w), lambda bi, i, off_ref: (bi, i, 0))
    fixed2 = lambda shape: pl.BlockSpec(shape, lambda bi, i, off_ref: (0, 0))
    return pl.pallas_call(
        functools.partial(_combine_kernel, cap=cap),
        grid_spec=pltpu.PrefetchScalarGridSpec(
            num_scalar_prefetch=1,
            grid=(b, s // tm),
            in_specs=[
                tile(D_MODEL),
                tile(LANES),
                pl.BlockSpec((N_EXPERTS, cap, D_MODEL), lambda bi, i, off_ref: (0, bi, 0)),
                tile(PLE_DIM),
                fixed2((1, D_MODEL)),
                pl.BlockSpec(wg.shape, lambda bi, i, off_ref: (0, 0), pipeline_mode=pl.Buffered(1)),
                pl.BlockSpec(wp.shape, lambda bi, i, off_ref: (0, 0), pipeline_mode=pl.Buffered(1)),
                fixed2((1, D_MODEL)),
            ],
            out_specs=tile(D_MODEL),
        ),
        out_shape=jax.ShapeDtypeStruct((b, s, D_MODEL), F32),
        compiler_params=_params("arbitrary", "arbitrary"),
        name="combine",
    )(off, x1, slot_t, y, p, gn, wg.astype(BF16), wp.astype(BF16), gp)


def kernel(x, p, norm_mix, w_in, w_pool, pool_scale, q_norm, k_norm, rpb, w_out, norm_ffn, w_router, w_gate, w_up, w_down, norm_ple, w_ple_gate, w_ple_proj, norm_ple_post):
    b, s, d = x.shape
    depth = w_in.shape[0]
    cap = EC_CAPACITY * s // N_EXPERTS
    row = lambda a: a.reshape(1, -1)
    for i in range(depth):
        u, q, k, v = _in_proj(x.reshape(b * s, d), row(norm_mix[i]), w_in[i],
                              row(jnp.tile(q_norm[i], ATTN_HEADS)), row(jnp.tile(k_norm[i], ATTN_HEADS)))
        shp = lambda a: a.reshape(b, s, -1)
        y_attn = _natten(shp(q), shp(k), shp(v), _attn_bias_table(rpb[i]))
        x1, h, aff_t = _mix(x, shp(u), y_attn, w_pool[i], row(pool_scale[i]), w_out[i],
                            row(norm_ffn[i]), w_router[i].T)
        slot, slot_t, off = _route(aff_t, cap)
        off = off[:, :OFF_STRIDE].reshape(-1)
        xe, gate = _dispatch(off, slot, aff_t, h, cap)
        y = _experts(xe, gate, w_gate[i], w_up[i], w_down[i])
        x = _combine(off, x1, slot_t, y, p[i], row(norm_ple[i]), w_ple_gate[i], w_ple_proj[i],
                     row(norm_ple_post[i]), cap)
    return x
```

```python
import functools

import jax
import jax.numpy as jnp
from jax import lax
from jax.experimental import pallas as pl
from jax.experimental.pallas import tpu as pltpu

D_MODEL = 1024
GRID_W = 64
POOL_WINDOWS = (2, 4, 8, 16)
POOL_WIDTH = D_MODEL // 2
POOL_GROUP = POOL_WIDTH // len(POOL_WINDOWS)
ATTN_HEADS = 8
HEAD_DIM = (D_MODEL // 2) // ATTN_HEADS
ATTN_WIDTH = ATTN_HEADS * HEAD_DIM
WIN_ROWS_MAX = 8
WIN_COLS = 16
N_EXPERTS = 16
EC_CAPACITY = 2
D_EXPERT = 2 * D_MODEL
PLE_DIM = 256
RMS_EPS = 1e-6

LANES = 128
POOL_HALO = 8
LOG2E = 1.4426950408889634
MASK_BIAS = -1e30
NATTEN_UNROLL = 4
ROUTE_TILE = 256
ROUTE_WIN = 64
ROUTE_GROUP = 4
OFF_STRIDE = 16
NOT_SELECTED = -(1 << 20)
assert 2 * ROUTE_WIN == LANES and ROUTE_GROUP % 2 == 0
PLE_ROWS = 256
MIX_ROWS = 512
PROJ_ROWS = 1024
EXPERT_ROWS = 1024
VMEM_LIMIT = 56 * 1024 * 1024

BF16 = jnp.bfloat16
F32 = jnp.float32


def _params(*sem):
    return pltpu.CompilerParams(dimension_semantics=sem, vmem_limit_bytes=VMEM_LIMIT)


def _rms(x, g):
    return x * lax.rsqrt(jnp.mean(x * x, axis=-1, keepdims=True) + RMS_EPS) * g


def _dot(a, b):
    return jnp.dot(a, b, preferred_element_type=F32)


def _dot_nt(a, b):
    return lax.dot_general(a, b, (((1,), (1,)), ((), ())), preferred_element_type=F32)


def _in_proj_kernel(x_ref, g_ref, w_ref, qg_ref, kg_ref,
                    u_ref, q_ref, k_ref, v_ref, wbf_ref):
    @pl.when(pl.program_id(0) == 0)
    def _():
        wbf_ref[...] = w_ref[...].astype(BF16)

    def head_norm(t, g):
        low = lax.broadcasted_iota(jnp.int32, (t.shape[0], LANES), 1) < HEAD_DIM
        out = []
        for j in range(t.shape[1] // LANES):
            tile = t[:, j * LANES:(j + 1) * LANES]
            sq = tile * tile
            sa = jnp.sum(jnp.where(low, sq, 0.0), axis=-1, keepdims=True)
            sb = jnp.sum(sq, axis=-1, keepdims=True) - sa
            ra = lax.rsqrt(sa * (1.0 / HEAD_DIM) + RMS_EPS)
            rb = lax.rsqrt(sb * (1.0 / HEAD_DIM) + RMS_EPS)
            out.append(tile * jnp.where(low, ra, rb) * g[:, j * LANES:(j + 1) * LANES])
        return jnp.concatenate(out, axis=-1)

    def norm_stage(rows):
        return _rms(x_ref[rows, :], g_ref[...]).astype(BF16)

    def proj_stage(h):
        return _dot(h, wbf_ref[...])

    def head_stage(rows, z):
        u_ref[rows, :] = z[:, :POOL_WIDTH]
        q = z[:, POOL_WIDTH:POOL_WIDTH + ATTN_WIDTH]
        k = z[:, POOL_WIDTH + ATTN_WIDTH:POOL_WIDTH + 2 * ATTN_WIDTH]
        v = z[:, POOL_WIDTH + 2 * ATTN_WIDTH:]
        q_ref[rows, :] = (head_norm(q, qg_ref[...]) * (HEAD_DIM ** -0.5 * LOG2E)).astype(BF16)
        k_ref[rows, :] = head_norm(k, kg_ref[...]).astype(BF16)
        v_ref[rows, :] = v.astype(BF16)

    groups = [slice(sub * PROJ_ROWS, (sub + 1) * PROJ_ROWS) for sub in range(x_ref.shape[0] // PROJ_ROWS)]
    zs = [proj_stage(norm_stage(rows)) for rows in groups]
    for rows, z in zip(groups, zs):
        head_stage(rows, z)


def _in_proj(x2, g, w, qg, kg, tm=1024):
    n = x2.shape[0]
    zw = w.shape[1]
    assert 2 * HEAD_DIM == LANES
    row = lambda i: (i, 0)
    fixed = lambda i: (0, 0)
    return pl.pallas_call(
        _in_proj_kernel,
        grid=(n // tm,),
        in_specs=[
            pl.BlockSpec((tm, D_MODEL), row),
            pl.BlockSpec((1, D_MODEL), fixed),
            pl.BlockSpec((D_MODEL, zw), fixed, pipeline_mode=pl.Buffered(1)),
            pl.BlockSpec((1, ATTN_WIDTH), fixed),
            pl.BlockSpec((1, ATTN_WIDTH), fixed),
        ],
        out_specs=[
            pl.BlockSpec((tm, POOL_WIDTH), row),
            pl.BlockSpec((tm, ATTN_WIDTH), row),
            pl.BlockSpec((tm, ATTN_WIDTH), row),
            pl.BlockSpec((tm, ATTN_WIDTH), row),
        ],
        out_shape=[
            jax.ShapeDtypeStruct((n, POOL_WIDTH), F32),
            jax.ShapeDtypeStruct((n, ATTN_WIDTH), BF16),
            jax.ShapeDtypeStruct((n, ATTN_WIDTH), BF16),
            jax.ShapeDtypeStruct((n, ATTN_WIDTH), BF16),
        ],
        scratch_shapes=[pltpu.VMEM((D_MODEL, zw), BF16)],
        compiler_params=_params("arbitrary"),
        name="in_proj",
    )(x2, g, w, qg, kg)


def _natten_kernel(q_ref, k_ref, v_ref, tbl_ref, o_ref, bias_ref, *, rows, kh):
    band = kh * GRID_W
    lane = lax.broadcasted_iota(jnp.int32, (GRID_W, LANES), 1)
    first = lane < HEAD_DIM

    @pl.when(pl.program_id(1) == 0)
    def _():
        for hh in range(2):
            for d0 in range(WIN_ROWS_MAX):
                for kk in range(kh // 2):
                    bias_ref[d0, hh * GRID_W:(hh + 1) * GRID_W, kk * LANES:(kk + 1) * LANES] = jnp.where(
                        lane < GRID_W, tbl_ref[hh, d0 + 2 * kk], tbl_ref[hh, d0 + 2 * kk + 1])

    def geometry(g):
        geo = []
        for r in range(g * NATTEN_UNROLL, (g + 1) * NATTEN_UNROLL):
            rs = min(max(r - kh // 2, 0), rows - kh)
            geo.append((rs - r + (WIN_ROWS_MAX - 1), r * GRID_W, rs * GRID_W))
        return geo

    def score_stage(g):
        scores = []
        for d0, q0, k0 in geometry(g):
            qr = q_ref[0, q0:q0 + GRID_W, :]
            zero = jnp.zeros_like(qr)
            q2 = jnp.concatenate([jnp.where(first, qr, zero), jnp.where(first, zero, qr)], axis=0)
            s = _dot_nt(q2, k_ref[0, k0:k0 + band, :]) + bias_ref[d0]
            scores.append((s, jnp.max(s, axis=-1, keepdims=True)))
        return scores

    def softmax_stage(scores):
        probs = []
        for s, m in scores:
            e = jnp.exp2(s - m)
            probs.append((e.astype(BF16), jnp.sum(e, axis=-1, keepdims=True)))
        return probs

    def value_stage(g, probs):
        for (d0, q0, k0), (e, l) in zip(geometry(g), probs):
            o = _dot(e, v_ref[0, k0:k0 + band, :]) / l
            o_ref[0, q0:q0 + GRID_W, :] = jnp.where(first, o[:GRID_W], o[GRID_W:]).astype(o_ref.dtype)

    ngroups = rows // NATTEN_UNROLL
    scores, probs = {}, {}
    for step in range(ngroups + 2):
        if step < ngroups:
            scores[step] = score_stage(step)
        if 0 <= step - 1 < ngroups:
            probs[step - 1] = softmax_stage(scores.pop(step - 1))
        if 0 <= step - 2 < ngroups:
            value_stage(step - 2, probs.pop(step - 2))


def _natten(q, k, v, bias):
    b, s, _ = q.shape
    rows = s // GRID_W
    kh = min(WIN_ROWS_MAX, rows)
    assert rows % NATTEN_UNROLL == 0 and kh % 2 == 0 and 2 * GRID_W == LANES
    pairs = ATTN_WIDTH // LANES
    blk = pl.BlockSpec((1, s, LANES), lambda p, bi: (bi, 0, p))
    return pl.pallas_call(
        functools.partial(_natten_kernel, rows=rows, kh=kh),
        grid=(pairs, b),
        in_specs=[blk, blk, blk,
                  pl.BlockSpec((2,) + bias.shape[1:], lambda p, bi: (p, 0, 0, 0))],
        out_specs=blk,
        out_shape=jax.ShapeDtypeStruct((b, s, ATTN_WIDTH), BF16),
        scratch_shapes=[pltpu.VMEM((WIN_ROWS_MAX, 2 * GRID_W, kh * GRID_W), F32)],
        compiler_params=_params("arbitrary", "arbitrary"),
        name="natten",
    )(q, k, v, bias)


def _attn_bias_table(rpb):
    c = jnp.arange(GRID_W)
    cs = jnp.clip(c - WIN_COLS // 2, 0, GRID_W - WIN_COLS)
    j = jnp.arange(GRID_W)
    valid = (j[None, :] >= cs[:, None]) & (j[None, :] < cs[:, None] + WIN_COLS)
    dc = j[None, :] - c[:, None] + (WIN_COLS - 1)
    pick = ((dc[None] == jnp.arange(2 * WIN_COLS - 1)[:, None, None]) & valid[None]).astype(F32)
    t = jnp.einsum('hrd,dcj->hrcj', rpb.astype(F32), pick, precision=lax.Precision.HIGHEST)
    t = jnp.where(valid, t * LOG2E, MASK_BIAS)
    return jnp.concatenate([t, t], axis=-1)


def _window_sum(upad, w, tm):
    n = upad.shape[0]
    fwd = upad
    span = 1
    while span < min(w, POOL_HALO):
        fwd = fwd + pltpu.roll(fwd, n - span, axis=0)
        span *= 2
    centre = slice(POOL_HALO, POOL_HALO + tm)
    if w == 2 * POOL_HALO:
        return fwd[0:tm] + fwd[centre]
    return pltpu.roll(fwd, w // 2, axis=0)[centre]


def _mix_kernel(x_ref, u_ref, up_ref, un_ref, ya_ref, ic_ref, wp_ref, ps_ref, wo_ref, g_ref, wr_ref,
                x1_ref, h_ref, aff_ref, upad_ref, wobf_ref, wpbf_ref, *, tm):
    i = pl.program_id(1)

    @pl.when((pl.program_id(0) == 0) & (i == 0))
    def _():
        wobf_ref[...] = wo_ref[...].astype(BF16)
        wpbf_ref[...] = wp_ref[...].astype(BF16)

    upad_ref[0:POOL_HALO, :] = jnp.where(i > 0, up_ref[0], 0.0)
    upad_ref[POOL_HALO:POOL_HALO + tm, :] = u_ref[0]
    upad_ref[POOL_HALO + tm:, :] = jnp.where(i < pl.num_programs(1) - 1, un_ref[0], 0.0)

    ngroups = tm // MIX_ROWS
    group_cols = [slice(gi * POOL_GROUP, (gi + 1) * POOL_GROUP) for gi in range(len(POOL_WINDOWS))]

    def pool_stage(g):
        ds = []
        for cols, w in zip(group_cols, POOL_WINDOWS):
            upad = upad_ref[g * MIX_ROWS:(g + 1) * MIX_ROWS + 2 * POOL_HALO, cols]
            edge = jnp.full((POOL_HALO, POOL_GROUP), 1.0 / w, F32)
            head = jnp.where(i == 0, ic_ref[0:POOL_HALO, cols], 1.0 / w) if g == 0 else edge
            tail = (jnp.where(i == pl.num_programs(1) - 1, ic_ref[POOL_HALO:, cols], 1.0 / w)
                    if g == ngroups - 1 else edge)
            inv = jnp.concatenate(
                [head, jnp.full((MIX_ROWS - 2 * POOL_HALO, POOL_GROUP), 1.0 / w, F32), tail], axis=0)
            d = _window_sum(upad, w, MIX_ROWS) * inv - upad[POOL_HALO:POOL_HALO + MIX_ROWS]
            ds.append(d.astype(BF16))
        return ds

    def proj_stage(g, ds):
        rows = slice(g * MIX_ROWS, (g + 1) * MIX_ROWS)
        ypool = [_dot(d, wpbf_ref[gi]) * ps_ref[:, cols] for gi, (d, cols) in enumerate(zip(ds, group_cols))]
        ypool = jnp.concatenate(ypool, axis=-1).astype(BF16)
        return _dot(ypool, wobf_ref[:POOL_WIDTH, :]) + _dot(ya_ref[0, rows, :], wobf_ref[POOL_WIDTH:, :])

    def route_stage(g, mix):
        rows = slice(g * MIX_ROWS, (g + 1) * MIX_ROWS)
        x1 = x_ref[0, rows, :] + mix
        x1_ref[0, rows, :] = x1
        h = _rms(x1, g_ref[...]).astype(BF16)
        h_ref[0, rows, :] = h
        logits = _dot_nt(wr_ref[...].astype(BF16), h)
        m = jnp.max(logits, axis=0, keepdims=True)
        e = jnp.exp(logits - m)
        aff_ref[0, :, rows] = e / jnp.sum(e, axis=0, keepdims=True)

    pooled, mixed = {}, {}
    for step in range(ngroups + 2):
        if step < ngroups:
            pooled[step] = pool_stage(step)
        if 0 <= step - 1 < ngroups:
            mixed[step - 1] = proj_stage(step - 1, pooled.pop(step - 1))
        if 0 <= step - 2 < ngroups:
            route_stage(step - 2, mixed.pop(step - 2))


def _pool_edge_inverse_counts(s):
    t = jnp.concatenate([jnp.arange(POOL_HALO), jnp.arange(s - POOL_HALO, s)])
    cols = []
    for w in POOL_WINDOWS:
        lo = jnp.clip(t - w // 2, 0, s - 1)
        hi = jnp.clip(t + (w - w // 2) - 1, 0, s - 1)
        inv = 1.0 / (hi - lo + 1).astype(F32)
        cols.append(jnp.broadcast_to(inv[:, None], (2 * POOL_HALO, POOL_GROUP)))
    return jnp.concatenate(cols, axis=-1)


def _mix(x, u, ya, wp, ps, wo, g, wr_t, tm=1024):
    b, s, _ = x.shape
    nt = s // tm
    hb = tm // POOL_HALO
    tile = lambda w: pl.BlockSpec((1, tm, w), lambda bi, i: (bi, i, 0))
    fixed2 = lambda shape: pl.BlockSpec(shape, lambda bi, i: (0, 0))
    return pl.pallas_call(
        functools.partial(_mix_kernel, tm=tm),
        grid=(b, nt),
        in_specs=[
            tile(D_MODEL),
            tile(POOL_WIDTH),
            pl.BlockSpec((1, POOL_HALO, POOL_WIDTH), lambda bi, i: (bi, jnp.maximum(i * hb - 1, 0), 0)),
            pl.BlockSpec((1, POOL_HALO, POOL_WIDTH),
                         lambda bi, i: (bi, jnp.minimum((i + 1) * hb, s // POOL_HALO - 1), 0)),
            tile(ATTN_WIDTH),
            fixed2((2 * POOL_HALO, POOL_WIDTH)),
            pl.BlockSpec(wp.shape, lambda bi, i: (0, 0, 0)),
            fixed2((1, POOL_WIDTH)),
            pl.BlockSpec(wo.shape, lambda bi, i: (0, 0), pipeline_mode=pl.Buffered(1)),
            fixed2((1, D_MODEL)),
            fixed2(wr_t.shape),
        ],
        out_specs=[
            tile(D_MODEL),
            tile(D_MODEL),
            pl.BlockSpec((1, N_EXPERTS, tm), lambda bi, i: (bi, 0, i)),
        ],
        out_shape=[
            jax.ShapeDtypeStruct((b, s, D_MODEL), F32),
            jax.ShapeDtypeStruct((b, s, D_MODEL), BF16),
            jax.ShapeDtypeStruct((b, N_EXPERTS, s), F32),
        ],
        scratch_shapes=[
            pltpu.VMEM((tm + 2 * POOL_HALO, POOL_WIDTH), F32),
            pltpu.VMEM(wo.shape, BF16),
            pltpu.VMEM(wp.shape, BF16),
        ],
        compiler_params=_params("arbitrary", "arbitrary"),
        name="mix",
    )(x, u, u, u, ya, _pool_edge_inverse_counts(s), wp, ps, wo, g, wr_t)


def _lane_cumsum_exclusive(m):
    e, s = m.shape
    r = lax.broadcasted_iota(jnp.int32, (LANES, LANES), 0)
    c = lax.broadcasted_iota(jnp.int32, (LANES, LANES), 1)
    upper = (r < c).astype(BF16)
    carry = jnp.zeros((e, 1), F32)
    out = []
    for blk in range(s // LANES):
        piece = m[:, blk * LANES:(blk + 1) * LANES]
        out.append(_dot(piece.astype(BF16), upper) + carry)
        carry = carry + jnp.sum(piece, axis=-1, keepdims=True)
    return jnp.concatenate(out, axis=-1)


def _route_kernel(aff_ref, slot_ref, slot_t_ref, off_ref, *, cap):
    aff = aff_ref[...]
    capf = jnp.float32(cap)

    def count_ge(cand_bits):
        return jnp.sum((aff >= pltpu.bitcast(cand_bits, F32)).astype(F32), axis=-1, keepdims=True)

    def search(step, ans):
        cand = ans | (jnp.int32(1) << (30 - step))
        return jnp.where(count_ge(cand) >= capf, cand, ans)

    thr = pltpu.bitcast(lax.fori_loop(0, 31, search, jnp.zeros((aff.shape[0], 1), jnp.int32)), F32)
    gt = aff > thr
    eq = aff == thr
    need = capf - jnp.sum(gt.astype(F32), axis=-1, keepdims=True)
    sel = gt | (eq & (_lane_cumsum_exclusive(eq.astype(F32)) < need))
    self = sel.astype(F32)
    slot = jnp.where(sel, _lane_cumsum_exclusive(self), float(NOT_SELECTED))
    slot_ref[...] = slot.astype(jnp.int32)
    pad = jnp.full((LANES - N_EXPERTS, slot.shape[1]), float(NOT_SELECTED), F32)
    for bi in range(slot_t_ref.shape[0]):
        mine = slot[bi * N_EXPERTS:(bi + 1) * N_EXPERTS]
        slot_t_ref[bi] = jnp.concatenate([mine, pad], axis=0).T.astype(jnp.int32)
    lane = lax.broadcasted_iota(jnp.int32, (slot.shape[0], LANES), 1)
    off = jnp.zeros((slot.shape[0], LANES), F32)
    run = jnp.zeros((slot.shape[0], 1), F32)
    for i in range(1, slot.shape[1] // ROUTE_TILE + 1):
        run = run + jnp.sum(self[:, (i - 1) * ROUTE_TILE:i * ROUTE_TILE], axis=-1, keepdims=True)
        off = jnp.where(lane == i, run, off)
    off_ref[...] = off.astype(jnp.int32)


def _route(aff_t, cap):
    b, e, s = aff_t.shape
    assert e == N_EXPERTS and s % ROUTE_TILE == 0 and s // ROUTE_TILE < OFF_STRIDE
    slot, slot_t, off = pl.pallas_call(
        functools.partial(_route_kernel, cap=cap),
        out_shape=[
            jax.ShapeDtypeStruct((b * e, s), jnp.int32),
            jax.ShapeDtypeStruct((b, s, LANES), jnp.int32),
            jax.ShapeDtypeStruct((b * e, LANES), jnp.int32),
        ],
        compiler_params=pltpu.CompilerParams(vmem_limit_bytes=VMEM_LIMIT),
        name="route",
    )(aff_t.reshape(b * e, s))
    return slot.reshape(b, e, s), slot_t, off


def _window_starts(off_ref, bi, ti, cap):
    starts = []
    ok = None
    for e in range(N_EXPERTS):
        base = (bi * N_EXPERTS + e) * OFF_STRIDE + ti
        lo = off_ref[base]
        hi = off_ref[base + 1]
        st = jnp.minimum(jnp.bitwise_and(lo, -16), cap - ROUTE_WIN)
        fits = hi <= st + ROUTE_WIN
        ok = fits if ok is None else jnp.logical_and(ok, fits)
        starts.append(pl.multiple_of(st, 16))
    return starts, ok


def _dispatch_kernel(off_ref, slot_ref, aff_ref, h_ref, xe_ref, gate_ref, *, cap):
    bi = pl.program_id(0)
    ti = pl.program_id(1)

    nsub = h_ref.shape[1] // ROUTE_TILE
    col_groups = [slice(sub * ROUTE_TILE, (sub + 1) * ROUTE_TILE) for sub in range(nsub)]
    geo = [_window_starts(off_ref, bi, ti * nsub + sub, cap) for sub in range(nsub)]
    ok = functools.reduce(jnp.logical_and, [fits for _, fits in geo])

    def zero_fill():
        xe_ref[...] = jnp.zeros_like(xe_ref)
        gate_ref[...] = jnp.zeros_like(gate_ref)

    def windowed(first):
        if first:
            zero_fill()
        row = lax.broadcasted_iota(jnp.int32, (ROUTE_WIN, ROUTE_TILE), 0)
        for cols, (starts, _) in zip(col_groups, geo):
            slot = slot_ref[0, :, cols]
            aff = aff_ref[0, :, cols]
            hits = [row == (slot[e:e + 1, :] - starts[e]) for e in range(N_EXPERTS)]
            onehot = jnp.concatenate([jnp.where(hh, 1.0, 0.0).astype(BF16) for hh in hits], axis=0)
            res = _dot(onehot, h_ref[0, cols, :])
            for e, hh in enumerate(hits):
                win = pl.ds(starts[e], ROUTE_WIN)
                xe_ref[e, win, :] += res[e * ROUTE_WIN:(e + 1) * ROUTE_WIN].astype(BF16)
                gate_ref[e, win, :] += jnp.sum(jnp.where(hh, aff[e:e + 1, :], 0.0), axis=-1, keepdims=True)

    pl.when(ok & (ti == 0))(functools.partial(windowed, True))
    pl.when(ok & (ti > 0))(functools.partial(windowed, False))

    @pl.when(jnp.logical_not(ok))
    def _():
        pl.when(ti == 0)(zero_fill)
        row = lax.broadcasted_iota(jnp.int32, (cap, ROUTE_TILE), 0)
        for cols in col_groups:
            slot = slot_ref[0, :, cols]
            aff = aff_ref[0, :, cols]
            h = h_ref[0, cols, :]
            for e in range(N_EXPERTS):
                hh = row == slot[e:e + 1, :]
                xe_ref[e] += _dot(jnp.where(hh, 1.0, 0.0).astype(BF16), h).astype(BF16)
                gate_ref[e] += jnp.sum(jnp.where(hh, aff[e:e + 1, :], 0.0), axis=-1, keepdims=True)


def _dispatch(off, slot, aff_t, h, cap, tm=1024):
    b, e, s = slot.shape
    assert tm % ROUTE_TILE == 0
    rows = pl.BlockSpec((1, e, tm), lambda bi, ti, off_ref: (bi, 0, ti))
    return pl.pallas_call(
        functools.partial(_dispatch_kernel, cap=cap),
        grid_spec=pltpu.PrefetchScalarGridSpec(
            num_scalar_prefetch=1,
            grid=(b, s // tm),
            in_specs=[rows, rows,
                      pl.BlockSpec((1, tm, D_MODEL), lambda bi, ti, off_ref: (bi, ti, 0))],
            out_specs=[
                pl.BlockSpec((e, cap, D_MODEL), lambda bi, ti, off_ref: (0, bi, 0)),
                pl.BlockSpec((e, cap, 1), lambda bi, ti, off_ref: (0, bi, 0)),
            ],
        ),
        out_shape=[
            jax.ShapeDtypeStruct((e, b * cap, D_MODEL), BF16),
            jax.ShapeDtypeStruct((e, b * cap, 1), F32),
        ],
        compiler_params=_params("arbitrary", "arbitrary"),
        name="dispatch",
    )(off, slot, aff_t, h)


def _experts_kernel(xe_ref, gate_ref, wg_ref, wu_ref, wd_ref, y_ref, acc_ref):
    f = pl.program_id(1)

    def hidden_chunk(first, last):
        wg = wg_ref[0].astype(BF16)
        wu = wu_ref[0].astype(BF16)
        wd = wd_ref[0].astype(BF16)
        for mb in range(xe_ref.shape[1] // EXPERT_ROWS):
            rows = slice(mb * EXPERT_ROWS, (mb + 1) * EXPERT_ROWS)
            xe = xe_ref[0, rows, :]
            a = _dot(xe, wg)
            b = _dot(xe, wu)
            part = _dot((a * jax.nn.sigmoid(a) * b).astype(BF16), wd)
            total = part if first else acc_ref[rows, :] + part
            if last:
                y_ref[0, rows, :] = (total * gate_ref[0, rows, :]).astype(y_ref.dtype)
            else:
                acc_ref[rows, :] = total

    nf = pl.num_programs(1)
    pl.when(f == 0)(functools.partial(hidden_chunk, True, False))
    pl.when((f > 0) & (f < nf - 1))(functools.partial(hidden_chunk, False, False))
    pl.when(f == nf - 1)(functools.partial(hidden_chunk, False, True))


def _experts(xe, gate, wg, wu, wd, tf=512):
    e, m, _ = xe.shape
    nf = D_EXPERT // tf
    assert nf >= 2
    return pl.pallas_call(
        _experts_kernel,
        grid=(e, nf),
        in_specs=[
            pl.BlockSpec((1, m, D_MODEL), lambda ei, f: (ei, 0, 0)),
            pl.BlockSpec((1, m, 1), lambda ei, f: (ei, 0, 0)),
            pl.BlockSpec((1, D_MODEL, tf), lambda ei, f: (ei, 0, f)),
            pl.BlockSpec((1, D_MODEL, tf), lambda ei, f: (ei, 0, f)),
            pl.BlockSpec((1, tf, D_MODEL), lambda ei, f: (ei, f, 0)),
        ],
        out_specs=pl.BlockSpec((1, m, D_MODEL), lambda ei, f: (ei, 0, 0)),
        out_shape=jax.ShapeDtypeStruct((e, m, D_MODEL), BF16),
        scratch_shapes=[pltpu.VMEM((m, D_MODEL), F32)],
        compiler_params=_params("arbitrary", "arbitrary"),
        name="experts",
    )(xe, gate, wg, wu, wd)


def _combine_kernel(off_ref, x1_ref, st_ref, y_ref, p_ref, gn_ref, wgbf_ref, wpbf_ref, gp_ref,
                    o_ref, *, cap):
    bi = pl.program_id(0)
    nsub = x1_ref.shape[1] // ROUTE_TILE
    row_groups = [slice(sub * ROUTE_TILE, (sub + 1) * ROUTE_TILE) for sub in range(nsub)]
    geo = [_window_starts(off_ref, bi, pl.program_id(1) * nsub + sub, cap) for sub in range(nsub)]
    ok = functools.reduce(jnp.logical_and, [fits for _, fits in geo])

    def token_slots(rows):
        return st_ref[0, rows, :]

    def windowed_scatter(rows, starts):
        st = token_slots(rows)
        lane = lax.broadcasted_iota(jnp.int32, (ROUTE_TILE, LANES), 1)
        low = lane < ROUTE_WIN
        total = None
        for g in range(N_EXPERTS // ROUTE_GROUP):
            halves = []
            wins = []
            for half in range(ROUTE_GROUP // 2):
                e0 = g * ROUTE_GROUP + 2 * half
                t0 = st[:, e0:e0 + 1] - starts[e0]
                t1 = st[:, e0 + 1:e0 + 2] + (ROUTE_WIN - starts[e0 + 1])
                halves.append(jnp.where(lane == jnp.where(low, t0, t1), 1.0, 0.0).astype(BF16))
                wins.append(y_ref[e0, pl.ds(starts[e0], ROUTE_WIN), :])
                wins.append(y_ref[e0 + 1, pl.ds(starts[e0 + 1], ROUTE_WIN), :])
            part = _dot(jnp.concatenate(halves, axis=1), jnp.concatenate(wins, axis=0))
            total = part if total is None else total + part
        return total

    def dense_scatter(rows):
        st = token_slots(rows)
        lane = lax.broadcasted_iota(jnp.int32, (ROUTE_TILE, cap), 1)
        total = None
        for e in range(N_EXPERTS):
            onehot = jnp.where(lane == st[:, e:e + 1], 1.0, 0.0).astype(BF16)
            part = _dot(onehot, y_ref[e])
            total = part if total is None else total + part
        return total

    def gate_stage(rows, ffn):
        x2 = x1_ref[0, rows, :] + ffn
        return x2, _dot(_rms(x2, gn_ref[...]).astype(BF16), wgbf_ref[...])

    def output_stage(rows, x2, gate_logits):
        emb = _rms(_dot(p_ref[0, rows, :].astype(BF16), wpbf_ref[...]), gp_ref[...])
        o_ref[0, rows, :] = x2 + jax.nn.sigmoid(gate_logits) * emb

    def pipeline(scatter_stage):
        per = PLE_ROWS // ROUTE_TILE
        groups = [slice(g * PLE_ROWS, (g + 1) * PLE_ROWS) for g in range(nsub // per)]
        ffn, gated = {}, {}
        for step in range(len(groups) + 2):
            if step < len(groups):
                ffn[step] = jnp.concatenate([scatter_stage(step * per + j) for j in range(per)], axis=0)
            if 0 <= step - 1 < len(groups):
                gated[step - 1] = gate_stage(groups[step - 1], ffn.pop(step - 1))
            if 0 <= step - 2 < len(groups):
                output_stage(groups[step - 2], *gated.pop(step - 2))

    pl.when(ok)(lambda: pipeline(lambda g: windowed_scatter(row_groups[g], geo[g][0])))
    pl.when(jnp.logical_not(ok))(lambda: pipeline(lambda g: dense_scatter(row_groups[g])))


def _combine(off, x1, slot_t, y, p, gn, wg, wp, gp, cap, tm=1024):
    b, s, _ = x1.shape
    assert tm % PLE_ROWS == 0 and PLE_ROWS % ROUTE_TILE == 0
    tile = lambda w: pl.BlockSpec((1, tm, w), lambda bi, i, off_ref: (bi, i, 0))
    fixed2 = lambda shape: pl.BlockSpec(shape, lambda bi, i, off_ref: (0, 0))
    return pl.pallas_call(
        functools.partial(_combine_kernel, cap=cap),
        grid_spec=pltpu.PrefetchScalarGridSpec(
            num_scalar_prefetch=1,
            grid=(b, s // tm),
            in_specs=[
                tile(D_MODEL),
                tile(LANES),
                pl.BlockSpec((N_EXPERTS, cap, D_MODEL), lambda bi, i, off_ref: (0, bi, 0)),
                tile(PLE_DIM),
                fixed2((1, D_MODEL)),
                pl.BlockSpec(wg.shape, lambda bi, i, off_ref: (0, 0), pipeline_mode=pl.Buffered(1)),
                pl.BlockSpec(wp.shape, lambda bi, i, off_ref: (0, 0), pipeline_mode=pl.Buffered(1)),
                fixed2((1, D_MODEL)),
            ],
            out_specs=tile(D_MODEL),
        ),
        out_shape=jax.ShapeDtypeStruct((b, s, D_MODEL), F32),
        compiler_params=_params("arbitrary", "arbitrary"),
        name="combine",
    )(off, x1, slot_t, y, p, gn, wg.astype(BF16), wp.astype(BF16), gp)


def kernel(x, p, norm_mix, w_in, w_pool, pool_scale, q_norm, k_norm, rpb, w_out, norm_ffn, w_router, w_gate, w_up, w_down, norm_ple, w_ple_gate, w_ple_proj, norm_ple_post):
    b, s, d = x.shape
    depth = w_in.shape[0]
    cap = EC_CAPACITY * s // N_EXPERTS
    row = lambda a: a.reshape(1, -1)
    for i in range(depth):
        u, q, k, v = _in_proj(x.reshape(b * s, d), row(norm_mix[i]), w_in[i],
                              row(jnp.tile(q_norm[i], ATTN_HEADS)), row(jnp.tile(k_norm[i], ATTN_HEADS)))
        shp = lambda a: a.reshape(b, s, -1)
        y_attn = _natten(shp(q), shp(k), shp(v), _attn_bias_table(rpb[i]))
        x1, h, aff_t = _mix(x, shp(u), y_attn, w_pool[i], row(pool_scale[i]), w_out[i],
                            row(norm_ffn[i]), w_router[i].T)
        slot, slot_t, off = _route(aff_t, cap)
        off = off[:, :OFF_STRIDE].reshape(-1)
        xe, gate = _dispatch(off, slot, aff_t, h, cap)
        y = _experts(xe, gate, w_gate[i], w_up[i], w_down[i])
        x = _combine(off, x1, slot_t, y, p[i], row(norm_ple[i]), w_ple_gate[i], w_ple_proj[i],
                     row(norm_ple_post[i]), cap)
    return x
```

```python
import functools

import jax
import jax.numpy as jnp
from jax import lax
from jax.experimental import pallas as pl
from jax.experimental.pallas import tpu as pltpu

D_MODEL = 1024
GRID_W = 64
POOL_WINDOWS = (2, 4, 8, 16)
POOL_WIDTH = D_MODEL // 2
POOL_GROUP = POOL_WIDTH // len(POOL_WINDOWS)
ATTN_HEADS = 8
HEAD_DIM = (D_MODEL // 2) // ATTN_HEADS
ATTN_WIDTH = ATTN_HEADS * HEAD_DIM
WIN_ROWS_MAX = 8
WIN_COLS = 16
N_EXPERTS = 16
EC_CAPACITY = 2
D_EXPERT = 2 * D_MODEL
PLE_DIM = 256
RMS_EPS = 1e-6

LANES = 128
POOL_HALO = 8
LOG2E = 1.4426950408889634
MASK_BIAS = -1e30
NATTEN_UNROLL = 2
ROUTE_TILE = 256
ROUTE_WIN = 64
ROUTE_GROUP = 4
OFF_STRIDE = 16
NOT_SELECTED = -(1 << 20)
assert 2 * ROUTE_WIN == LANES and ROUTE_GROUP % 2 == 0
PLE_ROWS = 256
MIX_ROWS = 512
PROJ_ROWS = 512
EXPERT_ROWS = 1024
VMEM_LIMIT = 56 * 1024 * 1024

BF16 = jnp.bfloat16
F32 = jnp.float32


def _params(*sem):
    return pltpu.CompilerParams(dimension_semantics=sem, vmem_limit_bytes=VMEM_LIMIT)


def _rms(x, g):
    return x * lax.rsqrt(jnp.mean(x * x, axis=-1, keepdims=True) + RMS_EPS) * g


def _dot(a, b):
    return jnp.dot(a, b, preferred_element_type=F32)


def _dot_nt(a, b):
    return lax.dot_general(a, b, (((1,), (1,)), ((), ())), preferred_element_type=F32)


def _in_proj_kernel(x_ref, g_ref, w_ref, qg_ref, kg_ref,
                    u_ref, q_ref, k_ref, v_ref, wbf_ref):
    @pl.when(pl.program_id(0) == 0)
    def _():
        wbf_ref[...] = w_ref[...].astype(BF16)

    def head_norm(t, g):
        low = lax.broadcasted_iota(jnp.int32, (t.shape[0], LANES), 1) < HEAD_DIM
        out = []
        for j in range(t.shape[1] // LANES):
            tile = t[:, j * LANES:(j + 1) * LANES]
            sq = tile * tile
            sa = jnp.sum(jnp.where(low, sq, 0.0), axis=-1, keepdims=True)
            sb = jnp.sum(sq, axis=-1, keepdims=True) - sa
            ra = lax.rsqrt(sa * (1.0 / HEAD_DIM) + RMS_EPS)
            rb = lax.rsqrt(sb * (1.0 / HEAD_DIM) + RMS_EPS)
            out.append(tile * jnp.where(low, ra, rb) * g[:, j * LANES:(j + 1) * LANES])
        return jnp.concatenate(out, axis=-1)

    def norm_stage(rows):
        return _rms(x_ref[rows, :], g_ref[...]).astype(BF16)

    def proj_stage(h):
        return _dot(h, wbf_ref[...])

    def head_stage(rows, z):
        u_ref[rows, :] = z[:, :POOL_WIDTH]
        q = z[:, POOL_WIDTH:POOL_WIDTH + ATTN_WIDTH]
        k = z[:, POOL_WIDTH + ATTN_WIDTH:POOL_WIDTH + 2 * ATTN_WIDTH]
        v = z[:, POOL_WIDTH + 2 * ATTN_WIDTH:]
        q_ref[rows, :] = (head_norm(q, qg_ref[...]) * (HEAD_DIM ** -0.5 * LOG2E)).astype(BF16)
        k_ref[rows, :] = head_norm(k, kg_ref[...]).astype(BF16)
        v_ref[rows, :] = v.astype(BF16)

    groups = [slice(sub * PROJ_ROWS, (sub + 1) * PROJ_ROWS) for sub in range(x_ref.shape[0] // PROJ_ROWS)]
    zs = [proj_stage(norm_stage(rows)) for rows in groups]
    for rows, z in zip(groups, zs):
        head_stage(rows, z)


def _in_proj(x2, g, w, qg, kg, tm=1024):
    n = x2.shape[0]
    zw = w.shape[1]
    assert 2 * HEAD_DIM == LANES
    row = lambda i: (i, 0)
    fixed = lambda i: (0, 0)
    return pl.pallas_call(
        _in_proj_kernel,
        grid=(n // tm,),
        in_specs=[
            pl.BlockSpec((tm, D_MODEL), row),
            pl.BlockSpec((1, D_MODEL), fixed),
            pl.BlockSpec((D_MODEL, zw), fixed, pipeline_mode=pl.Buffered(1)),
            pl.BlockSpec((1, ATTN_WIDTH), fixed),
            pl.BlockSpec((1, ATTN_WIDTH), fixed),
        ],
        out_specs=[
            pl.BlockSpec((tm, POOL_WIDTH), row),
            pl.BlockSpec((tm, ATTN_WIDTH), row),
            pl.BlockSpec((tm, ATTN_WIDTH), row),
            pl.BlockSpec((tm, ATTN_WIDTH), row),
        ],
        out_shape=[
            jax.ShapeDtypeStruct((n, POOL_WIDTH), F32),
            jax.ShapeDtypeStruct((n, ATTN_WIDTH), BF16),
            jax.ShapeDtypeStruct((n, ATTN_WIDTH), BF16),
            jax.ShapeDtypeStruct((n, ATTN_WIDTH), BF16),
        ],
        scratch_shapes=[pltpu.VMEM((D_MODEL, zw), BF16)],
        compiler_params=_params("arbitrary"),
        name="in_proj",
    )(x2, g, w, qg, kg)


def _natten_kernel(q_ref, k_ref, v_ref, tbl_ref, o_ref, bias_ref, *, rows, kh):
    band = kh * GRID_W
    lane = lax.broadcasted_iota(jnp.int32, (GRID_W, LANES), 1)
    first = lane < HEAD_DIM

    @pl.when(pl.program_id(1) == 0)
    def _():
        for hh in range(2):
            for d0 in range(WIN_ROWS_MAX):
                for kk in range(kh // 2):
                    bias_ref[d0, hh * GRID_W:(hh + 1) * GRID_W, kk * LANES:(kk + 1) * LANES] = jnp.where(
                        lane < GRID_W, tbl_ref[hh, d0 + 2 * kk], tbl_ref[hh, d0 + 2 * kk + 1])

    def geometry(g):
        geo = []
        for r in range(g * NATTEN_UNROLL, (g + 1) * NATTEN_UNROLL):
            rs = min(max(r - kh // 2, 0), rows - kh)
            geo.append((rs - r + (WIN_ROWS_MAX - 1), r * GRID_W, rs * GRID_W))
        return geo

    def score_stage(g):
        scores = []
        for d0, q0, k0 in geometry(g):
            qr = q_ref[0, q0:q0 + GRID_W, :]
            zero = jnp.zeros_like(qr)
            q2 = jnp.concatenate([jnp.where(first, qr, zero), jnp.where(first, zero, qr)], axis=0)
            s = _dot_nt(q2, k_ref[0, k0:k0 + band, :]) + bias_ref[d0]
            scores.append((s, jnp.max(s, axis=-1, keepdims=True)))
        return scores

    def softmax_stage(scores):
        probs = []
        for s, m in scores:
            e = jnp.exp2(s - m)
            probs.append((e.astype(BF16), jnp.sum(e, axis=-1, keepdims=True)))
        return probs

    def value_stage(g, probs):
        for (d0, q0, k0), (e, l) in zip(geometry(g), probs):
            o = _dot(e, v_ref[0, k0:k0 + band, :]) / l
            o_ref[0, q0:q0 + GRID_W, :] = jnp.where(first, o[:GRID_W], o[GRID_W:]).astype(o_ref.dtype)

    ngroups = rows // NATTEN_UNROLL
    scores, probs = {}, {}
    for step in range(ngroups + 2):
        if step < ngroups:
            scores[step] = score_stage(step)
        if 0 <= step - 1 < ngroups:
            probs[step - 1] = softmax_stage(scores.pop(step - 1))
        if 0 <= step - 2 < ngroups:
            value_stage(step - 2, probs.pop(step - 2))


def _natten(q, k, v, bias):
    b, s, _ = q.shape
    rows = s // GRID_W
    kh = min(WIN_ROWS_MAX, rows)
    assert rows % NATTEN_UNROLL == 0 and kh % 2 == 0 and 2 * GRID_W == LANES
    pairs = ATTN_WIDTH // LANES
    blk = pl.BlockSpec((1, s, LANES), lambda p, bi: (bi, 0, p))
    return pl.pallas_call(
        functools.partial(_natten_kernel, rows=rows, kh=kh),
        grid=(pairs, b),
        in_specs=[blk, blk, blk,
                  pl.BlockSpec((2,) + bias.shape[1:], lambda p, bi: (p, 0, 0, 0))],
        out_specs=blk,
        out_shape=jax.ShapeDtypeStruct((b, s, ATTN_WIDTH), BF16),
        scratch_shapes=[pltpu.VMEM((WIN_ROWS_MAX, 2 * GRID_W, kh * GRID_W), F32)],
        compiler_params=_params("arbitrary", "arbitrary"),
        name="natten",
    )(q, k, v, bias)


def _attn_bias_table(rpb):
    c = jnp.arange(GRID_W)
    cs = jnp.clip(c - WIN_COLS // 2, 0, GRID_W - WIN_COLS)
    j = jnp.arange(GRID_W)
    valid = (j[None, :] >= cs[:, None]) & (j[None, :] < cs[:, None] + WIN_COLS)
    dc = j[None, :] - c[:, None] + (WIN_COLS - 1)
    pick = ((dc[None] == jnp.arange(2 * WIN_COLS - 1)[:, None, None]) & valid[None]).astype(F32)
    t = jnp.einsum('hrd,dcj->hrcj', rpb.astype(F32), pick, precision=lax.Precision.HIGHEST)
    t = jnp.where(valid, t * LOG2E, MASK_BIAS)
    return jnp.concatenate([t, t], axis=-1)


def _window_sum(upad, w, tm):
    n = upad.shape[0]
    fwd = upad
    span = 1
    while span < min(w, POOL_HALO):
        fwd = fwd + pltpu.roll(fwd, n - span, axis=0)
        span *= 2
    centre = slice(POOL_HALO, POOL_HALO + tm)
    if w == 2 * POOL_HALO:
        return fwd[0:tm] + fwd[centre]
    return pltpu.roll(fwd, w // 2, axis=0)[centre]


def _mix_kernel(x_ref, u_ref, up_ref, un_ref, ya_ref, ic_ref, wp_ref, ps_ref, wo_ref, g_ref, wr_ref,
                x1_ref, h_ref, aff_ref, upad_ref, wobf_ref, wpbf_ref, *, tm):
    i = pl.program_id(1)

    @pl.when((pl.program_id(0) == 0) & (i == 0))
    def _():
        wobf_ref[...] = wo_ref[...].astype(BF16)
        wpbf_ref[...] = wp_ref[...].astype(BF16)

    upad_ref[0:POOL_HALO, :] = jnp.where(i > 0, up_ref[0], 0.0)
    upad_ref[POOL_HALO:POOL_HALO + tm, :] = u_ref[0]
    upad_ref[POOL_HALO + tm:, :] = jnp.where(i < pl.num_programs(1) - 1, un_ref[0], 0.0)

    ngroups = tm // MIX_ROWS
    group_cols = [slice(gi * POOL_GROUP, (gi + 1) * POOL_GROUP) for gi in range(len(POOL_WINDOWS))]

    def pool_stage(g):
        ds = []
        for cols, w in zip(group_cols, POOL_WINDOWS):
            upad = upad_ref[g * MIX_ROWS:(g + 1) * MIX_ROWS + 2 * POOL_HALO, cols]
            edge = jnp.full((POOL_HALO, POOL_GROUP), 1.0 / w, F32)
            head = jnp.where(i == 0, ic_ref[0:POOL_HALO, cols], 1.0 / w) if g == 0 else edge
            tail = (jnp.where(i == pl.num_programs(1) - 1, ic_ref[POOL_HALO:, cols], 1.0 / w)
                    if g == ngroups - 1 else edge)
            inv = jnp.concatenate(
                [head, jnp.full((MIX_ROWS - 2 * POOL_HALO, POOL_GROUP), 1.0 / w, F32), tail], axis=0)
            d = _window_sum(upad, w, MIX_ROWS) * inv - upad[POOL_HALO:POOL_HALO + MIX_ROWS]
            ds.append(d.astype(BF16))
        return ds

    def proj_stage(g, ds):
        rows = slice(g * MIX_ROWS, (g + 1) * MIX_ROWS)
        ypool = [_dot(d, wpbf_ref[gi]) * ps_ref[:, cols] for gi, (d, cols) in enumerate(zip(ds, group_cols))]
        ypool = jnp.concatenate(ypool, axis=-1).astype(BF16)
        return _dot(ypool, wobf_ref[:POOL_WIDTH, :]) + _dot(ya_ref[0, rows, :], wobf_ref[POOL_WIDTH:, :])

    def route_stage(g, mix):
        rows = slice(g * MIX_ROWS, (g + 1) * MIX_ROWS)
        x1 = x_ref[0, rows, :] + mix
        x1_ref[0, rows, :] = x1
        h = _rms(x1, g_ref[...]).astype(BF16)
        h_ref[0, rows, :] = h
        logits = _dot_nt(wr_ref[...].astype(BF16), h)
        m = jnp.max(logits, axis=0, keepdims=True)
        e = jnp.exp(logits - m)
        aff_ref[0, :, rows] = e / jnp.sum(e, axis=0, keepdims=True)

    pooled, mixed = {}, {}
    for step in range(ngroups + 2):
        if step < ngroups:
            pooled[step] = pool_stage(step)
        if 0 <= step - 1 < ngroups:
            mixed[step - 1] = proj_stage(step - 1, pooled.pop(step - 1))
        if 0 <= step - 2 < ngroups:
            route_stage(step - 2, mixed.pop(step - 2))


def _pool_edge_inverse_counts(s):
    t = jnp.concatenate([jnp.arange(POOL_HALO), jnp.arange(s - POOL_HALO, s)])
    cols = []
    for w in POOL_WINDOWS:
        lo = jnp.clip(t - w // 2, 0, s - 1)
        hi = jnp.clip(t + (w - w // 2) - 1, 0, s - 1)
        inv = 1.0 / (hi - lo + 1).astype(F32)
        cols.append(jnp.broadcast_to(inv[:, None], (2 * POOL_HALO, POOL_GROUP)))
    return jnp.concatenate(cols, axis=-1)


def _mix(x, u, ya, wp, ps, wo, g, wr_t, tm=1024):
    b, s, _ = x.shape
    nt = s // tm
    hb = tm // POOL_HALO
    tile = lambda w: pl.BlockSpec((1, tm, w), lambda bi, i: (bi, i, 0))
    fixed2 = lambda shape: pl.BlockSpec(shape, lambda bi, i: (0, 0))
    return pl.pallas_call(
        functools.partial(_mix_kernel, tm=tm),
        grid=(b, nt),
        in_specs=[
            tile(D_MODEL),
            tile(POOL_WIDTH),
            pl.BlockSpec((1, POOL_HALO, POOL_WIDTH), lambda bi, i: (bi, jnp.maximum(i * hb - 1, 0), 0)),
            pl.BlockSpec((1, POOL_HALO, POOL_WIDTH),
                         lambda bi, i: (bi, jnp.minimum((i + 1) * hb, s // POOL_HALO - 1), 0)),
            tile(ATTN_WIDTH),
            fixed2((2 * POOL_HALO, POOL_WIDTH)),
            pl.BlockSpec(wp.shape, lambda bi, i: (0, 0, 0)),
            fixed2((1, POOL_WIDTH)),
            pl.BlockSpec(wo.shape, lambda bi, i: (0, 0), pipeline_mode=pl.Buffered(1)),
            fixed2((1, D_MODEL)),
            fixed2(wr_t.shape),
        ],
        out_specs=[
            tile(D_MODEL),
            tile(D_MODEL),
            pl.BlockSpec((1, N_EXPERTS, tm), lambda bi, i: (bi, 0, i)),
        ],
        out_shape=[
            jax.ShapeDtypeStruct((b, s, D_MODEL), F32),
            jax.ShapeDtypeStruct((b, s, D_MODEL), BF16),
            jax.ShapeDtypeStruct((b, N_EXPERTS, s), F32),
        ],
        scratch_shapes=[
            pltpu.VMEM((tm + 2 * POOL_HALO, POOL_WIDTH), F32),
            pltpu.VMEM(wo.shape, BF16),
            pltpu.VMEM(wp.shape, BF16),
        ],
        compiler_params=_params("arbitrary", "arbitrary"),
        name="mix",
    )(x, u, u, u, ya, _pool_edge_inverse_counts(s), wp, ps, wo, g, wr_t)


def _lane_cumsum_exclusive(m):
    e, s = m.shape
    r = lax.broadcasted_iota(jnp.int32, (LANES, LANES), 0)
    c = lax.broadcasted_iota(jnp.int32, (LANES, LANES), 1)
    upper = (r < c).astype(BF16)
    carry = jnp.zeros((e, 1), F32)
    out = []
    for blk in range(s // LANES):
        piece = m[:, blk * LANES:(blk + 1) * LANES]
        out.append(_dot(piece.astype(BF16), upper) + carry)
        carry = carry + jnp.sum(piece, axis=-1, keepdims=True)
    return jnp.concatenate(out, axis=-1)


def _route_kernel(aff_ref, slot_ref, slot_t_ref, off_ref, *, cap):
    aff = aff_ref[...]
    capf = jnp.float32(cap)

    def count_ge(cand_bits):
        return jnp.sum((aff >= pltpu.bitcast(cand_bits, F32)).astype(F32), axis=-1, keepdims=True)

    def search(step, ans):
        cand = ans | (jnp.int32(1) << (30 - step))
        return jnp.where(count_ge(cand) >= capf, cand, ans)

    thr = pltpu.bitcast(lax.fori_loop(0, 31, search, jnp.zeros((aff.shape[0], 1), jnp.int32)), F32)
    gt = aff > thr
    eq = aff == thr
    need = capf - jnp.sum(gt.astype(F32), axis=-1, keepdims=True)
    sel = gt | (eq & (_lane_cumsum_exclusive(eq.astype(F32)) < need))
    self = sel.astype(F32)
    slot = jnp.where(sel, _lane_cumsum_exclusive(self), float(NOT_SELECTED))
    slot_ref[...] = slot.astype(jnp.int32)
    pad = jnp.full((LANES - N_EXPERTS, slot.shape[1]), float(NOT_SELECTED), F32)
    for bi in range(slot_t_ref.shape[0]):
        mine = slot[bi * N_EXPERTS:(bi + 1) * N_EXPERTS]
        slot_t_ref[bi] = jnp.concatenate([mine, pad], axis=0).T.astype(jnp.int32)
    lane = lax.broadcasted_iota(jnp.int32, (slot.shape[0], LANES), 1)
    off = jnp.zeros((slot.shape[0], LANES), F32)
    run = jnp.zeros((slot.shape[0], 1), F32)
    for i in range(1, slot.shape[1] // ROUTE_TILE + 1):
        run = run + jnp.sum(self[:, (i - 1) * ROUTE_TILE:i * ROUTE_TILE], axis=-1, keepdims=True)
        off = jnp.where(lane == i, run, off)
    off_ref[...] = off.astype(jnp.int32)


def _route(aff_t, cap):
    b, e, s = aff_t.shape
    assert e == N_EXPERTS and s % ROUTE_TILE == 0 and s // ROUTE_TILE < OFF_STRIDE
    slot, slot_t, off = pl.pallas_call(
        functools.partial(_route_kernel, cap=cap),
        out_shape=[
            jax.ShapeDtypeStruct((b * e, s), jnp.int32),
            jax.ShapeDtypeStruct((b, s, LANES), jnp.int32),
            jax.ShapeDtypeStruct((b * e, LANES), jnp.int32),
        ],
        compiler_params=pltpu.CompilerParams(vmem_limit_bytes=VMEM_LIMIT),
        name="route",
    )(aff_t.reshape(b * e, s))
    return slot.reshape(b, e, s), slot_t, off


def _window_starts(off_ref, bi, ti, cap):
    starts = []
    ok = None
    for e in range(N_EXPERTS):
        base = (bi * N_EXPERTS + e) * OFF_STRIDE + ti
        lo = off_ref[base]
        hi = off_ref[base + 1]
        st = jnp.minimum(jnp.bitwise_and(lo, -16), cap - ROUTE_WIN)
        fits = hi <= st + ROUTE_WIN
        ok = fits if ok is None else jnp.logical_and(ok, fits)
        starts.append(pl.multiple_of(st, 16))
    return starts, ok


def _dispatch_kernel(off_ref, slot_ref, aff_ref, h_ref, xe_ref, gate_ref, *, cap):
    bi = pl.program_id(0)
    ti = pl.program_id(1)

    @pl.when(ti == 0)
    def _():
        xe_ref[...] = jnp.zeros_like(xe_ref)
        gate_ref[...] = jnp.zeros_like(gate_ref)

    nsub = h_ref.shape[1] // ROUTE_TILE
    col_groups = [slice(sub * ROUTE_TILE, (sub + 1) * ROUTE_TILE) for sub in range(nsub)]
    geo = [_window_starts(off_ref, bi, ti * nsub + sub, cap) for sub in range(nsub)]
    ok = functools.reduce(jnp.logical_and, [fits for _, fits in geo])

    @pl.when(ok)
    def _():
        row = lax.broadcasted_iota(jnp.int32, (ROUTE_WIN, ROUTE_TILE), 0)
        for cols, (starts, _) in zip(col_groups, geo):
            slot = slot_ref[0, :, cols]
            aff = aff_ref[0, :, cols]
            hits = [row == (slot[e:e + 1, :] - starts[e]) for e in range(N_EXPERTS)]
            onehot = jnp.concatenate([jnp.where(hh, 1.0, 0.0).astype(BF16) for hh in hits], axis=0)
            res = _dot(onehot, h_ref[0, cols, :])
            for e, hh in enumerate(hits):
                win = pl.ds(starts[e], ROUTE_WIN)
                xe_ref[e, win, :] += res[e * ROUTE_WIN:(e + 1) * ROUTE_WIN].astype(BF16)
                gate_ref[e, win, :] += jnp.sum(jnp.where(hh, aff[e:e + 1, :], 0.0), axis=-1, keepdims=True)

    @pl.when(jnp.logical_not(ok))
    def _():
        row = lax.broadcasted_iota(jnp.int32, (cap, ROUTE_TILE), 0)
        for cols in col_groups:
            slot = slot_ref[0, :, cols]
            aff = aff_ref[0, :, cols]
            h = h_ref[0, cols, :]
            for e in range(N_EXPERTS):
                hh = row == slot[e:e + 1, :]
                xe_ref[e] += _dot(jnp.where(hh, 1.0, 0.0).astype(BF16), h).astype(BF16)
                gate_ref[e] += jnp.sum(jnp.where(hh, aff[e:e + 1, :], 0.0), axis=-1, keepdims=True)


def _dispatch(off, slot, aff_t, h, cap, tm=1024):
    b, e, s = slot.shape
    assert tm % ROUTE_TILE == 0
    rows = pl.BlockSpec((1, e, tm), lambda bi, ti, off_ref: (bi, 0, ti))
    return pl.pallas_call(
        functools.partial(_dispatch_kernel, cap=cap),
        grid_spec=pltpu.PrefetchScalarGridSpec(
            num_scalar_prefetch=1,
            grid=(b, s // tm),
            in_specs=[rows, rows,
                      pl.BlockSpec((1, tm, D_MODEL), lambda bi, ti, off_ref: (bi, ti, 0))],
            out_specs=[
                pl.BlockSpec((e, cap, D_MODEL), lambda bi, ti, off_ref: (0, bi, 0)),
                pl.BlockSpec((e, cap, 1), lambda bi, ti, off_ref: (0, bi, 0)),
            ],
        ),
        out_shape=[
            jax.ShapeDtypeStruct((e, b * cap, D_MODEL), BF16),
            jax.ShapeDtypeStruct((e, b * cap, 1), F32),
        ],
        compiler_params=_params("arbitrary", "arbitrary"),
        name="dispatch",
    )(off, slot, aff_t, h)


def _experts_kernel(xe_ref, gate_ref, wg_ref, wu_ref, wd_ref, y_ref, acc_ref):
    f = pl.program_id(1)

    def hidden_chunk(first, last):
        wg = wg_ref[0].astype(BF16)
        wu = wu_ref[0].astype(BF16)
        wd = wd_ref[0].astype(BF16)
        for mb in range(xe_ref.shape[1] // EXPERT_ROWS):
            rows = slice(mb * EXPERT_ROWS, (mb + 1) * EXPERT_ROWS)
            xe = xe_ref[0, rows, :]
            a = _dot(xe, wg)
            b = _dot(xe, wu)
            part = _dot((a * jax.nn.sigmoid(a) * b).astype(BF16), wd)
            total = part if first else acc_ref[rows, :] + part
            if last:
                y_ref[0, rows, :] = (total * gate_ref[0, rows, :]).astype(y_ref.dtype)
            else:
                acc_ref[rows, :] = total

    nf = pl.num_programs(1)
    pl.when(f == 0)(functools.partial(hidden_chunk, True, False))
    pl.when((f > 0) & (f < nf - 1))(functools.partial(hidden_chunk, False, False))
    pl.when(f == nf - 1)(functools.partial(hidden_chunk, False, True))


def _experts(xe, gate, wg, wu, wd, tf=512):
    e, m, _ = xe.shape
    nf = D_EXPERT // tf
    assert nf >= 2
    return pl.pallas_call(
        _experts_kernel,
        grid=(e, nf),
        in_specs=[
            pl.BlockSpec((1, m, D_MODEL), lambda ei, f: (ei, 0, 0)),
            pl.BlockSpec((1, m, 1), lambda ei, f: (ei, 0, 0)),
            pl.BlockSpec((1, D_MODEL, tf), lambda ei, f: (ei, 0, f)),
            pl.BlockSpec((1, D_MODEL, tf), lambda ei, f: (ei, 0, f)),
            pl.BlockSpec((1, tf, D_MODEL), lambda ei, f: (ei, f, 0)),
        ],
        out_specs=pl.BlockSpec((1, m, D_MODEL), lambda ei, f: (ei, 0, 0)),
        out_shape=jax.ShapeDtypeStruct((e, m, D_MODEL), BF16),
        scratch_shapes=[pltpu.VMEM((m, D_MODEL), F32)],
        compiler_params=_params("arbitrary", "arbitrary"),
        name="experts",
    )(xe, gate, wg, wu, wd)


def _combine_kernel(off_ref, x1_ref, st_ref, y_ref, p_ref, gn_ref, wgbf_ref, wpbf_ref, gp_ref,
                    o_ref, *, cap):
    bi = pl.program_id(0)
    nsub = x1_ref.shape[1] // ROUTE_TILE
    row_groups = [slice(sub * ROUTE_TILE, (sub + 1) * ROUTE_TILE) for sub in range(nsub)]
    geo = [_window_starts(off_ref, bi, pl.program_id(1) * nsub + sub, cap) for sub in range(nsub)]
    ok = functools.reduce(jnp.logical_and, [fits for _, fits in geo])

    def token_slots(rows):
        return st_ref[0, rows, :]

    def windowed_scatter(rows, starts):
        st = token_slots(rows)
        lane = lax.broadcasted_iota(jnp.int32, (ROUTE_TILE, LANES), 1)
        low = lane < ROUTE_WIN
        total = None
        for g in range(N_EXPERTS // ROUTE_GROUP):
            halves = []
            wins = []
            for half in range(ROUTE_GROUP // 2):
                e0 = g * ROUTE_GROUP + 2 * half
                t0 = st[:, e0:e0 + 1] - starts[e0]
                t1 = st[:, e0 + 1:e0 + 2] + (ROUTE_WIN - starts[e0 + 1])
                halves.append(jnp.where(lane == jnp.where(low, t0, t1), 1.0, 0.0).astype(BF16))
                wins.append(y_ref[e0, pl.ds(starts[e0], ROUTE_WIN), :])
                wins.append(y_ref[e0 + 1, pl.ds(starts[e0 + 1], ROUTE_WIN), :])
            part = _dot(jnp.concatenate(halves, axis=1), jnp.concatenate(wins, axis=0))
            total = part if total is None else total + part
        return total

    def dense_scatter(rows):
        st = token_slots(rows)
        lane = lax.broadcasted_iota(jnp.int32, (ROUTE_TILE, cap), 1)
        total = None
        for e in range(N_EXPERTS):
            onehot = jnp.where(lane == st[:, e:e + 1], 1.0, 0.0).astype(BF16)
            part = _dot(onehot, y_ref[e])
            total = part if total is None else total + part
        return total

    def gate_stage(rows, ffn):
        x2 = x1_ref[0, rows, :] + ffn
        return x2, _dot(_rms(x2, gn_ref[...]).astype(BF16), wgbf_ref[...])

    def output_stage(rows, x2, gate_logits):
        emb = _rms(_dot(p_ref[0, rows, :].astype(BF16), wpbf_ref[...]), gp_ref[...])
        o_ref[0, rows, :] = x2 + jax.nn.sigmoid(gate_logits) * emb

    def pipeline(scatter_stage):
        per = PLE_ROWS // ROUTE_TILE
        groups = [slice(g * PLE_ROWS, (g + 1) * PLE_ROWS) for g in range(nsub // per)]
        ffn, gated = {}, {}
        for step in range(len(groups) + 2):
            if step < len(groups):
                ffn[step] = jnp.concatenate([scatter_stage(step * per + j) for j in range(per)], axis=0)
            if 0 <= step - 1 < len(groups):
                gated[step - 1] = gate_stage(groups[step - 1], ffn.pop(step - 1))
            if 0 <= step - 2 < len(groups):
                output_stage(groups[step - 2], *gated.pop(step - 2))

    pl.when(ok)(lambda: pipeline(lambda g: windowed_scatter(row_groups[g], geo[g][0])))
    pl.when(jnp.logical_not(ok))(lambda: pipeline(lambda g: dense_scatter(row_groups[g])))


def _combine(off, x1, slot_t, y, p, gn, wg, wp, gp, cap, tm=1024):
    b, s, _ = x1.shape
    assert tm % PLE_ROWS == 0 and PLE_ROWS % ROUTE_TILE == 0
    tile = lambda w: pl.BlockSpec((1, tm, w), lambda bi, i, off_ref: (bi, i, 0))
    fixed2 = lambda shape: pl.BlockSpec(shape, lambda bi, i, off_ref: (0, 0))
    return pl.pallas_call(
        functools.partial(_combine_kernel, cap=cap),
        grid_spec=pltpu.PrefetchScalarGridSpec(
            num_scalar_prefetch=1,
            grid=(b, s // tm),
            in_specs=[
                tile(D_MODEL),
                tile(LANES),
                pl.BlockSpec((N_EXPERTS, cap, D_MODEL), lambda bi, i, off_ref: (0, bi, 0)),
                tile(PLE_DIM),
                fixed2((1, D_MODEL)),
                pl.BlockSpec(wg.shape, lambda bi, i, off_ref: (0, 0), pipeline_mode=pl.Buffered(1)),
                pl.BlockSpec(wp.shape, lambda bi, i, off_ref: (0, 0), pipeline_mode=pl.Buffered(1)),
                fixed2((1, D_MODEL)),
            ],
            out_specs=tile(D_MODEL),
        ),
        out_shape=jax.ShapeDtypeStruct((b, s, D_MODEL), F32),
        compiler_params=_params("arbitrary", "arbitrary"),
        name="combine",
    )(off, x1, slot_t, y, p, gn, wg.astype(BF16), wp.astype(BF16), gp)


def kernel(x, p, norm_mix, w_in, w_pool, pool_scale, q_norm, k_norm, rpb, w_out, norm_ffn, w_router, w_gate, w_up, w_down, norm_ple, w_ple_gate, w_ple_proj, norm_ple_post):
    b, s, d = x.shape
    depth = w_in.shape[0]
    cap = EC_CAPACITY * s // N_EXPERTS
    row = lambda a: a.reshape(1, -1)
    for i in range(depth):
        u, q, k, v = _in_proj(x.reshape(b * s, d), row(norm_mix[i]), w_in[i],
                              row(jnp.tile(q_norm[i], ATTN_HEADS)), row(jnp.tile(k_norm[i], ATTN_HEADS)))
        shp = lambda a: a.reshape(b, s, -1)
        y_attn = _natten(shp(q), shp(k), shp(v), _attn_bias_table(rpb[i]))
        x1, h, aff_t = _mix(x, shp(u), y_attn, w_pool[i], row(pool_scale[i]), w_out[i],
                            row(norm_ffn[i]), w_router[i].T)
        slot, slot_t, off = _route(aff_t, cap)
        off = off[:, :OFF_STRIDE].reshape(-1)
        xe, gate = _dispatch(off, slot, aff_t, h, cap)
        y = _experts(xe, gate, w_gate[i], w_up[i], w_down[i])
        x = _combine(off, x1, slot_t, y, p[i], row(norm_ple[i]), w_ple_gate[i], w_ple_proj[i],
                     row(norm_ple_post[i]), cap)
    return x
```

```python
import functools

import jax
import jax.numpy as jnp
from jax import lax
from jax.experimental import pallas as pl
from jax.experimental.pallas import tpu as pltpu

D_MODEL = 1024
GRID_W = 64
POOL_WINDOWS = (2, 4, 8, 16)
POOL_WIDTH = D_MODEL // 2
POOL_GROUP = POOL_WIDTH // len(POOL_WINDOWS)
ATTN_HEADS = 8
HEAD_DIM = (D_MODEL // 2) // ATTN_HEADS
ATTN_WIDTH = ATTN_HEADS * HEAD_DIM
WIN_ROWS_MAX = 8
WIN_COLS = 16
N_EXPERTS = 16
EC_CAPACITY = 2
D_EXPERT = 2 * D_MODEL
PLE_DIM = 256
RMS_EPS = 1e-6

LANES = 128
POOL_HALO = 8
LOG2E = 1.4426950408889634
MASK_BIAS = -1e30
NATTEN_UNROLL = 1
ROUTE_TILE = 256
ROUTE_WIN = 64
ROUTE_GROUP = 4
OFF_STRIDE = 16
NOT_SELECTED = -(1 << 20)
assert 2 * ROUTE_WIN == LANES and ROUTE_GROUP % 2 == 0
PLE_ROWS = 256
MIX_ROWS = 512
PROJ_ROWS = 512
EXPERT_ROWS = 1024
VMEM_LIMIT = 56 * 1024 * 1024

BF16 = jnp.bfloat16
F32 = jnp.float32


def _params(*sem):
    return pltpu.CompilerParams(dimension_semantics=sem, vmem_limit_bytes=VMEM_LIMIT)


def _rms(x, g):
    return x * lax.rsqrt(jnp.mean(x * x, axis=-1, keepdims=True) + RMS_EPS) * g


def _dot(a, b):
    return jnp.dot(a, b, preferred_element_type=F32)


def _dot_nt(a, b):
    return lax.dot_general(a, b, (((1,), (1,)), ((), ())), preferred_element_type=F32)


def _in_proj_kernel(x_ref, g_ref, w_ref, qg_ref, kg_ref,
                    u_ref, q_ref, k_ref, v_ref, wbf_ref):
    @pl.when(pl.program_id(0) == 0)
    def _():
        wbf_ref[...] = w_ref[...].astype(BF16)

    def head_norm(t, g):
        low = lax.broadcasted_iota(jnp.int32, (t.shape[0], LANES), 1) < HEAD_DIM
        out = []
        for j in range(t.shape[1] // LANES):
            tile = t[:, j * LANES:(j + 1) * LANES]
            sq = tile * tile
            sa = jnp.sum(jnp.where(low, sq, 0.0), axis=-1, keepdims=True)
            sb = jnp.sum(sq, axis=-1, keepdims=True) - sa
            ra = lax.rsqrt(sa * (1.0 / HEAD_DIM) + RMS_EPS)
            rb = lax.rsqrt(sb * (1.0 / HEAD_DIM) + RMS_EPS)
            out.append(tile * jnp.where(low, ra, rb) * g[:, j * LANES:(j + 1) * LANES])
        return jnp.concatenate(out, axis=-1)

    def norm_stage(rows):
        return _rms(x_ref[rows, :], g_ref[...]).astype(BF16)

    def proj_stage(h):
        return _dot(h, wbf_ref[...])

    def head_stage(rows, z):
        u_ref[rows, :] = z[:, :POOL_WIDTH]
        q = z[:, POOL_WIDTH:POOL_WIDTH + ATTN_WIDTH]
        k = z[:, POOL_WIDTH + ATTN_WIDTH:POOL_WIDTH + 2 * ATTN_WIDTH]
        v = z[:, POOL_WIDTH + 2 * ATTN_WIDTH:]
        q_ref[rows, :] = (head_norm(q, qg_ref[...]) * (HEAD_DIM ** -0.5 * LOG2E)).astype(BF16)
        k_ref[rows, :] = head_norm(k, kg_ref[...]).astype(BF16)
        v_ref[rows, :] = v.astype(BF16)

    groups = [slice(sub * PROJ_ROWS, (sub + 1) * PROJ_ROWS) for sub in range(x_ref.shape[0] // PROJ_ROWS)]
    zs = [proj_stage(norm_stage(rows)) for rows in groups]
    for rows, z in zip(groups, zs):
        head_stage(rows, z)


def _in_proj(x2, g, w, qg, kg, tm=1024):
    n = x2.shape[0]
    zw = w.shape[1]
    assert 2 * HEAD_DIM == LANES
    row = lambda i: (i, 0)
    fixed = lambda i: (0, 0)
    return pl.pallas_call(
        _in_proj_kernel,
        grid=(n // tm,),
        in_specs=[
            pl.BlockSpec((tm, D_MODEL), row),
            pl.BlockSpec((1, D_MODEL), fixed),
            pl.BlockSpec((D_MODEL, zw), fixed, pipeline_mode=pl.Buffered(1)),
            pl.BlockSpec((1, ATTN_WIDTH), fixed),
            pl.BlockSpec((1, ATTN_WIDTH), fixed),
        ],
        out_specs=[
            pl.BlockSpec((tm, POOL_WIDTH), row),
            pl.BlockSpec((tm, ATTN_WIDTH), row),
            pl.BlockSpec((tm, ATTN_WIDTH), row),
            pl.BlockSpec((tm, ATTN_WIDTH), row),
        ],
        out_shape=[
            jax.ShapeDtypeStruct((n, POOL_WIDTH), F32),
            jax.ShapeDtypeStruct((n, ATTN_WIDTH), BF16),
            jax.ShapeDtypeStruct((n, ATTN_WIDTH), BF16),
            jax.ShapeDtypeStruct((n, ATTN_WIDTH), BF16),
        ],
        scratch_shapes=[pltpu.VMEM((D_MODEL, zw), BF16)],
        compiler_params=_params("arbitrary"),
        name="in_proj",
    )(x2, g, w, qg, kg)


def _natten_kernel(q_ref, k_ref, v_ref, tbl_ref, o_ref, bias_ref, *, rows, kh):
    band = kh * GRID_W
    lane = lax.broadcasted_iota(jnp.int32, (GRID_W, LANES), 1)
    first = lane < HEAD_DIM

    @pl.when(pl.program_id(1) == 0)
    def _():
        for hh in range(2):
            for d0 in range(WIN_ROWS_MAX):
                for kk in range(kh // 2):
                    bias_ref[d0, hh * GRID_W:(hh + 1) * GRID_W, kk * LANES:(kk + 1) * LANES] = jnp.where(
                        lane < GRID_W, tbl_ref[hh, d0 + 2 * kk], tbl_ref[hh, d0 + 2 * kk + 1])

    def geometry(g):
        geo = []
        for r in range(g * NATTEN_UNROLL, (g + 1) * NATTEN_UNROLL):
            rs = min(max(r - kh // 2, 0), rows - kh)
            geo.append((rs - r + (WIN_ROWS_MAX - 1), r * GRID_W, rs * GRID_W))
        return geo

    def score_stage(g):
        scores = []
        for d0, q0, k0 in geometry(g):
            qr = q_ref[0, q0:q0 + GRID_W, :]
            zero = jnp.zeros_like(qr)
            q2 = jnp.concatenate([jnp.where(first, qr, zero), jnp.where(first, zero, qr)], axis=0)
            s = _dot_nt(q2, k_ref[0, k0:k0 + band, :]) + bias_ref[d0]
            scores.append((s, jnp.max(s, axis=-1, keepdims=True)))
        return scores

    def softmax_stage(scores):
        probs = []
        for s, m in scores:
            e = jnp.exp2(s - m)
            probs.append((e.astype(BF16), jnp.sum(e, axis=-1, keepdims=True)))
        return probs

    def value_stage(g, probs):
        for (d0, q0, k0), (e, l) in zip(geometry(g), probs):
            o = _dot(e, v_ref[0, k0:k0 + band, :]) / l
            o_ref[0, q0:q0 + GRID_W, :] = jnp.where(first, o[:GRID_W], o[GRID_W:]).astype(o_ref.dtype)

    ngroups = rows // NATTEN_UNROLL
    scores, probs = {}, {}
    for step in range(ngroups + 2):
        if step < ngroups:
            scores[step] = score_stage(step)
        if 0 <= step - 1 < ngroups:
            probs[step - 1] = softmax_stage(scores.pop(step - 1))
        if 0 <= step - 2 < ngroups:
            value_stage(step - 2, probs.pop(step - 2))


def _natten(q, k, v, bias):
    b, s, _ = q.shape
    rows = s // GRID_W
    kh = min(WIN_ROWS_MAX, rows)
    assert rows % NATTEN_UNROLL == 0 and kh % 2 == 0 and 2 * GRID_W == LANES
    pairs = ATTN_WIDTH // LANES
    blk = pl.BlockSpec((1, s, LANES), lambda p, bi: (bi, 0, p))
    return pl.pallas_call(
        functools.partial(_natten_kernel, rows=rows, kh=kh),
        grid=(pairs, b),
        in_specs=[blk, blk, blk,
                  pl.BlockSpec((2,) + bias.shape[1:], lambda p, bi: (p, 0, 0, 0))],
        out_specs=blk,
        out_shape=jax.ShapeDtypeStruct((b, s, ATTN_WIDTH), BF16),
        scratch_shapes=[pltpu.VMEM((WIN_ROWS_MAX, 2 * GRID_W, kh * GRID_W), F32)],
        compiler_params=_params("arbitrary", "arbitrary"),
        name="natten",
    )(q, k, v, bias)


def _attn_bias_table(rpb):
    c = jnp.arange(GRID_W)
    cs = jnp.clip(c - WIN_COLS // 2, 0, GRID_W - WIN_COLS)
    j = jnp.arange(GRID_W)
    valid = (j[None, :] >= cs[:, None]) & (j[None, :] < cs[:, None] + WIN_COLS)
    dc = j[None, :] - c[:, None] + (WIN_COLS - 1)
    pick = ((dc[None] == jnp.arange(2 * WIN_COLS - 1)[:, None, None]) & valid[None]).astype(F32)
    t = jnp.einsum('hrd,dcj->hrcj', rpb.astype(F32), pick, precision=lax.Precision.HIGHEST)
    t = jnp.where(valid, t * LOG2E, MASK_BIAS)
    return jnp.concatenate([t, t], axis=-1)


def _window_sum(upad, w, tm):
    n = upad.shape[0]
    fwd = upad
    span = 1
    while span < min(w, POOL_HALO):
        fwd = fwd + pltpu.roll(fwd, n - span, axis=0)
        span *= 2
    centre = slice(POOL_HALO, POOL_HALO + tm)
    if w == 2 * POOL_HALO:
        return fwd[0:tm] + fwd[centre]
    return pltpu.roll(fwd, w // 2, axis=0)[centre]


def _mix_kernel(x_ref, u_ref, up_ref, un_ref, ya_ref, ic_ref, wp_ref, ps_ref, wo_ref, g_ref, wr_ref,
                x1_ref, h_ref, aff_ref, upad_ref, wobf_ref, wpbf_ref, *, tm):
    i = pl.program_id(1)

    @pl.when((pl.program_id(0) == 0) & (i == 0))
    def _():
        wobf_ref[...] = wo_ref[...].astype(BF16)
        wpbf_ref[...] = wp_ref[...].astype(BF16)

    upad_ref[0:POOL_HALO, :] = jnp.where(i > 0, up_ref[0], 0.0)
    upad_ref[POOL_HALO:POOL_HALO + tm, :] = u_ref[0]
    upad_ref[POOL_HALO + tm:, :] = jnp.where(i < pl.num_programs(1) - 1, un_ref[0], 0.0)

    ngroups = tm // MIX_ROWS
    group_cols = [slice(gi * POOL_GROUP, (gi + 1) * POOL_GROUP) for gi in range(len(POOL_WINDOWS))]

    def pool_stage(g):
        ds = []
        for cols, w in zip(group_cols, POOL_WINDOWS):
            upad = upad_ref[g * MIX_ROWS:(g + 1) * MIX_ROWS + 2 * POOL_HALO, cols]
            edge = jnp.full((POOL_HALO, POOL_GROUP), 1.0 / w, F32)
            head = jnp.where(i == 0, ic_ref[0:POOL_HALO, cols], 1.0 / w) if g == 0 else edge
            tail = (jnp.where(i == pl.num_programs(1) - 1, ic_ref[POOL_HALO:, cols], 1.0 / w)
                    if g == ngroups - 1 else edge)
            inv = jnp.concatenate(
                [head, jnp.full((MIX_ROWS - 2 * POOL_HALO, POOL_GROUP), 1.0 / w, F32), tail], axis=0)
            d = _window_sum(upad, w, MIX_ROWS) * inv - upad[POOL_HALO:POOL_HALO + MIX_ROWS]
            ds.append(d.astype(BF16))
        return ds

    def proj_stage(g, ds):
        rows = slice(g * MIX_ROWS, (g + 1) * MIX_ROWS)
        ypool = [_dot(d, wpbf_ref[gi]) * ps_ref[:, cols] for gi, (d, cols) in enumerate(zip(ds, group_cols))]
        ypool = jnp.concatenate(ypool, axis=-1).astype(BF16)
        return _dot(ypool, wobf_ref[:POOL_WIDTH, :]) + _dot(ya_ref[0, rows, :], wobf_ref[POOL_WIDTH:, :])

    def route_stage(g, mix):
        rows = slice(g * MIX_ROWS, (g + 1) * MIX_ROWS)
        x1 = x_ref[0, rows, :] + mix
        x1_ref[0, rows, :] = x1
        h = _rms(x1, g_ref[...]).astype(BF16)
        h_ref[0, rows, :] = h
        logits = _dot_nt(wr_ref[...].astype(BF16), h)
        m = jnp.max(logits, axis=0, keepdims=True)
        e = jnp.exp(logits - m)
        aff_ref[0, :, rows] = e / jnp.sum(e, axis=0, keepdims=True)

    pooled, mixed = {}, {}
    for step in range(ngroups + 2):
        if step < ngroups:
            pooled[step] = pool_stage(step)
        if 0 <= step - 1 < ngroups:
            mixed[step - 1] = proj_stage(step - 1, pooled.pop(step - 1))
        if 0 <= step - 2 < ngroups:
            route_stage(step - 2, mixed.pop(step - 2))


def _pool_edge_inverse_counts(s):
    t = jnp.concatenate([jnp.arange(POOL_HALO), jnp.arange(s - POOL_HALO, s)])
    cols = []
    for w in POOL_WINDOWS:
        lo = jnp.clip(t - w // 2, 0, s - 1)
        hi = jnp.clip(t + (w - w // 2) - 1, 0, s - 1)
        inv = 1.0 / (hi - lo + 1).astype(F32)
        cols.append(jnp.broadcast_to(inv[:, None], (2 * POOL_HALO, POOL_GROUP)))
    return jnp.concatenate(cols, axis=-1)


def _mix(x, u, ya, wp, ps, wo, g, wr_t, tm=1024):
    b, s, _ = x.shape
    nt = s // tm
    hb = tm // POOL_HALO
    tile = lambda w: pl.BlockSpec((1, tm, w), lambda bi, i: (bi, i, 0))
    fixed2 = lambda shape: pl.BlockSpec(shape, lambda bi, i: (0, 0))
    return pl.pallas_call(
        functools.partial(_mix_kernel, tm=tm),
        grid=(b, nt),
        in_specs=[
            tile(D_MODEL),
            tile(POOL_WIDTH),
            pl.BlockSpec((1, POOL_HALO, POOL_WIDTH), lambda bi, i: (bi, jnp.maximum(i * hb - 1, 0), 0)),
            pl.BlockSpec((1, POOL_HALO, POOL_WIDTH),
                         lambda bi, i: (bi, jnp.minimum((i + 1) * hb, s // POOL_HALO - 1), 0)),
            tile(ATTN_WIDTH),
            fixed2((2 * POOL_HALO, POOL_WIDTH)),
            pl.BlockSpec(wp.shape, lambda bi, i: (0, 0, 0)),
            fixed2((1, POOL_WIDTH)),
            pl.BlockSpec(wo.shape, lambda bi, i: (0, 0), pipeline_mode=pl.Buffered(1)),
            fixed2((1, D_MODEL)),
            fixed2(wr_t.shape),
        ],
        out_specs=[
            tile(D_MODEL),
            tile(D_MODEL),
            pl.BlockSpec((1, N_EXPERTS, tm), lambda bi, i: (bi, 0, i)),
        ],
        out_shape=[
            jax.ShapeDtypeStruct((b, s, D_MODEL), F32),
            jax.ShapeDtypeStruct((b, s, D_MODEL), BF16),
            jax.ShapeDtypeStruct((b, N_EXPERTS, s), F32),
        ],
        scratch_shapes=[
            pltpu.VMEM((tm + 2 * POOL_HALO, POOL_WIDTH), F32),
            pltpu.VMEM(wo.shape, BF16),
            pltpu.VMEM(wp.shape, BF16),
        ],
        compiler_params=_params("arbitrary", "arbitrary"),
        name="mix",
    )(x, u, u, u, ya, _pool_edge_inverse_counts(s), wp, ps, wo, g, wr_t)


def _lane_cumsum_exclusive(m):
    e, s = m.shape
    r = lax.broadcasted_iota(jnp.int32, (LANES, LANES), 0)
    c = lax.broadcasted_iota(jnp.int32, (LANES, LANES), 1)
    upper = (r < c).astype(BF16)
    carry = jnp.zeros((e, 1), F32)
    out = []
    for blk in range(s // LANES):
        piece = m[:, blk * LANES:(blk + 1) * LANES]
        out.append(_dot(piece.astype(BF16), upper) + carry)
        carry = carry + jnp.sum(piece, axis=-1, keepdims=True)
    return jnp.concatenate(out, axis=-1)


def _route_kernel(aff_ref, slot_ref, slot_t_ref, off_ref, *, cap):
    aff = aff_ref[...]
    capf = jnp.float32(cap)

    def count_ge(cand_bits):
        return jnp.sum((aff >= pltpu.bitcast(cand_bits, F32)).astype(F32), axis=-1, keepdims=True)

    def search(step, ans):
        cand = ans | (jnp.int32(1) << (30 - step))
        return jnp.where(count_ge(cand) >= capf, cand, ans)

    thr = pltpu.bitcast(lax.fori_loop(0, 31, search, jnp.zeros((aff.shape[0], 1), jnp.int32)), F32)
    gt = aff > thr
    eq = aff == thr
    need = capf - jnp.sum(gt.astype(F32), axis=-1, keepdims=True)
    sel = gt | (eq & (_lane_cumsum_exclusive(eq.astype(F32)) < need))
    self = sel.astype(F32)
    slot = jnp.where(sel, _lane_cumsum_exclusive(self), float(NOT_SELECTED))
    slot_ref[...] = slot.astype(jnp.int32)
    pad = jnp.full((LANES - N_EXPERTS, slot.shape[1]), float(NOT_SELECTED), F32)
    for bi in range(slot_t_ref.shape[0]):
        mine = slot[bi * N_EXPERTS:(bi + 1) * N_EXPERTS]
        slot_t_ref[bi] = jnp.concatenate([mine, pad], axis=0).T.astype(jnp.int32)
    lane = lax.broadcasted_iota(jnp.int32, (slot.shape[0], LANES), 1)
    off = jnp.zeros((slot.shape[0], LANES), F32)
    run = jnp.zeros((slot.shape[0], 1), F32)
    for i in range(1, slot.shape[1] // ROUTE_TILE + 1):
        run = run + jnp.sum(self[:, (i - 1) * ROUTE_TILE:i * ROUTE_TILE], axis=-1, keepdims=True)
        off = jnp.where(lane == i, run, off)
    off_ref[...] = off.astype(jnp.int32)


def _route(aff_t, cap):
    b, e, s = aff_t.shape
    assert e == N_EXPERTS and s % ROUTE_TILE == 0 and s // ROUTE_TILE < OFF_STRIDE
    slot, slot_t, off = pl.pallas_call(
        functools.partial(_route_kernel, cap=cap),
        out_shape=[
            jax.ShapeDtypeStruct((b * e, s), jnp.int32),
            jax.ShapeDtypeStruct((b, s, LANES), jnp.int32),
            jax.ShapeDtypeStruct((b * e, LANES), jnp.int32),
        ],
        compiler_params=pltpu.CompilerParams(vmem_limit_bytes=VMEM_LIMIT),
        name="route",
    )(aff_t.reshape(b * e, s))
    return slot.reshape(b, e, s), slot_t, off


def _window_starts(off_ref, bi, ti, cap):
    starts = []
    ok = None
    for e in range(N_EXPERTS):
        base = (bi * N_EXPERTS + e) * OFF_STRIDE + ti
        lo = off_ref[base]
        hi = off_ref[base + 1]
        st = jnp.minimum(jnp.bitwise_and(lo, -16), cap - ROUTE_WIN)
        fits = hi <= st + ROUTE_WIN
        ok = fits if ok is None else jnp.logical_and(ok, fits)
        starts.append(pl.multiple_of(st, 16))
    return starts, ok


def _dispatch_kernel(off_ref, slot_ref, aff_ref, h_ref, xe_ref, gate_ref, *, cap):
    bi = pl.program_id(0)
    ti = pl.program_id(1)

    @pl.when(ti == 0)
    def _():
        xe_ref[...] = jnp.zeros_like(xe_ref)
        gate_ref[...] = jnp.zeros_like(gate_ref)

    nsub = h_ref.shape[1] // ROUTE_TILE
    col_groups = [slice(sub * ROUTE_TILE, (sub + 1) * ROUTE_TILE) for sub in range(nsub)]
    geo = [_window_starts(off_ref, bi, ti * nsub + sub, cap) for sub in range(nsub)]
    ok = functools.reduce(jnp.logical_and, [fits for _, fits in geo])

    @pl.when(ok)
    def _():
        row = lax.broadcasted_iota(jnp.int32, (ROUTE_WIN, ROUTE_TILE), 0)
        for cols, (starts, _) in zip(col_groups, geo):
            slot = slot_ref[0, :, cols]
            aff = aff_ref[0, :, cols]
            hits = [row == (slot[e:e + 1, :] - starts[e]) for e in range(N_EXPERTS)]
            onehot = jnp.concatenate([jnp.where(hh, 1.0, 0.0).astype(BF16) for hh in hits], axis=0)
            res = _dot(onehot, h_ref[0, cols, :])
            for e, hh in enumerate(hits):
                win = pl.ds(starts[e], ROUTE_WIN)
                xe_ref[e, win, :] += res[e * ROUTE_WIN:(e + 1) * ROUTE_WIN].astype(BF16)
                gate_ref[e, win, :] += jnp.sum(jnp.where(hh, aff[e:e + 1, :], 0.0), axis=-1, keepdims=True)

    @pl.when(jnp.logical_not(ok))
    def _():
        row = lax.broadcasted_iota(jnp.int32, (cap, ROUTE_TILE), 0)
        for cols in col_groups:
            slot = slot_ref[0, :, cols]
            aff = aff_ref[0, :, cols]
            h = h_ref[0, cols, :]
            for e in range(N_EXPERTS):
                hh = row == slot[e:e + 1, :]
                xe_ref[e] += _dot(jnp.where(hh, 1.0, 0.0).astype(BF16), h).astype(BF16)
                gate_ref[e] += jnp.sum(jnp.where(hh, aff[e:e + 1, :], 0.0), axis=-1, keepdims=True)


def _dispatch(off, slot, aff_t, h, cap, tm=1024):
    b, e, s = slot.shape
    assert tm % ROUTE_TILE == 0
    rows = pl.BlockSpec((1, e, tm), lambda bi, ti, off_ref: (bi, 0, ti))
    return pl.pallas_call(
        functools.partial(_dispatch_kernel, cap=cap),
        grid_spec=pltpu.PrefetchScalarGridSpec(
            num_scalar_prefetch=1,
            grid=(b, s // tm),
            in_specs=[rows, rows,
                      pl.BlockSpec((1, tm, D_MODEL), lambda bi, ti, off_ref: (bi, ti, 0))],
            out_specs=[
                pl.BlockSpec((e, cap, D_MODEL), lambda bi, ti, off_ref: (0, bi, 0)),
                pl.BlockSpec((e, cap, 1), lambda bi, ti, off_ref: (0, bi, 0)),
            ],
        ),
        out_shape=[
            jax.ShapeDtypeStruct((e, b * cap, D_MODEL), BF16),
            jax.ShapeDtypeStruct((e, b * cap, 1), F32),
        ],
        compiler_params=_params("arbitrary", "arbitrary"),
        name="dispatch",
    )(off, slot, aff_t, h)


def _experts_kernel(xe_ref, gate_ref, wg_ref, wu_ref, wd_ref, y_ref, acc_ref):
    f = pl.program_id(1)

    def hidden_chunk(first, last):
        wg = wg_ref[0].astype(BF16)
        wu = wu_ref[0].astype(BF16)
        wd = wd_ref[0].astype(BF16)
        for mb in range(xe_ref.shape[1] // EXPERT_ROWS):
            rows = slice(mb * EXPERT_ROWS, (mb + 1) * EXPERT_ROWS)
            xe = xe_ref[0, rows, :]
            a = _dot(xe, wg)
            b = _dot(xe, wu)
            part = _dot((a * jax.nn.sigmoid(a) * b).astype(BF16), wd)
            total = part if first else acc_ref[rows, :] + part
            if last:
                y_ref[0, rows, :] = (total * gate_ref[0, rows, :]).astype(y_ref.dtype)
            else:
                acc_ref[rows, :] = total

    nf = pl.num_programs(1)
    pl.when(f == 0)(functools.partial(hidden_chunk, True, False))
    pl.when((f > 0) & (f < nf - 1))(functools.partial(hidden_chunk, False, False))
    pl.when(f == nf - 1)(functools.partial(hidden_chunk, False, True))


def _experts(xe, gate, wg, wu, wd, tf=512):
    e, m, _ = xe.shape
    nf = D_EXPERT // tf
    assert nf >= 2
    return pl.pallas_call(
        _experts_kernel,
        grid=(e, nf),
        in_specs=[
            pl.BlockSpec((1, m, D_MODEL), lambda ei, f: (ei, 0, 0)),
            pl.BlockSpec((1, m, 1), lambda ei, f: (ei, 0, 0)),
            pl.BlockSpec((1, D_MODEL, tf), lambda ei, f: (ei, 0, f)),
            pl.BlockSpec((1, D_MODEL, tf), lambda ei, f: (ei, 0, f)),
            pl.BlockSpec((1, tf, D_MODEL), lambda ei, f: (ei, f, 0)),
        ],
        out_specs=pl.BlockSpec((1, m, D_MODEL), lambda ei, f: (ei, 0, 0)),
        out_shape=jax.ShapeDtypeStruct((e, m, D_MODEL), BF16),
        scratch_shapes=[pltpu.VMEM((m, D_MODEL), F32)],
        compiler_params=_params("arbitrary", "arbitrary"),
        name="experts",
    )(xe, gate, wg, wu, wd)


def _combine_kernel(off_ref, x1_ref, st_ref, y_ref, p_ref, gn_ref, wgbf_ref, wpbf_ref, gp_ref,
                    o_ref, *, cap):
    bi = pl.program_id(0)
    nsub = x1_ref.shape[1] // ROUTE_TILE
    row_groups = [slice(sub * ROUTE_TILE, (sub + 1) * ROUTE_TILE) for sub in range(nsub)]
    geo = [_window_starts(off_ref, bi, pl.program_id(1) * nsub + sub, cap) for sub in range(nsub)]
    ok = functools.reduce(jnp.logical_and, [fits for _, fits in geo])

    def token_slots(rows):
        return st_ref[0, rows, :]

    def windowed_scatter(rows, starts):
        st = token_slots(rows)
        lane = lax.broadcasted_iota(jnp.int32, (ROUTE_TILE, LANES), 1)
        low = lane < ROUTE_WIN
        total = None
        for g in range(N_EXPERTS // ROUTE_GROUP):
            halves = []
            wins = []
            for half in range(ROUTE_GROUP // 2):
                e0 = g * ROUTE_GROUP + 2 * half
                t0 = st[:, e0:e0 + 1] - starts[e0]
                t1 = st[:, e0 + 1:e0 + 2] + (ROUTE_WIN - starts[e0 + 1])
                halves.append(jnp.where(lane == jnp.where(low, t0, t1), 1.0, 0.0).astype(BF16))
                wins.append(y_ref[e0, pl.ds(starts[e0], ROUTE_WIN), :])
                wins.append(y_ref[e0 + 1, pl.ds(starts[e0 + 1], ROUTE_WIN), :])
            part = _dot(jnp.concatenate(halves, axis=1), jnp.concatenate(wins, axis=0))
            total = part if total is None else total + part
        return total

    def dense_scatter(rows):
        st = token_slots(rows)
        lane = lax.broadcasted_iota(jnp.int32, (ROUTE_TILE, cap), 1)
        total = None
        for e in range(N_EXPERTS):
            onehot = jnp.where(lane == st[:, e:e + 1], 1.0, 0.0).astype(BF16)
            part = _dot(onehot, y_ref[e])
            total = part if total is None else total + part
        return total

    def gate_stage(rows, ffn):
        x2 = x1_ref[0, rows, :] + ffn
        return x2, _dot(_rms(x2, gn_ref[...]).astype(BF16), wgbf_ref[...])

    def output_stage(rows, x2, gate_logits):
        emb = _rms(_dot(p_ref[0, rows, :].astype(BF16), wpbf_ref[...]), gp_ref[...])
        o_ref[0, rows, :] = x2 + jax.nn.sigmoid(gate_logits) * emb

    def pipeline(scatter_stage):
        per = PLE_ROWS // ROUTE_TILE
        groups = [slice(g * PLE_ROWS, (g + 1) * PLE_ROWS) for g in range(nsub // per)]
        ffn, gated = {}, {}
        for step in range(len(groups) + 2):
            if step < len(groups):
                ffn[step] = jnp.concatenate([scatter_stage(step * per + j) for j in range(per)], axis=0)
            if 0 <= step - 1 < len(groups):
                gated[step - 1] = gate_stage(groups[step - 1], ffn.pop(step - 1))
            if 0 <= step - 2 < len(groups):
                output_stage(groups[step - 2], *gated.pop(step - 2))

    pl.when(ok)(lambda: pipeline(lambda g: windowed_scatter(row_groups[g], geo[g][0])))
    pl.when(jnp.logical_not(ok))(lambda: pipeline(lambda g: dense_scatter(row_groups[g])))


def _combine(off, x1, slot_t, y, p, gn, wg, wp, gp, cap, tm=1024):
    b, s, _ = x1.shape
    assert tm % PLE_ROWS == 0 and PLE_ROWS % ROUTE_TILE == 0
    tile = lambda w: pl.BlockSpec((1, tm, w), lambda bi, i, off_ref: (bi, i, 0))
    fixed2 = lambda shape: pl.BlockSpec(shape, lambda bi, i, off_ref: (0, 0))
    return pl.pallas_call(
        functools.partial(_combine_kernel, cap=cap),
        grid_spec=pltpu.PrefetchScalarGridSpec(
            num_scalar_prefetch=1,
            grid=(b, s // tm),
            in_specs=[
                tile(D_MODEL),
                tile(LANES),
                pl.BlockSpec((N_EXPERTS, cap, D_MODEL), lambda bi, i, off_ref: (0, bi, 0)),
                tile(PLE_DIM),
                fixed2((1, D_MODEL)),
                pl.BlockSpec(wg.shape, lambda bi, i, off_ref: (0, 0), pipeline_mode=pl.Buffered(1)),
                pl.BlockSpec(wp.shape, lambda bi, i, off_ref: (0, 0), pipeline_mode=pl.Buffered(1)),
                fixed2((1, D_MODEL)),
            ],
            out_specs=tile(D_MODEL),
        ),
        out_shape=jax.ShapeDtypeStruct((b, s, D_MODEL), F32),
        compiler_params=_params("arbitrary", "arbitrary"),
        name="combine",
    )(off, x1, slot_t, y, p, gn, wg.astype(BF16), wp.astype(BF16), gp)


def kernel(x, p, norm_mix, w_in, w_pool, pool_scale, q_norm, k_norm, rpb, w_out, norm_ffn, w_router, w_gate, w_up, w_down, norm_ple, w_ple_gate, w_ple_proj, norm_ple_post):
    b, s, d = x.shape
    depth = w_in.shape[0]
    cap = EC_CAPACITY * s // N_EXPERTS
    row = lambda a: a.reshape(1, -1)
    for i in range(depth):
        u, q, k, v = _in_proj(x.reshape(b * s, d), row(norm_mix[i]), w_in[i],
                              row(jnp.tile(q_norm[i], ATTN_HEADS)), row(jnp.tile(k_norm[i], ATTN_HEADS)))
        shp = lambda a: a.reshape(b, s, -1)
        y_attn = _natten(shp(q), shp(k), shp(v), _attn_bias_table(rpb[i]))
        x1, h, aff_t = _mix(x, shp(u), y_attn, w_pool[i], row(pool_scale[i]), w_out[i],
                            row(norm_ffn[i]), w_router[i].T)
        slot, slot_t, off = _route(aff_t, cap)
        off = off[:, :OFF_STRIDE].reshape(-1)
        xe, gate = _dispatch(off, slot, aff_t, h, cap)
        y = _experts(xe, gate, w_gate[i], w_up[i], w_down[i])
        x = _combine(off, x1, slot_t, y, p[i], row(norm_ple[i]), w_ple_gate[i], w_ple_proj[i],
                     row(norm_ple_post[i]), cap)
    return x
```

```python
import functools

import jax
import jax.numpy as jnp
from jax import lax
from jax.experimental import pallas as pl
from jax.experimental.pallas import tpu as pltpu

D_MODEL = 1024
GRID_W = 64
POOL_WINDOWS = (2, 4, 8, 16)
POOL_WIDTH = D_MODEL // 2
POOL_GROUP = POOL_WIDTH // len(POOL_WINDOWS)
ATTN_HEADS = 8
HEAD_DIM = (D_MODEL // 2) // ATTN_HEADS
ATTN_WIDTH = ATTN_HEADS * HEAD_DIM
WIN_ROWS_MAX = 8
WIN_COLS = 16
N_EXPERTS = 16
EC_CAPACITY = 2
D_EXPERT = 2 * D_MODEL
PLE_DIM = 256
RMS_EPS = 1e-6

LANES = 128
POOL_HALO = 8
LOG2E = 1.4426950408889634
MASK_BIAS = -1e30
NATTEN_UNROLL = 2
ROUTE_TILE = 256
ROUTE_WIN = 64
ROUTE_GROUP = 4
OFF_STRIDE = 16
NOT_SELECTED = -(1 << 20)
assert 2 * ROUTE_WIN == LANES and ROUTE_GROUP % 2 == 0
PLE_ROWS = 256
MIX_ROWS = 512
PROJ_ROWS = 1024
EXPERT_ROWS = 1024
VMEM_LIMIT = 56 * 1024 * 1024

BF16 = jnp.bfloat16
F32 = jnp.float32


def _params(*sem):
    return pltpu.CompilerParams(dimension_semantics=sem, vmem_limit_bytes=VMEM_LIMIT)


def _rms(x, g):
    return x * lax.rsqrt(jnp.mean(x * x, axis=-1, keepdims=True) + RMS_EPS) * g


def _dot(a, b):
    return jnp.dot(a, b, preferred_element_type=F32)


def _dot_nt(a, b):
    return lax.dot_general(a, b, (((1,), (1,)), ((), ())), preferred_element_type=F32)


def _in_proj_kernel(x_ref, g_ref, w_ref, qg_ref, kg_ref,
                    u_ref, q_ref, k_ref, v_ref, wbf_ref):
    @pl.when(pl.program_id(0) == 0)
    def _():
        wbf_ref[...] = w_ref[...].astype(BF16)

    def head_norm(t, g):
        low = lax.broadcasted_iota(jnp.int32, (t.shape[0], LANES), 1) < HEAD_DIM
        out = []
        for j in range(t.shape[1] // LANES):
            tile = t[:, j * LANES:(j + 1) * LANES]
            sq = tile * tile
            sa = jnp.sum(jnp.where(low, sq, 0.0), axis=-1, keepdims=True)
            sb = jnp.sum(sq, axis=-1, keepdims=True) - sa
            ra = lax.rsqrt(sa * (1.0 / HEAD_DIM) + RMS_EPS)
            rb = lax.rsqrt(sb * (1.0 / HEAD_DIM) + RMS_EPS)
            out.append(tile * jnp.where(low, ra, rb) * g[:, j * LANES:(j + 1) * LANES])
        return jnp.concatenate(out, axis=-1)

    def norm_stage(rows):
        return _rms(x_ref[rows, :], g_ref[...]).astype(BF16)

    def proj(h, lo, width):
        return _dot(h, wbf_ref[:, lo:lo + width])

    q_lo, k_lo, v_lo = POOL_WIDTH, POOL_WIDTH + ATTN_WIDTH, POOL_WIDTH + 2 * ATTN_WIDTH
    for sub in range(x_ref.shape[0] // PROJ_ROWS):
        rows = slice(sub * PROJ_ROWS, (sub + 1) * PROJ_ROWS)
        h = norm_stage(rows)
        u_ref[rows, :] = proj(h, 0, POOL_WIDTH)
        q_ref[rows, :] = (head_norm(proj(h, q_lo, ATTN_WIDTH), qg_ref[...])
                          * (HEAD_DIM ** -0.5 * LOG2E)).astype(BF16)
        k_ref[rows, :] = head_norm(proj(h, k_lo, ATTN_WIDTH), kg_ref[...]).astype(BF16)
        v_ref[rows, :] = proj(h, v_lo, ATTN_WIDTH).astype(BF16)


def _in_proj(x2, g, w, qg, kg, tm=1024):
    n = x2.shape[0]
    zw = w.shape[1]
    assert 2 * HEAD_DIM == LANES
    row = lambda i: (i, 0)
    fixed = lambda i: (0, 0)
    return pl.pallas_call(
        _in_proj_kernel,
        grid=(n // tm,),
        in_specs=[
            pl.BlockSpec((tm, D_MODEL), row),
            pl.BlockSpec((1, D_MODEL), fixed),
            pl.BlockSpec((D_MODEL, zw), fixed, pipeline_mode=pl.Buffered(1)),
            pl.BlockSpec((1, ATTN_WIDTH), fixed),
            pl.BlockSpec((1, ATTN_WIDTH), fixed),
        ],
        out_specs=[
            pl.BlockSpec((tm, POOL_WIDTH), row),
            pl.BlockSpec((tm, ATTN_WIDTH), row),
            pl.BlockSpec((tm, ATTN_WIDTH), row),
            pl.BlockSpec((tm, ATTN_WIDTH), row),
        ],
        out_shape=[
            jax.ShapeDtypeStruct((n, POOL_WIDTH), F32),
            jax.ShapeDtypeStruct((n, ATTN_WIDTH), BF16),
            jax.ShapeDtypeStruct((n, ATTN_WIDTH), BF16),
            jax.ShapeDtypeStruct((n, ATTN_WIDTH), BF16),
        ],
        scratch_shapes=[pltpu.VMEM((D_MODEL, zw), BF16)],
        compiler_params=_params("arbitrary"),
        name="in_proj",
    )(x2, g, w, qg, kg)


def _natten_kernel(q_ref, k_ref, v_ref, tbl_ref, o_ref, bias_ref, *, rows, kh):
    band = kh * GRID_W
    lane = lax.broadcasted_iota(jnp.int32, (GRID_W, LANES), 1)
    first = lane < HEAD_DIM

    @pl.when(pl.program_id(1) == 0)
    def _():
        for hh in range(2):
            for d0 in range(WIN_ROWS_MAX):
                for kk in range(kh // 2):
                    bias_ref[d0, hh * GRID_W:(hh + 1) * GRID_W, kk * LANES:(kk + 1) * LANES] = jnp.where(
                        lane < GRID_W, tbl_ref[hh, d0 + 2 * kk], tbl_ref[hh, d0 + 2 * kk + 1])

    def geometry(g):
        geo = []
        for r in range(g * NATTEN_UNROLL, (g + 1) * NATTEN_UNROLL):
            rs = min(max(r - kh // 2, 0), rows - kh)
            geo.append((rs - r + (WIN_ROWS_MAX - 1), r * GRID_W, rs * GRID_W))
        return geo

    def score_stage(g):
        scores = []
        for d0, q0, k0 in geometry(g):
            qr = q_ref[0, q0:q0 + GRID_W, :]
            zero = jnp.zeros_like(qr)
            q2 = jnp.concatenate([jnp.where(first, qr, zero), jnp.where(first, zero, qr)], axis=0)
            s = _dot_nt(q2, k_ref[0, k0:k0 + band, :]) + bias_ref[d0]
            scores.append((s, jnp.max(s, axis=-1, keepdims=True)))
        return scores

    def softmax_stage(scores):
        probs = []
        for s, m in scores:
            e = jnp.exp2(s - m)
            probs.append((e.astype(BF16), jnp.sum(e, axis=-1, keepdims=True)))
        return probs

    def value_stage(g, probs):
        for (d0, q0, k0), (e, l) in zip(geometry(g), probs):
            o = _dot(e, v_ref[0, k0:k0 + band, :]) / l
            o_ref[0, q0:q0 + GRID_W, :] = jnp.where(first, o[:GRID_W], o[GRID_W:]).astype(o_ref.dtype)

    ngroups = rows // NATTEN_UNROLL
    scores, probs = {}, {}
    for step in range(ngroups + 2):
        if step < ngroups:
            scores[step] = score_stage(step)
        if 0 <= step - 1 < ngroups:
            probs[step - 1] = softmax_stage(scores.pop(step - 1))
        if 0 <= step - 2 < ngroups:
            value_stage(step - 2, probs.pop(step - 2))


def _natten(q, k, v, bias):
    b, s, _ = q.shape
    rows = s // GRID_W
    kh = min(WIN_ROWS_MAX, rows)
    assert rows % NATTEN_UNROLL == 0 and kh % 2 == 0 and 2 * GRID_W == LANES
    pairs = ATTN_WIDTH // LANES
    blk = pl.BlockSpec((1, s, LANES), lambda p, bi: (bi, 0, p))
    return pl.pallas_call(
        functools.partial(_natten_kernel, rows=rows, kh=kh),
        grid=(pairs, b),
        in_specs=[blk, blk, blk,
                  pl.BlockSpec((2,) + bias.shape[1:], lambda p, bi: (p, 0, 0, 0))],
        out_specs=blk,
        out_shape=jax.ShapeDtypeStruct((b, s, ATTN_WIDTH), BF16),
        scratch_shapes=[pltpu.VMEM((WIN_ROWS_MAX, 2 * GRID_W, kh * GRID_W), F32)],
        compiler_params=_params("arbitrary", "arbitrary"),
        name="natten",
    )(q, k, v, bias)


def _attn_bias_table(rpb):
    c = jnp.arange(GRID_W)
    cs = jnp.clip(c - WIN_COLS // 2, 0, GRID_W - WIN_COLS)
    j = jnp.arange(GRID_W)
    valid = (j[None, :] >= cs[:, None]) & (j[None, :] < cs[:, None] + WIN_COLS)
    dc = j[None, :] - c[:, None] + (WIN_COLS - 1)
    pick = ((dc[None] == jnp.arange(2 * WIN_COLS - 1)[:, None, None]) & valid[None]).astype(F32)
    t = jnp.einsum('hrd,dcj->hrcj', rpb.astype(F32), pick, precision=lax.Precision.HIGHEST)
    t = jnp.where(valid, t * LOG2E, MASK_BIAS)
    return jnp.concatenate([t, t], axis=-1)


def _window_sum(upad, w, tm):
    n = upad.shape[0]
    fwd = upad
    span = 1
    while span < min(w, POOL_HALO):
        fwd = fwd + pltpu.roll(fwd, n - span, axis=0)
        span *= 2
    centre = slice(POOL_HALO, POOL_HALO + tm)
    if w == 2 * POOL_HALO:
        return fwd[0:tm] + fwd[centre]
    return pltpu.roll(fwd, w // 2, axis=0)[centre]


def _mix_kernel(x_ref, u_ref, up_ref, un_ref, ya_ref, ic_ref, wp_ref, ps_ref, wo_ref, g_ref, wr_ref,
                x1_ref, h_ref, aff_ref, upad_ref, wobf_ref, wpbf_ref, *, tm):
    i = pl.program_id(1)

    @pl.when((pl.program_id(0) == 0) & (i == 0))
    def _():
        wobf_ref[...] = wo_ref[...].astype(BF16)
        wpbf_ref[...] = wp_ref[...].astype(BF16)

    upad_ref[0:POOL_HALO, :] = jnp.where(i > 0, up_ref[0], 0.0)
    upad_ref[POOL_HALO:POOL_HALO + tm, :] = u_ref[0]
    upad_ref[POOL_HALO + tm:, :] = jnp.where(i < pl.num_programs(1) - 1, un_ref[0], 0.0)

    ngroups = tm // MIX_ROWS
    group_cols = [slice(gi * POOL_GROUP, (gi + 1) * POOL_GROUP) for gi in range(len(POOL_WINDOWS))]

    def pool_stage(g):
        ds = []
        for cols, w in zip(group_cols, POOL_WINDOWS):
            upad = upad_ref[g * MIX_ROWS:(g + 1) * MIX_ROWS + 2 * POOL_HALO, cols]
            edge = jnp.full((POOL_HALO, POOL_GROUP), 1.0 / w, F32)
            head = jnp.where(i == 0, ic_ref[0:POOL_HALO, cols], 1.0 / w) if g == 0 else edge
            tail = (jnp.where(i == pl.num_programs(1) - 1, ic_ref[POOL_HALO:, cols], 1.0 / w)
                    if g == ngroups - 1 else edge)
            inv = jnp.concatenate(
                [head, jnp.full((MIX_ROWS - 2 * POOL_HALO, POOL_GROUP), 1.0 / w, F32), tail], axis=0)
            d = _window_sum(upad, w, MIX_ROWS) * inv - upad[POOL_HALO:POOL_HALO + MIX_ROWS]
            ds.append(d.astype(BF16))
        return ds

    def proj_stage(g, ds):
        rows = slice(g * MIX_ROWS, (g + 1) * MIX_ROWS)
        ypool = [_dot(d, wpbf_ref[gi]) * ps_ref[:, cols] for gi, (d, cols) in enumerate(zip(ds, group_cols))]
        ypool = jnp.concatenate(ypool, axis=-1).astype(BF16)
        return _dot(ypool, wobf_ref[:POOL_WIDTH, :]) + _dot(ya_ref[0, rows, :], wobf_ref[POOL_WIDTH:, :])

    def route_stage(g, mix):
        rows = slice(g * MIX_ROWS, (g + 1) * MIX_ROWS)
        x1 = x_ref[0, rows, :] + mix
        x1_ref[0, rows, :] = x1
        h = _rms(x1, g_ref[...]).astype(BF16)
        h_ref[0, rows, :] = h
        logits = _dot_nt(wr_ref[...].astype(BF16), h)
        m = jnp.max(logits, axis=0, keepdims=True)
        e = jnp.exp(logits - m)
        aff_ref[0, :, rows] = e / jnp.sum(e, axis=0, keepdims=True)

    pooled, mixed = {}, {}
    for step in range(ngroups + 2):
        if step < ngroups:
            pooled[step] = pool_stage(step)
        if 0 <= step - 1 < ngroups:
            mixed[step - 1] = proj_stage(step - 1, pooled.pop(step - 1))
        if 0 <= step - 2 < ngroups:
            route_stage(step - 2, mixed.pop(step - 2))


def _pool_edge_inverse_counts(s):
    t = jnp.concatenate([jnp.arange(POOL_HALO), jnp.arange(s - POOL_HALO, s)])
    cols = []
    for w in POOL_WINDOWS:
        lo = jnp.clip(t - w // 2, 0, s - 1)
        hi = jnp.clip(t + (w - w // 2) - 1, 0, s - 1)
        inv = 1.0 / (hi - lo + 1).astype(F32)
        cols.append(jnp.broadcast_to(inv[:, None], (2 * POOL_HALO, POOL_GROUP)))
    return jnp.concatenate(cols, axis=-1)


def _mix(x, u, ya, wp, ps, wo, g, wr_t, tm=1024):
    b, s, _ = x.shape
    nt = s // tm
    hb = tm // POOL_HALO
    tile = lambda w: pl.BlockSpec((1, tm, w), lambda bi, i: (bi, i, 0))
    fixed2 = lambda shape: pl.BlockSpec(shape, lambda bi, i: (0, 0))
    return pl.pallas_call(
        functools.partial(_mix_kernel, tm=tm),
        grid=(b, nt),
        in_specs=[
            tile(D_MODEL),
            tile(POOL_WIDTH),
            pl.BlockSpec((1, POOL_HALO, POOL_WIDTH), lambda bi, i: (bi, jnp.maximum(i * hb - 1, 0), 0)),
            pl.BlockSpec((1, POOL_HALO, POOL_WIDTH),
                         lambda bi, i: (bi, jnp.minimum((i + 1) * hb, s // POOL_HALO - 1), 0)),
            tile(ATTN_WIDTH),
            fixed2((2 * POOL_HALO, POOL_WIDTH)),
            pl.BlockSpec(wp.shape, lambda bi, i: (0, 0, 0)),
            fixed2((1, POOL_WIDTH)),
            pl.BlockSpec(wo.shape, lambda bi, i: (0, 0), pipeline_mode=pl.Buffered(1)),
            fixed2((1, D_MODEL)),
            fixed2(wr_t.shape),
        ],
        out_specs=[
            tile(D_MODEL),
            tile(D_MODEL),
            pl.BlockSpec((1, N_EXPERTS, tm), lambda bi, i: (bi, 0, i)),
        ],
        out_shape=[
            jax.ShapeDtypeStruct((b, s, D_MODEL), F32),
            jax.ShapeDtypeStruct((b, s, D_MODEL), BF16),
            jax.ShapeDtypeStruct((b, N_EXPERTS, s), F32),
        ],
        scratch_shapes=[
            pltpu.VMEM((tm + 2 * POOL_HALO, POOL_WIDTH), F32),
            pltpu.VMEM(wo.shape, BF16),
            pltpu.VMEM(wp.shape, BF16),
        ],
        compiler_params=_params("arbitrary", "arbitrary"),
        name="mix",
    )(x, u, u, u, ya, _pool_edge_inverse_counts(s), wp, ps, wo, g, wr_t)


def _lane_cumsum_exclusive(m):
    e, s = m.shape
    r = lax.broadcasted_iota(jnp.int32, (LANES, LANES), 0)
    c = lax.broadcasted_iota(jnp.int32, (LANES, LANES), 1)
    upper = (r < c).astype(BF16)
    carry = jnp.zeros((e, 1), F32)
    out = []
    for blk in range(s // LANES):
        piece = m[:, blk * LANES:(blk + 1) * LANES]
        out.append(_dot(piece.astype(BF16), upper) + carry)
        carry = carry + jnp.sum(piece, axis=-1, keepdims=True)
    return jnp.concatenate(out, axis=-1)


def _route_kernel(aff_ref, slot_ref, slot_t_ref, off_ref, *, cap):
    aff = aff_ref[...]
    capf = jnp.float32(cap)

    def count_ge(cand_bits):
        return jnp.sum((aff >= pltpu.bitcast(cand_bits, F32)).astype(F32), axis=-1, keepdims=True)

    def search(step, ans):
        cand = ans | (jnp.int32(1) << (30 - step))
        return jnp.where(count_ge(cand) >= capf, cand, ans)

    thr = pltpu.bitcast(lax.fori_loop(0, 31, search, jnp.zeros((aff.shape[0], 1), jnp.int32)), F32)
    gt = aff > thr
    eq = aff == thr
    need = capf - jnp.sum(gt.astype(F32), axis=-1, keepdims=True)
    sel = gt | (eq & (_lane_cumsum_exclusive(eq.astype(F32)) < need))
    self = sel.astype(F32)
    slot = jnp.where(sel, _lane_cumsum_exclusive(self), float(NOT_SELECTED))
    slot_ref[...] = slot.astype(jnp.int32)
    pad = jnp.full((LANES - N_EXPERTS, slot.shape[1]), float(NOT_SELECTED), F32)
    for bi in range(slot_t_ref.shape[0]):
        mine = slot[bi * N_EXPERTS:(bi + 1) * N_EXPERTS]
        slot_t_ref[bi] = jnp.concatenate([mine, pad], axis=0).T.astype(jnp.int32)
    lane = lax.broadcasted_iota(jnp.int32, (slot.shape[0], LANES), 1)
    off = jnp.zeros((slot.shape[0], LANES), F32)
    run = jnp.zeros((slot.shape[0], 1), F32)
    for i in range(1, slot.shape[1] // ROUTE_TILE + 1):
        run = run + jnp.sum(self[:, (i - 1) * ROUTE_TILE:i * ROUTE_TILE], axis=-1, keepdims=True)
        off = jnp.where(lane == i, run, off)
    off_ref[...] = off.astype(jnp.int32)


def _route(aff_t, cap):
    b, e, s = aff_t.shape
    assert e == N_EXPERTS and s % ROUTE_TILE == 0 and s // ROUTE_TILE < OFF_STRIDE
    slot, slot_t, off = pl.pallas_call(
        functools.partial(_route_kernel, cap=cap),
        out_shape=[
            jax.ShapeDtypeStruct((b * e, s), jnp.int32),
            jax.ShapeDtypeStruct((b, s, LANES), jnp.int32),
            jax.ShapeDtypeStruct((b * e, LANES), jnp.int32),
        ],
        compiler_params=pltpu.CompilerParams(vmem_limit_bytes=VMEM_LIMIT),
        name="route",
    )(aff_t.reshape(b * e, s))
    return slot.reshape(b, e, s), slot_t, off


def _window_starts(off_ref, bi, ti, cap):
    starts = []
    ok = None
    for e in range(N_EXPERTS):
        base = (bi * N_EXPERTS + e) * OFF_STRIDE + ti
        lo = off_ref[base]
        hi = off_ref[base + 1]
        st = jnp.minimum(jnp.bitwise_and(lo, -16), cap - ROUTE_WIN)
        fits = hi <= st + ROUTE_WIN
        ok = fits if ok is None else jnp.logical_and(ok, fits)
        starts.append(pl.multiple_of(st, 16))
    return starts, ok


def _dispatch_kernel(off_ref, slot_ref, aff_ref, h_ref, xe_ref, gate_ref, *, cap):
    bi = pl.program_id(0)
    ti = pl.program_id(1)

    @pl.when(ti == 0)
    def _():
        xe_ref[...] = jnp.zeros_like(xe_ref)
        gate_ref[...] = jnp.zeros_like(gate_ref)

    nsub = h_ref.shape[1] // ROUTE_TILE
    col_groups = [slice(sub * ROUTE_TILE, (sub + 1) * ROUTE_TILE) for sub in range(nsub)]
    geo = [_window_starts(off_ref, bi, ti * nsub + sub, cap) for sub in range(nsub)]
    ok = functools.reduce(jnp.logical_and, [fits for _, fits in geo])

    @pl.when(ok)
    def _():
        row = lax.broadcasted_iota(jnp.int32, (ROUTE_WIN, ROUTE_TILE), 0)
        for cols, (starts, _) in zip(col_groups, geo):
            slot = slot_ref[0, :, cols]
            aff = aff_ref[0, :, cols]
            hits = [row == (slot[e:e + 1, :] - starts[e]) for e in range(N_EXPERTS)]
            onehot = jnp.concatenate([jnp.where(hh, 1.0, 0.0).astype(BF16) for hh in hits], axis=0)
            res = _dot(onehot, h_ref[0, cols, :])
            for e, hh in enumerate(hits):
                win = pl.ds(starts[e], ROUTE_WIN)
                xe_ref[e, win, :] += res[e * ROUTE_WIN:(e + 1) * ROUTE_WIN].astype(BF16)
                gate_ref[e, win, :] += jnp.sum(jnp.where(hh, aff[e:e + 1, :], 0.0), axis=-1, keepdims=True)

    @pl.when(jnp.logical_not(ok))
    def _():
        row = lax.broadcasted_iota(jnp.int32, (cap, ROUTE_TILE), 0)
        for cols in col_groups:
            slot = slot_ref[0, :, cols]
            aff = aff_ref[0, :, cols]
            h = h_ref[0, cols, :]
            for e in range(N_EXPERTS):
                hh = row == slot[e:e + 1, :]
                xe_ref[e] += _dot(jnp.where(hh, 1.0, 0.0).astype(BF16), h).astype(BF16)
                gate_ref[e] += jnp.sum(jnp.where(hh, aff[e:e + 1, :], 0.0), axis=-1, keepdims=True)


def _dispatch(off, slot, aff_t, h, cap, tm=1024):
    b, e, s = slot.shape
    assert tm % ROUTE_TILE == 0
    rows = pl.BlockSpec((1, e, tm), lambda bi, ti, off_ref: (bi, 0, ti))
    return pl.pallas_call(
        functools.partial(_dispatch_kernel, cap=cap),
        grid_spec=pltpu.PrefetchScalarGridSpec(
            num_scalar_prefetch=1,
            grid=(b, s // tm),
            in_specs=[rows, rows,
                      pl.BlockSpec((1, tm, D_MODEL), lambda bi, ti, off_ref: (bi, ti, 0))],
            out_specs=[
                pl.BlockSpec((e, cap, D_MODEL), lambda bi, ti, off_ref: (0, bi, 0)),
                pl.BlockSpec((e, cap, 1), lambda bi, ti, off_ref: (0, bi, 0)),
            ],
        ),
        out_shape=[
            jax.ShapeDtypeStruct((e, b * cap, D_MODEL), BF16),
            jax.ShapeDtypeStruct((e, b * cap, 1), F32),
        ],
        compiler_params=_params("arbitrary", "arbitrary"),
        name="dispatch",
    )(off, slot, aff_t, h)


def _experts_kernel(xe_ref, gate_ref, wg_ref, wu_ref, wd_ref, y_ref, acc_ref):
    f = pl.program_id(1)

    def hidden_chunk(first, last):
        wg = wg_ref[0].astype(BF16)
        wu = wu_ref[0].astype(BF16)
        wd = wd_ref[0].astype(BF16)
        for mb in range(xe_ref.shape[1] // EXPERT_ROWS):
            rows = slice(mb * EXPERT_ROWS, (mb + 1) * EXPERT_ROWS)
            xe = xe_ref[0, rows, :]
            a = _dot(xe, wg)
            b = _dot(xe, wu)
            part = _dot((a * jax.nn.sigmoid(a) * b).astype(BF16), wd)
            total = part if first else acc_ref[rows, :] + part
            if last:
                y_ref[0, rows, :] = (total * gate_ref[0, rows, :]).astype(y_ref.dtype)
            else:
                acc_ref[rows, :] = total

    nf = pl.num_programs(1)
    pl.when(f == 0)(functools.partial(hidden_chunk, True, False))
    pl.when((f > 0) & (f < nf - 1))(functools.partial(hidden_chunk, False, False))
    pl.when(f == nf - 1)(functools.partial(hidden_chunk, False, True))


def _experts(xe, gate, wg, wu, wd, tf=512):
    e, m, _ = xe.shape
    nf = D_EXPERT // tf
    assert nf >= 2
    return pl.pallas_call(
        _experts_kernel,
        grid=(e, nf),
        in_specs=[
            pl.BlockSpec((1, m, D_MODEL), lambda ei, f: (ei, 0, 0)),
            pl.BlockSpec((1, m, 1), lambda ei, f: (ei, 0, 0)),
            pl.BlockSpec((1, D_MODEL, tf), lambda ei, f: (ei, 0, f)),
            pl.BlockSpec((1, D_MODEL, tf), lambda ei, f: (ei, 0, f)),
            pl.BlockSpec((1, tf, D_MODEL), lambda ei, f: (ei, f, 0)),
        ],
        out_specs=pl.BlockSpec((1, m, D_MODEL), lambda ei, f: (ei, 0, 0)),
        out_shape=jax.ShapeDtypeStruct((e, m, D_MODEL), BF16),
        scratch_shapes=[pltpu.VMEM((m, D_MODEL), F32)],
        compiler_params=_params("arbitrary", "arbitrary"),
        name="experts",
    )(xe, gate, wg, wu, wd)


def _combine_kernel(off_ref, x1_ref, st_ref, y_ref, p_ref, gn_ref, wgbf_ref, wpbf_ref, gp_ref,
                    o_ref, *, cap):
    bi = pl.program_id(0)
    nsub = x1_ref.shape[1] // ROUTE_TILE
    row_groups = [slice(sub * ROUTE_TILE, (sub + 1) * ROUTE_TILE) for sub in range(nsub)]
    geo = [_window_starts(off_ref, bi, pl.program_id(1) * nsub + sub, cap) for sub in range(nsub)]
    ok = functools.reduce(jnp.logical_and, [fits for _, fits in geo])

    def token_slots(rows):
        return st_ref[0, rows, :]

    def windowed_scatter(rows, starts):
        st = token_slots(rows)
        lane = lax.broadcasted_iota(jnp.int32, (ROUTE_TILE, LANES), 1)
        low = lane < ROUTE_WIN
        total = None
        for g in range(N_EXPERTS // ROUTE_GROUP):
            halves = []
            wins = []
            for half in range(ROUTE_GROUP // 2):
                e0 = g * ROUTE_GROUP + 2 * half
                t0 = st[:, e0:e0 + 1] - starts[e0]
                t1 = st[:, e0 + 1:e0 + 2] + (ROUTE_WIN - starts[e0 + 1])
                halves.append(jnp.where(lane == jnp.where(low, t0, t1), 1.0, 0.0).astype(BF16))
                wins.append(y_ref[e0, pl.ds(starts[e0], ROUTE_WIN), :])
                wins.append(y_ref[e0 + 1, pl.ds(starts[e0 + 1], ROUTE_WIN), :])
            part = _dot(jnp.concatenate(halves, axis=1), jnp.concatenate(wins, axis=0))
            total = part if total is None else total + part
        return total

    def dense_scatter(rows):
        st = token_slots(rows)
        lane = lax.broadcasted_iota(jnp.int32, (ROUTE_TILE, cap), 1)
        total = None
        for e in range(N_EXPERTS):
            onehot = jnp.where(lane == st[:, e:e + 1], 1.0, 0.0).astype(BF16)
            part = _dot(onehot, y_ref[e])
            total = part if total is None else total + part
        return total

    def gate_stage(rows, ffn):
        x2 = x1_ref[0, rows, :] + ffn
        return x2, _dot(_rms(x2, gn_ref[...]).astype(BF16), wgbf_ref[...])

    def output_stage(rows, x2, gate_logits):
        emb = _rms(_dot(p_ref[0, rows, :].astype(BF16), wpbf_ref[...]), gp_ref[...])
        o_ref[0, rows, :] = x2 + jax.nn.sigmoid(gate_logits) * emb

    def pipeline(scatter_stage):
        per = PLE_ROWS // ROUTE_TILE
        groups = [slice(g * PLE_ROWS, (g + 1) * PLE_ROWS) for g in range(nsub // per)]
        ffn, gated = {}, {}
        for step in range(len(groups) + 2):
            if step < len(groups):
                ffn[step] = jnp.concatenate([scatter_stage(step * per + j) for j in range(per)], axis=0)
            if 0 <= step - 1 < len(groups):
                gated[step - 1] = gate_stage(groups[step - 1], ffn.pop(step - 1))
            if 0 <= step - 2 < len(groups):
                output_stage(groups[step - 2], *gated.pop(step - 2))

    pl.when(ok)(lambda: pipeline(lambda g: windowed_scatter(row_groups[g], geo[g][0])))
    pl.when(jnp.logical_not(ok))(lambda: pipeline(lambda g: dense_scatter(row_groups[g])))


def _combine(off, x1, slot_t, y, p, gn, wg, wp, gp, cap, tm=1024):
    b, s, _ = x1.shape
    assert tm % PLE_ROWS == 0 and PLE_ROWS % ROUTE_TILE == 0
    tile = lambda w: pl.BlockSpec((1, tm, w), lambda bi, i, off_ref: (bi, i, 0))
    fixed2 = lambda shape: pl.BlockSpec(shape, lambda bi, i, off_ref: (0, 0))
    return pl.pallas_call(
        functools.partial(_combine_kernel, cap=cap),
        grid_spec=pltpu.PrefetchScalarGridSpec(
            num_scalar_prefetch=1,
            grid=(b, s // tm),
            in_specs=[
                tile(D_MODEL),
                tile(LANES),
                pl.BlockSpec((N_EXPERTS, cap, D_MODEL), lambda bi, i, off_ref: (0, bi, 0)),
                tile(PLE_DIM),
                fixed2((1, D_MODEL)),
                pl.BlockSpec(wg.shape, lambda bi, i, off_ref: (0, 0), pipeline_mode=pl.Buffered(1)),
                pl.BlockSpec(wp.shape, lambda bi, i, off_ref: (0, 0), pipeline_mode=pl.Buffered(1)),
                fixed2((1, D_MODEL)),
            ],
            out_specs=tile(D_MODEL),
        ),
        out_shape=jax.ShapeDtypeStruct((b, s, D_MODEL), F32),
        compiler_params=_params("arbitrary", "arbitrary"),
        name="combine",
    )(off, x1, slot_t, y, p, gn, wg.astype(BF16), wp.astype(BF16), gp)


def kernel(x, p, norm_mix, w_in, w_pool, pool_scale, q_norm, k_norm, rpb, w_out, norm_ffn, w_router, w_gate, w_up, w_down, norm_ple, w_ple_gate, w_ple_proj, norm_ple_post):
    b, s, d = x.shape
    depth = w_in.shape[0]
    cap = EC_CAPACITY * s // N_EXPERTS
    row = lambda a: a.reshape(1, -1)
    for i in range(depth):
        u, q, k, v = _in_proj(x.reshape(b * s, d), row(norm_mix[i]), w_in[i],
                              row(jnp.tile(q_norm[i], ATTN_HEADS)), row(jnp.tile(k_norm[i], ATTN_HEADS)))
        shp = lambda a: a.reshape(b, s, -1)
        y_attn = _natten(shp(q), shp(k), shp(v), _attn_bias_table(rpb[i]))
        x1, h, aff_t = _mix(x, shp(u), y_attn, w_pool[i], row(pool_scale[i]), w_out[i],
                            row(norm_ffn[i]), w_router[i].T)
        slot, slot_t, off = _route(aff_t, cap)
        off = off[:, :OFF_STRIDE].reshape(-1)
        xe, gate = _dispatch(off, slot, aff_t, h, cap)
        y = _experts(xe, gate, w_gate[i], w_up[i], w_down[i])
        x = _combine(off, x1, slot_t, y, p[i], row(norm_ple[i]), w_ple_gate[i], w_ple_proj[i],
                     row(norm_ple_post[i]), cap)
    return x
```

```python
import functools

import jax
import jax.numpy as jnp
from jax import lax
from jax.experimental import pallas as pl
from jax.experimental.pallas import tpu as pltpu

D_MODEL = 1024
GRID_W = 64
POOL_WINDOWS = (2, 4, 8, 16)
POOL_WIDTH = D_MODEL // 2
POOL_GROUP = POOL_WIDTH // len(POOL_WINDOWS)
ATTN_HEADS = 8
HEAD_DIM = (D_MODEL // 2) // ATTN_HEADS
ATTN_WIDTH = ATTN_HEADS * HEAD_DIM
WIN_ROWS_MAX = 8
WIN_COLS = 16
N_EXPERTS = 16
EC_CAPACITY = 2
D_EXPERT = 2 * D_MODEL
PLE_DIM = 256
RMS_EPS = 1e-6

LANES = 128
POOL_HALO = 8
LOG2E = 1.4426950408889634
MASK_BIAS = -1e30
NATTEN_UNROLL = 2
ROUTE_TILE = 256
ROUTE_WIN = 64
ROUTE_GROUP = 4
OFF_STRIDE = 16
NOT_SELECTED = -(1 << 20)
assert 2 * ROUTE_WIN == LANES and ROUTE_GROUP % 2 == 0
PLE_ROWS = 128
MIX_ROWS = 512
PROJ_ROWS = 1024
EXPERT_ROWS = 1024
VMEM_LIMIT = 56 * 1024 * 1024

BF16 = jnp.bfloat16
F32 = jnp.float32


def _params(*sem):
    return pltpu.CompilerParams(dimension_semantics=sem, vmem_limit_bytes=VMEM_LIMIT)


def _rms(x, g):
    return x * lax.rsqrt(jnp.mean(x * x, axis=-1, keepdims=True) + RMS_EPS) * g


def _dot(a, b):
    return jnp.dot(a, b, preferred_element_type=F32)


def _dot_nt(a, b):
    return lax.dot_general(a, b, (((1,), (1,)), ((), ())), preferred_element_type=F32)


def _in_proj_kernel(x_ref, g_ref, w_ref, qg_ref, kg_ref,
                    u_ref, q_ref, k_ref, v_ref, wbf_ref):
    @pl.when(pl.program_id(0) == 0)
    def _():
        wbf_ref[...] = w_ref[...].astype(BF16)

    def head_norm(t, g):
        low = lax.broadcasted_iota(jnp.int32, (t.shape[0], LANES), 1) < HEAD_DIM
        out = []
        for j in range(t.shape[1] // LANES):
            tile = t[:, j * LANES:(j + 1) * LANES]
            sq = tile * tile
            sa = jnp.sum(jnp.where(low, sq, 0.0), axis=-1, keepdims=True)
            sb = jnp.sum(sq, axis=-1, keepdims=True) - sa
            ra = lax.rsqrt(sa * (1.0 / HEAD_DIM) + RMS_EPS)
            rb = lax.rsqrt(sb * (1.0 / HEAD_DIM) + RMS_EPS)
            out.append(tile * jnp.where(low, ra, rb) * g[:, j * LANES:(j + 1) * LANES])
        return jnp.concatenate(out, axis=-1)

    def norm_stage(rows):
        return _rms(x_ref[rows, :], g_ref[...]).astype(BF16)

    def proj(h, lo, width):
        return _dot(h, wbf_ref[:, lo:lo + width])

    q_lo, k_lo, v_lo = POOL_WIDTH, POOL_WIDTH + ATTN_WIDTH, POOL_WIDTH + 2 * ATTN_WIDTH
    for sub in range(x_ref.shape[0] // PROJ_ROWS):
        rows = slice(sub * PROJ_ROWS, (sub + 1) * PROJ_ROWS)
        h = norm_stage(rows)
        u_ref[rows, :] = proj(h, 0, POOL_WIDTH)
        q_ref[rows, :] = (head_norm(proj(h, q_lo, ATTN_WIDTH), qg_ref[...])
                          * (HEAD_DIM ** -0.5 * LOG2E)).astype(BF16)
        k_ref[rows, :] = head_norm(proj(h, k_lo, ATTN_WIDTH), kg_ref[...]).astype(BF16)
        v_ref[rows, :] = proj(h, v_lo, ATTN_WIDTH).astype(BF16)


def _in_proj(x2, g, w, qg, kg, tm=1024):
    n = x2.shape[0]
    zw = w.shape[1]
    assert 2 * HEAD_DIM == LANES
    row = lambda i: (i, 0)
    fixed = lambda i: (0, 0)
    return pl.pallas_call(
        _in_proj_kernel,
        grid=(n // tm,),
        in_specs=[
            pl.BlockSpec((tm, D_MODEL), row),
            pl.BlockSpec((1, D_MODEL), fixed),
            pl.BlockSpec((D_MODEL, zw), fixed, pipeline_mode=pl.Buffered(1)),
            pl.BlockSpec((1, ATTN_WIDTH), fixed),
            pl.BlockSpec((1, ATTN_WIDTH), fixed),
        ],
        out_specs=[
            pl.BlockSpec((tm, POOL_WIDTH), row),
            pl.BlockSpec((tm, ATTN_WIDTH), row),
            pl.BlockSpec((tm, ATTN_WIDTH), row),
            pl.BlockSpec((tm, ATTN_WIDTH), row),
        ],
        out_shape=[
            jax.ShapeDtypeStruct((n, POOL_WIDTH), F32),
            jax.ShapeDtypeStruct((n, ATTN_WIDTH), BF16),
            jax.ShapeDtypeStruct((n, ATTN_WIDTH), BF16),
            jax.ShapeDtypeStruct((n, ATTN_WIDTH), BF16),
        ],
        scratch_shapes=[pltpu.VMEM((D_MODEL, zw), BF16)],
        compiler_params=_params("arbitrary"),
        name="in_proj",
    )(x2, g, w, qg, kg)


def _natten_kernel(q_ref, k_ref, v_ref, tbl_ref, o_ref, bias_ref, *, rows, kh):
    band = kh * GRID_W
    lane = lax.broadcasted_iota(jnp.int32, (GRID_W, LANES), 1)
    first = lane < HEAD_DIM

    @pl.when(pl.program_id(1) == 0)
    def _():
        for hh in range(2):
            for d0 in range(WIN_ROWS_MAX):
                for kk in range(kh // 2):
                    bias_ref[d0, hh * GRID_W:(hh + 1) * GRID_W, kk * LANES:(kk + 1) * LANES] = jnp.where(
                        lane < GRID_W, tbl_ref[hh, d0 + 2 * kk], tbl_ref[hh, d0 + 2 * kk + 1])

    def geometry(g):
        geo = []
        for r in range(g * NATTEN_UNROLL, (g + 1) * NATTEN_UNROLL):
            rs = min(max(r - kh // 2, 0), rows - kh)
            geo.append((rs - r + (WIN_ROWS_MAX - 1), r * GRID_W, rs * GRID_W))
        return geo

    def score_stage(g):
        scores = []
        for d0, q0, k0 in geometry(g):
            qr = q_ref[0, q0:q0 + GRID_W, :]
            zero = jnp.zeros_like(qr)
            q2 = jnp.concatenate([jnp.where(first, qr, zero), jnp.where(first, zero, qr)], axis=0)
            s = _dot_nt(q2, k_ref[0, k0:k0 + band, :]) + bias_ref[d0]
            scores.append((s, jnp.max(s, axis=-1, keepdims=True)))
        return scores

    def softmax_stage(scores):
        probs = []
        for s, m in scores:
            e = jnp.exp2(s - m)
            probs.append((e.astype(BF16), jnp.sum(e, axis=-1, keepdims=True)))
        return probs

    def value_stage(g, probs):
        for (d0, q0, k0), (e, l) in zip(geometry(g), probs):
            o = _dot(e, v_ref[0, k0:k0 + band, :]) / l
            o_ref[0, q0:q0 + GRID_W, :] = jnp.where(first, o[:GRID_W], o[GRID_W:]).astype(o_ref.dtype)

    ngroups = rows // NATTEN_UNROLL
    scores, probs = {}, {}
    for step in range(ngroups + 2):
        if step < ngroups:
            scores[step] = score_stage(step)
        if 0 <= step - 1 < ngroups:
            probs[step - 1] = softmax_stage(scores.pop(step - 1))
        if 0 <= step - 2 < ngroups:
            value_stage(step - 2, probs.pop(step - 2))


def _natten(q, k, v, bias):
    b, s, _ = q.shape
    rows = s // GRID_W
    kh = min(WIN_ROWS_MAX, rows)
    assert rows % NATTEN_UNROLL == 0 and kh % 2 == 0 and 2 * GRID_W == LANES
    pairs = ATTN_WIDTH // LANES
    blk = pl.BlockSpec((1, s, LANES), lambda p, bi: (bi, 0, p))
    return pl.pallas_call(
        functools.partial(_natten_kernel, rows=rows, kh=kh),
        grid=(pairs, b),
        in_specs=[blk, blk, blk,
                  pl.BlockSpec((2,) + bias.shape[1:], lambda p, bi: (p, 0, 0, 0))],
        out_specs=blk,
        out_shape=jax.ShapeDtypeStruct((b, s, ATTN_WIDTH), BF16),
        scratch_shapes=[pltpu.VMEM((WIN_ROWS_MAX, 2 * GRID_W, kh * GRID_W), F32)],
        compiler_params=_params("arbitrary", "arbitrary"),
        name="natten",
    )(q, k, v, bias)


def _attn_bias_table(rpb):
    c = jnp.arange(GRID_W)
    cs = jnp.clip(c - WIN_COLS // 2, 0, GRID_W - WIN_COLS)
    j = jnp.arange(GRID_W)
    valid = (j[None, :] >= cs[:, None]) & (j[None, :] < cs[:, None] + WIN_COLS)
    dc = j[None, :] - c[:, None] + (WIN_COLS - 1)
    pick = ((dc[None] == jnp.arange(2 * WIN_COLS - 1)[:, None, None]) & valid[None]).astype(F32)
    t = jnp.einsum('hrd,dcj->hrcj', rpb.astype(F32), pick, precision=lax.Precision.HIGHEST)
    t = jnp.where(valid, t * LOG2E, MASK_BIAS)
    return jnp.concatenate([t, t], axis=-1)


def _window_sum(upad, w, tm):
    n = upad.shape[0]
    fwd = upad
    span = 1
    while span < min(w, POOL_HALO):
        fwd = fwd + pltpu.roll(fwd, n - span, axis=0)
        span *= 2
    centre = slice(POOL_HALO, POOL_HALO + tm)
    if w == 2 * POOL_HALO:
        return fwd[0:tm] + fwd[centre]
    return pltpu.roll(fwd, w // 2, axis=0)[centre]


def _mix_kernel(x_ref, u_ref, up_ref, un_ref, ya_ref, ic_ref, wp_ref, ps_ref, wo_ref, g_ref, wr_ref,
                x1_ref, h_ref, aff_ref, upad_ref, wobf_ref, wpbf_ref, *, tm):
    i = pl.program_id(1)

    @pl.when((pl.program_id(0) == 0) & (i == 0))
    def _():
        wobf_ref[...] = wo_ref[...].astype(BF16)
        wpbf_ref[...] = wp_ref[...].astype(BF16)

    upad_ref[0:POOL_HALO, :] = jnp.where(i > 0, up_ref[0], 0.0)
    upad_ref[POOL_HALO:POOL_HALO + tm, :] = u_ref[0]
    upad_ref[POOL_HALO + tm:, :] = jnp.where(i < pl.num_programs(1) - 1, un_ref[0], 0.0)

    ngroups = tm // MIX_ROWS
    group_cols = [slice(gi * POOL_GROUP, (gi + 1) * POOL_GROUP) for gi in range(len(POOL_WINDOWS))]

    def pool_stage(g):
        ds = []
        for cols, w in zip(group_cols, POOL_WINDOWS):
            upad = upad_ref[g * MIX_ROWS:(g + 1) * MIX_ROWS + 2 * POOL_HALO, cols]
            edge = jnp.full((POOL_HALO, POOL_GROUP), 1.0 / w, F32)
            head = jnp.where(i == 0, ic_ref[0:POOL_HALO, cols], 1.0 / w) if g == 0 else edge
            tail = (jnp.where(i == pl.num_programs(1) - 1, ic_ref[POOL_HALO:, cols], 1.0 / w)
                    if g == ngroups - 1 else edge)
            inv = jnp.concatenate(
                [head, jnp.full((MIX_ROWS - 2 * POOL_HALO, POOL_GROUP), 1.0 / w, F32), tail], axis=0)
            d = _window_sum(upad, w, MIX_ROWS) * inv - upad[POOL_HALO:POOL_HALO + MIX_ROWS]
            ds.append(d.astype(BF16))
        return ds

    def proj_stage(g, ds):
        rows = slice(g * MIX_ROWS, (g + 1) * MIX_ROWS)
        ypool = [_dot(d, wpbf_ref[gi]) * ps_ref[:, cols] for gi, (d, cols) in enumerate(zip(ds, group_cols))]
        ypool = jnp.concatenate(ypool, axis=-1).astype(BF16)
        return _dot(ypool, wobf_ref[:POOL_WIDTH, :]) + _dot(ya_ref[0, rows, :], wobf_ref[POOL_WIDTH:, :])

    def route_stage(g, mix):
        rows = slice(g * MIX_ROWS, (g + 1) * MIX_ROWS)
        x1 = x_ref[0, rows, :] + mix
        x1_ref[0, rows, :] = x1
        h = _rms(x1, g_ref[...]).astype(BF16)
        h_ref[0, rows, :] = h
        logits = _dot_nt(wr_ref[...].astype(BF16), h)
        m = jnp.max(logits, axis=0, keepdims=True)
        e = jnp.exp(logits - m)
        aff_ref[0, :, rows] = e / jnp.sum(e, axis=0, keepdims=True)

    pooled, mixed = {}, {}
    for step in range(ngroups + 2):
        if step < ngroups:
            pooled[step] = pool_stage(step)
        if 0 <= step - 1 < ngroups:
            mixed[step - 1] = proj_stage(step - 1, pooled.pop(step - 1))
        if 0 <= step - 2 < ngroups:
            route_stage(step - 2, mixed.pop(step - 2))


def _pool_edge_inverse_counts(s):
    t = jnp.concatenate([jnp.arange(POOL_HALO), jnp.arange(s - POOL_HALO, s)])
    cols = []
    for w in POOL_WINDOWS:
        lo = jnp.clip(t - w // 2, 0, s - 1)
        hi = jnp.clip(t + (w - w // 2) - 1, 0, s - 1)
        inv = 1.0 / (hi - lo + 1).astype(F32)
        cols.append(jnp.broadcast_to(inv[:, None], (2 * POOL_HALO, POOL_GROUP)))
    return jnp.concatenate(cols, axis=-1)


def _mix(x, u, ya, wp, ps, wo, g, wr_t, tm=1024):
    b, s, _ = x.shape
    nt = s // tm
    hb = tm // POOL_HALO
    tile = lambda w: pl.BlockSpec((1, tm, w), lambda bi, i: (bi, i, 0))
    fixed2 = lambda shape: pl.BlockSpec(shape, lambda bi, i: (0, 0))
    return pl.pallas_call(
        functools.partial(_mix_kernel, tm=tm),
        grid=(b, nt),
        in_specs=[
            tile(D_MODEL),
            tile(POOL_WIDTH),
            pl.BlockSpec((1, POOL_HALO, POOL_WIDTH), lambda bi, i: (bi, jnp.maximum(i * hb - 1, 0), 0)),
            pl.BlockSpec((1, POOL_HALO, POOL_WIDTH),
                         lambda bi, i: (bi, jnp.minimum((i + 1) * hb, s // POOL_HALO - 1), 0)),
            tile(ATTN_WIDTH),
            fixed2((2 * POOL_HALO, POOL_WIDTH)),
            pl.BlockSpec(wp.shape, lambda bi, i: (0, 0, 0)),
            fixed2((1, POOL_WIDTH)),
            pl.BlockSpec(wo.shape, lambda bi, i: (0, 0), pipeline_mode=pl.Buffered(1)),
            fixed2((1, D_MODEL)),
            fixed2(wr_t.shape),
        ],
        out_specs=[
            tile(D_MODEL),
            tile(D_MODEL),
            pl.BlockSpec((1, N_EXPERTS, tm), lambda bi, i: (bi, 0, i)),
        ],
        out_shape=[
            jax.ShapeDtypeStruct((b, s, D_MODEL), F32),
            jax.ShapeDtypeStruct((b, s, D_MODEL), BF16),
            jax.ShapeDtypeStruct((b, N_EXPERTS, s), F32),
        ],
        scratch_shapes=[
            pltpu.VMEM((tm + 2 * POOL_HALO, POOL_WIDTH), F32),
            pltpu.VMEM(wo.shape, BF16),
            pltpu.VMEM(wp.shape, BF16),
        ],
        compiler_params=_params("arbitrary", "arbitrary"),
        name="mix",
    )(x, u, u, u, ya, _pool_edge_inverse_counts(s), wp, ps, wo, g, wr_t)


def _lane_cumsum_exclusive(m):
    e, s = m.shape
    r = lax.broadcasted_iota(jnp.int32, (LANES, LANES), 0)
    c = lax.broadcasted_iota(jnp.int32, (LANES, LANES), 1)
    upper = (r < c).astype(BF16)
    carry = jnp.zeros((e, 1), F32)
    out = []
    for blk in range(s // LANES):
        piece = m[:, blk * LANES:(blk + 1) * LANES]
        out.append(_dot(piece.astype(BF16), upper) + carry)
        carry = carry + jnp.sum(piece, axis=-1, keepdims=True)
    return jnp.concatenate(out, axis=-1)


def _route_kernel(aff_ref, slot_ref, slot_t_ref, off_ref, *, cap):
    aff = aff_ref[...]
    capf = jnp.float32(cap)

    def count_ge(cand_bits):
        return jnp.sum((aff >= pltpu.bitcast(cand_bits, F32)).astype(F32), axis=-1, keepdims=True)

    def search(step, ans):
        cand = ans | (jnp.int32(1) << (30 - step))
        return jnp.where(count_ge(cand) >= capf, cand, ans)

    thr = pltpu.bitcast(lax.fori_loop(0, 31, search, jnp.zeros((aff.shape[0], 1), jnp.int32)), F32)
    gt = aff > thr
    eq = aff == thr
    need = capf - jnp.sum(gt.astype(F32), axis=-1, keepdims=True)
    sel = gt | (eq & (_lane_cumsum_exclusive(eq.astype(F32)) < need))
    self = sel.astype(F32)
    slot = jnp.where(sel, _lane_cumsum_exclusive(self), float(NOT_SELECTED))
    slot_ref[...] = slot.astype(jnp.int32)
    pad = jnp.full((LANES - N_EXPERTS, slot.shape[1]), float(NOT_SELECTED), F32)
    for bi in range(slot_t_ref.shape[0]):
        mine = slot[bi * N_EXPERTS:(bi + 1) * N_EXPERTS]
        slot_t_ref[bi] = jnp.concatenate([mine, pad], axis=0).T.astype(jnp.int32)
    lane = lax.broadcasted_iota(jnp.int32, (slot.shape[0], LANES), 1)
    off = jnp.zeros((slot.shape[0], LANES), F32)
    run = jnp.zeros((slot.shape[0], 1), F32)
    for i in range(1, slot.shape[1] // ROUTE_TILE + 1):
        run = run + jnp.sum(self[:, (i - 1) * ROUTE_TILE:i * ROUTE_TILE], axis=-1, keepdims=True)
        off = jnp.where(lane == i, run, off)
    off_ref[...] = off.astype(jnp.int32)


def _route(aff_t, cap):
    b, e, s = aff_t.shape
    assert e == N_EXPERTS and s % ROUTE_TILE == 0 and s // ROUTE_TILE < OFF_STRIDE
    slot, slot_t, off = pl.pallas_call(
        functools.partial(_route_kernel, cap=cap),
        out_shape=[
            jax.ShapeDtypeStruct((b * e, s), jnp.int32),
            jax.ShapeDtypeStruct((b, s, LANES), jnp.int32),
            jax.ShapeDtypeStruct((b * e, LANES), jnp.int32),
        ],
        compiler_params=pltpu.CompilerParams(vmem_limit_bytes=VMEM_LIMIT),
        name="route",
    )(aff_t.reshape(b * e, s))
    return slot.reshape(b, e, s), slot_t, off


def _window_starts(off_ref, bi, ti, cap):
    starts = []
    ok = None
    for e in range(N_EXPERTS):
        base = (bi * N_EXPERTS + e) * OFF_STRIDE + ti
        lo = off_ref[base]
        hi = off_ref[base + 1]
        st = jnp.minimum(jnp.bitwise_and(lo, -16), cap - ROUTE_WIN)
        fits = hi <= st + ROUTE_WIN
        ok = fits if ok is None else jnp.logical_and(ok, fits)
        starts.append(pl.multiple_of(st, 16))
    return starts, ok


def _dispatch_kernel(off_ref, slot_ref, aff_ref, h_ref, xe_ref, gate_ref, *, cap):
    bi = pl.program_id(0)
    ti = pl.program_id(1)

    @pl.when(ti == 0)
    def _():
        xe_ref[...] = jnp.zeros_like(xe_ref)
        gate_ref[...] = jnp.zeros_like(gate_ref)

    nsub = h_ref.shape[1] // ROUTE_TILE
    col_groups = [slice(sub * ROUTE_TILE, (sub + 1) * ROUTE_TILE) for sub in range(nsub)]
    geo = [_window_starts(off_ref, bi, ti * nsub + sub, cap) for sub in range(nsub)]
    ok = functools.reduce(jnp.logical_and, [fits for _, fits in geo])

    @pl.when(ok)
    def _():
        row = lax.broadcasted_iota(jnp.int32, (ROUTE_WIN, ROUTE_TILE), 0)
        for cols, (starts, _) in zip(col_groups, geo):
            slot = slot_ref[0, :, cols]
            aff = aff_ref[0, :, cols]
            hits = [row == (slot[e:e + 1, :] - starts[e]) for e in range(N_EXPERTS)]
            onehot = jnp.concatenate([jnp.where(hh, 1.0, 0.0).astype(BF16) for hh in hits], axis=0)
            res = _dot(onehot, h_ref[0, cols, :])
            for e, hh in enumerate(hits):
                win = pl.ds(starts[e], ROUTE_WIN)
                xe_ref[e, win, :] += res[e * ROUTE_WIN:(e + 1) * ROUTE_WIN].astype(BF16)
                gate_ref[e, win, :] += jnp.sum(jnp.where(hh, aff[e:e + 1, :], 0.0), axis=-1, keepdims=True)

    @pl.when(jnp.logical_not(ok))
    def _():
        row = lax.broadcasted_iota(jnp.int32, (cap, ROUTE_TILE), 0)
        for cols in col_groups:
            slot = slot_ref[0, :, cols]
            aff = aff_ref[0, :, cols]
            h = h_ref[0, cols, :]
            for e in range(N_EXPERTS):
                hh = row == slot[e:e + 1, :]
                xe_ref[e] += _dot(jnp.where(hh, 1.0, 0.0).astype(BF16), h).astype(BF16)
                gate_ref[e] += jnp.sum(jnp.where(hh, aff[e:e + 1, :], 0.0), axis=-1, keepdims=True)


def _dispatch(off, slot, aff_t, h, cap, tm=1024):
    b, e, s = slot.shape
    assert tm % ROUTE_TILE == 0
    rows = pl.BlockSpec((1, e, tm), lambda bi, ti, off_ref: (bi, 0, ti))
    return pl.pallas_call(
        functools.partial(_dispatch_kernel, cap=cap),
        grid_spec=pltpu.PrefetchScalarGridSpec(
            num_scalar_prefetch=1,
            grid=(b, s // tm),
            in_specs=[rows, rows,
                      pl.BlockSpec((1, tm, D_MODEL), lambda bi, ti, off_ref: (bi, ti, 0))],
            out_specs=[
                pl.BlockSpec((e, cap, D_MODEL), lambda bi, ti, off_ref: (0, bi, 0)),
                pl.BlockSpec((e, cap, 1), lambda bi, ti, off_ref: (0, bi, 0)),
            ],
        ),
        out_shape=[
            jax.ShapeDtypeStruct((e, b * cap, D_MODEL), BF16),
            jax.ShapeDtypeStruct((e, b * cap, 1), F32),
        ],
        compiler_params=_params("arbitrary", "arbitrary"),
        name="dispatch",
    )(off, slot, aff_t, h)


def _experts_kernel(xe_ref, gate_ref, wg_ref, wu_ref, wd_ref, y_ref, acc_ref):
    f = pl.program_id(1)

    def hidden_chunk(first, last):
        wg = wg_ref[0].astype(BF16)
        wu = wu_ref[0].astype(BF16)
        wd = wd_ref[0].astype(BF16)
        for mb in range(xe_ref.shape[1] // EXPERT_ROWS):
            rows = slice(mb * EXPERT_ROWS, (mb + 1) * EXPERT_ROWS)
            xe = xe_ref[0, rows, :]
            a = _dot(xe, wg)
            b = _dot(xe, wu)
            part = _dot((a * jax.nn.sigmoid(a) * b).astype(BF16), wd)
            total = part if first else acc_ref[rows, :] + part
            if last:
                y_ref[0, rows, :] = (total * gate_ref[0, rows, :]).astype(y_ref.dtype)
            else:
                acc_ref[rows, :] = total

    nf = pl.num_programs(1)
    pl.when(f == 0)(functools.partial(hidden_chunk, True, False))
    pl.when((f > 0) & (f < nf - 1))(functools.partial(hidden_chunk, False, False))
    pl.when(f == nf - 1)(functools.partial(hidden_chunk, False, True))


def _experts(xe, gate, wg, wu, wd, tf=512):
    e, m, _ = xe.shape
    nf = D_EXPERT // tf
    assert nf >= 2
    return pl.pallas_call(
        _experts_kernel,
        grid=(e, nf),
        in_specs=[
            pl.BlockSpec((1, m, D_MODEL), lambda ei, f: (ei, 0, 0)),
            pl.BlockSpec((1, m, 1), lambda ei, f: (ei, 0, 0)),
            pl.BlockSpec((1, D_MODEL, tf), lambda ei, f: (ei, 0, f)),
            pl.BlockSpec((1, D_MODEL, tf), lambda ei, f: (ei, 0, f)),
            pl.BlockSpec((1, tf, D_MODEL), lambda ei, f: (ei, f, 0)),
        ],
        out_specs=pl.BlockSpec((1, m, D_MODEL), lambda ei, f: (ei, 0, 0)),
        out_shape=jax.ShapeDtypeStruct((e, m, D_MODEL), BF16),
        scratch_shapes=[pltpu.VMEM((m, D_MODEL), F32)],
        compiler_params=_params("arbitrary", "arbitrary"),
        name="experts",
    )(xe, gate, wg, wu, wd)


def _combine_kernel(off_ref, x1_ref, st_ref, y_ref, p_ref, gn_ref, wgbf_ref, wpbf_ref, gp_ref,
                    o_ref, *, cap):
    bi = pl.program_id(0)
    nsub = x1_ref.shape[1] // ROUTE_TILE
    row_groups = [slice(sub * ROUTE_TILE, (sub + 1) * ROUTE_TILE) for sub in range(nsub)]
    geo = [_window_starts(off_ref, bi, pl.program_id(1) * nsub + sub, cap) for sub in range(nsub)]
    ok = functools.reduce(jnp.logical_and, [fits for _, fits in geo])

    def token_slots(rows):
        return st_ref[0, rows, :]

    def windowed_scatter(rows, starts):
        st = token_slots(rows)
        lane = lax.broadcasted_iota(jnp.int32, (ROUTE_TILE, LANES), 1)
        low = lane < ROUTE_WIN
        total = None
        for g in range(N_EXPERTS // ROUTE_GROUP):
            halves = []
            wins = []
            for half in range(ROUTE_GROUP // 2):
                e0 = g * ROUTE_GROUP + 2 * half
                t0 = st[:, e0:e0 + 1] - starts[e0]
                t1 = st[:, e0 + 1:e0 + 2] + (ROUTE_WIN - starts[e0 + 1])
                halves.append(jnp.where(lane == jnp.where(low, t0, t1), 1.0, 0.0).astype(BF16))
                wins.append(y_ref[e0, pl.ds(starts[e0], ROUTE_WIN), :])
                wins.append(y_ref[e0 + 1, pl.ds(starts[e0 + 1], ROUTE_WIN), :])
            part = _dot(jnp.concatenate(halves, axis=1), jnp.concatenate(wins, axis=0))
            total = part if total is None else total + part
        return total

    def dense_scatter(rows):
        st = token_slots(rows)
        lane = lax.broadcasted_iota(jnp.int32, (ROUTE_TILE, cap), 1)
        total = None
        for e in range(N_EXPERTS):
            onehot = jnp.where(lane == st[:, e:e + 1], 1.0, 0.0).astype(BF16)
            part = _dot(onehot, y_ref[e])
            total = part if total is None else total + part
        return total

    def gate_stage(rows, ffn):
        x2 = x1_ref[0, rows, :] + ffn
        return x2, _dot(_rms(x2, gn_ref[...]).astype(BF16), wgbf_ref[...])

    def output_stage(rows, x2, gate_logits):
        emb = _rms(_dot(p_ref[0, rows, :].astype(BF16), wpbf_ref[...]), gp_ref[...])
        o_ref[0, rows, :] = x2 + jax.nn.sigmoid(gate_logits) * emb

    def pipeline(scatter_stage):
        split = ROUTE_TILE // PLE_ROWS
        part = lambda g, j: slice(g * ROUTE_TILE + j * PLE_ROWS, g * ROUTE_TILE + (j + 1) * PLE_ROWS)
        ffn, gated = {}, {}
        for step in range(nsub + 2):
            if step < nsub:
                ffn[step] = scatter_stage(step)
            if 0 <= step - 1 < nsub:
                f = ffn.pop(step - 1)
                for j in range(split):
                    gated[step - 1, j] = gate_stage(part(step - 1, j), f[j * PLE_ROWS:(j + 1) * PLE_ROWS])
            if 0 <= step - 2 < nsub:
                for j in range(split):
                    output_stage(part(step - 2, j), *gated.pop((step - 2, j)))

    pl.when(ok)(lambda: pipeline(lambda g: windowed_scatter(row_groups[g], geo[g][0])))
    pl.when(jnp.logical_not(ok))(lambda: pipeline(lambda g: dense_scatter(row_groups[g])))


def _combine(off, x1, slot_t, y, p, gn, wg, wp, gp, cap, tm=1024):
    b, s, _ = x1.shape
    assert tm % ROUTE_TILE == 0 and ROUTE_TILE % PLE_ROWS == 0
    tile = lambda w: pl.BlockSpec((1, tm, w), lambda bi, i, off_ref: (bi, i, 0))
    fixed2 = lambda shape: pl.BlockSpec(shape, lambda bi, i, off_ref: (0, 0))
    return pl.pallas_call(
        functools.partial(_combine_kernel, cap=cap),
        grid_spec=pltpu.PrefetchScalarGridSpec(
            num_scalar_prefetch=1,
            grid=(b, s // tm),
            in_specs=[
                tile(D_MODEL),
                tile(LANES),
                pl.BlockSpec((N_EXPERTS, cap, D_MODEL), lambda bi, i, off_ref: (0, bi, 0)),
                tile(PLE_DIM),
                fixed2((1, D_MODEL)),
                pl.BlockSpec(wg.shape, lambda bi, i, off_ref: (0, 0), pipeline_mode=pl.Buffered(1)),
                pl.BlockSpec(wp.shape, lambda bi, i, off_ref: (0, 0), pipeline_mode=pl.Buffered(1)),
                fixed2((1, D_MODEL)),
            ],
            out_specs=tile(D_MODEL),
        ),
        out_shape=jax.ShapeDtypeStruct((b, s, D_MODEL), F32),
        compiler_params=_params("arbitrary", "arbitrary"),
        name="combine",
    )(off, x1, slot_t, y, p, gn, wg.astype(BF16), wp.astype(BF16), gp)


def kernel(x, p, norm_mix, w_in, w_pool, pool_scale, q_norm, k_norm, rpb, w_out, norm_ffn, w_router, w_gate, w_up, w_down, norm_ple, w_ple_gate, w_ple_proj, norm_ple_post):
    b, s, d = x.shape
    depth = w_in.shape[0]
    cap = EC_CAPACITY * s // N_EXPERTS
    row = lambda a: a.reshape(1, -1)
    for i in range(depth):
        u, q, k, v = _in_proj(x.reshape(b * s, d), row(norm_mix[i]), w_in[i],
                              row(jnp.tile(q_norm[i], ATTN_HEADS)), row(jnp.tile(k_norm[i], ATTN_HEADS)))
        shp = lambda a: a.reshape(b, s, -1)
        y_attn = _natten(shp(q), shp(k), shp(v), _attn_bias_table(rpb[i]))
        x1, h, aff_t = _mix(x, shp(u), y_attn, w_pool[i], row(pool_scale[i]), w_out[i],
                            row(norm_ffn[i]), w_router[i].T)
        slot, slot_t, off = _route(aff_t, cap)
        off = off[:, :OFF_STRIDE].reshape(-1)
        xe, gate = _dispatch(off, slot, aff_t, h, cap)
        y = _experts(xe, gate, w_gate[i], w_up[i], w_down[i])
        x = _combine(off, x1, slot_t, y, p[i], row(norm_ple[i]), w_ple_gate[i], w_ple_proj[i],
                     row(norm_ple_post[i]), cap)
    return x
```

```python
import functools

import jax
import jax.numpy as jnp
from jax import lax
from jax.experimental import pallas as pl
from jax.experimental.pallas import tpu as pltpu

D_MODEL = 1024
GRID_W = 64
POOL_WINDOWS = (2, 4, 8, 16)
POOL_WIDTH = D_MODEL // 2
POOL_GROUP = POOL_WIDTH // len(POOL_WINDOWS)
ATTN_HEADS = 8
HEAD_DIM = (D_MODEL // 2) // ATTN_HEADS
ATTN_WIDTH = ATTN_HEADS * HEAD_DIM
WIN_ROWS_MAX = 8
WIN_COLS = 16
N_EXPERTS = 16
EC_CAPACITY = 2
D_EXPERT = 2 * D_MODEL
PLE_DIM = 256
RMS_EPS = 1e-6

LANES = 128
POOL_HALO = 8
LOG2E = 1.4426950408889634
MASK_BIAS = -1e30
NATTEN_UNROLL = 2
ROUTE_TILE = 256
ROUTE_WIN = 64
ROUTE_GROUP = 4
GATHER_GROUP = 8
OFF_STRIDE = 16
NOT_SELECTED = -(1 << 20)
assert 2 * ROUTE_WIN == LANES and ROUTE_GROUP % 2 == 0
PLE_ROWS = 256
MIX_ROWS = 512
PROJ_ROWS = 1024
EXPERT_ROWS = 1024
VMEM_LIMIT = 56 * 1024 * 1024

BF16 = jnp.bfloat16
F32 = jnp.float32


def _params(*sem):
    return pltpu.CompilerParams(dimension_semantics=sem, vmem_limit_bytes=VMEM_LIMIT)


def _rms(x, g):
    return x * lax.rsqrt(jnp.mean(x * x, axis=-1, keepdims=True) + RMS_EPS) * g


def _dot(a, b):
    return jnp.dot(a, b, preferred_element_type=F32)


def _dot_nt(a, b):
    return lax.dot_general(a, b, (((1,), (1,)), ((), ())), preferred_element_type=F32)


def _in_proj_kernel(x_ref, g_ref, w_ref, qg_ref, kg_ref,
                    u_ref, q_ref, k_ref, v_ref, wbf_ref):
    @pl.when(pl.program_id(0) == 0)
    def _():
        wbf_ref[...] = w_ref[...].astype(BF16)

    def head_norm(t, g):
        low = lax.broadcasted_iota(jnp.int32, (t.shape[0], LANES), 1) < HEAD_DIM
        out = []
        for j in range(t.shape[1] // LANES):
            tile = t[:, j * LANES:(j + 1) * LANES]
            sq = tile * tile
            sa = jnp.sum(jnp.where(low, sq, 0.0), axis=-1, keepdims=True)
            sb = jnp.sum(sq, axis=-1, keepdims=True) - sa
            ra = lax.rsqrt(sa * (1.0 / HEAD_DIM) + RMS_EPS)
            rb = lax.rsqrt(sb * (1.0 / HEAD_DIM) + RMS_EPS)
            out.append(tile * jnp.where(low, ra, rb) * g[:, j * LANES:(j + 1) * LANES])
        return jnp.concatenate(out, axis=-1)

    def norm_stage(rows):
        return _rms(x_ref[rows, :], g_ref[...]).astype(BF16)

    def proj(h, lo, width):
        return _dot(h, wbf_ref[:, lo:lo + width])

    q_lo, k_lo, v_lo = POOL_WIDTH, POOL_WIDTH + ATTN_WIDTH, POOL_WIDTH + 2 * ATTN_WIDTH
    for sub in range(x_ref.shape[0] // PROJ_ROWS):
        rows = slice(sub * PROJ_ROWS, (sub + 1) * PROJ_ROWS)
        h = norm_stage(rows)
        u_ref[rows, :] = proj(h, 0, POOL_WIDTH)
        q_ref[rows, :] = (head_norm(proj(h, q_lo, ATTN_WIDTH), qg_ref[...])
                          * (HEAD_DIM ** -0.5 * LOG2E)).astype(BF16)
        k_ref[rows, :] = head_norm(proj(h, k_lo, ATTN_WIDTH), kg_ref[...]).astype(BF16)
        v_ref[rows, :] = proj(h, v_lo, ATTN_WIDTH).astype(BF16)


def _in_proj(x2, g, w, qg, kg, tm=1024):
    n = x2.shape[0]
    zw = w.shape[1]
    assert 2 * HEAD_DIM == LANES
    row = lambda i: (i, 0)
    fixed = lambda i: (0, 0)
    return pl.pallas_call(
        _in_proj_kernel,
        grid=(n // tm,),
        in_specs=[
            pl.BlockSpec((tm, D_MODEL), row),
            pl.BlockSpec((1, D_MODEL), fixed),
            pl.BlockSpec((D_MODEL, zw), fixed, pipeline_mode=pl.Buffered(1)),
            pl.BlockSpec((1, ATTN_WIDTH), fixed),
            pl.BlockSpec((1, ATTN_WIDTH), fixed),
        ],
        out_specs=[
            pl.BlockSpec((tm, POOL_WIDTH), row),
            pl.BlockSpec((tm, ATTN_WIDTH), row),
            pl.BlockSpec((tm, ATTN_WIDTH), row),
            pl.BlockSpec((tm, ATTN_WIDTH), row),
        ],
        out_shape=[
            jax.ShapeDtypeStruct((n, POOL_WIDTH), F32),
            jax.ShapeDtypeStruct((n, ATTN_WIDTH), BF16),
            jax.ShapeDtypeStruct((n, ATTN_WIDTH), BF16),
            jax.ShapeDtypeStruct((n, ATTN_WIDTH), BF16),
        ],
        scratch_shapes=[pltpu.VMEM((D_MODEL, zw), BF16)],
        compiler_params=_params("arbitrary"),
        name="in_proj",
    )(x2, g, w, qg, kg)


def _natten_kernel(q_ref, k_ref, v_ref, tbl_ref, o_ref, bias_ref, *, rows, kh):
    band = kh * GRID_W
    lane = lax.broadcasted_iota(jnp.int32, (GRID_W, LANES), 1)
    first = lane < HEAD_DIM

    @pl.when(pl.program_id(1) == 0)
    def _():
        for hh in range(2):
            for d0 in range(WIN_ROWS_MAX):
                for kk in range(kh // 2):
                    bias_ref[d0, hh * GRID_W:(hh + 1) * GRID_W, kk * LANES:(kk + 1) * LANES] = jnp.where(
                        lane < GRID_W, tbl_ref[hh, d0 + 2 * kk], tbl_ref[hh, d0 + 2 * kk + 1])

    def geometry(g):
        geo = []
        for r in range(g * NATTEN_UNROLL, (g + 1) * NATTEN_UNROLL):
            rs = min(max(r - kh // 2, 0), rows - kh)
            geo.append((rs - r + (WIN_ROWS_MAX - 1), r * GRID_W, rs * GRID_W))
        return geo

    def score_stage(g):
        scores = []
        for d0, q0, k0 in geometry(g):
            qr = q_ref[0, q0:q0 + GRID_W, :]
            zero = jnp.zeros_like(qr)
            q2 = jnp.concatenate([jnp.where(first, qr, zero), jnp.where(first, zero, qr)], axis=0)
            s = _dot_nt(q2, k_ref[0, k0:k0 + band, :]) + bias_ref[d0]
            scores.append((s, jnp.max(s, axis=-1, keepdims=True)))
        return scores

    def softmax_stage(scores):
        probs = []
        for s, m in scores:
            e = jnp.exp2(s - m)
            probs.append((e.astype(BF16), jnp.sum(e, axis=-1, keepdims=True)))
        return probs

    def value_stage(g, probs):
        for (d0, q0, k0), (e, l) in zip(geometry(g), probs):
            o = _dot(e, v_ref[0, k0:k0 + band, :]) / l
            o_ref[0, q0:q0 + GRID_W, :] = jnp.where(first, o[:GRID_W], o[GRID_W:]).astype(o_ref.dtype)

    ngroups = rows // NATTEN_UNROLL
    scores, probs = {}, {}
    for step in range(ngroups + 2):
        if step < ngroups:
            scores[step] = score_stage(step)
        if 0 <= step - 1 < ngroups:
            probs[step - 1] = softmax_stage(scores.pop(step - 1))
        if 0 <= step - 2 < ngroups:
            value_stage(step - 2, probs.pop(step - 2))


def _natten(q, k, v, bias):
    b, s, _ = q.shape
    rows = s // GRID_W
    kh = min(WIN_ROWS_MAX, rows)
    assert rows % NATTEN_UNROLL == 0 and kh % 2 == 0 and 2 * GRID_W == LANES
    pairs = ATTN_WIDTH // LANES
    blk = pl.BlockSpec((1, s, LANES), lambda p, bi: (bi, 0, p))
    return pl.pallas_call(
        functools.partial(_natten_kernel, rows=rows, kh=kh),
        grid=(pairs, b),
        in_specs=[blk, blk, blk,
                  pl.BlockSpec((2,) + bias.shape[1:], lambda p, bi: (p, 0, 0, 0))],
        out_specs=blk,
        out_shape=jax.ShapeDtypeStruct((b, s, ATTN_WIDTH), BF16),
        scratch_shapes=[pltpu.VMEM((WIN_ROWS_MAX, 2 * GRID_W, kh * GRID_W), F32)],
        compiler_params=_params("arbitrary", "arbitrary"),
        name="natten",
    )(q, k, v, bias)


def _attn_bias_table(rpb):
    c = jnp.arange(GRID_W)
    cs = jnp.clip(c - WIN_COLS // 2, 0, GRID_W - WIN_COLS)
    j = jnp.arange(GRID_W)
    valid = (j[None, :] >= cs[:, None]) & (j[None, :] < cs[:, None] + WIN_COLS)
    dc = j[None, :] - c[:, None] + (WIN_COLS - 1)
    pick = ((dc[None] == jnp.arange(2 * WIN_COLS - 1)[:, None, None]) & valid[None]).astype(F32)
    t = jnp.einsum('hrd,dcj->hrcj', rpb.astype(F32), pick, precision=lax.Precision.HIGHEST)
    t = jnp.where(valid, t * LOG2E, MASK_BIAS)
    return jnp.concatenate([t, t], axis=-1)


def _window_sum(upad, w, tm):
    n = upad.shape[0]
    fwd = upad
    span = 1
    while span < min(w, POOL_HALO):
        fwd = fwd + pltpu.roll(fwd, n - span, axis=0)
        span *= 2
    centre = slice(POOL_HALO, POOL_HALO + tm)
    if w == 2 * POOL_HALO:
        return fwd[0:tm] + fwd[centre]
    return pltpu.roll(fwd, w // 2, axis=0)[centre]


def _mix_kernel(x_ref, u_ref, up_ref, un_ref, ya_ref, ic_ref, wp_ref, ps_ref, wo_ref, g_ref, wr_ref,
                x1_ref, h_ref, aff_ref, upad_ref, wobf_ref, wpbf_ref, *, tm):
    i = pl.program_id(1)

    @pl.when((pl.program_id(0) == 0) & (i == 0))
    def _():
        wobf_ref[...] = wo_ref[...].astype(BF16)
        wpbf_ref[...] = wp_ref[...].astype(BF16)

    upad_ref[0:POOL_HALO, :] = jnp.where(i > 0, up_ref[0], 0.0)
    upad_ref[POOL_HALO:POOL_HALO + tm, :] = u_ref[0]
    upad_ref[POOL_HALO + tm:, :] = jnp.where(i < pl.num_programs(1) - 1, un_ref[0], 0.0)

    ngroups = tm // MIX_ROWS
    group_cols = [slice(gi * POOL_GROUP, (gi + 1) * POOL_GROUP) for gi in range(len(POOL_WINDOWS))]

    def pool_stage(g):
        ds = []
        for cols, w in zip(group_cols, POOL_WINDOWS):
            upad = upad_ref[g * MIX_ROWS:(g + 1) * MIX_ROWS + 2 * POOL_HALO, cols]
            edge = jnp.full((POOL_HALO, POOL_GROUP), 1.0 / w, F32)
            head = jnp.where(i == 0, ic_ref[0:POOL_HALO, cols], 1.0 / w) if g == 0 else edge
            tail = (jnp.where(i == pl.num_programs(1) - 1, ic_ref[POOL_HALO:, cols], 1.0 / w)
                    if g == ngroups - 1 else edge)
            inv = jnp.concatenate(
                [head, jnp.full((MIX_ROWS - 2 * POOL_HALO, POOL_GROUP), 1.0 / w, F32), tail], axis=0)
            d = _window_sum(upad, w, MIX_ROWS) * inv - upad[POOL_HALO:POOL_HALO + MIX_ROWS]
            ds.append(d.astype(BF16))
        return ds

    def proj_stage(g, ds):
        rows = slice(g * MIX_ROWS, (g + 1) * MIX_ROWS)
        ypool = [_dot(d, wpbf_ref[gi]) * ps_ref[:, cols] for gi, (d, cols) in enumerate(zip(ds, group_cols))]
        ypool = jnp.concatenate(ypool, axis=-1).astype(BF16)
        return _dot(ypool, wobf_ref[:POOL_WIDTH, :]) + _dot(ya_ref[0, rows, :], wobf_ref[POOL_WIDTH:, :])

    def route_stage(g, mix):
        rows = slice(g * MIX_ROWS, (g + 1) * MIX_ROWS)
        x1 = x_ref[0, rows, :] + mix
        x1_ref[0, rows, :] = x1
        h = _rms(x1, g_ref[...]).astype(BF16)
        h_ref[0, rows, :] = h
        logits = _dot_nt(wr_ref[...].astype(BF16), h)
        m = jnp.max(logits, axis=0, keepdims=True)
        e = jnp.exp(logits - m)
        aff_ref[0, :, rows] = e / jnp.sum(e, axis=0, keepdims=True)

    pooled, mixed = {}, {}
    for step in range(ngroups + 2):
        if step < ngroups:
            pooled[step] = pool_stage(step)
        if 0 <= step - 1 < ngroups:
            mixed[step - 1] = proj_stage(step - 1, pooled.pop(step - 1))
        if 0 <= step - 2 < ngroups:
            route_stage(step - 2, mixed.pop(step - 2))


def _pool_edge_inverse_counts(s):
    t = jnp.concatenate([jnp.arange(POOL_HALO), jnp.arange(s - POOL_HALO, s)])
    cols = []
    for w in POOL_WINDOWS:
        lo = jnp.clip(t - w // 2, 0, s - 1)
        hi = jnp.clip(t + (w - w // 2) - 1, 0, s - 1)
        inv = 1.0 / (hi - lo + 1).astype(F32)
        cols.append(jnp.broadcast_to(inv[:, None], (2 * POOL_HALO, POOL_GROUP)))
    return jnp.concatenate(cols, axis=-1)


def _mix(x, u, ya, wp, ps, wo, g, wr_t, tm=1024):
    b, s, _ = x.shape
    nt = s // tm
    hb = tm // POOL_HALO
    tile = lambda w: pl.BlockSpec((1, tm, w), lambda bi, i: (bi, i, 0))
    fixed2 = lambda shape: pl.BlockSpec(shape, lambda bi, i: (0, 0))
    return pl.pallas_call(
        functools.partial(_mix_kernel, tm=tm),
        grid=(b, nt),
        in_specs=[
            tile(D_MODEL),
            tile(POOL_WIDTH),
            pl.BlockSpec((1, POOL_HALO, POOL_WIDTH), lambda bi, i: (bi, jnp.maximum(i * hb - 1, 0), 0)),
            pl.BlockSpec((1, POOL_HALO, POOL_WIDTH),
                         lambda bi, i: (bi, jnp.minimum((i + 1) * hb, s // POOL_HALO - 1), 0)),
            tile(ATTN_WIDTH),
            fixed2((2 * POOL_HALO, POOL_WIDTH)),
            pl.BlockSpec(wp.shape, lambda bi, i: (0, 0, 0)),
            fixed2((1, POOL_WIDTH)),
            pl.BlockSpec(wo.shape, lambda bi, i: (0, 0), pipeline_mode=pl.Buffered(1)),
            fixed2((1, D_MODEL)),
            fixed2(wr_t.shape),
        ],
        out_specs=[
            tile(D_MODEL),
            tile(D_MODEL),
            pl.BlockSpec((1, N_EXPERTS, tm), lambda bi, i: (bi, 0, i)),
        ],
        out_shape=[
            jax.ShapeDtypeStruct((b, s, D_MODEL), F32),
            jax.ShapeDtypeStruct((b, s, D_MODEL), BF16),
            jax.ShapeDtypeStruct((b, N_EXPERTS, s), F32),
        ],
        scratch_shapes=[
            pltpu.VMEM((tm + 2 * POOL_HALO, POOL_WIDTH), F32),
            pltpu.VMEM(wo.shape, BF16),
            pltpu.VMEM(wp.shape, BF16),
        ],
        compiler_params=_params("arbitrary", "arbitrary"),
        name="mix",
    )(x, u, u, u, ya, _pool_edge_inverse_counts(s), wp, ps, wo, g, wr_t)


def _lane_cumsum_exclusive(m):
    e, s = m.shape
    r = lax.broadcasted_iota(jnp.int32, (LANES, LANES), 0)
    c = lax.broadcasted_iota(jnp.int32, (LANES, LANES), 1)
    upper = (r < c).astype(BF16)
    carry = jnp.zeros((e, 1), F32)
    out = []
    for blk in range(s // LANES):
        piece = m[:, blk * LANES:(blk + 1) * LANES]
        out.append(_dot(piece.astype(BF16), upper) + carry)
        carry = carry + jnp.sum(piece, axis=-1, keepdims=True)
    return jnp.concatenate(out, axis=-1)


def _route_kernel(aff_ref, slot_ref, slot_t_ref, off_ref, *, cap):
    aff = aff_ref[...]
    capf = jnp.float32(cap)

    def count_ge(cand_bits):
        return jnp.sum((aff >= pltpu.bitcast(cand_bits, F32)).astype(F32), axis=-1, keepdims=True)

    def search(step, ans):
        cand = ans | (jnp.int32(1) << (30 - step))
        return jnp.where(count_ge(cand) >= capf, cand, ans)

    thr = pltpu.bitcast(lax.fori_loop(0, 31, search, jnp.zeros((aff.shape[0], 1), jnp.int32)), F32)
    gt = aff > thr
    eq = aff == thr
    need = capf - jnp.sum(gt.astype(F32), axis=-1, keepdims=True)
    sel = gt | (eq & (_lane_cumsum_exclusive(eq.astype(F32)) < need))
    self = sel.astype(F32)
    slot = jnp.where(sel, _lane_cumsum_exclusive(self), float(NOT_SELECTED))
    slot_ref[...] = slot.astype(jnp.int32)
    pad = jnp.full((LANES - N_EXPERTS, slot.shape[1]), float(NOT_SELECTED), F32)
    for bi in range(slot_t_ref.shape[0]):
        mine = slot[bi * N_EXPERTS:(bi + 1) * N_EXPERTS]
        slot_t_ref[bi] = jnp.concatenate([mine, pad], axis=0).T.astype(jnp.int32)
    lane = lax.broadcasted_iota(jnp.int32, (slot.shape[0], LANES), 1)
    off = jnp.zeros((slot.shape[0], LANES), F32)
    run = jnp.zeros((slot.shape[0], 1), F32)
    for i in range(1, slot.shape[1] // ROUTE_TILE + 1):
        run = run + jnp.sum(self[:, (i - 1) * ROUTE_TILE:i * ROUTE_TILE], axis=-1, keepdims=True)
        off = jnp.where(lane == i, run, off)
    off_ref[...] = off.astype(jnp.int32)


def _route(aff_t, cap):
    b, e, s = aff_t.shape
    assert e == N_EXPERTS and s % ROUTE_TILE == 0 and s // ROUTE_TILE < OFF_STRIDE
    slot, slot_t, off = pl.pallas_call(
        functools.partial(_route_kernel, cap=cap),
        out_shape=[
            jax.ShapeDtypeStruct((b * e, s), jnp.int32),
            jax.ShapeDtypeStruct((b, s, LANES), jnp.int32),
            jax.ShapeDtypeStruct((b * e, LANES), jnp.int32),
        ],
        compiler_params=pltpu.CompilerParams(vmem_limit_bytes=VMEM_LIMIT),
        name="route",
    )(aff_t.reshape(b * e, s))
    return slot.reshape(b, e, s), slot_t, off


def _window_starts(off_ref, bi, ti, cap):
    starts = []
    ok = None
    for e in range(N_EXPERTS):
        base = (bi * N_EXPERTS + e) * OFF_STRIDE + ti
        lo = off_ref[base]
        hi = off_ref[base + 1]
        st = jnp.minimum(jnp.bitwise_and(lo, -16), cap - ROUTE_WIN)
        fits = hi <= st + ROUTE_WIN
        ok = fits if ok is None else jnp.logical_and(ok, fits)
        starts.append(pl.multiple_of(st, 16))
    return starts, ok


def _dispatch_kernel(off_ref, slot_ref, aff_ref, h_ref, xe_ref, gate_ref, *, cap):
    bi = pl.program_id(0)
    ti = pl.program_id(1)

    @pl.when(ti == 0)
    def _():
        xe_ref[...] = jnp.zeros_like(xe_ref)
        gate_ref[...] = jnp.zeros_like(gate_ref)

    nsub = h_ref.shape[1] // ROUTE_TILE
    col_groups = [slice(sub * ROUTE_TILE, (sub + 1) * ROUTE_TILE) for sub in range(nsub)]
    geo = [_window_starts(off_ref, bi, ti * nsub + sub, cap) for sub in range(nsub)]
    ok = functools.reduce(jnp.logical_and, [fits for _, fits in geo])

    @pl.when(ok)
    def _():
        row = lax.broadcasted_iota(jnp.int32, (ROUTE_WIN, ROUTE_TILE), 0)
        for cols, (starts, _) in zip(col_groups, geo):
            slot = slot_ref[0, :, cols]
            aff = aff_ref[0, :, cols]
            for e0 in range(0, N_EXPERTS, GATHER_GROUP):
                experts = range(e0, e0 + GATHER_GROUP)
                hits = [row == (slot[e:e + 1, :] - starts[e]) for e in experts]
                onehot = jnp.concatenate([jnp.where(hh, 1.0, 0.0).astype(BF16) for hh in hits], axis=0)
                res = _dot(onehot, h_ref[0, cols, :])
                for k, (e, hh) in enumerate(zip(experts, hits)):
                    win = pl.ds(starts[e], ROUTE_WIN)
                    xe_ref[e, win, :] += res[k * ROUTE_WIN:(k + 1) * ROUTE_WIN].astype(BF16)
                    gate_ref[e, win, :] += jnp.sum(jnp.where(hh, aff[e:e + 1, :], 0.0), axis=-1, keepdims=True)

    @pl.when(jnp.logical_not(ok))
    def _():
        row = lax.broadcasted_iota(jnp.int32, (cap, ROUTE_TILE), 0)
        for cols in col_groups:
            slot = slot_ref[0, :, cols]
            aff = aff_ref[0, :, cols]
            h = h_ref[0, cols, :]
            for e in range(N_EXPERTS):
                hh = row == slot[e:e + 1, :]
                xe_ref[e] += _dot(jnp.where(hh, 1.0, 0.0).astype(BF16), h).astype(BF16)
                gate_ref[e] += jnp.sum(jnp.where(hh, aff[e:e + 1, :], 0.0), axis=-1, keepdims=True)


def _dispatch(off, slot, aff_t, h, cap, tm=1024):
    b, e, s = slot.shape
    assert tm % ROUTE_TILE == 0
    rows = pl.BlockSpec((1, e, tm), lambda bi, ti, off_ref: (bi, 0, ti))
    return pl.pallas_call(
        functools.partial(_dispatch_kernel, cap=cap),
        grid_spec=pltpu.PrefetchScalarGridSpec(
            num_scalar_prefetch=1,
            grid=(b, s // tm),
            in_specs=[rows, rows,
                      pl.BlockSpec((1, tm, D_MODEL), lambda bi, ti, off_ref: (bi, ti, 0))],
            out_specs=[
                pl.BlockSpec((e, cap, D_MODEL), lambda bi, ti, off_ref: (0, bi, 0)),
                pl.BlockSpec((e, cap, 1), lambda bi, ti, off_ref: (0, bi, 0)),
            ],
        ),
        out_shape=[
            jax.ShapeDtypeStruct((e, b * cap, D_MODEL), BF16),
            jax.ShapeDtypeStruct((e, b * cap, 1), F32),
        ],
        compiler_params=_params("arbitrary", "arbitrary"),
        name="dispatch",
    )(off, slot, aff_t, h)


def _experts_kernel(xe_ref, gate_ref, wg_ref, wu_ref, wd_ref, y_ref, acc_ref):
    f = pl.program_id(1)

    def hidden_chunk(first, last):
        wg = wg_ref[0].astype(BF16)
        wu = wu_ref[0].astype(BF16)
        wd = wd_ref[0].astype(BF16)
        for mb in range(xe_ref.shape[1] // EXPERT_ROWS):
            rows = slice(mb * EXPERT_ROWS, (mb + 1) * EXPERT_ROWS)
            xe = xe_ref[0, rows, :]
            a = _dot(xe, wg)
            b = _dot(xe, wu)
            part = _dot((a * jax.nn.sigmoid(a) * b).astype(BF16), wd)
            total = part if first else acc_ref[rows, :] + part
            if last:
                y_ref[0, rows, :] = (total * gate_ref[0, rows, :]).astype(y_ref.dtype)
            else:
                acc_ref[rows, :] = total

    nf = pl.num_programs(1)
    pl.when(f == 0)(functools.partial(hidden_chunk, True, False))
    pl.when((f > 0) & (f < nf - 1))(functools.partial(hidden_chunk, False, False))
    pl.when(f == nf - 1)(functools.partial(hidden_chunk, False, True))


def _experts(xe, gate, wg, wu, wd, tf=512):
    e, m, _ = xe.shape
    nf = D_EXPERT // tf
    assert nf >= 2
    return pl.pallas_call(
        _experts_kernel,
        grid=(e, nf),
        in_specs=[
            pl.BlockSpec((1, m, D_MODEL), lambda ei, f: (ei, 0, 0)),
            pl.BlockSpec((1, m, 1), lambda ei, f: (ei, 0, 0)),
            pl.BlockSpec((1, D_MODEL, tf), lambda ei, f: (ei, 0, f)),
            pl.BlockSpec((1, D_MODEL, tf), lambda ei, f: (ei, 0, f)),
            pl.BlockSpec((1, tf, D_MODEL), lambda ei, f: (ei, f, 0)),
        ],
        out_specs=pl.BlockSpec((1, m, D_MODEL), lambda ei, f: (ei, 0, 0)),
        out_shape=jax.ShapeDtypeStruct((e, m, D_MODEL), BF16),
        scratch_shapes=[pltpu.VMEM((m, D_MODEL), F32)],
        compiler_params=_params("arbitrary", "arbitrary"),
        name="experts",
    )(xe, gate, wg, wu, wd)


def _combine_kernel(off_ref, x1_ref, st_ref, y_ref, p_ref, gn_ref, wgbf_ref, wpbf_ref, gp_ref,
                    o_ref, *, cap):
    bi = pl.program_id(0)
    nsub = x1_ref.shape[1] // ROUTE_TILE
    row_groups = [slice(sub * ROUTE_TILE, (sub + 1) * ROUTE_TILE) for sub in range(nsub)]
    geo = [_window_starts(off_ref, bi, pl.program_id(1) * nsub + sub, cap) for sub in range(nsub)]
    ok = functools.reduce(jnp.logical_and, [fits for _, fits in geo])

    def token_slots(rows):
        return st_ref[0, rows, :]

    def windowed_scatter(rows, starts):
        st = token_slots(rows)
        lane = lax.broadcasted_iota(jnp.int32, (ROUTE_TILE, LANES), 1)
        low = lane < ROUTE_WIN
        total = None
        for g in range(N_EXPERTS // ROUTE_GROUP):
            halves = []
            wins = []
            for half in range(ROUTE_GROUP // 2):
                e0 = g * ROUTE_GROUP + 2 * half
                t0 = st[:, e0:e0 + 1] - starts[e0]
                t1 = st[:, e0 + 1:e0 + 2] + (ROUTE_WIN - starts[e0 + 1])
                halves.append(jnp.where(lane == jnp.where(low, t0, t1), 1.0, 0.0).astype(BF16))
                wins.append(y_ref[e0, pl.ds(starts[e0], ROUTE_WIN), :])
                wins.append(y_ref[e0 + 1, pl.ds(starts[e0 + 1], ROUTE_WIN), :])
            part = _dot(jnp.concatenate(halves, axis=1), jnp.concatenate(wins, axis=0))
            total = part if total is None else total + part
        return total

    def dense_scatter(rows):
        st = token_slots(rows)
        lane = lax.broadcasted_iota(jnp.int32, (ROUTE_TILE, cap), 1)
        total = None
        for e in range(N_EXPERTS):
            onehot = jnp.where(lane == st[:, e:e + 1], 1.0, 0.0).astype(BF16)
            part = _dot(onehot, y_ref[e])
            total = part if total is None else total + part
        return total

    def gate_stage(rows, ffn):
        x2 = x1_ref[0, rows, :] + ffn
        return x2, _dot(_rms(x2, gn_ref[...]).astype(BF16), wgbf_ref[...])

    def output_stage(rows, x2, gate_logits):
        emb = _rms(_dot(p_ref[0, rows, :].astype(BF16), wpbf_ref[...]), gp_ref[...])
        o_ref[0, rows, :] = x2 + jax.nn.sigmoid(gate_logits) * emb

    def pipeline(scatter_stage):
        per = PLE_ROWS // ROUTE_TILE
        groups = [slice(g * PLE_ROWS, (g + 1) * PLE_ROWS) for g in range(nsub // per)]
        ffn, gated = {}, {}
        for step in range(len(groups) + 2):
            if step < len(groups):
                ffn[step] = jnp.concatenate([scatter_stage(step * per + j) for j in range(per)], axis=0)
            if 0 <= step - 1 < len(groups):
                gated[step - 1] = gate_stage(groups[step - 1], ffn.pop(step - 1))
            if 0 <= step - 2 < len(groups):
                output_stage(groups[step - 2], *gated.pop(step - 2))

    pl.when(ok)(lambda: pipeline(lambda g: windowed_scatter(row_groups[g], geo[g][0])))
    pl.when(jnp.logical_not(ok))(lambda: pipeline(lambda g: dense_scatter(row_groups[g])))


def _combine(off, x1, slot_t, y, p, gn, wg, wp, gp, cap, tm=1024):
    b, s, _ = x1.shape
    assert tm % PLE_ROWS == 0 and PLE_ROWS % ROUTE_TILE == 0
    tile = lambda w: pl.BlockSpec((1, tm, w), lambda bi, i, off_ref: (bi, i, 0))
    fixed2 = lambda shape: pl.BlockSpec(shape, lambda bi, i, off_ref: (0, 0))
    return pl.pallas_call(
        functools.partial(_combine_kernel, cap=cap),
        grid_spec=pltpu.PrefetchScalarGridSpec(
            num_scalar_prefetch=1,
            grid=(b, s // tm),
            in_specs=[
                tile(D_MODEL),
                tile(LANES),
                pl.BlockSpec((N_EXPERTS, cap, D_MODEL), lambda bi, i, off_ref: (0, bi, 0)),
                tile(PLE_DIM),
                fixed2((1, D_MODEL)),
                pl.BlockSpec(wg.shape, lambda bi, i, off_ref: (0, 0), pipeline_mode=pl.Buffered(1)),
                pl.BlockSpec(wp.shape, lambda bi, i, off_ref: (0, 0), pipeline_mode=pl.Buffered(1)),
                fixed2((1, D_MODEL)),
            ],
            out_specs=tile(D_MODEL),
        ),
        out_shape=jax.ShapeDtypeStruct((b, s, D_MODEL), F32),
        compiler_params=_params("arbitrary", "arbitrary"),
        name="combine",
    )(off, x1, slot_t, y, p, gn, wg.astype(BF16), wp.astype(BF16), gp)


def kernel(x, p, norm_mix, w_in, w_pool, pool_scale, q_norm, k_norm, rpb, w_out, norm_ffn, w_router, w_gate, w_up, w_down, norm_ple, w_ple_gate, w_ple_proj, norm_ple_post):
    b, s, d = x.shape
    depth = w_in.shape[0]
    cap = EC_CAPACITY * s // N_EXPERTS
    row = lambda a: a.reshape(1, -1)
    for i in range(depth):
        u, q, k, v = _in_proj(x.reshape(b * s, d), row(norm_mix[i]), w_in[i],
                              row(jnp.tile(q_norm[i], ATTN_HEADS)), row(jnp.tile(k_norm[i], ATTN_HEADS)))
        shp = lambda a: a.reshape(b, s, -1)
        y_attn = _natten(shp(q), shp(k), shp(v), _attn_bias_table(rpb[i]))
        x1, h, aff_t = _mix(x, shp(u), y_attn, w_pool[i], row(pool_scale[i]), w_out[i],
                            row(norm_ffn[i]), w_router[i].T)
        slot, slot_t, off = _route(aff_t, cap)
        off = off[:, :OFF_STRIDE].reshape(-1)
        xe, gate = _dispatch(off, slot, aff_t, h, cap)
        y = _experts(xe, gate, w_gate[i], w_up[i], w_down[i])
        x = _combine(off, x1, slot_t, y, p[i], row(norm_ple[i]), w_ple_gate[i], w_ple_proj[i],
                     row(norm_ple_post[i]), cap)
    return x
```

```python
import functools

import jax
import jax.numpy as jnp
from jax import lax
from jax.experimental import pallas as pl
from jax.experimental.pallas import tpu as pltpu

D_MODEL = 1024
GRID_W = 64
POOL_WINDOWS = (2, 4, 8, 16)
POOL_WIDTH = D_MODEL // 2
POOL_GROUP = POOL_WIDTH // len(POOL_WINDOWS)
ATTN_HEADS = 8
HEAD_DIM = (D_MODEL // 2) // ATTN_HEADS
ATTN_WIDTH = ATTN_HEADS * HEAD_DIM
WIN_ROWS_MAX = 8
WIN_COLS = 16
N_EXPERTS = 16
EC_CAPACITY = 2
D_EXPERT = 2 * D_MODEL
PLE_DIM = 256
RMS_EPS = 1e-6

LANES = 128
POOL_HALO = 8
LOG2E = 1.4426950408889634
MASK_BIAS = -1e30
NATTEN_UNROLL = 2
ROUTE_TILE = 256
ROUTE_WIN = 64
ROUTE_GROUP = 4
OFF_STRIDE = 16
NOT_SELECTED = -(1 << 20)
assert 2 * ROUTE_WIN == LANES and ROUTE_GROUP % 2 == 0
PLE_ROWS = 256
MIX_ROWS = 512
PROJ_ROWS = 1024
EXPERT_ROWS = 1024
VMEM_LIMIT = 56 * 1024 * 1024

BF16 = jnp.bfloat16
F32 = jnp.float32


def _params(*sem):
    return pltpu.CompilerParams(dimension_semantics=sem, vmem_limit_bytes=VMEM_LIMIT)


def _rms(x, g):
    return x * lax.rsqrt(jnp.mean(x * x, axis=-1, keepdims=True) + RMS_EPS) * g


def _dot(a, b):
    return jnp.dot(a, b, preferred_element_type=F32)


def _dot_nt(a, b):
    return lax.dot_general(a, b, (((1,), (1,)), ((), ())), preferred_element_type=F32)


def _in_proj_kernel(x_ref, g_ref, w_ref, qg_ref, kg_ref,
                    u_ref, q_ref, k_ref, v_ref, wbf_ref):
    @pl.when(pl.program_id(0) == 0)
    def _():
        wbf_ref[...] = w_ref[...].astype(BF16)

    def head_norm(t, g):
        low = lax.broadcasted_iota(jnp.int32, (t.shape[0], LANES), 1) < HEAD_DIM
        out = []
        for j in range(t.shape[1] // LANES):
            tile = t[:, j * LANES:(j + 1) * LANES]
            sq = tile * tile
            sa = jnp.sum(jnp.where(low, sq, 0.0), axis=-1, keepdims=True)
            sb = jnp.sum(sq, axis=-1, keepdims=True) - sa
            ra = lax.rsqrt(sa * (1.0 / HEAD_DIM) + RMS_EPS)
            rb = lax.rsqrt(sb * (1.0 / HEAD_DIM) + RMS_EPS)
            out.append(tile * jnp.where(low, ra, rb) * g[:, j * LANES:(j + 1) * LANES])
        return jnp.concatenate(out, axis=-1)

    def norm_stage(rows):
        return _rms(x_ref[rows, :], g_ref[...]).astype(BF16)

    def proj(h, lo, width):
        return _dot(h, wbf_ref[:, lo:lo + width])

    q_lo, k_lo, v_lo = POOL_WIDTH, POOL_WIDTH + ATTN_WIDTH, POOL_WIDTH + 2 * ATTN_WIDTH
    for sub in range(x_ref.shape[0] // PROJ_ROWS):
        rows = slice(sub * PROJ_ROWS, (sub + 1) * PROJ_ROWS)
        h = norm_stage(rows)
        u_ref[rows, :] = proj(h, 0, POOL_WIDTH)
        q_ref[rows, :] = (head_norm(proj(h, q_lo, ATTN_WIDTH), qg_ref[...])
                          * (HEAD_DIM ** -0.5 * LOG2E)).astype(BF16)
        k_ref[rows, :] = head_norm(proj(h, k_lo, ATTN_WIDTH), kg_ref[...]).astype(BF16)
        v_ref[rows, :] = proj(h, v_lo, ATTN_WIDTH).astype(BF16)


def _in_proj(x2, g, w, qg, kg, tm=1024):
    n = x2.shape[0]
    zw = w.shape[1]
    assert 2 * HEAD_DIM == LANES
    row = lambda i: (i, 0)
    fixed = lambda i: (0, 0)
    return pl.pallas_call(
        _in_proj_kernel,
        grid=(n // tm,),
        in_specs=[
            pl.BlockSpec((tm, D_MODEL), row),
            pl.BlockSpec((1, D_MODEL), fixed),
            pl.BlockSpec((D_MODEL, zw), fixed, pipeline_mode=pl.Buffered(1)),
            pl.BlockSpec((1, ATTN_WIDTH), fixed),
            pl.BlockSpec((1, ATTN_WIDTH), fixed),
        ],
        out_specs=[
            pl.BlockSpec((tm, POOL_WIDTH), row),
            pl.BlockSpec((tm, ATTN_WIDTH), row),
            pl.BlockSpec((tm, ATTN_WIDTH), row),
            pl.BlockSpec((tm, ATTN_WIDTH), row),
        ],
        out_shape=[
            jax.ShapeDtypeStruct((n, POOL_WIDTH), F32),
            jax.ShapeDtypeStruct((n, ATTN_WIDTH), BF16),
            jax.ShapeDtypeStruct((n, ATTN_WIDTH), BF16),
            jax.ShapeDtypeStruct((n, ATTN_WIDTH), BF16),
        ],
        scratch_shapes=[pltpu.VMEM((D_MODEL, zw), BF16)],
        compiler_params=_params("arbitrary"),
        name="in_proj",
    )(x2, g, w, qg, kg)


def _natten_kernel(q_ref, k_ref, v_ref, tbl_ref, o_ref, bias_ref, *, rows, kh):
    band = kh * GRID_W
    lane = lax.broadcasted_iota(jnp.int32, (GRID_W, LANES), 1)
    first = lane < HEAD_DIM

    @pl.when(pl.program_id(1) == 0)
    def _():
        for hh in range(2):
            for d0 in range(WIN_ROWS_MAX):
                for kk in range(kh // 2):
                    bias_ref[d0, hh * GRID_W:(hh + 1) * GRID_W, kk * LANES:(kk + 1) * LANES] = jnp.where(
                        lane < GRID_W, tbl_ref[hh, d0 + 2 * kk], tbl_ref[hh, d0 + 2 * kk + 1])

    def geometry(g):
        geo = []
        for r in range(g * NATTEN_UNROLL, (g + 1) * NATTEN_UNROLL):
            rs = min(max(r - kh // 2, 0), rows - kh)
            geo.append((rs - r + (WIN_ROWS_MAX - 1), r * GRID_W, rs * GRID_W))
        return geo

    def score_stage(g):
        scores = []
        for d0, q0, k0 in geometry(g):
            qr = q_ref[0, q0:q0 + GRID_W, :]
            zero = jnp.zeros_like(qr)
            q2 = jnp.concatenate([jnp.where(first, qr, zero), jnp.where(first, zero, qr)], axis=0)
            s = _dot_nt(q2, k_ref[0, k0:k0 + band, :]) + bias_ref[d0]
            scores.append((s, jnp.max(s, axis=-1, keepdims=True)))
        return scores

    def softmax_stage(scores):
        probs = []
        for s, m in scores:
            e = jnp.exp2(s - m)
            probs.append((e.astype(BF16), jnp.sum(e, axis=-1, keepdims=True)))
        return probs

    def value_stage(g, probs):
        for (d0, q0, k0), (e, l) in zip(geometry(g), probs):
            o = _dot(e, v_ref[0, k0:k0 + band, :]) / l
            o_ref[0, q0:q0 + GRID_W, :] = jnp.where(first, o[:GRID_W], o[GRID_W:]).astype(o_ref.dtype)

    ngroups = rows // NATTEN_UNROLL
    scores, probs = {}, {}
    for step in range(ngroups + 2):
        if 0 <= step - 2 < ngroups:
            value_stage(step - 2, probs.pop(step - 2))
        if step < ngroups:
            scores[step] = score_stage(step)
        if 0 <= step - 1 < ngroups:
            probs[step - 1] = softmax_stage(scores.pop(step - 1))


def _natten(q, k, v, bias):
    b, s, _ = q.shape
    rows = s // GRID_W
    kh = min(WIN_ROWS_MAX, rows)
    assert rows % NATTEN_UNROLL == 0 and kh % 2 == 0 and 2 * GRID_W == LANES
    pairs = ATTN_WIDTH // LANES
    blk = pl.BlockSpec((1, s, LANES), lambda p, bi: (bi, 0, p))
    return pl.pallas_call(
        functools.partial(_natten_kernel, rows=rows, kh=kh),
        grid=(pairs, b),
        in_specs=[blk, blk, blk,
                  pl.BlockSpec((2,) + bias.shape[1:], lambda p, bi: (p, 0, 0, 0))],
        out_specs=blk,
        out_shape=jax.ShapeDtypeStruct((b, s, ATTN_WIDTH), BF16),
        scratch_shapes=[pltpu.VMEM((WIN_ROWS_MAX, 2 * GRID_W, kh * GRID_W), F32)],
        compiler_params=_params("arbitrary", "arbitrary"),
        name="natten",
    )(q, k, v, bias)


def _attn_bias_table(rpb):
    c = jnp.arange(GRID_W)
    cs = jnp.clip(c - WIN_COLS // 2, 0, GRID_W - WIN_COLS)
    j = jnp.arange(GRID_W)
    valid = (j[None, :] >= cs[:, None]) & (j[None, :] < cs[:, None] + WIN_COLS)
    dc = j[None, :] - c[:, None] + (WIN_COLS - 1)
    pick = ((dc[None] == jnp.arange(2 * WIN_COLS - 1)[:, None, None]) & valid[None]).astype(F32)
    t = jnp.einsum('hrd,dcj->hrcj', rpb.astype(F32), pick, precision=lax.Precision.HIGHEST)
    t = jnp.where(valid, t * LOG2E, MASK_BIAS)
    return jnp.concatenate([t, t], axis=-1)


def _window_sum(upad, w, tm):
    n = upad.shape[0]
    fwd = upad
    span = 1
    while span < min(w, POOL_HALO):
        fwd = fwd + pltpu.roll(fwd, n - span, axis=0)
        span *= 2
    centre = slice(POOL_HALO, POOL_HALO + tm)
    if w == 2 * POOL_HALO:
        return fwd[0:tm] + fwd[centre]
    return pltpu.roll(fwd, w // 2, axis=0)[centre]


def _mix_kernel(x_ref, u_ref, up_ref, un_ref, ya_ref, ic_ref, wp_ref, ps_ref, wo_ref, g_ref, wr_ref,
                x1_ref, h_ref, aff_ref, upad_ref, wobf_ref, wpbf_ref, *, tm):
    i = pl.program_id(1)

    @pl.when((pl.program_id(0) == 0) & (i == 0))
    def _():
        wobf_ref[...] = wo_ref[...].astype(BF16)
        wpbf_ref[...] = wp_ref[...].astype(BF16)

    upad_ref[0:POOL_HALO, :] = jnp.where(i > 0, up_ref[0], 0.0)
    upad_ref[POOL_HALO:POOL_HALO + tm, :] = u_ref[0]
    upad_ref[POOL_HALO + tm:, :] = jnp.where(i < pl.num_programs(1) - 1, un_ref[0], 0.0)

    ngroups = tm // MIX_ROWS
    group_cols = [slice(gi * POOL_GROUP, (gi + 1) * POOL_GROUP) for gi in range(len(POOL_WINDOWS))]

    def pool_stage(g):
        ds = []
        for cols, w in zip(group_cols, POOL_WINDOWS):
            upad = upad_ref[g * MIX_ROWS:(g + 1) * MIX_ROWS + 2 * POOL_HALO, cols]
            edge = jnp.full((POOL_HALO, POOL_GROUP), 1.0 / w, F32)
            head = jnp.where(i == 0, ic_ref[0:POOL_HALO, cols], 1.0 / w) if g == 0 else edge
            tail = (jnp.where(i == pl.num_programs(1) - 1, ic_ref[POOL_HALO:, cols], 1.0 / w)
                    if g == ngroups - 1 else edge)
            inv = jnp.concatenate(
                [head, jnp.full((MIX_ROWS - 2 * POOL_HALO, POOL_GROUP), 1.0 / w, F32), tail], axis=0)
            d = _window_sum(upad, w, MIX_ROWS) * inv - upad[POOL_HALO:POOL_HALO + MIX_ROWS]
            ds.append(d.astype(BF16))
        return ds

    def proj_stage(g, ds):
        rows = slice(g * MIX_ROWS, (g + 1) * MIX_ROWS)
        ypool = [_dot(d, wpbf_ref[gi]) * ps_ref[:, cols] for gi, (d, cols) in enumerate(zip(ds, group_cols))]
        ypool = jnp.concatenate(ypool, axis=-1).astype(BF16)
        return _dot(ypool, wobf_ref[:POOL_WIDTH, :]) + _dot(ya_ref[0, rows, :], wobf_ref[POOL_WIDTH:, :])

    def route_stage(g, mix):
        rows = slice(g * MIX_ROWS, (g + 1) * MIX_ROWS)
        x1 = x_ref[0, rows, :] + mix
        x1_ref[0, rows, :] = x1
        h = _rms(x1, g_ref[...]).astype(BF16)
        h_ref[0, rows, :] = h
        logits = _dot_nt(wr_ref[...].astype(BF16), h)
        m = jnp.max(logits, axis=0, keepdims=True)
        e = jnp.exp(logits - m)
        aff_ref[0, :, rows] = e / jnp.sum(e, axis=0, keepdims=True)

    pooled, mixed = {}, {}
    for step in range(ngroups + 2):
        if step < ngroups:
            pooled[step] = pool_stage(step)
        if 0 <= step - 1 < ngroups:
            mixed[step - 1] = proj_stage(step - 1, pooled.pop(step - 1))
        if 0 <= step - 2 < ngroups:
            route_stage(step - 2, mixed.pop(step - 2))


def _pool_edge_inverse_counts(s):
    t = jnp.concatenate([jnp.arange(POOL_HALO), jnp.arange(s - POOL_HALO, s)])
    cols = []
    for w in POOL_WINDOWS:
        lo = jnp.clip(t - w // 2, 0, s - 1)
        hi = jnp.clip(t + (w - w // 2) - 1, 0, s - 1)
        inv = 1.0 / (hi - lo + 1).astype(F32)
        cols.append(jnp.broadcast_to(inv[:, None], (2 * POOL_HALO, POOL_GROUP)))
    return jnp.concatenate(cols, axis=-1)


def _mix(x, u, ya, wp, ps, wo, g, wr_t, tm=1024):
    b, s, _ = x.shape
    nt = s // tm
    hb = tm // POOL_HALO
    tile = lambda w: pl.BlockSpec((1, tm, w), lambda bi, i: (bi, i, 0))
    fixed2 = lambda shape: pl.BlockSpec(shape, lambda bi, i: (0, 0))
    return pl.pallas_call(
        functools.partial(_mix_kernel, tm=tm),
        grid=(b, nt),
        in_specs=[
            tile(D_MODEL),
            tile(POOL_WIDTH),
            pl.BlockSpec((1, POOL_HALO, POOL_WIDTH), lambda bi, i: (bi, jnp.maximum(i * hb - 1, 0), 0)),
            pl.BlockSpec((1, POOL_HALO, POOL_WIDTH),
                         lambda bi, i: (bi, jnp.minimum((i + 1) * hb, s // POOL_HALO - 1), 0)),
            tile(ATTN_WIDTH),
            fixed2((2 * POOL_HALO, POOL_WIDTH)),
            pl.BlockSpec(wp.shape, lambda bi, i: (0, 0, 0)),
            fixed2((1, POOL_WIDTH)),
            pl.BlockSpec(wo.shape, lambda bi, i: (0, 0), pipeline_mode=pl.Buffered(1)),
            fixed2((1, D_MODEL)),
            fixed2(wr_t.shape),
        ],
        out_specs=[
            tile(D_MODEL),
            tile(D_MODEL),
            pl.BlockSpec((1, N_EXPERTS, tm), lambda bi, i: (bi, 0, i)),
        ],
        out_shape=[
            jax.ShapeDtypeStruct((b, s, D_MODEL), F32),
            jax.ShapeDtypeStruct((b, s, D_MODEL), BF16),
            jax.ShapeDtypeStruct((b, N_EXPERTS, s), F32),
        ],
        scratch_shapes=[
            pltpu.VMEM((tm + 2 * POOL_HALO, POOL_WIDTH), F32),
            pltpu.VMEM(wo.shape, BF16),
            pltpu.VMEM(wp.shape, BF16),
        ],
        compiler_params=_params("arbitrary", "arbitrary"),
        name="mix",
    )(x, u, u, u, ya, _pool_edge_inverse_counts(s), wp, ps, wo, g, wr_t)


def _lane_cumsum_exclusive(m):
    e, s = m.shape
    r = lax.broadcasted_iota(jnp.int32, (LANES, LANES), 0)
    c = lax.broadcasted_iota(jnp.int32, (LANES, LANES), 1)
    upper = (r < c).astype(BF16)
    carry = jnp.zeros((e, 1), F32)
    out = []
    for blk in range(s // LANES):
        piece = m[:, blk * LANES:(blk + 1) * LANES]
        out.append(_dot(piece.astype(BF16), upper) + carry)
        carry = carry + jnp.sum(piece, axis=-1, keepdims=True)
    return jnp.concatenate(out, axis=-1)


def _route_kernel(aff_ref, slot_ref, slot_t_ref, off_ref, *, cap):
    aff = aff_ref[...]
    capf = jnp.float32(cap)

    def count_ge(cand_bits):
        return jnp.sum((aff >= pltpu.bitcast(cand_bits, F32)).astype(F32), axis=-1, keepdims=True)

    def search(step, ans):
        cand = ans | (jnp.int32(1) << (30 - step))
        return jnp.where(count_ge(cand) >= capf, cand, ans)

    thr = pltpu.bitcast(lax.fori_loop(0, 31, search, jnp.zeros((aff.shape[0], 1), jnp.int32)), F32)
    gt = aff > thr
    eq = aff == thr
    need = capf - jnp.sum(gt.astype(F32), axis=-1, keepdims=True)
    sel = gt | (eq & (_lane_cumsum_exclusive(eq.astype(F32)) < need))
    self = sel.astype(F32)
    slot = jnp.where(sel, _lane_cumsum_exclusive(self), float(NOT_SELECTED))
    slot_ref[...] = slot.astype(jnp.int32)
    pad = jnp.full((LANES - N_EXPERTS, slot.shape[1]), float(NOT_SELECTED), F32)
    for bi in range(slot_t_ref.shape[0]):
        mine = slot[bi * N_EXPERTS:(bi + 1) * N_EXPERTS]
        slot_t_ref[bi] = jnp.concatenate([mine, pad], axis=0).T.astype(jnp.int32)
    lane = lax.broadcasted_iota(jnp.int32, (slot.shape[0], LANES), 1)
    off = jnp.zeros((slot.shape[0], LANES), F32)
    run = jnp.zeros((slot.shape[0], 1), F32)
    for i in range(1, slot.shape[1] // ROUTE_TILE + 1):
        run = run + jnp.sum(self[:, (i - 1) * ROUTE_TILE:i * ROUTE_TILE], axis=-1, keepdims=True)
        off = jnp.where(lane == i, run, off)
    off_ref[...] = off.astype(jnp.int32)


def _route(aff_t, cap):
    b, e, s = aff_t.shape
    assert e == N_EXPERTS and s % ROUTE_TILE == 0 and s // ROUTE_TILE < OFF_STRIDE
    slot, slot_t, off = pl.pallas_call(
        functools.partial(_route_kernel, cap=cap),
        out_shape=[
            jax.ShapeDtypeStruct((b * e, s), jnp.int32),
            jax.ShapeDtypeStruct((b, s, LANES), jnp.int32),
            jax.ShapeDtypeStruct((b * e, LANES), jnp.int32),
        ],
        compiler_params=pltpu.CompilerParams(vmem_limit_bytes=VMEM_LIMIT),
        name="route",
    )(aff_t.reshape(b * e, s))
    return slot.reshape(b, e, s), slot_t, off


def _window_starts(off_ref, bi, ti, cap):
    starts = []
    ok = None
    for e in range(N_EXPERTS):
        base = (bi * N_EXPERTS + e) * OFF_STRIDE + ti
        lo = off_ref[base]
        hi = off_ref[base + 1]
        st = jnp.minimum(jnp.bitwise_and(lo, -16), cap - ROUTE_WIN)
        fits = hi <= st + ROUTE_WIN
        ok = fits if ok is None else jnp.logical_and(ok, fits)
        starts.append(pl.multiple_of(st, 16))
    return starts, ok


def _dispatch_kernel(off_ref, slot_ref, aff_ref, h_ref, xe_ref, gate_ref, *, cap):
    bi = pl.program_id(0)
    ti = pl.program_id(1)

    @pl.when(ti == 0)
    def _():
        xe_ref[...] = jnp.zeros_like(xe_ref)
        gate_ref[...] = jnp.zeros_like(gate_ref)

    nsub = h_ref.shape[1] // ROUTE_TILE
    col_groups = [slice(sub * ROUTE_TILE, (sub + 1) * ROUTE_TILE) for sub in range(nsub)]
    geo = [_window_starts(off_ref, bi, ti * nsub + sub, cap) for sub in range(nsub)]
    ok = functools.reduce(jnp.logical_and, [fits for _, fits in geo])

    @pl.when(ok)
    def _():
        row = lax.broadcasted_iota(jnp.int32, (ROUTE_WIN, ROUTE_TILE), 0)
        for cols, (starts, _) in zip(col_groups, geo):
            slot = slot_ref[0, :, cols]
            aff = aff_ref[0, :, cols]
            hits = [row == (slot[e:e + 1, :] - starts[e]) for e in range(N_EXPERTS)]
            onehot = jnp.concatenate([jnp.where(hh, 1.0, 0.0).astype(BF16) for hh in hits], axis=0)
            res = _dot(onehot, h_ref[0, cols, :])
            for e, hh in enumerate(hits):
                win = pl.ds(starts[e], ROUTE_WIN)
                xe_ref[e, win, :] += res[e * ROUTE_WIN:(e + 1) * ROUTE_WIN].astype(BF16)
                gate_ref[e, win, :] += jnp.sum(jnp.where(hh, aff[e:e + 1, :], 0.0), axis=-1, keepdims=True)

    @pl.when(jnp.logical_not(ok))
    def _():
        row = lax.broadcasted_iota(jnp.int32, (cap, ROUTE_TILE), 0)
        for cols in col_groups:
            slot = slot_ref[0, :, cols]
            aff = aff_ref[0, :, cols]
            h = h_ref[0, cols, :]
            for e in range(N_EXPERTS):
                hh = row == slot[e:e + 1, :]
                xe_ref[e] += _dot(jnp.where(hh, 1.0, 0.0).astype(BF16), h).astype(BF16)
                gate_ref[e] += jnp.sum(jnp.where(hh, aff[e:e + 1, :], 0.0), axis=-1, keepdims=True)


def _dispatch(off, slot, aff_t, h, cap, tm=1024):
    b, e, s = slot.shape
    assert tm % ROUTE_TILE == 0
    rows = pl.BlockSpec((1, e, tm), lambda bi, ti, off_ref: (bi, 0, ti))
    return pl.pallas_call(
        functools.partial(_dispatch_kernel, cap=cap),
        grid_spec=pltpu.PrefetchScalarGridSpec(
            num_scalar_prefetch=1,
            grid=(b, s // tm),
            in_specs=[rows, rows,
                      pl.BlockSpec((1, tm, D_MODEL), lambda bi, ti, off_ref: (bi, ti, 0))],
            out_specs=[
                pl.BlockSpec((e, cap, D_MODEL), lambda bi, ti, off_ref: (0, bi, 0)),
                pl.BlockSpec((e, cap, 1), lambda bi, ti, off_ref: (0, bi, 0)),
            ],
        ),
        out_shape=[
            jax.ShapeDtypeStruct((e, b * cap, D_MODEL), BF16),
            jax.ShapeDtypeStruct((e, b * cap, 1), F32),
        ],
        compiler_params=_params("arbitrary", "arbitrary"),
        name="dispatch",
    )(off, slot, aff_t, h)


def _experts_kernel(xe_ref, gate_ref, wg_ref, wu_ref, wd_ref, y_ref, acc_ref):
    f = pl.program_id(1)

    def hidden_chunk(first, last):
        wg = wg_ref[0].astype(BF16)
        wu = wu_ref[0].astype(BF16)
        wd = wd_ref[0].astype(BF16)
        for mb in range(xe_ref.shape[1] // EXPERT_ROWS):
            rows = slice(mb * EXPERT_ROWS, (mb + 1) * EXPERT_ROWS)
            xe = xe_ref[0, rows, :]
            a = _dot(xe, wg)
            b = _dot(xe, wu)
            part = _dot((a * jax.nn.sigmoid(a) * b).astype(BF16), wd)
            total = part if first else acc_ref[rows, :] + part
            if last:
                y_ref[0, rows, :] = (total * gate_ref[0, rows, :]).astype(y_ref.dtype)
            else:
                acc_ref[rows, :] = total

    nf = pl.num_programs(1)
    pl.when(f == 0)(functools.partial(hidden_chunk, True, False))
    pl.when((f > 0) & (f < nf - 1))(functools.partial(hidden_chunk, False, False))
    pl.when(f == nf - 1)(functools.partial(hidden_chunk, False, True))


def _experts(xe, gate, wg, wu, wd, tf=512):
    e, m, _ = xe.shape
    nf = D_EXPERT // tf
    assert nf >= 2
    return pl.pallas_call(
        _experts_kernel,
        grid=(e, nf),
        in_specs=[
            pl.BlockSpec((1, m, D_MODEL), lambda ei, f: (ei, 0, 0)),
            pl.BlockSpec((1, m, 1), lambda ei, f: (ei, 0, 0)),
            pl.BlockSpec((1, D_MODEL, tf), lambda ei, f: (ei, 0, f)),
            pl.BlockSpec((1, D_MODEL, tf), lambda ei, f: (ei, 0, f)),
            pl.BlockSpec((1, tf, D_MODEL), lambda ei, f: (ei, f, 0)),
        ],
        out_specs=pl.BlockSpec((1, m, D_MODEL), lambda ei, f: (ei, 0, 0)),
        out_shape=jax.ShapeDtypeStruct((e, m, D_MODEL), BF16),
        scratch_shapes=[pltpu.VMEM((m, D_MODEL), F32)],
        compiler_params=_params("arbitrary", "arbitrary"),
        name="experts",
    )(xe, gate, wg, wu, wd)


def _combine_kernel(off_ref, x1_ref, st_ref, y_ref, p_ref, gn_ref, wgbf_ref, wpbf_ref, gp_ref,
                    o_ref, *, cap):
    bi = pl.program_id(0)
    nsub = x1_ref.shape[1] // ROUTE_TILE
    row_groups = [slice(sub * ROUTE_TILE, (sub + 1) * ROUTE_TILE) for sub in range(nsub)]
    geo = [_window_starts(off_ref, bi, pl.program_id(1) * nsub + sub, cap) for sub in range(nsub)]
    ok = functools.reduce(jnp.logical_and, [fits for _, fits in geo])

    def token_slots(rows):
        return st_ref[0, rows, :]

    def windowed_scatter(rows, starts):
        st = token_slots(rows)
        lane = lax.broadcasted_iota(jnp.int32, (ROUTE_TILE, LANES), 1)
        low = lane < ROUTE_WIN
        total = None
        for g in range(N_EXPERTS // ROUTE_GROUP):
            halves = []
            wins = []
            for half in range(ROUTE_GROUP // 2):
                e0 = g * ROUTE_GROUP + 2 * half
                t0 = st[:, e0:e0 + 1] - starts[e0]
                t1 = st[:, e0 + 1:e0 + 2] + (ROUTE_WIN - starts[e0 + 1])
                halves.append(jnp.where(lane == jnp.where(low, t0, t1), 1.0, 0.0).astype(BF16))
                wins.append(y_ref[e0, pl.ds(starts[e0], ROUTE_WIN), :])
                wins.append(y_ref[e0 + 1, pl.ds(starts[e0 + 1], ROUTE_WIN), :])
            part = _dot(jnp.concatenate(halves, axis=1), jnp.concatenate(wins, axis=0))
            total = part if total is None else total + part
        return total

    def dense_scatter(rows):
        st = token_slots(rows)
        lane = lax.broadcasted_iota(jnp.int32, (ROUTE_TILE, cap), 1)
        total = None
        for e in range(N_EXPERTS):
            onehot = jnp.where(lane == st[:, e:e + 1], 1.0, 0.0).astype(BF16)
            part = _dot(onehot, y_ref[e])
            total = part if total is None else total + part
        return total

    def gate_stage(rows, ffn):
        x2 = x1_ref[0, rows, :] + ffn
        return x2, _dot(_rms(x2, gn_ref[...]).astype(BF16), wgbf_ref[...])

    def output_stage(rows, x2, gate_logits):
        emb = _rms(_dot(p_ref[0, rows, :].astype(BF16), wpbf_ref[...]), gp_ref[...])
        o_ref[0, rows, :] = x2 + jax.nn.sigmoid(gate_logits) * emb

    def pipeline(scatter_stage):
        per = PLE_ROWS // ROUTE_TILE
        groups = [slice(g * PLE_ROWS, (g + 1) * PLE_ROWS) for g in range(nsub // per)]
        ffn, gated = {}, {}
        for step in range(len(groups) + 2):
            if step < len(groups):
                ffn[step] = jnp.concatenate([scatter_stage(step * per + j) for j in range(per)], axis=0)
            if 0 <= step - 1 < len(groups):
                gated[step - 1] = gate_stage(groups[step - 1], ffn.pop(step - 1))
            if 0 <= step - 2 < len(groups):
                output_stage(groups[step - 2], *gated.pop(step - 2))

    pl.when(ok)(lambda: pipeline(lambda g: windowed_scatter(row_groups[g], geo[g][0])))
    pl.when(jnp.logical_not(ok))(lambda: pipeline(lambda g: dense_scatter(row_groups[g])))


def _combine(off, x1, slot_t, y, p, gn, wg, wp, gp, cap, tm=1024):
    b, s, _ = x1.shape
    assert tm % PLE_ROWS == 0 and PLE_ROWS % ROUTE_TILE == 0
    tile = lambda w: pl.BlockSpec((1, tm, w), lambda bi, i, off_ref: (bi, i, 0))
    fixed2 = lambda shape: pl.BlockSpec(shape, lambda bi, i, off_ref: (0, 0))
    return pl.pallas_call(
        functools.partial(_combine_kernel, cap=cap),
        grid_spec=pltpu.PrefetchScalarGridSpec(
            num_scalar_prefetch=1,
            grid=(b, s // tm),
            in_specs=[
                tile(D_MODEL),
                tile(LANES),
                pl.BlockSpec((N_EXPERTS, cap, D_MODEL), lambda bi, i, off_ref: (0, bi, 0)),
                tile(PLE_DIM),
                fixed2((1, D_MODEL)),
                pl.BlockSpec(wg.shape, lambda bi, i, off_ref: (0, 0), pipeline_mode=pl.Buffered(1)),
                pl.BlockSpec(wp.shape, lambda bi, i, off_ref: (0, 0), pipeline_mode=pl.Buffered(1)),
                fixed2((1, D_MODEL)),
            ],
            out_specs=tile(D_MODEL),
        ),
        out_shape=jax.ShapeDtypeStruct((b, s, D_MODEL), F32),
        compiler_params=_params("arbitrary", "arbitrary"),
        name="combine",
    )(off, x1, slot_t, y, p, gn, wg.astype(BF16), wp.astype(BF16), gp)


def kernel(x, p, norm_mix, w_in, w_pool, pool_scale, q_norm, k_norm, rpb, w_out, norm_ffn, w_router, w_gate, w_up, w_down, norm_ple, w_ple_gate, w_ple_proj, norm_ple_post):
    b, s, d = x.shape
    depth = w_in.shape[0]
    cap = EC_CAPACITY * s // N_EXPERTS
    row = lambda a: a.reshape(1, -1)
    for i in range(depth):
        u, q, k, v = _in_proj(x.reshape(b * s, d), row(norm_mix[i]), w_in[i],
                              row(jnp.tile(q_norm[i], ATTN_HEADS)), row(jnp.tile(k_norm[i], ATTN_HEADS)))
        shp = lambda a: a.reshape(b, s, -1)
        y_attn = _natten(shp(q), shp(k), shp(v), _attn_bias_table(rpb[i]))
        x1, h, aff_t = _mix(x, shp(u), y_attn, w_pool[i], row(pool_scale[i]), w_out[i],
                            row(norm_ffn[i]), w_router[i].T)
        slot, slot_t, off = _route(aff_t, cap)
        off = off[:, :OFF_STRIDE].reshape(-1)
        xe, gate = _dispatch(off, slot, aff_t, h, cap)
        y = _experts(xe, gate, w_gate[i], w_up[i], w_down[i])
        x = _combine(off, x1, slot_t, y, p[i], row(norm_ple[i]), w_ple_gate[i], w_ple_proj[i],
                     row(norm_ple_post[i]), cap)
    return x
```

```python
import functools

import jax
import jax.numpy as jnp
from jax import lax
from jax.experimental import pallas as pl
from jax.experimental.pallas import tpu as pltpu

D_MODEL = 1024
GRID_W = 64
POOL_WINDOWS = (2, 4, 8, 16)
POOL_WIDTH = D_MODEL // 2
POOL_GROUP = POOL_WIDTH // len(POOL_WINDOWS)
ATTN_HEADS = 8
HEAD_DIM = (D_MODEL // 2) // ATTN_HEADS
ATTN_WIDTH = ATTN_HEADS * HEAD_DIM
WIN_ROWS_MAX = 8
WIN_COLS = 16
N_EXPERTS = 16
EC_CAPACITY = 2
D_EXPERT = 2 * D_MODEL
PLE_DIM = 256
RMS_EPS = 1e-6

LANES = 128
POOL_HALO = 8
LOG2E = 1.4426950408889634
MASK_BIAS = -1e30
NATTEN_UNROLL = 2
ROUTE_TILE = 256
ROUTE_WIN = 64
ROUTE_GROUP = 4
OFF_STRIDE = 16
NOT_SELECTED = -(1 << 20)
assert 2 * ROUTE_WIN == LANES and ROUTE_GROUP % 2 == 0
PLE_ROWS = 256
MIX_ROWS = 512
PROJ_ROWS = 1024
EXPERT_ROWS = 1024
VMEM_LIMIT = 56 * 1024 * 1024

BF16 = jnp.bfloat16
F32 = jnp.float32


def _params(*sem):
    return pltpu.CompilerParams(dimension_semantics=sem, vmem_limit_bytes=VMEM_LIMIT)


def _rms(x, g):
    return x * lax.rsqrt(jnp.mean(x * x, axis=-1, keepdims=True) + RMS_EPS) * g


def _dot(a, b):
    return jnp.dot(a, b, preferred_element_type=F32)


def _dot_nt(a, b):
    return lax.dot_general(a, b, (((1,), (1,)), ((), ())), preferred_element_type=F32)


def _in_proj_kernel(x_ref, g_ref, w_ref, qg_ref, kg_ref,
                    u_ref, q_ref, k_ref, v_ref, wbf_ref):
    @pl.when(pl.program_id(0) == 0)
    def _():
        wbf_ref[...] = w_ref[...].astype(BF16)

    def head_norm(t, g):
        low = lax.broadcasted_iota(jnp.int32, (t.shape[0], LANES), 1) < HEAD_DIM
        out = []
        for j in range(t.shape[1] // LANES):
            tile = t[:, j * LANES:(j + 1) * LANES]
            sq = tile * tile
            sa = jnp.sum(jnp.where(low, sq, 0.0), axis=-1, keepdims=True)
            sb = jnp.sum(sq, axis=-1, keepdims=True) - sa
            ra = lax.rsqrt(sa * (1.0 / HEAD_DIM) + RMS_EPS)
            rb = lax.rsqrt(sb * (1.0 / HEAD_DIM) + RMS_EPS)
            out.append(tile * jnp.where(low, ra, rb) * g[:, j * LANES:(j + 1) * LANES])
        return jnp.concatenate(out, axis=-1)

    def norm_stage(rows):
        return _rms(x_ref[rows, :], g_ref[...]).astype(BF16)

    def proj(h, lo, width):
        return _dot(h, wbf_ref[:, lo:lo + width])

    q_lo, k_lo, v_lo = POOL_WIDTH, POOL_WIDTH + ATTN_WIDTH, POOL_WIDTH + 2 * ATTN_WIDTH
    for sub in range(x_ref.shape[0] // PROJ_ROWS):
        rows = slice(sub * PROJ_ROWS, (sub + 1) * PROJ_ROWS)
        h = norm_stage(rows)
        u_ref[rows, :] = proj(h, 0, POOL_WIDTH)
        q_ref[rows, :] = (head_norm(proj(h, q_lo, ATTN_WIDTH), qg_ref[...])
                          * (HEAD_DIM ** -0.5 * LOG2E)).astype(BF16)
        k_ref[rows, :] = head_norm(proj(h, k_lo, ATTN_WIDTH), kg_ref[...]).astype(BF16)
        v_ref[rows, :] = proj(h, v_lo, ATTN_WIDTH).astype(BF16)


def _in_proj(x2, g, w, qg, kg, tm=1024):
    n = x2.shape[0]
    zw = w.shape[1]
    assert 2 * HEAD_DIM == LANES
    row = lambda i: (i, 0)
    fixed = lambda i: (0, 0)
    return pl.pallas_call(
        _in_proj_kernel,
        grid=(n // tm,),
        in_specs=[
            pl.BlockSpec((tm, D_MODEL), row),
            pl.BlockSpec((1, D_MODEL), fixed),
            pl.BlockSpec((D_MODEL, zw), fixed, pipeline_mode=pl.Buffered(1)),
            pl.BlockSpec((1, ATTN_WIDTH), fixed),
            pl.BlockSpec((1, ATTN_WIDTH), fixed),
        ],
        out_specs=[
            pl.BlockSpec((tm, POOL_WIDTH), row),
            pl.BlockSpec((tm, ATTN_WIDTH), row),
            pl.BlockSpec((tm, ATTN_WIDTH), row),
            pl.BlockSpec((tm, ATTN_WIDTH), row),
        ],
        out_shape=[
            jax.ShapeDtypeStruct((n, POOL_WIDTH), F32),
            jax.ShapeDtypeStruct((n, ATTN_WIDTH), BF16),
            jax.ShapeDtypeStruct((n, ATTN_WIDTH), BF16),
            jax.ShapeDtypeStruct((n, ATTN_WIDTH), BF16),
        ],
        scratch_shapes=[pltpu.VMEM((D_MODEL, zw), BF16)],
        compiler_params=_params("arbitrary"),
        name="in_proj",
    )(x2, g, w, qg, kg)


def _natten_kernel(q_ref, k_ref, v_ref, tbl_ref, o_ref, bias_ref, *, rows, kh):
    band = kh * GRID_W
    lane = lax.broadcasted_iota(jnp.int32, (GRID_W, LANES), 1)
    first = lane < HEAD_DIM

    @pl.when(pl.program_id(1) == 0)
    def _():
        for hh in range(2):
            for d0 in range(WIN_ROWS_MAX):
                for kk in range(kh // 2):
                    bias_ref[d0, hh * GRID_W:(hh + 1) * GRID_W, kk * LANES:(kk + 1) * LANES] = jnp.where(
                        lane < GRID_W, tbl_ref[hh, d0 + 2 * kk], tbl_ref[hh, d0 + 2 * kk + 1])

    def geometry(g):
        geo = []
        for r in range(g * NATTEN_UNROLL, (g + 1) * NATTEN_UNROLL):
            rs = min(max(r - kh // 2, 0), rows - kh)
            geo.append((rs - r + (WIN_ROWS_MAX - 1), r * GRID_W, rs * GRID_W))
        return geo

    def score_stage(g):
        scores = []
        for d0, q0, k0 in geometry(g):
            qr = q_ref[0, q0:q0 + GRID_W, :]
            zero = jnp.zeros_like(qr)
            q2 = jnp.concatenate([jnp.where(first, qr, zero), jnp.where(first, zero, qr)], axis=0)
            s = _dot_nt(q2, k_ref[0, k0:k0 + band, :]) + bias_ref[d0]
            scores.append((s, jnp.max(s, axis=-1, keepdims=True)))
        return scores

    def softmax_stage(scores):
        probs = []
        for s, m in scores:
            e = jnp.exp2(s - m)
            probs.append((e.astype(BF16), jnp.sum(e, axis=-1, keepdims=True)))
        return probs

    def value_stage(g, probs):
        for (d0, q0, k0), (e, l) in zip(geometry(g), probs):
            o = _dot(e, v_ref[0, k0:k0 + band, :]) / l
            o_ref[0, q0:q0 + GRID_W, :] = jnp.where(first, o[:GRID_W], o[GRID_W:]).astype(o_ref.dtype)

    ngroups = rows // NATTEN_UNROLL
    scores, probs = {}, {}
    for step in range(ngroups + 2):
        if step < ngroups:
            scores[step] = score_stage(step)
        if 0 <= step - 1 < ngroups:
            probs[step - 1] = softmax_stage(scores.pop(step - 1))
        if 0 <= step - 2 < ngroups:
            value_stage(step - 2, probs.pop(step - 2))


def _natten(q, k, v, bias):
    b, s, _ = q.shape
    rows = s // GRID_W
    kh = min(WIN_ROWS_MAX, rows)
    assert rows % NATTEN_UNROLL == 0 and kh % 2 == 0 and 2 * GRID_W == LANES
    pairs = ATTN_WIDTH // LANES
    blk = pl.BlockSpec((1, s, LANES), lambda p, bi: (bi, 0, p))
    return pl.pallas_call(
        functools.partial(_natten_kernel, rows=rows, kh=kh),
        grid=(pairs, b),
        in_specs=[blk, blk, blk,
                  pl.BlockSpec((2,) + bias.shape[1:], lambda p, bi: (p, 0, 0, 0))],
        out_specs=blk,
        out_shape=jax.ShapeDtypeStruct((b, s, ATTN_WIDTH), BF16),
        scratch_shapes=[pltpu.VMEM((WIN_ROWS_MAX, 2 * GRID_W, kh * GRID_W), F32)],
        compiler_params=_params("arbitrary", "arbitrary"),
        name="natten",
    )(q, k, v, bias)


def _attn_bias_table(rpb):
    c = jnp.arange(GRID_W)
    cs = jnp.clip(c - WIN_COLS // 2, 0, GRID_W - WIN_COLS)
    j = jnp.arange(GRID_W)
    valid = (j[None, :] >= cs[:, None]) & (j[None, :] < cs[:, None] + WIN_COLS)
    dc = j[None, :] - c[:, None] + (WIN_COLS - 1)
    pick = ((dc[None] == jnp.arange(2 * WIN_COLS - 1)[:, None, None]) & valid[None]).astype(F32)
    t = jnp.einsum('hrd,dcj->hrcj', rpb.astype(F32), pick, precision=lax.Precision.HIGHEST)
    t = jnp.where(valid, t * LOG2E, MASK_BIAS)
    return jnp.concatenate([t, t], axis=-1)


def _window_sum(upad, w, tm):
    n = upad.shape[0]
    fwd = upad
    span = 1
    while span < min(w, POOL_HALO):
        fwd = fwd + pltpu.roll(fwd, n - span, axis=0)
        span *= 2
    centre = slice(POOL_HALO, POOL_HALO + tm)
    if w == 2 * POOL_HALO:
        return fwd[0:tm] + fwd[centre]
    return pltpu.roll(fwd, w // 2, axis=0)[centre]


def _mix_kernel(x_ref, u_ref, up_ref, un_ref, ya_ref, ic_ref, wp_ref, ps_ref, wo_ref, g_ref, wr_ref,
                x1_ref, h_ref, aff_ref, upad_ref, wobf_ref, wpbf_ref, *, tm):
    i = pl.program_id(1)

    @pl.when((pl.program_id(0) == 0) & (i == 0))
    def _():
        wobf_ref[...] = wo_ref[...].astype(BF16)
        wpbf_ref[...] = wp_ref[...].astype(BF16)

    upad_ref[0:POOL_HALO, :] = jnp.where(i > 0, up_ref[0], 0.0)
    upad_ref[POOL_HALO:POOL_HALO + tm, :] = u_ref[0]
    upad_ref[POOL_HALO + tm:, :] = jnp.where(i < pl.num_programs(1) - 1, un_ref[0], 0.0)

    ngroups = tm // MIX_ROWS
    group_cols = [slice(gi * POOL_GROUP, (gi + 1) * POOL_GROUP) for gi in range(len(POOL_WINDOWS))]

    def pool_stage(g):
        ds = []
        for cols, w in zip(group_cols, POOL_WINDOWS):
            upad = upad_ref[g * MIX_ROWS:(g + 1) * MIX_ROWS + 2 * POOL_HALO, cols]
            edge = jnp.full((POOL_HALO, POOL_GROUP), 1.0 / w, F32)
            head = jnp.where(i == 0, ic_ref[0:POOL_HALO, cols], 1.0 / w) if g == 0 else edge
            tail = (jnp.where(i == pl.num_programs(1) - 1, ic_ref[POOL_HALO:, cols], 1.0 / w)
                    if g == ngroups - 1 else edge)
            inv = jnp.concatenate(
                [head, jnp.full((MIX_ROWS - 2 * POOL_HALO, POOL_GROUP), 1.0 / w, F32), tail], axis=0)
            d = _window_sum(upad, w, MIX_ROWS) * inv - upad[POOL_HALO:POOL_HALO + MIX_ROWS]
            ds.append(d.astype(BF16))
        return ds

    def proj_stage(g, ds):
        rows = slice(g * MIX_ROWS, (g + 1) * MIX_ROWS)
        ypool = [_dot(d, wpbf_ref[gi]) * ps_ref[:, cols] for gi, (d, cols) in enumerate(zip(ds, group_cols))]
        ypool = jnp.concatenate(ypool, axis=-1).astype(BF16)
        return _dot(ypool, wobf_ref[:POOL_WIDTH, :]) + _dot(ya_ref[0, rows, :], wobf_ref[POOL_WIDTH:, :])

    def route_stage(g, mix):
        rows = slice(g * MIX_ROWS, (g + 1) * MIX_ROWS)
        x1 = x_ref[0, rows, :] + mix
        x1_ref[0, rows, :] = x1
        h = _rms(x1, g_ref[...]).astype(BF16)
        h_ref[0, rows, :] = h
        logits = _dot_nt(wr_ref[...].astype(BF16), h)
        m = jnp.max(logits, axis=0, keepdims=True)
        e = jnp.exp(logits - m)
        aff_ref[0, :, rows] = e / jnp.sum(e, axis=0, keepdims=True)

    pooled, mixed = {}, {}
    for step in range(ngroups + 2):
        if step < ngroups:
            pooled[step] = pool_stage(step)
        if 0 <= step - 1 < ngroups:
            mixed[step - 1] = proj_stage(step - 1, pooled.pop(step - 1))
        if 0 <= step - 2 < ngroups:
            route_stage(step - 2, mixed.pop(step - 2))


def _pool_edge_inverse_counts(s):
    t = jnp.concatenate([jnp.arange(POOL_HALO), jnp.arange(s - POOL_HALO, s)])
    cols = []
    for w in POOL_WINDOWS:
        lo = jnp.clip(t - w // 2, 0, s - 1)
        hi = jnp.clip(t + (w - w // 2) - 1, 0, s - 1)
        inv = 1.0 / (hi - lo + 1).astype(F32)
        cols.append(jnp.broadcast_to(inv[:, None], (2 * POOL_HALO, POOL_GROUP)))
    return jnp.concatenate(cols, axis=-1)


def _mix(x, u, ya, wp, ps, wo, g, wr_t, tm=1024):
    b, s, _ = x.shape
    nt = s // tm
    hb = tm // POOL_HALO
    tile = lambda w: pl.BlockSpec((1, tm, w), lambda bi, i: (bi, i, 0))
    fixed2 = lambda shape: pl.BlockSpec(shape, lambda bi, i: (0, 0))
    return pl.pallas_call(
        functools.partial(_mix_kernel, tm=tm),
        grid=(b, nt),
        in_specs=[
            tile(D_MODEL),
            tile(POOL_WIDTH),
            pl.BlockSpec((1, POOL_HALO, POOL_WIDTH), lambda bi, i: (bi, jnp.maximum(i * hb - 1, 0), 0)),
            pl.BlockSpec((1, POOL_HALO, POOL_WIDTH),
                         lambda bi, i: (bi, jnp.minimum((i + 1) * hb, s // POOL_HALO - 1), 0)),
            tile(ATTN_WIDTH),
            fixed2((2 * POOL_HALO, POOL_WIDTH)),
            pl.BlockSpec(wp.shape, lambda bi, i: (0, 0, 0)),
            fixed2((1, POOL_WIDTH)),
            pl.BlockSpec(wo.shape, lambda bi, i: (0, 0), pipeline_mode=pl.Buffered(1)),
            fixed2((1, D_MODEL)),
            fixed2(wr_t.shape),
        ],
        out_specs=[
            tile(D_MODEL),
            tile(D_MODEL),
            pl.BlockSpec((1, N_EXPERTS, tm), lambda bi, i: (bi, 0, i)),
        ],
        out_shape=[
            jax.ShapeDtypeStruct((b, s, D_MODEL), F32),
            jax.ShapeDtypeStruct((b, s, D_MODEL), BF16),
            jax.ShapeDtypeStruct((b, N_EXPERTS, s), F32),
        ],
        scratch_shapes=[
            pltpu.VMEM((tm + 2 * POOL_HALO, POOL_WIDTH), F32),
            pltpu.VMEM(wo.shape, BF16),
            pltpu.VMEM(wp.shape, BF16),
        ],
        compiler_params=_params("arbitrary", "arbitrary"),
        name="mix",
    )(x, u, u, u, ya, _pool_edge_inverse_counts(s), wp, ps, wo, g, wr_t)


def _lane_cumsum_exclusive(m):
    e, s = m.shape
    r = lax.broadcasted_iota(jnp.int32, (LANES, LANES), 0)
    c = lax.broadcasted_iota(jnp.int32, (LANES, LANES), 1)
    upper = (r < c).astype(BF16)
    carry = jnp.zeros((e, 1), F32)
    out = []
    for blk in range(s // LANES):
        piece = m[:, blk * LANES:(blk + 1) * LANES]
        out.append(_dot(piece.astype(BF16), upper) + carry)
        carry = carry + jnp.sum(piece, axis=-1, keepdims=True)
    return jnp.concatenate(out, axis=-1)


def _route_kernel(aff_ref, slot_ref, slot_t_ref, off_ref, *, cap):
    aff = aff_ref[...]
    capf = jnp.float32(cap)

    def count_ge(cand_bits):
        return jnp.sum((aff >= pltpu.bitcast(cand_bits, F32)).astype(F32), axis=-1, keepdims=True)

    def search(step, ans):
        cand = ans | (jnp.int32(1) << (30 - step))
        return jnp.where(count_ge(cand) >= capf, cand, ans)

    thr = pltpu.bitcast(lax.fori_loop(0, 31, search, jnp.zeros((aff.shape[0], 1), jnp.int32)), F32)
    gt = aff > thr
    eq = aff == thr
    need = capf - jnp.sum(gt.astype(F32), axis=-1, keepdims=True)
    sel = gt | (eq & (_lane_cumsum_exclusive(eq.astype(F32)) < need))
    self = sel.astype(F32)
    slot = jnp.where(sel, _lane_cumsum_exclusive(self), float(NOT_SELECTED))
    slot_ref[...] = slot.astype(jnp.int32)
    pad = jnp.full((LANES - N_EXPERTS, slot.shape[1]), float(NOT_SELECTED), F32)
    for bi in range(slot_t_ref.shape[0]):
        mine = slot[bi * N_EXPERTS:(bi + 1) * N_EXPERTS]
        slot_t_ref[bi] = jnp.concatenate([mine, pad], axis=0).T.astype(jnp.int32)
    lane = lax.broadcasted_iota(jnp.int32, (slot.shape[0], LANES), 1)
    off = jnp.zeros((slot.shape[0], LANES), F32)
    run = jnp.zeros((slot.shape[0], 1), F32)
    for i in range(1, slot.shape[1] // ROUTE_TILE + 1):
        run = run + jnp.sum(self[:, (i - 1) * ROUTE_TILE:i * ROUTE_TILE], axis=-1, keepdims=True)
        off = jnp.where(lane == i, run, off)
    off_ref[...] = off.astype(jnp.int32)


def _route(aff_t, cap):
    b, e, s = aff_t.shape
    assert e == N_EXPERTS and s % ROUTE_TILE == 0 and s // ROUTE_TILE < OFF_STRIDE
    slot, slot_t, off = pl.pallas_call(
        functools.partial(_route_kernel, cap=cap),
        out_shape=[
            jax.ShapeDtypeStruct((b * e, s), jnp.int32),
            jax.ShapeDtypeStruct((b, s, LANES), jnp.int32),
            jax.ShapeDtypeStruct((b * e, LANES), jnp.int32),
        ],
        compiler_params=pltpu.CompilerParams(vmem_limit_bytes=VMEM_LIMIT),
        name="route",
    )(aff_t.reshape(b * e, s))
    return slot.reshape(b, e, s), slot_t, off


def _window_starts(off_ref, bi, ti, cap):
    starts = []
    ok = None
    for e in range(N_EXPERTS):
        base = (bi * N_EXPERTS + e) * OFF_STRIDE + ti
        lo = off_ref[base]
        hi = off_ref[base + 1]
        st = jnp.minimum(jnp.bitwise_and(lo, -16), cap - ROUTE_WIN)
        fits = hi <= st + ROUTE_WIN
        ok = fits if ok is None else jnp.logical_and(ok, fits)
        starts.append(pl.multiple_of(st, 16))
    return starts, ok


def _dispatch_kernel(off_ref, slot_ref, aff_ref, h_ref, xe_ref, gate_ref, *, cap):
    bi = pl.program_id(0)
    ti = pl.program_id(1)

    @pl.when(ti == 0)
    def _():
        xe_ref[...] = jnp.zeros_like(xe_ref)
        gate_ref[...] = jnp.zeros_like(gate_ref)

    nsub = h_ref.shape[1] // ROUTE_TILE
    col_groups = [slice(sub * ROUTE_TILE, (sub + 1) * ROUTE_TILE) for sub in range(nsub)]
    geo = [_window_starts(off_ref, bi, ti * nsub + sub, cap) for sub in range(nsub)]
    ok = functools.reduce(jnp.logical_and, [fits for _, fits in geo])

    @pl.when(ok)
    def _():
        row = lax.broadcasted_iota(jnp.int32, (ROUTE_WIN, ROUTE_TILE), 0)
        for cols, (starts, _) in zip(col_groups, geo):
            slot = slot_ref[0, :, cols]
            aff = aff_ref[0, :, cols]
            hits = [row == (slot[e:e + 1, :] - starts[e]) for e in range(N_EXPERTS)]
            onehot = jnp.concatenate([jnp.where(hh, 1.0, 0.0).astype(BF16) for hh in hits], axis=0)
            res = _dot(onehot, h_ref[0, cols, :])
            for e, hh in enumerate(hits):
                win = pl.ds(starts[e], ROUTE_WIN)
                xe_ref[e, win, :] += res[e * ROUTE_WIN:(e + 1) * ROUTE_WIN].astype(BF16)
                gate_ref[e, win, :] += jnp.sum(jnp.where(hh, aff[e:e + 1, :], 0.0), axis=-1, keepdims=True)

    @pl.when(jnp.logical_not(ok))
    def _():
        row = lax.broadcasted_iota(jnp.int32, (cap, ROUTE_TILE), 0)
        for cols in col_groups:
            slot = slot_ref[0, :, cols]
            aff = aff_ref[0, :, cols]
            h = h_ref[0, cols, :]
            for e in range(N_EXPERTS):
                hh = row == slot[e:e + 1, :]
                xe_ref[e] += _dot(jnp.where(hh, 1.0, 0.0).astype(BF16), h).astype(BF16)
                gate_ref[e] += jnp.sum(jnp.where(hh, aff[e:e + 1, :], 0.0), axis=-1, keepdims=True)


def _dispatch(off, slot, aff_t, h, cap, tm=1024):
    b, e, s = slot.shape
    assert tm % ROUTE_TILE == 0
    rows = pl.BlockSpec((1, e, tm), lambda bi, ti, off_ref: (bi, 0, ti))
    return pl.pallas_call(
        functools.partial(_dispatch_kernel, cap=cap),
        grid_spec=pltpu.PrefetchScalarGridSpec(
            num_scalar_prefetch=1,
            grid=(b, s // tm),
            in_specs=[rows, rows,
                      pl.BlockSpec((1, tm, D_MODEL), lambda bi, ti, off_ref: (bi, ti, 0))],
            out_specs=[
                pl.BlockSpec((e, cap, D_MODEL), lambda bi, ti, off_ref: (0, bi, 0)),
                pl.BlockSpec((e, cap, 1), lambda bi, ti, off_ref: (0, bi, 0)),
            ],
        ),
        out_shape=[
            jax.ShapeDtypeStruct((e, b * cap, D_MODEL), BF16),
            jax.ShapeDtypeStruct((e, b * cap, 1), F32),
        ],
        compiler_params=_params("arbitrary", "arbitrary"),
        name="dispatch",
    )(off, slot, aff_t, h)


def _experts_kernel(xe_ref, gate_ref, wg_ref, wu_ref, wd_ref, y_ref, acc_ref):
    f = pl.program_id(1)

    def hidden_chunk(first, last):
        wg = wg_ref[0].astype(BF16)
        wu = wu_ref[0].astype(BF16)
        wd = wd_ref[0].astype(BF16)
        groups = [slice(mb * EXPERT_ROWS, (mb + 1) * EXPERT_ROWS) for mb in range(xe_ref.shape[1] // EXPERT_ROWS)]

        def up_stage(rows):
            xe = xe_ref[0, rows, :]
            return _dot(xe, wg), _dot(xe, wu)

        def down_stage(rows, a, b):
            part = _dot((a * jax.nn.sigmoid(a) * b).astype(BF16), wd)
            total = part if first else acc_ref[rows, :] + part
            if last:
                y_ref[0, rows, :] = (total * gate_ref[0, rows, :]).astype(y_ref.dtype)
            else:
                acc_ref[rows, :] = total

        ups = {0: up_stage(groups[0])}
        for mb, rows in enumerate(groups):
            if mb + 1 < len(groups):
                ups[mb + 1] = up_stage(groups[mb + 1])
            down_stage(rows, *ups.pop(mb))

    nf = pl.num_programs(1)
    pl.when(f == 0)(functools.partial(hidden_chunk, True, False))
    pl.when((f > 0) & (f < nf - 1))(functools.partial(hidden_chunk, False, False))
    pl.when(f == nf - 1)(functools.partial(hidden_chunk, False, True))


def _experts(xe, gate, wg, wu, wd, tf=512):
    e, m, _ = xe.shape
    nf = D_EXPERT // tf
    assert nf >= 2
    return pl.pallas_call(
        _experts_kernel,
        grid=(e, nf),
        in_specs=[
            pl.BlockSpec((1, m, D_MODEL), lambda ei, f: (ei, 0, 0)),
            pl.BlockSpec((1, m, 1), lambda ei, f: (ei, 0, 0)),
            pl.BlockSpec((1, D_MODEL, tf), lambda ei, f: (ei, 0, f)),
            pl.BlockSpec((1, D_MODEL, tf), lambda ei, f: (ei, 0, f)),
            pl.BlockSpec((1, tf, D_MODEL), lambda ei, f: (ei, f, 0)),
        ],
        out_specs=pl.BlockSpec((1, m, D_MODEL), lambda ei, f: (ei, 0, 0)),
        out_shape=jax.ShapeDtypeStruct((e, m, D_MODEL), BF16),
        scratch_shapes=[pltpu.VMEM((m, D_MODEL), F32)],
        compiler_params=_params("arbitrary", "arbitrary"),
        name="experts",
    )(xe, gate, wg, wu, wd)


def _combine_kernel(off_ref, x1_ref, st_ref, y_ref, p_ref, gn_ref, wgbf_ref, wpbf_ref, gp_ref,
                    o_ref, *, cap):
    bi = pl.program_id(0)
    nsub = x1_ref.shape[1] // ROUTE_TILE
    row_groups = [slice(sub * ROUTE_TILE, (sub + 1) * ROUTE_TILE) for sub in range(nsub)]
    geo = [_window_starts(off_ref, bi, pl.program_id(1) * nsub + sub, cap) for sub in range(nsub)]
    ok = functools.reduce(jnp.logical_and, [fits for _, fits in geo])

    def token_slots(rows):
        return st_ref[0, rows, :]

    def windowed_scatter(rows, starts):
        st = token_slots(rows)
        lane = lax.broadcasted_iota(jnp.int32, (ROUTE_TILE, LANES), 1)
        low = lane < ROUTE_WIN
        total = None
        for g in range(N_EXPERTS // ROUTE_GROUP):
            halves = []
            wins = []
            for half in range(ROUTE_GROUP // 2):
                e0 = g * ROUTE_GROUP + 2 * half
                t0 = st[:, e0:e0 + 1] - starts[e0]
                t1 = st[:, e0 + 1:e0 + 2] + (ROUTE_WIN - starts[e0 + 1])
                halves.append(jnp.where(lane == jnp.where(low, t0, t1), 1.0, 0.0).astype(BF16))
                wins.append(y_ref[e0, pl.ds(starts[e0], ROUTE_WIN), :])
                wins.append(y_ref[e0 + 1, pl.ds(starts[e0 + 1], ROUTE_WIN), :])
            part = _dot(jnp.concatenate(halves, axis=1), jnp.concatenate(wins, axis=0))
            total = part if total is None else total + part
        return total

    def dense_scatter(rows):
        st = token_slots(rows)
        lane = lax.broadcasted_iota(jnp.int32, (ROUTE_TILE, cap), 1)
        total = None
        for e in range(N_EXPERTS):
            onehot = jnp.where(lane == st[:, e:e + 1], 1.0, 0.0).astype(BF16)
            part = _dot(onehot, y_ref[e])
            total = part if total is None else total + part
        return total

    def gate_stage(rows, ffn):
        x2 = x1_ref[0, rows, :] + ffn
        return x2, _dot(_rms(x2, gn_ref[...]).astype(BF16), wgbf_ref[...])

    def output_stage(rows, x2, gate_logits):
        emb = _rms(_dot(p_ref[0, rows, :].astype(BF16), wpbf_ref[...]), gp_ref[...])
        o_ref[0, rows, :] = x2 + jax.nn.sigmoid(gate_logits) * emb

    def pipeline(scatter_stage):
        per = PLE_ROWS // ROUTE_TILE
        groups = [slice(g * PLE_ROWS, (g + 1) * PLE_ROWS) for g in range(nsub // per)]
        ffn, gated = {}, {}
        for step in range(len(groups) + 2):
            if step < len(groups):
                ffn[step] = jnp.concatenate([scatter_stage(step * per + j) for j in range(per)], axis=0)
            if 0 <= step - 1 < len(groups):
                gated[step - 1] = gate_stage(groups[step - 1], ffn.pop(step - 1))
            if 0 <= step - 2 < len(groups):
                output_stage(groups[step - 2], *gated.pop(step - 2))

    pl.when(ok)(lambda: pipeline(lambda g: windowed_scatter(row_groups[g], geo[g][0])))
    pl.when(jnp.logical_not(ok))(lambda: pipeline(lambda g: dense_scatter(row_groups[g])))


def _combine(off, x1, slot_t, y, p, gn, wg, wp, gp, cap, tm=1024):
    b, s, _ = x1.shape
    assert tm % PLE_ROWS == 0 and PLE_ROWS % ROUTE_TILE == 0
    tile = lambda w: pl.BlockSpec((1, tm, w), lambda bi, i, off_ref: (bi, i, 0))
    fixed2 = lambda shape: pl.BlockSpec(shape, lambda bi, i, off_ref: (0, 0))
    return pl.pallas_call(
        functools.partial(_combine_kernel, cap=cap),
        grid_spec=pltpu.PrefetchScalarGridSpec(
            num_scalar_prefetch=1,
            grid=(b, s // tm),
            in_specs=[
                tile(D_MODEL),
                tile(LANES),
                pl.BlockSpec((N_EXPERTS, cap, D_MODEL), lambda bi, i, off_ref: (0, bi, 0)),
                tile(PLE_DIM),
                fixed2((1, D_MODEL)),
                pl.BlockSpec(wg.shape, lambda bi, i, off_ref: (0, 0), pipeline_mode=pl.Buffered(1)),
                pl.BlockSpec(wp.shape, lambda bi, i, off_ref: (0, 0), pipeline_mode=pl.Buffered(1)),
                fixed2((1, D_MODEL)),
            ],
            out_specs=tile(D_MODEL),
        ),
        out_shape=jax.ShapeDtypeStruct((b, s, D_MODEL), F32),
        compiler_params=_params("arbitrary", "arbitrary"),
        name="combine",
    )(off, x1, slot_t, y, p, gn, wg.astype(BF16), wp.astype(BF16), gp)


def kernel(x, p, norm_mix, w_in, w_pool, pool_scale, q_norm, k_norm, rpb, w_out, norm_ffn, w_router, w_gate, w_up, w_down, norm_ple, w_ple_gate, w_ple_proj, norm_ple_post):
    b, s, d = x.shape
    depth = w_in.shape[0]
    cap = EC_CAPACITY * s // N_EXPERTS
    row = lambda a: a.reshape(1, -1)
    for i in range(depth):
        u, q, k, v = _in_proj(x.reshape(b * s, d), row(norm_mix[i]), w_in[i],
                              row(jnp.tile(q_norm[i], ATTN_HEADS)), row(jnp.tile(k_norm[i], ATTN_HEADS)))
        shp = lambda a: a.reshape(b, s, -1)
        y_attn = _natten(shp(q), shp(k), shp(v), _attn_bias_table(rpb[i]))
        x1, h, aff_t = _mix(x, shp(u), y_attn, w_pool[i], row(pool_scale[i]), w_out[i],
                            row(norm_ffn[i]), w_router[i].T)
        slot, slot_t, off = _route(aff_t, cap)
        off = off[:, :OFF_STRIDE].reshape(-1)
        xe, gate = _dispatch(off, slot, aff_t, h, cap)
        y = _experts(xe, gate, w_gate[i], w_up[i], w_down[i])
        x = _combine(off, x1, slot_t, y, p[i], row(norm_ple[i]), w_ple_gate[i], w_ple_proj[i],
                     row(norm_ple_post[i]), cap)
    return x
```

```python
import functools

import jax
import jax.numpy as jnp
from jax import lax
from jax.experimental import pallas as pl
from jax.experimental.pallas import tpu as pltpu

D_MODEL = 1024
GRID_W = 64
POOL_WINDOWS = (2, 4, 8, 16)
POOL_WIDTH = D_MODEL // 2
POOL_GROUP = POOL_WIDTH // len(POOL_WINDOWS)
ATTN_HEADS = 8
HEAD_DIM = (D_MODEL // 2) // ATTN_HEADS
ATTN_WIDTH = ATTN_HEADS * HEAD_DIM
WIN_ROWS_MAX = 8
WIN_COLS = 16
N_EXPERTS = 16
EC_CAPACITY = 2
D_EXPERT = 2 * D_MODEL
PLE_DIM = 256
RMS_EPS = 1e-6

LANES = 128
POOL_HALO = 8
LOG2E = 1.4426950408889634
MASK_BIAS = -1e30
NATTEN_UNROLL = 2
ROUTE_TILE = 256
ROUTE_WIN = 64
ROUTE_GROUP = 4
OFF_STRIDE = 16
NOT_SELECTED = -(1 << 20)
assert 2 * ROUTE_WIN == LANES and ROUTE_GROUP % 2 == 0
PLE_ROWS = 256
MIX_ROWS = 512
PROJ_ROWS = 1024
EXPERT_ROWS = 1024
VMEM_LIMIT = 56 * 1024 * 1024

BF16 = jnp.bfloat16
F32 = jnp.float32


def _params(*sem):
    return pltpu.CompilerParams(dimension_semantics=sem, vmem_limit_bytes=VMEM_LIMIT)


def _rms(x, g):
    return x * lax.rsqrt(jnp.mean(x * x, axis=-1, keepdims=True) + RMS_EPS) * g


def _dot(a, b):
    return jnp.dot(a, b, preferred_element_type=F32)


def _dot_nt(a, b):
    return lax.dot_general(a, b, (((1,), (1,)), ((), ())), preferred_element_type=F32)


def _in_proj_kernel(x_ref, g_ref, w_ref, qg_ref, kg_ref,
                    u_ref, q_ref, k_ref, v_ref, wbf_ref):
    @pl.when(pl.program_id(0) == 0)
    def _():
        wbf_ref[...] = w_ref[...].astype(BF16)

    def head_norm(t, g):
        low = lax.broadcasted_iota(jnp.int32, (t.shape[0], LANES), 1) < HEAD_DIM
        out = []
        for j in range(t.shape[1] // LANES):
            tile = t[:, j * LANES:(j + 1) * LANES]
            sq = tile * tile
            sa = jnp.sum(jnp.where(low, sq, 0.0), axis=-1, keepdims=True)
            sb = jnp.sum(sq, axis=-1, keepdims=True) - sa
            ra = lax.rsqrt(sa * (1.0 / HEAD_DIM) + RMS_EPS)
            rb = lax.rsqrt(sb * (1.0 / HEAD_DIM) + RMS_EPS)
            out.append(tile * jnp.where(low, ra, rb) * g[:, j * LANES:(j + 1) * LANES])
        return jnp.concatenate(out, axis=-1)

    def norm_stage(rows):
        return _rms(x_ref[rows, :], g_ref[...]).astype(BF16)

    def proj(h, lo, width):
        return _dot(h, wbf_ref[:, lo:lo + width])

    q_lo, k_lo, v_lo = POOL_WIDTH, POOL_WIDTH + ATTN_WIDTH, POOL_WIDTH + 2 * ATTN_WIDTH
    for sub in range(x_ref.shape[0] // PROJ_ROWS):
        rows = slice(sub * PROJ_ROWS, (sub + 1) * PROJ_ROWS)
        h = norm_stage(rows)
        u_ref[rows, :] = proj(h, 0, POOL_WIDTH)
        q_ref[rows, :] = (head_norm(proj(h, q_lo, ATTN_WIDTH), qg_ref[...])
                          * (HEAD_DIM ** -0.5 * LOG2E)).astype(BF16)
        k_ref[rows, :] = head_norm(proj(h, k_lo, ATTN_WIDTH), kg_ref[...]).astype(BF16)
        v_ref[rows, :] = proj(h, v_lo, ATTN_WIDTH).astype(BF16)


def _in_proj(x2, g, w, qg, kg, tm=1024):
    n = x2.shape[0]
    zw = w.shape[1]
    assert 2 * HEAD_DIM == LANES
    row = lambda i: (i, 0)
    fixed = lambda i: (0, 0)
    return pl.pallas_call(
        _in_proj_kernel,
        grid=(n // tm,),
        in_specs=[
            pl.BlockSpec((tm, D_MODEL), row),
            pl.BlockSpec((1, D_MODEL), fixed),
            pl.BlockSpec((D_MODEL, zw), fixed, pipeline_mode=pl.Buffered(1)),
            pl.BlockSpec((1, ATTN_WIDTH), fixed),
            pl.BlockSpec((1, ATTN_WIDTH), fixed),
        ],
        out_specs=[
            pl.BlockSpec((tm, POOL_WIDTH), row),
            pl.BlockSpec((tm, ATTN_WIDTH), row),
            pl.BlockSpec((tm, ATTN_WIDTH), row),
            pl.BlockSpec((tm, ATTN_WIDTH), row),
        ],
        out_shape=[
            jax.ShapeDtypeStruct((n, POOL_WIDTH), F32),
            jax.ShapeDtypeStruct((n, ATTN_WIDTH), BF16),
            jax.ShapeDtypeStruct((n, ATTN_WIDTH), BF16),
            jax.ShapeDtypeStruct((n, ATTN_WIDTH), BF16),
        ],
        scratch_shapes=[pltpu.VMEM((D_MODEL, zw), BF16)],
        compiler_params=_params("arbitrary"),
        name="in_proj",
    )(x2, g, w, qg, kg)


def _natten_kernel(q_ref, k_ref, v_ref, tbl_ref, o_ref, bias_ref, *, rows, kh):
    band = kh * GRID_W
    lane = lax.broadcasted_iota(jnp.int32, (GRID_W, LANES), 1)
    first = lane < HEAD_DIM

    @pl.when(pl.program_id(1) == 0)
    def _():
        for hh in range(2):
            for d0 in range(WIN_ROWS_MAX):
                for kk in range(kh // 2):
                    bias_ref[d0, hh * GRID_W:(hh + 1) * GRID_W, kk * LANES:(kk + 1) * LANES] = jnp.where(
                        lane < GRID_W, tbl_ref[hh, d0 + 2 * kk], tbl_ref[hh, d0 + 2 * kk + 1])

    def geometry(g):
        geo = []
        for r in range(g * NATTEN_UNROLL, (g + 1) * NATTEN_UNROLL):
            rs = min(max(r - kh // 2, 0), rows - kh)
            geo.append((rs - r + (WIN_ROWS_MAX - 1), r * GRID_W, rs * GRID_W))
        return geo

    def score_stage(g):
        scores = []
        for d0, q0, k0 in geometry(g):
            qr = q_ref[0, q0:q0 + GRID_W, :]
            zero = jnp.zeros_like(qr)
            q2 = jnp.concatenate([jnp.where(first, qr, zero), jnp.where(first, zero, qr)], axis=0)
            s = _dot_nt(q2, k_ref[0, k0:k0 + band, :]) + bias_ref[d0]
            scores.append((s, jnp.max(s, axis=-1, keepdims=True)))
        return scores

    def softmax_stage(scores):
        probs = []
        for s, m in scores:
            e = jnp.exp2(s - m)
            probs.append((e.astype(BF16), jnp.sum(e, axis=-1, keepdims=True)))
        return probs

    def value_stage(g, probs):
        for (d0, q0, k0), (e, l) in zip(geometry(g), probs):
            o = _dot(e, v_ref[0, k0:k0 + band, :]) / l
            o_ref[0, q0:q0 + GRID_W, :] = jnp.where(first, o[:GRID_W], o[GRID_W:]).astype(o_ref.dtype)

    ngroups = rows // NATTEN_UNROLL
    scores, probs = {}, {}
    for step in range(ngroups + 2):
        if step < ngroups:
            scores[step] = score_stage(step)
        if 0 <= step - 1 < ngroups:
            probs[step - 1] = softmax_stage(scores.pop(step - 1))
        if 0 <= step - 2 < ngroups:
            value_stage(step - 2, probs.pop(step - 2))


def _natten(q, k, v, bias):
    b, s, _ = q.shape
    rows = s // GRID_W
    kh = min(WIN_ROWS_MAX, rows)
    assert rows % NATTEN_UNROLL == 0 and kh % 2 == 0 and 2 * GRID_W == LANES
    pairs = ATTN_WIDTH // LANES
    blk = pl.BlockSpec((1, s, LANES), lambda p, bi: (bi, 0, p))
    return pl.pallas_call(
        functools.partial(_natten_kernel, rows=rows, kh=kh),
        grid=(pairs, b),
        in_specs=[blk, blk, blk,
                  pl.BlockSpec((2,) + bias.shape[1:], lambda p, bi: (p, 0, 0, 0))],
        out_specs=blk,
        out_shape=jax.ShapeDtypeStruct((b, s, ATTN_WIDTH), BF16),
        scratch_shapes=[pltpu.VMEM((WIN_ROWS_MAX, 2 * GRID_W, kh * GRID_W), F32)],
        compiler_params=_params("arbitrary", "arbitrary"),
        name="natten",
    )(q, k, v, bias)


def _attn_bias_table(rpb):
    c = jnp.arange(GRID_W)
    cs = jnp.clip(c - WIN_COLS // 2, 0, GRID_W - WIN_COLS)
    j = jnp.arange(GRID_W)
    valid = (j[None, :] >= cs[:, None]) & (j[None, :] < cs[:, None] + WIN_COLS)
    dc = j[None, :] - c[:, None] + (WIN_COLS - 1)
    pick = ((dc[None] == jnp.arange(2 * WIN_COLS - 1)[:, None, None]) & valid[None]).astype(F32)
    t = jnp.einsum('hrd,dcj->hrcj', rpb.astype(F32), pick, precision=lax.Precision.HIGHEST)
    t = jnp.where(valid, t * LOG2E, MASK_BIAS)
    return jnp.concatenate([t, t], axis=-1)


def _window_sum(upad, w, tm):
    n = upad.shape[0]
    fwd = upad
    span = 1
    while span < min(w, POOL_HALO):
        fwd = fwd + pltpu.roll(fwd, n - span, axis=0)
        span *= 2
    centre = slice(POOL_HALO, POOL_HALO + tm)
    if w == 2 * POOL_HALO:
        return fwd[0:tm] + fwd[centre]
    return pltpu.roll(fwd, w // 2, axis=0)[centre]


def _mix_kernel(x_ref, u_ref, up_ref, un_ref, ya_ref, ic_ref, wp_ref, ps_ref, wo_ref, g_ref, wr_ref,
                x1_ref, h_ref, aff_ref, upad_ref, wobf_ref, wpbf_ref, *, tm):
    i = pl.program_id(1)

    @pl.when((pl.program_id(0) == 0) & (i == 0))
    def _():
        wobf_ref[...] = wo_ref[...].astype(BF16)
        wpbf_ref[...] = wp_ref[...].astype(BF16)

    upad_ref[0:POOL_HALO, :] = jnp.where(i > 0, up_ref[0], 0.0)
    upad_ref[POOL_HALO:POOL_HALO + tm, :] = u_ref[0]
    upad_ref[POOL_HALO + tm:, :] = jnp.where(i < pl.num_programs(1) - 1, un_ref[0], 0.0)

    ngroups = tm // MIX_ROWS
    group_cols = [slice(gi * POOL_GROUP, (gi + 1) * POOL_GROUP) for gi in range(len(POOL_WINDOWS))]

    def pool_stage(g):
        ds = []
        for cols, w in zip(group_cols, POOL_WINDOWS):
            upad = upad_ref[g * MIX_ROWS:(g + 1) * MIX_ROWS + 2 * POOL_HALO, cols]
            edge = jnp.full((POOL_HALO, POOL_GROUP), 1.0 / w, F32)
            head = jnp.where(i == 0, ic_ref[0:POOL_HALO, cols], 1.0 / w) if g == 0 else edge
            tail = (jnp.where(i == pl.num_programs(1) - 1, ic_ref[POOL_HALO:, cols], 1.0 / w)
                    if g == ngroups - 1 else edge)
            inv = jnp.concatenate(
                [head, jnp.full((MIX_ROWS - 2 * POOL_HALO, POOL_GROUP), 1.0 / w, F32), tail], axis=0)
            d = _window_sum(upad, w, MIX_ROWS) * inv - upad[POOL_HALO:POOL_HALO + MIX_ROWS]
            ds.append(d.astype(BF16))
        return ds

    def proj_stage(g, ds):
        rows = slice(g * MIX_ROWS, (g + 1) * MIX_ROWS)
        ypool = [_dot(d, wpbf_ref[gi]) * ps_ref[:, cols] for gi, (d, cols) in enumerate(zip(ds, group_cols))]
        ypool = jnp.concatenate(ypool, axis=-1).astype(BF16)
        return _dot(ypool, wobf_ref[:POOL_WIDTH, :]) + _dot(ya_ref[0, rows, :], wobf_ref[POOL_WIDTH:, :])

    def route_stage(g, mix):
        rows = slice(g * MIX_ROWS, (g + 1) * MIX_ROWS)
        x1 = x_ref[0, rows, :] + mix
        x1_ref[0, rows, :] = x1
        h = _rms(x1, g_ref[...]).astype(BF16)
        h_ref[0, rows, :] = h
        logits = _dot_nt(wr_ref[...].astype(BF16), h)
        m = jnp.max(logits, axis=0, keepdims=True)
        e = jnp.exp(logits - m)
        aff_ref[0, :, rows] = e / jnp.sum(e, axis=0, keepdims=True)

    pooled, mixed = {}, {}
    for step in range(ngroups + 2):
        if step < ngroups:
            pooled[step] = pool_stage(step)
        if 0 <= step - 1 < ngroups:
            mixed[step - 1] = proj_stage(step - 1, pooled.pop(step - 1))
        if 0 <= step - 2 < ngroups:
            route_stage(step - 2, mixed.pop(step - 2))


def _pool_edge_inverse_counts(s):
    t = jnp.concatenate([jnp.arange(POOL_HALO), jnp.arange(s - POOL_HALO, s)])
    cols = []
    for w in POOL_WINDOWS:
        lo = jnp.clip(t - w // 2, 0, s - 1)
        hi = jnp.clip(t + (w - w // 2) - 1, 0, s - 1)
        inv = 1.0 / (hi - lo + 1).astype(F32)
        cols.append(jnp.broadcast_to(inv[:, None], (2 * POOL_HALO, POOL_GROUP)))
    return jnp.concatenate(cols, axis=-1)


def _mix(x, u, ya, wp, ps, wo, g, wr_t, tm=1024):
    b, s, _ = x.shape
    nt = s // tm
    hb = tm // POOL_HALO
    tile = lambda w: pl.BlockSpec((1, tm, w), lambda bi, i: (bi, i, 0))
    fixed2 = lambda shape: pl.BlockSpec(shape, lambda bi, i: (0, 0))
    return pl.pallas_call(
        functools.partial(_mix_kernel, tm=tm),
        grid=(b, nt),
        in_specs=[
            tile(D_MODEL),
            tile(POOL_WIDTH),
            pl.BlockSpec((1, POOL_HALO, POOL_WIDTH), lambda bi, i: (bi, jnp.maximum(i * hb - 1, 0), 0)),
            pl.BlockSpec((1, POOL_HALO, POOL_WIDTH),
                         lambda bi, i: (bi, jnp.minimum((i + 1) * hb, s // POOL_HALO - 1), 0)),
            tile(ATTN_WIDTH),
            fixed2((2 * POOL_HALO, POOL_WIDTH)),
            pl.BlockSpec(wp.shape, lambda bi, i: (0, 0, 0)),
            fixed2((1, POOL_WIDTH)),
            pl.BlockSpec(wo.shape, lambda bi, i: (0, 0), pipeline_mode=pl.Buffered(1)),
            fixed2((1, D_MODEL)),
            fixed2(wr_t.shape),
        ],
        out_specs=[
            tile(D_MODEL),
            tile(D_MODEL),
            pl.BlockSpec((1, N_EXPERTS, tm), lambda bi, i: (bi, 0, i)),
        ],
        out_shape=[
            jax.ShapeDtypeStruct((b, s, D_MODEL), F32),
            jax.ShapeDtypeStruct((b, s, D_MODEL), BF16),
            jax.ShapeDtypeStruct((b, N_EXPERTS, s), F32),
        ],
        scratch_shapes=[
            pltpu.VMEM((tm + 2 * POOL_HALO, POOL_WIDTH), F32),
            pltpu.VMEM(wo.shape, BF16),
            pltpu.VMEM(wp.shape, BF16),
        ],
        compiler_params=_params("arbitrary", "arbitrary"),
        name="mix",
    )(x, u, u, u, ya, _pool_edge_inverse_counts(s), wp, ps, wo, g, wr_t)


def _lane_cumsum_exclusive(m):
    e, s = m.shape
    r = lax.broadcasted_iota(jnp.int32, (LANES, LANES), 0)
    c = lax.broadcasted_iota(jnp.int32, (LANES, LANES), 1)
    upper = (r < c).astype(BF16)
    carry = jnp.zeros((e, 1), F32)
    out = []
    for blk in range(s // LANES):
        piece = m[:, blk * LANES:(blk + 1) * LANES]
        out.append(_dot(piece.astype(BF16), upper) + carry)
        carry = carry + jnp.sum(piece, axis=-1, keepdims=True)
    return jnp.concatenate(out, axis=-1)


def _route_kernel(aff_ref, slot_ref, slot_t_ref, off_ref, *, cap):
    aff = aff_ref[...]
    capf = jnp.float32(cap)

    def count_ge(cand_bits):
        return jnp.sum((aff >= pltpu.bitcast(cand_bits, F32)).astype(F32), axis=-1, keepdims=True)

    def search(step, ans):
        cand = ans | (jnp.int32(1) << (30 - step))
        return jnp.where(count_ge(cand) >= capf, cand, ans)

    thr = pltpu.bitcast(lax.fori_loop(0, 31, search, jnp.zeros((aff.shape[0], 1), jnp.int32)), F32)
    gt = aff > thr
    eq = aff == thr
    need = capf - jnp.sum(gt.astype(F32), axis=-1, keepdims=True)
    sel = gt | (eq & (_lane_cumsum_exclusive(eq.astype(F32)) < need))
    self = sel.astype(F32)
    slot = jnp.where(sel, _lane_cumsum_exclusive(self), float(NOT_SELECTED))
    slot_ref[...] = slot.astype(jnp.int32)
    pad = jnp.full((LANES - N_EXPERTS, slot.shape[1]), float(NOT_SELECTED), F32)
    for bi in range(slot_t_ref.shape[0]):
        mine = slot[bi * N_EXPERTS:(bi + 1) * N_EXPERTS]
        slot_t_ref[bi] = jnp.concatenate([mine, pad], axis=0).T.astype(jnp.int32)
    lane = lax.broadcasted_iota(jnp.int32, (slot.shape[0], LANES), 1)
    off = jnp.zeros((slot.shape[0], LANES), F32)
    run = jnp.zeros((slot.shape[0], 1), F32)
    for i in range(1, slot.shape[1] // ROUTE_TILE + 1):
        run = run + jnp.sum(self[:, (i - 1) * ROUTE_TILE:i * ROUTE_TILE], axis=-1, keepdims=True)
        off = jnp.where(lane == i, run, off)
    off_ref[...] = off.astype(jnp.int32)


def _route(aff_t, cap):
    b, e, s = aff_t.shape
    assert e == N_EXPERTS and s % ROUTE_TILE == 0 and s // ROUTE_TILE < OFF_STRIDE
    slot, slot_t, off = pl.pallas_call(
        functools.partial(_route_kernel, cap=cap),
        out_shape=[
            jax.ShapeDtypeStruct((b * e, s), jnp.int32),
            jax.ShapeDtypeStruct((b, s, LANES), jnp.int32),
            jax.ShapeDtypeStruct((b * e, LANES), jnp.int32),
        ],
        compiler_params=pltpu.CompilerParams(vmem_limit_bytes=VMEM_LIMIT),
        name="route",
    )(aff_t.reshape(b * e, s))
    return slot.reshape(b, e, s), slot_t, off


def _window_starts(off_ref, bi, ti, cap):
    starts = []
    ok = None
    for e in range(N_EXPERTS):
        base = (bi * N_EXPERTS + e) * OFF_STRIDE + ti
        lo = off_ref[base]
        hi = off_ref[base + 1]
        st = jnp.minimum(jnp.bitwise_and(lo, -16), cap - ROUTE_WIN)
        fits = hi <= st + ROUTE_WIN
        ok = fits if ok is None else jnp.logical_and(ok, fits)
        starts.append(pl.multiple_of(st, 16))
    return starts, ok


def _dispatch_kernel(off_ref, slot_ref, aff_ref, h_ref, xe_ref, gate_ref, *, cap):
    bi = pl.program_id(0)
    ti = pl.program_id(1)

    @pl.when(ti == 0)
    def _():
        xe_ref[...] = jnp.zeros_like(xe_ref)
        gate_ref[...] = jnp.zeros_like(gate_ref)

    nsub = h_ref.shape[1] // ROUTE_TILE
    col_groups = [slice(sub * ROUTE_TILE, (sub + 1) * ROUTE_TILE) for sub in range(nsub)]
    geo = [_window_starts(off_ref, bi, ti * nsub + sub, cap) for sub in range(nsub)]
    ok = functools.reduce(jnp.logical_and, [fits for _, fits in geo])

    @pl.when(ok)
    def _():
        row = lax.broadcasted_iota(jnp.int32, (ROUTE_WIN, ROUTE_TILE), 0)
        for cols, (starts, _) in zip(col_groups, geo):
            slot = slot_ref[0, :, cols]
            aff = aff_ref[0, :, cols]
            hits = [row == (slot[e:e + 1, :] - starts[e]) for e in range(N_EXPERTS)]
            onehot = jnp.concatenate([jnp.where(hh, 1.0, 0.0).astype(BF16) for hh in hits], axis=0)
            res = _dot(onehot, h_ref[0, cols, :])
            for e, hh in enumerate(hits):
                win = pl.ds(starts[e], ROUTE_WIN)
                xe_ref[e, win, :] += res[e * ROUTE_WIN:(e + 1) * ROUTE_WIN].astype(BF16)
                gate_ref[e, win, :] += jnp.sum(jnp.where(hh, aff[e:e + 1, :], 0.0), axis=-1, keepdims=True)

    @pl.when(jnp.logical_not(ok))
    def _():
        row = lax.broadcasted_iota(jnp.int32, (cap, ROUTE_TILE), 0)
        for cols in col_groups:
            slot = slot_ref[0, :, cols]
            aff = aff_ref[0, :, cols]
            h = h_ref[0, cols, :]
            for e in range(N_EXPERTS):
                hh = row == slot[e:e + 1, :]
                xe_ref[e] += _dot(jnp.where(hh, 1.0, 0.0).astype(BF16), h).astype(BF16)
                gate_ref[e] += jnp.sum(jnp.where(hh, aff[e:e + 1, :], 0.0), axis=-1, keepdims=True)


def _dispatch(off, slot, aff_t, h, cap, tm=1024):
    b, e, s = slot.shape
    assert tm % ROUTE_TILE == 0
    rows = pl.BlockSpec((1, e, tm), lambda bi, ti, off_ref: (bi, 0, ti))
    return pl.pallas_call(
        functools.partial(_dispatch_kernel, cap=cap),
        grid_spec=pltpu.PrefetchScalarGridSpec(
            num_scalar_prefetch=1,
            grid=(b, s // tm),
            in_specs=[rows, rows,
                      pl.BlockSpec((1, tm, D_MODEL), lambda bi, ti, off_ref: (bi, ti, 0))],
            out_specs=[
                pl.BlockSpec((e, cap, D_MODEL), lambda bi, ti, off_ref: (0, bi, 0)),
                pl.BlockSpec((e, cap, 1), lambda bi, ti, off_ref: (0, bi, 0)),
            ],
        ),
        out_shape=[
            jax.ShapeDtypeStruct((e, b * cap, D_MODEL), BF16),
            jax.ShapeDtypeStruct((e, b * cap, 1), F32),
        ],
        compiler_params=_params("arbitrary", "arbitrary"),
        name="dispatch",
    )(off, slot, aff_t, h)


def _experts_kernel(xe_ref, gate_ref, wg_ref, wu_ref, wd_ref, y_ref, acc_ref):
    f = pl.program_id(1)

    def hidden_chunk(first, last):
        wg = wg_ref[0].astype(BF16)
        wu = wu_ref[0].astype(BF16)
        wd = wd_ref[0].astype(BF16)
        groups = [slice(mb * EXPERT_ROWS, (mb + 1) * EXPERT_ROWS) for mb in range(xe_ref.shape[1] // EXPERT_ROWS)]

        def up_stage(rows):
            xe = xe_ref[0, rows, :]
            return _dot(xe, wg), _dot(xe, wu)

        def down_stage(rows, a, b):
            part = _dot((a * jax.nn.sigmoid(a) * b).astype(BF16), wd)
            total = part if first else acc_ref[rows, :] + part
            if last:
                y_ref[0, rows, :] = (total * gate_ref[0, rows, :]).astype(y_ref.dtype)
            else:
                acc_ref[rows, :] = total

        ups = {0: up_stage(groups[0])}
        for mb, rows in enumerate(groups):
            if mb + 1 < len(groups):
                ups[mb + 1] = up_stage(groups[mb + 1])
            down_stage(rows, *ups.pop(mb))

    nf = pl.num_programs(1)
    pl.when(f == 0)(functools.partial(hidden_chunk, True, False))
    pl.when((f > 0) & (f < nf - 1))(functools.partial(hidden_chunk, False, False))
    pl.when(f == nf - 1)(functools.partial(hidden_chunk, False, True))


def _experts(xe, gate, wg, wu, wd, tf=512):
    e, m, _ = xe.shape
    nf = D_EXPERT // tf
    assert nf >= 2
    return pl.pallas_call(
        _experts_kernel,
        grid=(e, nf),
        in_specs=[
            pl.BlockSpec((1, m, D_MODEL), lambda ei, f: (ei, 0, 0)),
            pl.BlockSpec((1, m, 1), lambda ei, f: (ei, 0, 0)),
            pl.BlockSpec((1, D_MODEL, tf), lambda ei, f: (ei, 0, f)),
            pl.BlockSpec((1, D_MODEL, tf), lambda ei, f: (ei, 0, f)),
            pl.BlockSpec((1, tf, D_MODEL), lambda ei, f: (ei, f, 0)),
        ],
        out_specs=pl.BlockSpec((1, m, D_MODEL), lambda ei, f: (ei, 0, 0)),
        out_shape=jax.ShapeDtypeStruct((e, m, D_MODEL), BF16),
        scratch_shapes=[pltpu.VMEM((m, D_MODEL), F32)],
        compiler_params=_params("arbitrary", "arbitrary"),
        name="experts",
    )(xe, gate, wg, wu, wd)


def _combine_kernel(off_ref, x1_ref, st_ref, y_ref, p_ref, gn_ref, wgbf_ref, wpbf_ref, gp_ref,
                    o_ref, *, cap):
    bi = pl.program_id(0)
    nsub = x1_ref.shape[1] // ROUTE_TILE
    row_groups = [slice(sub * ROUTE_TILE, (sub + 1) * ROUTE_TILE) for sub in range(nsub)]
    geo = [_window_starts(off_ref, bi, pl.program_id(1) * nsub + sub, cap) for sub in range(nsub)]
    ok = functools.reduce(jnp.logical_and, [fits for _, fits in geo])

    def token_slots(rows):
        return st_ref[0, rows, :]

    def windowed_scatter(rows, starts):
        st = token_slots(rows)
        lane = lax.broadcasted_iota(jnp.int32, (ROUTE_TILE, LANES), 1)
        low = lane < ROUTE_WIN
        total = None
        for g in range(N_EXPERTS // ROUTE_GROUP):
            halves = []
            wins = []
            for half in range(ROUTE_GROUP // 2):
                e0 = g * ROUTE_GROUP + 2 * half
                t0 = st[:, e0:e0 + 1] - starts[e0]
                t1 = st[:, e0 + 1:e0 + 2] + (ROUTE_WIN - starts[e0 + 1])
                halves.append(jnp.where(lane == jnp.where(low, t0, t1), 1.0, 0.0).astype(BF16))
                wins.append(y_ref[e0, pl.ds(starts[e0], ROUTE_WIN), :])
                wins.append(y_ref[e0 + 1, pl.ds(starts[e0 + 1], ROUTE_WIN), :])
            part = _dot(jnp.concatenate(halves, axis=1), jnp.concatenate(wins, axis=0))
            total = part if total is None else total + part
        return total

    def dense_scatter(rows):
        st = token_slots(rows)
        lane = lax.broadcasted_iota(jnp.int32, (ROUTE_TILE, cap), 1)
        total = None
        for e in range(N_EXPERTS):
            onehot = jnp.where(lane == st[:, e:e + 1], 1.0, 0.0).astype(BF16)
            part = _dot(onehot, y_ref[e])
            total = part if total is None else total + part
        return total

    def gate_stage(rows, ffn):
        x2 = x1_ref[0, rows, :] + ffn
        emb = _rms(_dot(p_ref[0, rows, :].astype(BF16), wpbf_ref[...]), gp_ref[...])
        return x2, _dot(_rms(x2, gn_ref[...]).astype(BF16), wgbf_ref[...]), emb

    def output_stage(rows, x2, gate_logits, emb):
        o_ref[0, rows, :] = x2 + jax.nn.sigmoid(gate_logits) * emb

    def pipeline(scatter_stage):
        per = PLE_ROWS // ROUTE_TILE
        groups = [slice(g * PLE_ROWS, (g + 1) * PLE_ROWS) for g in range(nsub // per)]
        ffn, gated = {}, {}
        for step in range(len(groups) + 2):
            if step < len(groups):
                ffn[step] = jnp.concatenate([scatter_stage(step * per + j) for j in range(per)], axis=0)
            if 0 <= step - 1 < len(groups):
                gated[step - 1] = gate_stage(groups[step - 1], ffn.pop(step - 1))
            if 0 <= step - 2 < len(groups):
                output_stage(groups[step - 2], *gated.pop(step - 2))

    pl.when(ok)(lambda: pipeline(lambda g: windowed_scatter(row_groups[g], geo[g][0])))
    pl.when(jnp.logical_not(ok))(lambda: pipeline(lambda g: dense_scatter(row_groups[g])))


def _combine(off, x1, slot_t, y, p, gn, wg, wp, gp, cap, tm=1024):
    b, s, _ = x1.shape
    assert tm % PLE_ROWS == 0 and PLE_ROWS % ROUTE_TILE == 0
    tile = lambda w: pl.BlockSpec((1, tm, w), lambda bi, i, off_ref: (bi, i, 0))
    fixed2 = lambda shape: pl.BlockSpec(shape, lambda bi, i, off_ref: (0, 0))
    return pl.pallas_call(
        functools.partial(_combine_kernel, cap=cap),
        grid_spec=pltpu.PrefetchScalarGridSpec(
            num_scalar_prefetch=1,
            grid=(b, s // tm),
            in_specs=[
                tile(D_MODEL),
                tile(LANES),
                pl.BlockSpec((N_EXPERTS, cap, D_MODEL), lambda bi, i, off_ref: (0, bi, 0)),
                tile(PLE_DIM),
                fixed2((1, D_MODEL)),
                pl.BlockSpec(wg.shape, lambda bi, i, off_ref: (0, 0), pipeline_mode=pl.Buffered(1)),
                pl.BlockSpec(wp.shape, lambda bi, i, off_ref: (0, 0), pipeline_mode=pl.Buffered(1)),
                fixed2((1, D_MODEL)),
            ],
            out_specs=tile(D_MODEL),
        ),
        out_shape=jax.ShapeDtypeStruct((b, s, D_MODEL), F32),
        compiler_params=_params("arbitrary", "arbitrary"),
        name="combine",
    )(off, x1, slot_t, y, p, gn, wg.astype(BF16), wp.astype(BF16), gp)


def kernel(x, p, norm_mix, w_in, w_pool, pool_scale, q_norm, k_norm, rpb, w_out, norm_ffn, w_router, w_gate, w_up, w_down, norm_ple, w_ple_gate, w_ple_proj, norm_ple_post):
    b, s, d = x.shape
    depth = w_in.shape[0]
    cap = EC_CAPACITY * s // N_EXPERTS
    row = lambda a: a.reshape(1, -1)
    for i in range(depth):
        u, q, k, v = _in_proj(x.reshape(b * s, d), row(norm_mix[i]), w_in[i],
                              row(jnp.tile(q_norm[i], ATTN_HEADS)), row(jnp.tile(k_norm[i], ATTN_HEADS)))
        shp = lambda a: a.reshape(b, s, -1)
        y_attn = _natten(shp(q), shp(k), shp(v), _attn_bias_table(rpb[i]))
        x1, h, aff_t = _mix(x, shp(u), y_attn, w_pool[i], row(pool_scale[i]), w_out[i],
                            row(norm_ffn[i]), w_router[i].T)
        slot, slot_t, off = _route(aff_t, cap)
        off = off[:, :OFF_STRIDE].reshape(-1)
        xe, gate = _dispatch(off, slot, aff_t, h, cap)
        y = _experts(xe, gate, w_gate[i], w_up[i], w_down[i])
        x = _combine(off, x1, slot_t, y, p[i], row(norm_ple[i]), w_ple_gate[i], w_ple_proj[i],
                     row(norm_ple_post[i]), cap)
    return x
```

```python
import functools

import jax
import jax.numpy as jnp
from jax import lax
from jax.experimental import pallas as pl
from jax.experimental.pallas import tpu as pltpu

D_MODEL = 1024
GRID_W = 64
POOL_WINDOWS = (2, 4, 8, 16)
POOL_WIDTH = D_MODEL // 2
POOL_GROUP = POOL_WIDTH // len(POOL_WINDOWS)
ATTN_HEADS = 8
HEAD_DIM = (D_MODEL // 2) // ATTN_HEADS
ATTN_WIDTH = ATTN_HEADS * HEAD_DIM
WIN_ROWS_MAX = 8
WIN_COLS = 16
N_EXPERTS = 16
EC_CAPACITY = 2
D_EXPERT = 2 * D_MODEL
PLE_DIM = 256
RMS_EPS = 1e-6

LANES = 128
POOL_HALO = 8
LOG2E = 1.4426950408889634
MASK_BIAS = -1e30
NATTEN_UNROLL = 2
ROUTE_TILE = 256
ROUTE_WIN = 64
ROUTE_GROUP = 4
OFF_STRIDE = 16
NOT_SELECTED = -(1 << 20)
assert 2 * ROUTE_WIN == LANES and ROUTE_GROUP % 2 == 0
PLE_ROWS = 256
MIX_ROWS = 512
PROJ_ROWS = 1024
EXPERT_ROWS = 1024
VMEM_LIMIT = 56 * 1024 * 1024

BF16 = jnp.bfloat16
F32 = jnp.float32


def _params(*sem):
    return pltpu.CompilerParams(dimension_semantics=sem, vmem_limit_bytes=VMEM_LIMIT)


def _rms(x, g):
    return x * lax.rsqrt(jnp.mean(x * x, axis=-1, keepdims=True) + RMS_EPS) * g


def _dot(a, b):
    return jnp.dot(a, b, preferred_element_type=F32)


def _dot_nt(a, b):
    return lax.dot_general(a, b, (((1,), (1,)), ((), ())), preferred_element_type=F32)


def _in_proj_kernel(x_ref, g_ref, w_ref, qg_ref, kg_ref,
                    u_ref, q_ref, k_ref, v_ref, wbf_ref):
    @pl.when(pl.program_id(0) == 0)
    def _():
        wbf_ref[...] = w_ref[...].astype(BF16)

    def head_norm(t, g):
        low = lax.broadcasted_iota(jnp.int32, (t.shape[0], LANES), 1) < HEAD_DIM
        out = []
        for j in range(t.shape[1] // LANES):
            tile = t[:, j * LANES:(j + 1) * LANES]
            sq = tile * tile
            sa = jnp.sum(jnp.where(low, sq, 0.0), axis=-1, keepdims=True)
            sb = jnp.sum(sq, axis=-1, keepdims=True) - sa
            ra = lax.rsqrt(sa * (1.0 / HEAD_DIM) + RMS_EPS)
            rb = lax.rsqrt(sb * (1.0 / HEAD_DIM) + RMS_EPS)
            out.append(tile * jnp.where(low, ra, rb) * g[:, j * LANES:(j + 1) * LANES])
        return jnp.concatenate(out, axis=-1)

    def norm_stage(rows):
        return _rms(x_ref[rows, :], g_ref[...]).astype(BF16)

    def proj(h, lo, width):
        return _dot(h, wbf_ref[:, lo:lo + width])

    q_lo, k_lo, v_lo = POOL_WIDTH, POOL_WIDTH + ATTN_WIDTH, POOL_WIDTH + 2 * ATTN_WIDTH
    for sub in range(x_ref.shape[0] // PROJ_ROWS):
        rows = slice(sub * PROJ_ROWS, (sub + 1) * PROJ_ROWS)
        h = norm_stage(rows)
        u_ref[rows, :] = proj(h, 0, POOL_WIDTH)
        q_ref[rows, :] = (head_norm(proj(h, q_lo, ATTN_WIDTH), qg_ref[...])
                          * (HEAD_DIM ** -0.5 * LOG2E)).astype(BF16)
        k_ref[rows, :] = head_norm(proj(h, k_lo, ATTN_WIDTH), kg_ref[...]).astype(BF16)
        v_ref[rows, :] = proj(h, v_lo, ATTN_WIDTH).astype(BF16)


def _in_proj(x2, g, w, qg, kg, tm=1024):
    n = x2.shape[0]
    zw = w.shape[1]
    assert 2 * HEAD_DIM == LANES
    row = lambda i: (i, 0)
    fixed = lambda i: (0, 0)
    return pl.pallas_call(
        _in_proj_kernel,
        grid=(n // tm,),
        in_specs=[
            pl.BlockSpec((tm, D_MODEL), row),
            pl.BlockSpec((1, D_MODEL), fixed),
            pl.BlockSpec((D_MODEL, zw), fixed, pipeline_mode=pl.Buffered(1)),
            pl.BlockSpec((1, ATTN_WIDTH), fixed),
            pl.BlockSpec((1, ATTN_WIDTH), fixed),
        ],
        out_specs=[
            pl.BlockSpec((tm, POOL_WIDTH), row),
            pl.BlockSpec((tm, ATTN_WIDTH), row),
            pl.BlockSpec((tm, ATTN_WIDTH), row),
            pl.BlockSpec((tm, ATTN_WIDTH), row),
        ],
        out_shape=[
            jax.ShapeDtypeStruct((n, POOL_WIDTH), F32),
            jax.ShapeDtypeStruct((n, ATTN_WIDTH), BF16),
            jax.ShapeDtypeStruct((n, ATTN_WIDTH), BF16),
            jax.ShapeDtypeStruct((n, ATTN_WIDTH), BF16),
        ],
        scratch_shapes=[pltpu.VMEM((D_MODEL, zw), BF16)],
        compiler_params=_params("arbitrary"),
        name="in_proj",
    )(x2, g, w, qg, kg)


def _natten_kernel(q_ref, k_ref, v_ref, tbl_ref, o_ref, bias_ref, *, rows, kh):
    band = kh * GRID_W
    lane = lax.broadcasted_iota(jnp.int32, (GRID_W, LANES), 1)
    first = lane < HEAD_DIM

    @pl.when(pl.program_id(1) == 0)
    def _():
        for hh in range(2):
            for d0 in range(WIN_ROWS_MAX):
                for kk in range(kh // 2):
                    bias_ref[d0, hh * GRID_W:(hh + 1) * GRID_W, kk * LANES:(kk + 1) * LANES] = jnp.where(
                        lane < GRID_W, tbl_ref[hh, d0 + 2 * kk], tbl_ref[hh, d0 + 2 * kk + 1])

    def geometry(g):
        geo = []
        for r in range(g * NATTEN_UNROLL, (g + 1) * NATTEN_UNROLL):
            rs = min(max(r - kh // 2, 0), rows - kh)
            geo.append((rs - r + (WIN_ROWS_MAX - 1), r * GRID_W, rs * GRID_W))
        return geo

    def score_stage(g):
        scores = []
        for d0, q0, k0 in geometry(g):
            qr = q_ref[0, q0:q0 + GRID_W, :]
            zero = jnp.zeros_like(qr)
            q2 = jnp.concatenate([jnp.where(first, qr, zero), jnp.where(first, zero, qr)], axis=0)
            s = _dot_nt(q2, k_ref[0, k0:k0 + band, :]) + bias_ref[d0]
            scores.append((s, jnp.max(s, axis=-1, keepdims=True)))
        return scores

    def softmax_stage(scores):
        probs = []
        for s, m in scores:
            e = jnp.exp2(s - m)
            probs.append((e.astype(BF16), jnp.sum(e, axis=-1, keepdims=True)))
        return probs

    def value_stage(g, probs):
        for (d0, q0, k0), (e, l) in zip(geometry(g), probs):
            o = _dot(e, v_ref[0, k0:k0 + band, :]) / l
            o_ref[0, q0:q0 + GRID_W, :] = jnp.where(first, o[:GRID_W], o[GRID_W:]).astype(o_ref.dtype)

    ngroups = rows // NATTEN_UNROLL
    scores, probs = {}, {}
    for step in range(ngroups + 2):
        if step < ngroups:
            scores[step] = score_stage(step)
        if 0 <= step - 1 < ngroups:
            probs[step - 1] = softmax_stage(scores.pop(step - 1))
        if 0 <= step - 2 < ngroups:
            value_stage(step - 2, probs.pop(step - 2))


def _natten(q, k, v, bias):
    b, s, _ = q.shape
    rows = s // GRID_W
    kh = min(WIN_ROWS_MAX, rows)
    assert rows % NATTEN_UNROLL == 0 and kh % 2 == 0 and 2 * GRID_W == LANES
    pairs = ATTN_WIDTH // LANES
    blk = pl.BlockSpec((1, s, LANES), lambda p, bi: (bi, 0, p))
    return pl.pallas_call(
        functools.partial(_natten_kernel, rows=rows, kh=kh),
        grid=(pairs, b),
        in_specs=[blk, blk, blk,
                  pl.BlockSpec((2,) + bias.shape[1:], lambda p, bi: (p, 0, 0, 0))],
        out_specs=blk,
        out_shape=jax.ShapeDtypeStruct((b, s, ATTN_WIDTH), BF16),
        scratch_shapes=[pltpu.VMEM((WIN_ROWS_MAX, 2 * GRID_W, kh * GRID_W), F32)],
        compiler_params=_params("arbitrary", "arbitrary"),
        name="natten",
    )(q, k, v, bias)


def _attn_bias_table(rpb):
    c = jnp.arange(GRID_W)
    cs = jnp.clip(c - WIN_COLS // 2, 0, GRID_W - WIN_COLS)
    j = jnp.arange(GRID_W)
    valid = (j[None, :] >= cs[:, None]) & (j[None, :] < cs[:, None] + WIN_COLS)
    dc = j[None, :] - c[:, None] + (WIN_COLS - 1)
    pick = ((dc[None] == jnp.arange(2 * WIN_COLS - 1)[:, None, None]) & valid[None]).astype(F32)
    t = jnp.einsum('hrd,dcj->hrcj', rpb.astype(F32), pick, precision=lax.Precision.HIGHEST)
    t = jnp.where(valid, t * LOG2E, MASK_BIAS)
    return jnp.concatenate([t, t], axis=-1)


def _window_sum(upad, w, tm):
    n = upad.shape[0]
    fwd = upad
    span = 1
    while span < min(w, POOL_HALO):
        fwd = fwd + pltpu.roll(fwd, n - span, axis=0)
        span *= 2
    centre = slice(POOL_HALO, POOL_HALO + tm)
    if w == 2 * POOL_HALO:
        return fwd[0:tm] + fwd[centre]
    return pltpu.roll(fwd, w // 2, axis=0)[centre]


def _mix_kernel(x_ref, u_ref, up_ref, un_ref, ya_ref, ic_ref, wp_ref, ps_ref, wo_ref, g_ref, wr_ref,
                x1_ref, h_ref, aff_ref, upad_ref, wobf_ref, wpbf_ref, *, tm):
    i = pl.program_id(1)

    @pl.when((pl.program_id(0) == 0) & (i == 0))
    def _():
        wobf_ref[...] = wo_ref[...].astype(BF16)
        wpbf_ref[...] = wp_ref[...].astype(BF16)

    upad_ref[0:POOL_HALO, :] = jnp.where(i > 0, up_ref[0], 0.0)
    upad_ref[POOL_HALO:POOL_HALO + tm, :] = u_ref[0]
    upad_ref[POOL_HALO + tm:, :] = jnp.where(i < pl.num_programs(1) - 1, un_ref[0], 0.0)

    ngroups = tm // MIX_ROWS
    group_cols = [slice(gi * POOL_GROUP, (gi + 1) * POOL_GROUP) for gi in range(len(POOL_WINDOWS))]

    def pool_stage(g):
        ds = []
        for cols, w in zip(group_cols, POOL_WINDOWS):
            upad = upad_ref[g * MIX_ROWS:(g + 1) * MIX_ROWS + 2 * POOL_HALO, cols]
            edge = jnp.full((POOL_HALO, POOL_GROUP), 1.0 / w, F32)
            head = jnp.where(i == 0, ic_ref[0:POOL_HALO, cols], 1.0 / w) if g == 0 else edge
            tail = (jnp.where(i == pl.num_programs(1) - 1, ic_ref[POOL_HALO:, cols], 1.0 / w)
                    if g == ngroups - 1 else edge)
            inv = jnp.concatenate(
                [head, jnp.full((MIX_ROWS - 2 * POOL_HALO, POOL_GROUP), 1.0 / w, F32), tail], axis=0)
            d = _window_sum(upad, w, MIX_ROWS) * inv - upad[POOL_HALO:POOL_HALO + MIX_ROWS]
            ds.append(d.astype(BF16))
        return ds

    def proj_stage(g, ds):
        rows = slice(g * MIX_ROWS, (g + 1) * MIX_ROWS)
        ypool = [_dot(d, wpbf_ref[gi]) * ps_ref[:, cols] for gi, (d, cols) in enumerate(zip(ds, group_cols))]
        ypool = jnp.concatenate(ypool, axis=-1).astype(BF16)
        return _dot(ypool, wobf_ref[:POOL_WIDTH, :]) + _dot(ya_ref[0, rows, :], wobf_ref[POOL_WIDTH:, :])

    def route_stage(g, mix):
        rows = slice(g * MIX_ROWS, (g + 1) * MIX_ROWS)
        x1 = x_ref[0, rows, :] + mix
        x1_ref[0, rows, :] = x1
        h = _rms(x1, g_ref[...]).astype(BF16)
        h_ref[0, rows, :] = h
        logits = _dot_nt(wr_ref[...].astype(BF16), h)
        m = jnp.max(logits, axis=0, keepdims=True)
        e = jnp.exp(logits - m)
        aff_ref[0, :, rows] = e / jnp.sum(e, axis=0, keepdims=True)

    pooled, mixed = {}, {}
    for step in range(ngroups + 2):
        if step < ngroups:
            pooled[step] = pool_stage(step)
        if 0 <= step - 1 < ngroups:
            mixed[step - 1] = proj_stage(step - 1, pooled.pop(step - 1))
        if 0 <= step - 2 < ngroups:
            route_stage(step - 2, mixed.pop(step - 2))


def _pool_edge_inverse_counts(s):
    t = jnp.concatenate([jnp.arange(POOL_HALO), jnp.arange(s - POOL_HALO, s)])
    cols = []
    for w in POOL_WINDOWS:
        lo = jnp.clip(t - w // 2, 0, s - 1)
        hi = jnp.clip(t + (w - w // 2) - 1, 0, s - 1)
        inv = 1.0 / (hi - lo + 1).astype(F32)
        cols.append(jnp.broadcast_to(inv[:, None], (2 * POOL_HALO, POOL_GROUP)))
    return jnp.concatenate(cols, axis=-1)


def _mix(x, u, ya, wp, ps, wo, g, wr_t, tm=1024):
    b, s, _ = x.shape
    nt = s // tm
    hb = tm // POOL_HALO
    tile = lambda w: pl.BlockSpec((1, tm, w), lambda bi, i: (bi, i, 0))
    fixed2 = lambda shape: pl.BlockSpec(shape, lambda bi, i: (0, 0))
    return pl.pallas_call(
        functools.partial(_mix_kernel, tm=tm),
        grid=(b, nt),
        in_specs=[
            tile(D_MODEL),
            tile(POOL_WIDTH),
            pl.BlockSpec((1, POOL_HALO, POOL_WIDTH), lambda bi, i: (bi, jnp.maximum(i * hb - 1, 0), 0)),
            pl.BlockSpec((1, POOL_HALO, POOL_WIDTH),
                         lambda bi, i: (bi, jnp.minimum((i + 1) * hb, s // POOL_HALO - 1), 0)),
            tile(ATTN_WIDTH),
            fixed2((2 * POOL_HALO, POOL_WIDTH)),
            pl.BlockSpec(wp.shape, lambda bi, i: (0, 0, 0)),
            fixed2((1, POOL_WIDTH)),
            pl.BlockSpec(wo.shape, lambda bi, i: (0, 0), pipeline_mode=pl.Buffered(1)),
            fixed2((1, D_MODEL)),
            fixed2(wr_t.shape),
        ],
        out_specs=[
            tile(D_MODEL),
            tile(D_MODEL),
            pl.BlockSpec((1, N_EXPERTS, tm), lambda bi, i: (bi, 0, i)),
        ],
        out_shape=[
            jax.ShapeDtypeStruct((b, s, D_MODEL), F32),
            jax.ShapeDtypeStruct((b, s, D_MODEL), BF16),
            jax.ShapeDtypeStruct((b, N_EXPERTS, s), F32),
        ],
        scratch_shapes=[
            pltpu.VMEM((tm + 2 * POOL_HALO, POOL_WIDTH), F32),
            pltpu.VMEM(wo.shape, BF16),
            pltpu.VMEM(wp.shape, BF16),
        ],
        compiler_params=_params("arbitrary", "arbitrary"),
        name="mix",
    )(x, u, u, u, ya, _pool_edge_inverse_counts(s), wp, ps, wo, g, wr_t)


def _lane_cumsum_exclusive(m):
    e, s = m.shape
    r = lax.broadcasted_iota(jnp.int32, (LANES, LANES), 0)
    c = lax.broadcasted_iota(jnp.int32, (LANES, LANES), 1)
    upper = (r < c).astype(BF16)
    carry = jnp.zeros((e, 1), F32)
    out = []
    for blk in range(s // LANES):
        piece = m[:, blk * LANES:(blk + 1) * LANES]
        out.append(_dot(piece.astype(BF16), upper) + carry)
        carry = carry + jnp.sum(piece, axis=-1, keepdims=True)
    return jnp.concatenate(out, axis=-1)


def _route_kernel(aff_ref, slot_ref, slot_t_ref, off_ref, *, cap):
    aff = aff_ref[...]
    capf = jnp.float32(cap)

    def count_ge(cand_bits):
        return jnp.sum((aff >= pltpu.bitcast(cand_bits, F32)).astype(F32), axis=-1, keepdims=True)

    def search(step, ans):
        cand = ans | (jnp.int32(1) << (30 - step))
        return jnp.where(count_ge(cand) >= capf, cand, ans)

    thr = pltpu.bitcast(lax.fori_loop(0, 31, search, jnp.zeros((aff.shape[0], 1), jnp.int32)), F32)
    gt = aff > thr
    eq = aff == thr
    need = capf - jnp.sum(gt.astype(F32), axis=-1, keepdims=True)
    sel = gt | (eq & (_lane_cumsum_exclusive(eq.astype(F32)) < need))
    self = sel.astype(F32)
    slot = jnp.where(sel, _lane_cumsum_exclusive(self), float(NOT_SELECTED))
    slot_ref[...] = slot.astype(jnp.int32)
    pad = jnp.full((LANES - N_EXPERTS, slot.shape[1]), float(NOT_SELECTED), F32)
    for bi in range(slot_t_ref.shape[0]):
        mine = slot[bi * N_EXPERTS:(bi + 1) * N_EXPERTS]
        slot_t_ref[bi] = jnp.concatenate([mine, pad], axis=0).T.astype(jnp.int32)
    lane = lax.broadcasted_iota(jnp.int32, (slot.shape[0], LANES), 1)
    off = jnp.zeros((slot.shape[0], LANES), F32)
    run = jnp.zeros((slot.shape[0], 1), F32)
    for i in range(1, slot.shape[1] // ROUTE_TILE + 1):
        run = run + jnp.sum(self[:, (i - 1) * ROUTE_TILE:i * ROUTE_TILE], axis=-1, keepdims=True)
        off = jnp.where(lane == i, run, off)
    off_ref[...] = off.astype(jnp.int32)


def _route(aff_t, cap):
    b, e, s = aff_t.shape
    assert e == N_EXPERTS and s % ROUTE_TILE == 0 and s // ROUTE_TILE < OFF_STRIDE
    slot, slot_t, off = pl.pallas_call(
        functools.partial(_route_kernel, cap=cap),
        out_shape=[
            jax.ShapeDtypeStruct((b * e, s), jnp.int32),
            jax.ShapeDtypeStruct((b, s, LANES), jnp.int32),
            jax.ShapeDtypeStruct((b * e, LANES), jnp.int32),
        ],
        compiler_params=pltpu.CompilerParams(vmem_limit_bytes=VMEM_LIMIT),
        name="route",
    )(aff_t.reshape(b * e, s))
    return slot.reshape(b, e, s), slot_t, off


def _window_table(off, b, cap):
    lo = off[:, :OFF_STRIDE]
    hi = off[:, 1:OFF_STRIDE + 1]
    st = jnp.minimum(lo & -16, cap - ROUTE_WIN)
    fits = (hi <= st + ROUTE_WIN).reshape(b, N_EXPERTS, OFF_STRIDE).all(axis=1)
    return jnp.concatenate([st.reshape(-1), fits.astype(jnp.int32).reshape(-1)])


def _window_starts(off_ref, bi, ti, cap):
    del cap
    flags = off_ref.shape[0] // (N_EXPERTS + 1) * N_EXPERTS
    starts = [pl.multiple_of(off_ref[(bi * N_EXPERTS + e) * OFF_STRIDE + ti], 16) for e in range(N_EXPERTS)]
    return starts, off_ref[flags + bi * OFF_STRIDE + ti] != 0


def _dispatch_kernel(off_ref, slot_ref, aff_ref, h_ref, xe_ref, gate_ref, *, cap):
    bi = pl.program_id(0)
    ti = pl.program_id(1)

    @pl.when(ti == 0)
    def _():
        xe_ref[...] = jnp.zeros_like(xe_ref)
        gate_ref[...] = jnp.zeros_like(gate_ref)

    nsub = h_ref.shape[1] // ROUTE_TILE
    col_groups = [slice(sub * ROUTE_TILE, (sub + 1) * ROUTE_TILE) for sub in range(nsub)]
    geo = [_window_starts(off_ref, bi, ti * nsub + sub, cap) for sub in range(nsub)]
    ok = functools.reduce(jnp.logical_and, [fits for _, fits in geo])

    @pl.when(ok)
    def _():
        row = lax.broadcasted_iota(jnp.int32, (ROUTE_WIN, ROUTE_TILE), 0)
        for cols, (starts, _) in zip(col_groups, geo):
            slot = slot_ref[0, :, cols]
            aff = aff_ref[0, :, cols]
            hits = [row == (slot[e:e + 1, :] - starts[e]) for e in range(N_EXPERTS)]
            onehot = jnp.concatenate([jnp.where(hh, 1.0, 0.0).astype(BF16) for hh in hits], axis=0)
            res = _dot(onehot, h_ref[0, cols, :])
            for e, hh in enumerate(hits):
                win = pl.ds(starts[e], ROUTE_WIN)
                xe_ref[e, win, :] += res[e * ROUTE_WIN:(e + 1) * ROUTE_WIN].astype(BF16)
                gate_ref[e, win, :] += jnp.sum(jnp.where(hh, aff[e:e + 1, :], 0.0), axis=-1, keepdims=True)

    @pl.when(jnp.logical_not(ok))
    def _():
        row = lax.broadcasted_iota(jnp.int32, (cap, ROUTE_TILE), 0)
        for cols in col_groups:
            slot = slot_ref[0, :, cols]
            aff = aff_ref[0, :, cols]
            h = h_ref[0, cols, :]
            for e in range(N_EXPERTS):
                hh = row == slot[e:e + 1, :]
                xe_ref[e] += _dot(jnp.where(hh, 1.0, 0.0).astype(BF16), h).astype(BF16)
                gate_ref[e] += jnp.sum(jnp.where(hh, aff[e:e + 1, :], 0.0), axis=-1, keepdims=True)


def _dispatch(off, slot, aff_t, h, cap, tm=1024):
    b, e, s = slot.shape
    assert tm % ROUTE_TILE == 0
    rows = pl.BlockSpec((1, e, tm), lambda bi, ti, off_ref: (bi, 0, ti))
    return pl.pallas_call(
        functools.partial(_dispatch_kernel, cap=cap),
        grid_spec=pltpu.PrefetchScalarGridSpec(
            num_scalar_prefetch=1,
            grid=(b, s // tm),
            in_specs=[rows, rows,
                      pl.BlockSpec((1, tm, D_MODEL), lambda bi, ti, off_ref: (bi, ti, 0))],
            out_specs=[
                pl.BlockSpec((e, cap, D_MODEL), lambda bi, ti, off_ref: (0, bi, 0)),
                pl.BlockSpec((e, cap, 1), lambda bi, ti, off_ref: (0, bi, 0)),
            ],
        ),
        out_shape=[
            jax.ShapeDtypeStruct((e, b * cap, D_MODEL), BF16),
            jax.ShapeDtypeStruct((e, b * cap, 1), F32),
        ],
        compiler_params=_params("arbitrary", "arbitrary"),
        name="dispatch",
    )(off, slot, aff_t, h)


def _experts_kernel(xe_ref, gate_ref, wg_ref, wu_ref, wd_ref, y_ref, acc_ref):
    f = pl.program_id(1)

    def hidden_chunk(first, last):
        wg = wg_ref[0].astype(BF16)
        wu = wu_ref[0].astype(BF16)
        wd = wd_ref[0].astype(BF16)
        groups = [slice(mb * EXPERT_ROWS, (mb + 1) * EXPERT_ROWS) for mb in range(xe_ref.shape[1] // EXPERT_ROWS)]

        def up_stage(rows):
            xe = xe_ref[0, rows, :]
            return _dot(xe, wg), _dot(xe, wu)

        def down_stage(rows, a, b):
            part = _dot((a * jax.nn.sigmoid(a) * b).astype(BF16), wd)
            total = part if first else acc_ref[rows, :] + part
            if last:
                y_ref[0, rows, :] = (total * gate_ref[0, rows, :]).astype(y_ref.dtype)
            else:
                acc_ref[rows, :] = total

        ups = {0: up_stage(groups[0])}
        for mb, rows in enumerate(groups):
            if mb + 1 < len(groups):
                ups[mb + 1] = up_stage(groups[mb + 1])
            down_stage(rows, *ups.pop(mb))

    nf = pl.num_programs(1)
    pl.when(f == 0)(functools.partial(hidden_chunk, True, False))
    pl.when((f > 0) & (f < nf - 1))(functools.partial(hidden_chunk, False, False))
    pl.when(f == nf - 1)(functools.partial(hidden_chunk, False, True))


def _experts(xe, gate, wg, wu, wd, tf=512):
    e, m, _ = xe.shape
    nf = D_EXPERT // tf
    assert nf >= 2
    return pl.pallas_call(
        _experts_kernel,
        grid=(e, nf),
        in_specs=[
            pl.BlockSpec((1, m, D_MODEL), lambda ei, f: (ei, 0, 0)),
            pl.BlockSpec((1, m, 1), lambda ei, f: (ei, 0, 0)),
            pl.BlockSpec((1, D_MODEL, tf), lambda ei, f: (ei, 0, f)),
            pl.BlockSpec((1, D_MODEL, tf), lambda ei, f: (ei, 0, f)),
            pl.BlockSpec((1, tf, D_MODEL), lambda ei, f: (ei, f, 0)),
        ],
        out_specs=pl.BlockSpec((1, m, D_MODEL), lambda ei, f: (ei, 0, 0)),
        out_shape=jax.ShapeDtypeStruct((e, m, D_MODEL), BF16),
        scratch_shapes=[pltpu.VMEM((m, D_MODEL), F32)],
        compiler_params=_params("arbitrary", "arbitrary"),
        name="experts",
    )(xe, gate, wg, wu, wd)


def _combine_kernel(off_ref, x1_ref, st_ref, y_ref, p_ref, gn_ref, wgbf_ref, wpbf_ref, gp_ref,
                    o_ref, *, cap):
    bi = pl.program_id(0)
    nsub = x1_ref.shape[1] // ROUTE_TILE
    row_groups = [slice(sub * ROUTE_TILE, (sub + 1) * ROUTE_TILE) for sub in range(nsub)]
    geo = [_window_starts(off_ref, bi, pl.program_id(1) * nsub + sub, cap) for sub in range(nsub)]
    ok = functools.reduce(jnp.logical_and, [fits for _, fits in geo])

    def token_slots(rows):
        return st_ref[0, rows, :]

    def windowed_scatter(rows, starts):
        st = token_slots(rows)
        lane = lax.broadcasted_iota(jnp.int32, (ROUTE_TILE, LANES), 1)
        low = lane < ROUTE_WIN
        total = None
        for g in range(N_EXPERTS // ROUTE_GROUP):
            halves = []
            wins = []
            for half in range(ROUTE_GROUP // 2):
                e0 = g * ROUTE_GROUP + 2 * half
                t0 = st[:, e0:e0 + 1] - starts[e0]
                t1 = st[:, e0 + 1:e0 + 2] + (ROUTE_WIN - starts[e0 + 1])
                halves.append(jnp.where(lane == jnp.where(low, t0, t1), 1.0, 0.0).astype(BF16))
                wins.append(y_ref[e0, pl.ds(starts[e0], ROUTE_WIN), :])
                wins.append(y_ref[e0 + 1, pl.ds(starts[e0 + 1], ROUTE_WIN), :])
            part = _dot(jnp.concatenate(halves, axis=1), jnp.concatenate(wins, axis=0))
            total = part if total is None else total + part
        return total

    def dense_scatter(rows):
        st = token_slots(rows)
        lane = lax.broadcasted_iota(jnp.int32, (ROUTE_TILE, cap), 1)
        total = None
        for e in range(N_EXPERTS):
            onehot = jnp.where(lane == st[:, e:e + 1], 1.0, 0.0).astype(BF16)
            part = _dot(onehot, y_ref[e])
            total = part if total is None else total + part
        return total

    def gate_stage(rows, ffn):
        x2 = x1_ref[0, rows, :] + ffn
        return x2, _dot(_rms(x2, gn_ref[...]).astype(BF16), wgbf_ref[...])

    def output_stage(rows, x2, gate_logits):
        emb = _rms(_dot(p_ref[0, rows, :].astype(BF16), wpbf_ref[...]), gp_ref[...])
        o_ref[0, rows, :] = x2 + jax.nn.sigmoid(gate_logits) * emb

    def pipeline(scatter_stage):
        per = PLE_ROWS // ROUTE_TILE
        groups = [slice(g * PLE_ROWS, (g + 1) * PLE_ROWS) for g in range(nsub // per)]
        ffn, gated = {}, {}
        for step in range(len(groups) + 2):
            if step < len(groups):
                ffn[step] = jnp.concatenate([scatter_stage(step * per + j) for j in range(per)], axis=0)
            if 0 <= step - 1 < len(groups):
                gated[step - 1] = gate_stage(groups[step - 1], ffn.pop(step - 1))
            if 0 <= step - 2 < len(groups):
                output_stage(groups[step - 2], *gated.pop(step - 2))

    pl.when(ok)(lambda: pipeline(lambda g: windowed_scatter(row_groups[g], geo[g][0])))
    pl.when(jnp.logical_not(ok))(lambda: pipeline(lambda g: dense_scatter(row_groups[g])))


def _combine(off, x1, slot_t, y, p, gn, wg, wp, gp, cap, tm=1024):
    b, s, _ = x1.shape
    assert tm % PLE_ROWS == 0 and PLE_ROWS % ROUTE_TILE == 0
    tile = lambda w: pl.BlockSpec((1, tm, w), lambda bi, i, off_ref: (bi, i, 0))
    fixed2 = lambda shape: pl.BlockSpec(shape, lambda bi, i, off_ref: (0, 0))
    return pl.pallas_call(
        functools.partial(_combine_kernel, cap=cap),
        grid_spec=pltpu.PrefetchScalarGridSpec(
            num_scalar_prefetch=1,
            grid=(b, s // tm),
            in_specs=[
                tile(D_MODEL),
                tile(LANES),
                pl.BlockSpec((N_EXPERTS, cap, D_MODEL), lambda bi, i, off_ref: (0, bi, 0)),
                tile(PLE_DIM),
                fixed2((1, D_MODEL)),
                pl.BlockSpec(wg.shape, lambda bi, i, off_ref: (0, 0), pipeline_mode=pl.Buffered(1)),
                pl.BlockSpec(wp.shape, lambda bi, i, off_ref: (0, 0), pipeline_mode=pl.Buffered(1)),
                fixed2((1, D_MODEL)),
            ],
            out_specs=tile(D_MODEL),
        ),
        out_shape=jax.ShapeDtypeStruct((b, s, D_MODEL), F32),
        compiler_params=_params("arbitrary", "arbitrary"),
        name="combine",
    )(off, x1, slot_t, y, p, gn, wg.astype(BF16), wp.astype(BF16), gp)


def kernel(x, p, norm_mix, w_in, w_pool, pool_scale, q_norm, k_norm, rpb, w_out, norm_ffn, w_router, w_gate, w_up, w_down, norm_ple, w_ple_gate, w_ple_proj, norm_ple_post):
    b, s, d = x.shape
    depth = w_in.shape[0]
    cap = EC_CAPACITY * s // N_EXPERTS
    row = lambda a: a.reshape(1, -1)
    for i in range(depth):
        u, q, k, v = _in_proj(x.reshape(b * s, d), row(norm_mix[i]), w_in[i],
                              row(jnp.tile(q_norm[i], ATTN_HEADS)), row(jnp.tile(k_norm[i], ATTN_HEADS)))
        shp = lambda a: a.reshape(b, s, -1)
        y_attn = _natten(shp(q), shp(k), shp(v), _attn_bias_table(rpb[i]))
        x1, h, aff_t = _mix(x, shp(u), y_attn, w_pool[i], row(pool_scale[i]), w_out[i],
                            row(norm_ffn[i]), w_router[i].T)
        slot, slot_t, off = _route(aff_t, cap)
        off = _window_table(off, b, cap)
        xe, gate = _dispatch(off, slot, aff_t, h, cap)
        y = _experts(xe, gate, w_gate[i], w_up[i], w_down[i])
        x = _combine(off, x1, slot_t, y, p[i], row(norm_ple[i]), w_ple_gate[i], w_ple_proj[i],
                     row(norm_ple_post[i]), cap)
    return x
```
